```python
import math
import jax
import jax.numpy as jnp
from jax import lax
import numpy as np

D_MODEL = 2048
BATCH = 4
SEQ = 4096
DEPTH = 1

CHUNK = 64

CONV_CH = 1024
CONV_TAPS = 31
S5_WIDTH = 1024
S5_GROUP = 16
S5_GROUPS = S5_WIDTH // S5_GROUP
S5_STATE = 64
N_BRANCHES = 2
COL_CONV_V = CONV_CH
COL_CONV_G = 2 * CONV_CH
COL_S5 = 2 * CONV_CH + S5_WIDTH
IN_COLS = COL_S5 + N_BRANCHES * D_MODEL
N_EXPERTS = 32
TOP_K = 4
D_FF = 2048
SWIGLU_ALPHA = 1.702
SWIGLU_LIMIT = 7.0
MOE_BLOCK = 256
EPS = 1e-6
N_ADA = 6
ADA_SCALE = 0.5

kernel_name = 'hybrid_conformer_s5_moe_adaln_block'


def rms_norm(x, g):
    xf = x.astype(jnp.float32)
    y = xf * lax.rsqrt(jnp.mean(xf * xf, axis=-1, keepdims=True) + EPS)
    return (y * g.astype(jnp.float32)).astype(x.dtype)


def layer_norm(x, g, b):
    xf = x.astype(jnp.float32)
    mu = jnp.mean(xf, axis=-1, keepdims=True)
    xc = xf - mu
    var = jnp.mean(xc * xc, axis=-1, keepdims=True)
    y = xc * lax.rsqrt(var + EPS) * g.astype(jnp.float32) + b.astype(jnp.float32)
    return y.astype(x.dtype)


def causal_depthwise_conv(v, w, b):
    y = lax.conv_general_dilated(
        v, w[:, None, :].astype(v.dtype), window_strides=(1,),
        padding=[(CONV_TAPS - 1, 0)],
        dimension_numbers=('NWC', 'WIO', 'NWC'),
        feature_group_count=v.shape[-1])
    return y + b.astype(v.dtype)


def _complex_linear_combine(left, right):
    ar1, ai1, br1, bi1 = left
    ar2, ai2, br2, bi2 = right
    ar = ar2 * ar1 - ai2 * ai1
    ai = ar2 * ai1 + ai2 * ar1
    br = ar2 * br1 - ai2 * bi1 + br2
    bi = ar2 * bi1 + ai2 * br1 + bi2
    return (ar, ai, br, bi)


def s5_layer(u, a_re, a_im, log_dt, b_re, b_im, c_re, c_im, d_skip):
    f32 = jnp.float32
    bsz, seq, _ = u.shape
    uf = u.astype(f32)
    a_re = a_re.astype(f32)
    a_im = a_im.astype(f32)
    dt = jnp.exp(log_dt.astype(f32))[:, None]
    mag = jnp.exp(a_re * dt)
    abar_re = mag * jnp.cos(a_im * dt)
    abar_im = mag * jnp.sin(a_im * dt)
    den = a_re * a_re + a_im * a_im
    num_re = abar_re - 1.0
    coef_re = (num_re * a_re + abar_im * a_im) / den
    coef_im = (abar_im * a_re - num_re * a_im) / den
    b_re = b_re.astype(f32)
    b_im = b_im.astype(f32)
    bb_re = coef_re[:, :, None] * b_re - coef_im[:, :, None] * b_im
    bb_im = coef_re[:, :, None] * b_im + coef_im[:, :, None] * b_re
    ug = uf.reshape(bsz, seq, S5_GROUPS, S5_GROUP).transpose(1, 0, 2, 3)
    bu_re = jnp.einsum('lbgh,gnh->lbgn', ug, bb_re)
    bu_im = jnp.einsum('lbgh,gnh->lbgn', ug, bb_im)
    ar = jnp.broadcast_to(abar_re[None, None], (seq, 1, S5_GROUPS, S5_STATE))
    ai = jnp.broadcast_to(abar_im[None, None], (seq, 1, S5_GROUPS, S5_STATE))
    _, _, s_re, s_im = lax.associative_scan(_complex_linear_combine, (ar, ai, bu_re, bu_im), axis=0)
    y = (jnp.einsum('ghn,lbgn->blgh', c_re.astype(f32), s_re)
         - jnp.einsum('ghn,lbgn->blgh', c_im.astype(f32), s_im))
    y = y.reshape(bsz, seq, S5_WIDTH) + d_skip.astype(f32) * uf
    return y.astype(u.dtype)


def parallel_mixer(h, w_in, gate_b, dw_w, dw_b, cln_g, cln_b, cp_w, cp_b,
                   s5_a_re, s5_a_im, s5_log_dt, s5_b_re, s5_b_im, s5_c_re, s5_c_im, s5_d,
                   glu_wa, glu_wb, w_out):
    proj = h @ w_in
    conv_v = proj[..., :COL_CONV_V]
    conv_g = proj[..., COL_CONV_V:COL_CONV_G]
    s5_in = proj[..., COL_CONV_G:COL_S5]
    gate_logits = proj[..., COL_S5:] + gate_b
    v = conv_v * jax.nn.sigmoid(conv_g)
    v = causal_depthwise_conv(v, dw_w, dw_b)
    v = jax.nn.silu(layer_norm(v, cln_g, cln_b))
    y_conv = v @ cp_w + cp_b
    y = jax.nn.gelu(s5_layer(s5_in, s5_a_re, s5_a_im, s5_log_dt, s5_b_re, s5_b_im,
                             s5_c_re, s5_c_im, s5_d))
    y_s5 = (y @ glu_wa) * jax.nn.sigmoid(y @ glu_wb)
    gates = jax.nn.sigmoid(gate_logits.astype(jnp.float32)).astype(h.dtype)
    merged = gates[..., :D_MODEL] * y_conv + gates[..., D_MODEL:] * y_s5
    return merged @ w_out


def moe_ffn(h, router_w, router_b, w1, b1, w2, b2):
    bsz, seq, dm = h.shape
    n_tok = bsz * seq
    n_asg = n_tok * TOP_K
    xt = h.reshape(n_tok, dm)
    logits = (xt @ router_w + router_b).astype(jnp.float32)
    top_val, top_idx = lax.top_k(logits, TOP_K)
    gates = jax.nn.softmax(top_val, axis=-1)
    flat_e = top_idx.reshape(n_asg).astype(jnp.int32)
    flat_tok = jnp.broadcast_to(jnp.arange(n_tok, dtype=jnp.int32)[:, None], (n_tok, TOP_K)).reshape(n_asg)
    flat_g = gates.reshape(n_asg)
    order = jnp.argsort(flat_e)
    e_s = flat_e[order]
    tok_s = flat_tok[order]
    g_s = flat_g[order]
    counts = jnp.bincount(flat_e, length=N_EXPERTS).astype(jnp.int32)
    starts = jnp.cumsum(counts) - counts
    padded = (counts + MOE_BLOCK - 1) // MOE_BLOCK * MOE_BLOCK
    pends = jnp.cumsum(padded)
    pstarts = pends - padded
    dest = pstarts[e_s] + jnp.arange(n_asg, dtype=jnp.int32) - starts[e_s]
    n_blocks = -(-n_asg // MOE_BLOCK) + N_EXPERTS
    n_rows = n_blocks * MOE_BLOCK
    tok_buf = jnp.zeros((n_rows,), jnp.int32).at[dest].set(tok_s)
    g_buf = jnp.zeros((n_rows,), jnp.float32).at[dest].set(g_s)
    blk_start = jnp.arange(n_blocks, dtype=jnp.int32) * MOE_BLOCK
    blk_e = jnp.minimum(jnp.searchsorted(pends, blk_start, side='right'), N_EXPERTS - 1).astype(jnp.int32)

    def expert_block(args):
        tok, gate, e = args
        xb = xt[tok]
        hb = xb @ w1[e] + b1[e]
        x_glu = jnp.minimum(hb[:, 0::2], SWIGLU_LIMIT)
        x_lin = jnp.clip(hb[:, 1::2], -SWIGLU_LIMIT, SWIGLU_LIMIT)
        act = x_glu * jax.nn.sigmoid(SWIGLU_ALPHA * x_glu) * (x_lin + 1.0)
        yb = act @ w2[e] + b2[e]
        return yb * gate[:, None].astype(yb.dtype)

    y_buf = lax.map(expert_block, (tok_buf.reshape(n_blocks, MOE_BLOCK),
                                   g_buf.reshape(n_blocks, MOE_BLOCK), blk_e))
    y = jnp.zeros((n_tok, dm), h.dtype).at[tok_buf].add(y_buf.reshape(n_rows, dm).astype(h.dtype))
    return y.reshape(bsz, seq, dm)


def setup_inputs(seed: int = 0) -> dict:
    key = jax.random.key(seed)
    ks = jax.random.split(key, 40)
    f32 = jnp.float32

    def nrm(k, shape, scale):
        return jax.random.normal(k, shape, f32) * scale

    L_ = DEPTH
    D = D_MODEL
    n_idx = jnp.arange(S5_STATE, dtype=f32)
    return {
        'x': nrm(ks[0], (BATCH, SEQ, D), 1.0),
        'c': nrm(ks[1], (BATCH, D), 1.0),
        'ada_w': nrm(ks[2], (L_, D, N_ADA * D), ADA_SCALE * D ** -0.5),
        'ada_b': nrm(ks[3], (L_, N_ADA * D), 0.02),
        'pre_mix_g': 1.0 + nrm(ks[4], (L_, D), 0.05),
        'post_mix_g': 1.0 + nrm(ks[5], (L_, D), 0.05),
        'pre_ffn_g': 1.0 + nrm(ks[6], (L_, D), 0.05),
        'post_ffn_g': 1.0 + nrm(ks[7], (L_, D), 0.05),
        'w_in': nrm(ks[8], (L_, D, IN_COLS), D ** -0.5),
        'gate_b': nrm(ks[9], (L_, N_BRANCHES * D), 0.02),
        'dw_w': nrm(ks[10], (L_, CONV_TAPS, CONV_CH), CONV_TAPS ** -0.5),
        'dw_b': nrm(ks[11], (L_, CONV_CH), 0.02),
        'cln_g': 1.0 + nrm(ks[12], (L_, CONV_CH), 0.05),
        'cln_b': nrm(ks[13], (L_, CONV_CH), 0.02),
        'cp_w': nrm(ks[14], (L_, CONV_CH, D), CONV_CH ** -0.5),
        'cp_b': nrm(ks[15], (L_, D), 0.02),
        's5_a_re': -0.5 + nrm(ks[16], (L_, S5_GROUPS, S5_STATE), 0.01),
        's5_a_im': math.pi * n_idx + nrm(ks[17], (L_, S5_GROUPS, S5_STATE), 0.01),
        's5_log_dt': jax.random.uniform(ks[18], (L_, S5_GROUPS), f32, math.log(1e-3), math.log(1e-1)),
        's5_b_re': nrm(ks[19], (L_, S5_GROUPS, S5_STATE, S5_GROUP), (2 * S5_GROUP) ** -0.5),
        's5_b_im': nrm(ks[20], (L_, S5_GROUPS, S5_STATE, S5_GROUP), (2 * S5_GROUP) ** -0.5),
        's5_c_re': nrm(ks[21], (L_, S5_GROUPS, S5_GROUP, S5_STATE), (2 * S5_STATE) ** -0.5),
        's5_c_im': nrm(ks[22], (L_, S5_GROUPS, S5_GROUP, S5_STATE), (2 * S5_STATE) ** -0.5),
        's5_d': nrm(ks[23], (L_, S5_WIDTH), 1.0),
        'glu_wa': nrm(ks[24], (L_, S5_WIDTH, D), S5_WIDTH ** -0.5),
        'glu_wb': nrm(ks[25], (L_, S5_WIDTH, D), S5_WIDTH ** -0.5),
        'w_out': nrm(ks[26], (L_, D, D), D ** -0.5),
        'router_w': nrm(ks[27], (L_, D, N_EXPERTS), D ** -0.5),
        'router_b': nrm(ks[28], (L_, N_EXPERTS), 0.01),
        'w1': nrm(ks[29], (L_, N_EXPERTS, D, 2 * D_FF), D ** -0.5),
        'b1': nrm(ks[30], (L_, N_EXPERTS, 2 * D_FF), 0.02),
        'w2': nrm(ks[31], (L_, N_EXPERTS, D_FF, D), D_FF ** -0.5),
        'b2': nrm(ks[32], (L_, N_EXPERTS, D), 0.02),
    }


def reference(x, c, ada_w, ada_b, pre_mix_g, post_mix_g, pre_ffn_g, post_ffn_g,
              w_in, gate_b, dw_w, dw_b, cln_g, cln_b, cp_w, cp_b,
              s5_a_re, s5_a_im, s5_log_dt, s5_b_re, s5_b_im, s5_c_re, s5_c_im, s5_d,
              glu_wa, glu_wb, w_out, router_w, router_b, w1, b1, w2, b2):
    for l in range(DEPTH):
        ada = jax.nn.silu(c) @ ada_w[l] + ada_b[l]
        sh1, sc1, gt1, sh2, sc2, gt2 = jnp.split(ada[:, None, :], N_ADA, axis=-1)
        h = rms_norm(x, pre_mix_g[l]) * (1.0 + sc1) + sh1
        m = parallel_mixer(h, w_in[l], gate_b[l], dw_w[l], dw_b[l], cln_g[l], cln_b[l],
                           cp_w[l], cp_b[l], s5_a_re[l], s5_a_im[l], s5_log_dt[l],
                           s5_b_re[l], s5_b_im[l], s5_c_re[l], s5_c_im[l], s5_d[l],
                           glu_wa[l], glu_wb[l], w_out[l])
        x = x + gt1 * rms_norm(m, post_mix_g[l])
        h = rms_norm(x, pre_ffn_g[l]) * (1.0 + sc2) + sh2
        f = moe_ffn(h, router_w[l], router_b[l], w1[l], b1[l], w2[l], b2[l])
        x = x + gt2 * rms_norm(f, post_ffn_g[l])
    return x
```

```python
import functools
import math

import jax
import jax.numpy as jnp
from jax import lax
from jax.experimental import pallas as pl
from jax.experimental.pallas import tpu as pltpu

EPS = 1e-6
N_ADA = 6
TOP_K = 4
MOE_BLOCK = 256
SWIGLU_ALPHA = 1.702
SWIGLU_LIMIT = 7.0
S5_CHUNK = 16
V7X_LANES = 128
V7X_SUBLANES = 8
V7X_MXU = 256
VMEM_LIMIT = 56 * 1024 * 1024

F32 = jnp.float32
BF16 = jnp.bfloat16


def _cparams(sem):
    return pltpu.CompilerParams(dimension_semantics=sem, vmem_limit_bytes=VMEM_LIMIT)


def _sigmoid(v):
    return 1.0 / (1.0 + jnp.exp(-v))


def _rms(v, g):
    return v * lax.rsqrt(jnp.mean(v * v, axis=-1, keepdims=True) + EPS) * g


def _ada_kernel(c_ref, w_ref, b_ref, o_ref):
    c = c_ref[...]
    s = (c * _sigmoid(c)).astype(BF16)
    o_ref[...] = jnp.dot(s, w_ref[...].astype(BF16), preferred_element_type=F32) + b_ref[...]


def _ada(c, ada_w, ada_b):
    bsz, d = c.shape
    n = ada_w.shape[1]
    tn = 1024 if n % 1024 == 0 else n
    return pl.pallas_call(
        _ada_kernel,
        grid=(n // tn,),
        in_specs=[
            pl.BlockSpec((bsz, d), lambda j: (0, 0)),
            pl.BlockSpec((d, tn), lambda j: (0, j)),
            pl.BlockSpec((1, tn), lambda j: (0, j)),
        ],
        out_specs=pl.BlockSpec((bsz, tn), lambda j: (0, j)),
        out_shape=jax.ShapeDtypeStruct((bsz, n), F32),
        compiler_params=_cparams(("arbitrary",)),
        name="ada",
    )(c, ada_w, ada_b.reshape(1, n))


def _inproj_kernel(x_ref, ada_ref, g_ref, w_ref, o_ref, h_ref):
    @pl.when(pl.program_id(1) == 0)
    def _():
        y = _rms(x_ref[...], g_ref[...])
        h = y * (1.0 + ada_ref[0, 1:2, :]) + ada_ref[0, 0:1, :]
        h_ref[...] = h.astype(BF16)

    o_ref[...] = jnp.dot(h_ref[...], w_ref[...], preferred_element_type=F32).astype(BF16)


def _inproj(x2, ada3, g, w_bf, seq, tm, tn):
    t, d = x2.shape
    n = w_bf.shape[1]
    per_b = seq // tm
    return pl.pallas_call(
        _inproj_kernel,
        grid=(t // tm, n // tn),
        in_specs=[
            pl.BlockSpec((tm, d), lambda i, j: (i, 0)),
            pl.BlockSpec((1, N_ADA, d), lambda i, j: (i // per_b, 0, 0)),
            pl.BlockSpec((1, d), lambda i, j: (0, 0)),
            pl.BlockSpec((d, tn), lambda i, j: (0, j)),
        ],
        out_specs=pl.BlockSpec((tm, tn), lambda i, j: (i, j)),
        out_shape=jax.ShapeDtypeStruct((t, n), BF16),
        scratch_shapes=[pltpu.VMEM((tm, d), BF16)],
        compiler_params=_cparams(("arbitrary", "arbitrary")),
        name="inproj",
    )(x2, ada3, g.reshape(1, d), w_bf)


CONV_HALO = 32
CONV_ROWS = 64


def _conv_kernel(pv_ref, pg_ref, w_ref, b_ref, g_ref, beta_ref, o_ref, vext, cbuf, *, taps, tl):
    t = pl.program_id(1)
    ch = cbuf.shape[1]

    @pl.when(t == 0)
    def _():
        vext[0:CONV_HALO, :] = jnp.zeros((CONV_HALO, ch), F32)

    @pl.when(t > 0)
    def _():
        vext[0:CONV_HALO, :] = vext[tl:tl + CONV_HALO, :]

    pv = pv_ref[...].astype(F32)
    pg = pg_ref[...].astype(F32)
    vext[CONV_HALO:CONV_HALO + tl, :] = pv * _sigmoid(pg)

    off = CONV_HALO - (taps - 1)
    for cb in range(ch // V7X_LANES):
        lanes = slice(cb * V7X_LANES, (cb + 1) * V7X_LANES)

        for rc in range(tl // CONV_ROWS):
            r0 = rc * CONV_ROWS
            acc = jnp.zeros((CONV_ROWS, V7X_LANES), F32)
            for k in range(taps):
                acc = acc + w_ref[k:k + 1, lanes] * vext[r0 + off + k:r0 + off + k + CONV_ROWS, lanes]
            cbuf[r0:r0 + CONV_ROWS, lanes] = acc + b_ref[:, lanes]

    v = cbuf[...]
    mu = jnp.mean(v, axis=-1, keepdims=True)
    xc = v - mu
    var = jnp.mean(xc * xc, axis=-1, keepdims=True)
    y = xc * lax.rsqrt(var + EPS) * g_ref[...] + beta_ref[...]
    o_ref[...] = (y * _sigmoid(y)).astype(BF16)


def _conv_branch(proj, dw_w, dw_b, cln_g, cln_b, bsz, seq, tl):
    taps, ch = dw_w.shape
    assert taps - 1 <= CONV_HALO and tl % CONV_ROWS == 0 and ch % V7X_LANES == 0
    per_b = seq // tl
    vec = lambda a: a.reshape(1, ch)
    return pl.pallas_call(
        functools.partial(_conv_kernel, taps=taps, tl=tl),
        grid=(bsz, per_b),
        in_specs=[
            pl.BlockSpec((tl, ch), lambda b, t: (b * per_b + t, 0)),
            pl.BlockSpec((tl, ch), lambda b, t: (b * per_b + t, 1)),
            pl.BlockSpec((taps, ch), lambda b, t: (0, 0)),
            pl.BlockSpec((1, ch), lambda b, t: (0, 0)),
            pl.BlockSpec((1, ch), lambda b, t: (0, 0)),
            pl.BlockSpec((1, ch), lambda b, t: (0, 0)),
        ],
        out_specs=pl.BlockSpec((tl, ch), lambda b, t: (b * per_b + t, 0)),
        out_shape=jax.ShapeDtypeStruct((bsz * seq, ch), BF16),
        scratch_shapes=[pltpu.VMEM((CONV_HALO + tl, ch), F32), pltpu.VMEM((tl, ch), F32)],
        compiler_params=_cparams(("arbitrary", "arbitrary")),
        name="conv",
    )(proj, proj, dw_w, vec(dw_b), vec(cln_g), vec(cln_b))


def _s5_matrices(a_re, a_im, log_dt, b_re, b_im, c_re, c_im):
    hp = lax.Precision.HIGHEST
    ng, ns = a_re.shape
    nh = b_re.shape[-1]
    tc = S5_CHUNK
    dt = jnp.exp(log_dt)[:, None]
    mag = jnp.exp(a_re * dt)
    abar_re = mag * jnp.cos(a_im * dt)
    abar_im = mag * jnp.sin(a_im * dt)
    den = a_re * a_re + a_im * a_im
    num_re = abar_re - 1.0
    coef_re = (num_re * a_re + abar_im * a_im) / den
    coef_im = (abar_im * a_re - num_re * a_im) / den
    bb_re = coef_re[:, :, None] * b_re - coef_im[:, :, None] * b_im
    bb_im = coef_re[:, :, None] * b_im + coef_im[:, :, None] * b_re
    p = jnp.arange(tc + 1, dtype=F32)[:, None, None]
    pmag = jnp.exp(p * (a_re * dt)[None])
    pw_re = pmag * jnp.cos(p * (a_im * dt)[None])
    pw_im = pmag * jnp.sin(p * (a_im * dt)[None])
    cp_re = c_re[None] * pw_re[:, :, None, :] - c_im[None] * pw_im[:, :, None, :]
    cp_im = c_re[None] * pw_im[:, :, None, :] + c_im[None] * pw_re[:, :, None, :]
    kern = (jnp.einsum('pgon,gni->pgoi', cp_re[:tc], bb_re, precision=hp)
            - jnp.einsum('pgon,gni->pgoi', cp_im[:tc], bb_im, precision=hp))
    s_idx = jnp.arange(tc)[:, None]
    t_idx = jnp.arange(tc)[None, :]
    tau = jnp.clip(t_idx - s_idx, 0, tc - 1)
    kst = jnp.where((t_idx >= s_idx)[:, :, None, None, None], kern[tau], 0.0)
    m_intra = kst.transpose(2, 0, 4, 1, 3).reshape(ng, tc * nh, tc * nh)
    rp_re = pw_re[:tc][::-1]
    rp_im = pw_im[:tc][::-1]
    ms_re = rp_re[:, :, :, None] * bb_re[None] - rp_im[:, :, :, None] * bb_im[None]
    ms_im = rp_re[:, :, :, None] * bb_im[None] + rp_im[:, :, :, None] * bb_re[None]
    ms_re = ms_re.transpose(1, 0, 3, 2).reshape(ng, tc * nh, ns)
    ms_im = ms_im.transpose(1, 0, 3, 2).reshape(ng, tc * nh, ns)
    mo_re = cp_re[1:].transpose(1, 3, 0, 2).reshape(ng, ns, tc * nh)
    mo_im = (-cp_im[1:]).transpose(1, 3, 0, 2).reshape(ng, ns, tc * nh)
    return m_intra, ms_re, ms_im, mo_re, mo_im, pw_re[tc], pw_im[tc]


def _s5_kernel(u_ref, mi_ref, msr_ref, msi_ref, mor_ref, moi_ref, lr_ref, li_ref, d_ref, o_ref,
               zr_ref, zi_ref, *, gb, ns, bsz, n_chunks):
    for g in range(gb):
        lanes = slice(g * ns, (g + 1) * ns)
        u = u_ref[g]
        zr_ref[:, lanes] = jnp.dot(u, msr_ref[g], preferred_element_type=F32)
        zi_ref[:, lanes] = jnp.dot(u, msi_ref[g], preferred_element_type=F32)

    lam_r = lr_ref[0]
    lam_i = li_ref[0]

    def step(c, carry):
        zr, zi = carry
        rows = pl.ds(pl.multiple_of(c * bsz, bsz), bsz)
        wr = zr_ref[rows, :]
        wi = zi_ref[rows, :]
        zr_ref[rows, :] = zr
        zi_ref[rows, :] = zi
        return (lam_r * zr - lam_i * zi + wr, lam_r * zi + lam_i * zr + wi)

    zero = jnp.zeros((bsz, gb * ns), F32)
    lax.fori_loop(0, n_chunks, step, (zero, zero))

    for g in range(gb):
        lanes = slice(g * ns, (g + 1) * ns)
        u = u_ref[g]
        y = jnp.dot(u, mi_ref[g], preferred_element_type=F32)
        y = y + jnp.dot(zr_ref[:, lanes].astype(BF16), mor_ref[g], preferred_element_type=F32)
        y = y + jnp.dot(zi_ref[:, lanes].astype(BF16), moi_ref[g], preferred_element_type=F32)
        y = y + d_ref[g] * u.astype(F32)
        y = 0.5 * y * (1.0 + jnp.tanh(math.sqrt(2.0 / math.pi) * (y + 0.044715 * (y * y * y))))
        o_ref[g] = y.astype(BF16)


def _s5_branch(u_flat, mats, d_tiled, bsz, gb):
    m_intra, ms_re, ms_im, mo_re, mo_im, lam_re, lam_im = mats
    ng, rows, w = u_flat.shape
    ns = ms_re.shape[-1]
    n_chunks = rows // bsz
    gspec = lambda shape: pl.BlockSpec((gb,) + shape, lambda i: (i, 0, 0))
    lam = lambda a: a.reshape(ng // gb, 1, gb * ns)
    return pl.pallas_call(
        functools.partial(_s5_kernel, gb=gb, ns=ns, bsz=bsz, n_chunks=n_chunks),
        grid=(ng // gb,),
        in_specs=[
            gspec((rows, w)),
            gspec((w, w)),
            gspec((w, ns)), gspec((w, ns)),
            gspec((ns, w)), gspec((ns, w)),
            pl.BlockSpec((1, 1, gb * ns), lambda i: (i, 0, 0)),
            pl.BlockSpec((1, 1, gb * ns), lambda i: (i, 0, 0)),
            gspec((1, w)),
        ],
        out_specs=gspec((rows, w)),
        out_shape=jax.ShapeDtypeStruct((ng, rows, w), BF16),
        scratch_shapes=[pltpu.VMEM((rows, gb * ns), F32), pltpu.VMEM((rows, gb * ns), F32)],
        compiler_params=_cparams(("arbitrary",)),
        name="s5",
    )(u_flat, m_intra.astype(BF16), ms_re.astype(BF16), ms_im.astype(BF16),
      mo_re.astype(BF16), mo_im.astype(BF16), lam(lam_re), lam(lam_im), d_tiled)


def _merge_kernel(vc_ref, ys_ref, l1_ref, l2_ref, cpw_ref, cpb_ref, wa_ref, wb_ref, gb1_ref, gb2_ref, o_ref):
    y_conv = jnp.dot(vc_ref[...], cpw_ref[...], preferred_element_type=F32) + cpb_ref[...]
    ys = ys_ref[...]
    a = jnp.dot(ys, wa_ref[...], preferred_element_type=F32)
    b = jnp.dot(ys, wb_ref[...], preferred_element_type=F32)
    y_s5 = a * _sigmoid(b)
    g1 = _sigmoid(l1_ref[...].astype(F32) + gb1_ref[...])
    g2 = _sigmoid(l2_ref[...].astype(F32) + gb2_ref[...])
    o_ref[...] = (g1 * y_conv + g2 * y_s5).astype(BF16)


def _merge(vc, ys, proj, cp_w, cp_b, wa, wb, gate_b, tm, col0):
    t, ch = vc.shape
    d = cp_w.shape[1]
    tn = ch
    assert col0 % tn == 0 and d % tn == 0
    cb, nj = col0 // tn, d // tn
    wcol = lambda rows: pl.BlockSpec((rows, tn), lambda j, i: (0, j))
    return pl.pallas_call(
        _merge_kernel,
        grid=(nj, t // tm),
        in_specs=[
            pl.BlockSpec((tm, ch), lambda j, i: (i, 0)),
            pl.BlockSpec((tm, ch), lambda j, i: (i, 0)),
            pl.BlockSpec((tm, tn), lambda j, i: (i, cb + j)),
            pl.BlockSpec((tm, tn), lambda j, i: (i, cb + nj + j)),
            wcol(ch), wcol(1), wcol(ch), wcol(ch), wcol(1), wcol(1),
        ],
        out_specs=pl.BlockSpec((tm, tn), lambda j, i: (i, j)),
        out_shape=jax.ShapeDtypeStruct((t, d), BF16),
        compiler_params=_cparams(("arbitrary", "arbitrary")),
        name="merge",
    )(vc, ys, proj, proj, cp_w.astype(BF16), cp_b.reshape(1, d), wa.astype(BF16), wb.astype(BF16),
      gate_b[:d].reshape(1, d), gate_b[d:].reshape(1, d))


def _split_bf16(v):
    hi = v.astype(BF16)
    lo = (v - hi.astype(F32)).astype(BF16)
    return hi, lo


def _outproj_kernel(m_ref, x_ref, ada_ref, wo_ref, g1_ref, g2_ref, rwh_ref, rwl_ref, rb_ref,
                    x1_ref, h2_ref, lg_ref):
    m = jnp.dot(m_ref[...], wo_ref[...], preferred_element_type=F32)
    x1 = x_ref[...] + ada_ref[0, 2:3, :] * _rms(m, g1_ref[...])
    x1_ref[...] = x1
    h2 = _rms(x1, g2_ref[...]) * (1.0 + ada_ref[0, 4:5, :]) + ada_ref[0, 3:4, :]
    h2_ref[...] = h2
    hh, hl = _split_bf16(h2)
    lg = jnp.dot(hh, rwh_ref[...], preferred_element_type=F32)
    lg = lg + jnp.dot(hl, rwh_ref[...], preferred_element_type=F32)
    lg = lg + jnp.dot(hh, rwl_ref[...], preferred_element_type=F32)
    lg_ref[...] = lg + rb_ref[...]


def _outproj(merged, x2, ada3, w_out, post_mix_g, pre_ffn_g, router_w, router_b, seq, tm):
    t, d = x2.shape
    ne = router_w.shape[1]
    nep = max(V7X_LANES, ne)
    rw = jnp.zeros((d, nep), F32).at[:, :ne].set(router_w)
    rw_hi, rw_lo = _split_bf16(rw)
    rb = jnp.zeros((1, nep), F32).at[0, :ne].set(router_b)
    per_b = seq // tm
    full = lambda shape: pl.BlockSpec(shape, lambda i: (0, 0))
    row = lambda w: pl.BlockSpec((tm, w), lambda i: (i, 0))
    x1, h2, lg = pl.pallas_call(
        _outproj_kernel,
        grid=(t // tm,),
        in_specs=[
            row(d), row(d),
            pl.BlockSpec((1, N_ADA, d), lambda i: (i // per_b, 0, 0)),
            full((d, d)), full((1, d)), full((1, d)), full((d, nep)), full((d, nep)), full((1, nep)),
        ],
        out_specs=[row(d), row(d), row(nep)],
        out_shape=[jax.ShapeDtypeStruct((t, d), F32), jax.ShapeDtypeStruct((t, d), F32),
                   jax.ShapeDtypeStruct((t, nep), F32)],
        compiler_params=_cparams(("arbitrary",)),
        name="outproj",
    )(merged, x2, ada3, w_out.astype(BF16), post_mix_g.reshape(1, d), pre_ffn_g.reshape(1, d),
      rw_hi, rw_lo, rb)
    return x1, h2, lg[:, :ne]


def _row_gather(idx_ref, src_hbm, buf, sem, n):
    def row_copy(r):
        return pltpu.make_async_copy(src_hbm.at[pl.ds(idx_ref[0, 0, r], 1), :], buf.at[pl.ds(r, 1), :], sem)

    def start(r, carry):
        row_copy(r).start()
        return carry

    def wait(r, carry):
        row_copy(r).wait()
        return carry

    lax.fori_loop(0, n, start, 0)
    lax.fori_loop(0, n, wait, 0)


def _gather_kernel(tok_ref, h_hbm, o_ref, buf, sem):
    _row_gather(tok_ref, h_hbm, buf, sem, MOE_BLOCK)
    o_ref[...] = buf[...].astype(BF16)


def _moe_gather(tok_buf, h2, n_blocks):
    t, d = h2.shape
    return pl.pallas_call(
        _gather_kernel,
        grid=(n_blocks,),
        in_specs=[
            pl.BlockSpec((1, 1, MOE_BLOCK), lambda i: (i, 0, 0), memory_space=pltpu.SMEM),
            pl.BlockSpec(memory_space=pl.ANY),
        ],
        out_specs=pl.BlockSpec((MOE_BLOCK, d), lambda i: (i, 0)),
        out_shape=jax.ShapeDtypeStruct((n_blocks * MOE_BLOCK, d), BF16),
        scratch_shapes=[pltpu.VMEM((MOE_BLOCK, d), F32), pltpu.SemaphoreType.DMA(())],
        compiler_params=_cparams(("arbitrary",)),
        name="moe_gather",
    )(tok_buf.reshape(n_blocks, 1, MOE_BLOCK), h2)


def _expert_changed(e_ref, i):
    return jnp.logical_or(i == 0, e_ref[i] != e_ref[jnp.maximum(i - 1, 0)])


def _up_kernel(e_ref, x_ref, w_ref, bg_ref, bl_ref, p_ref, o_ref, wp_ref):
    i = pl.program_id(1)
    tn = wp_ref.shape[1]
    half = V7X_MXU // 2

    @pl.when(_expert_changed(e_ref, i))
    def _():
        for cb in range(tn // V7X_MXU):
            cols = slice(cb * V7X_MXU, (cb + 1) * V7X_MXU)
            wp_ref[:, cols] = jnp.dot(w_ref[0, :, cols].astype(BF16), p_ref[...],
                                      preferred_element_type=F32).astype(BF16)

    hb = jnp.dot(x_ref[...], wp_ref[...], preferred_element_type=F32)
    for cb in range(tn // V7X_MXU):
        out_cols = slice(cb * half, (cb + 1) * half)
        x_glu = hb[:, cb * V7X_MXU:cb * V7X_MXU + half] + bg_ref[0, :, out_cols]
        x_lin = hb[:, cb * V7X_MXU + half:(cb + 1) * V7X_MXU] + bl_ref[0, :, out_cols]
        x_glu = jnp.minimum(x_glu, SWIGLU_LIMIT)
        x_lin = jnp.clip(x_lin, -SWIGLU_LIMIT, SWIGLU_LIMIT)
        act = x_glu * _sigmoid(SWIGLU_ALPHA * x_glu) * (x_lin + 1.0)
        o_ref[:, out_cols] = act.astype(BF16)


def _moe_up(blk_e, xs, w1, b1, tn):
    n_rows, d = xs.shape
    ne, _, f2 = w1.shape
    f = f2 // 2
    n_blocks = n_rows // MOE_BLOCK
    half = V7X_MXU // 2
    c = jnp.arange(V7X_MXU)
    perm = jnp.zeros((V7X_MXU, V7X_MXU), BF16).at[c, (c % 2) * half + c // 2].set(1.0)
    b1g = b1[:, 0::2].reshape(ne, 1, f)
    b1l = b1[:, 1::2].reshape(ne, 1, f)
    return pl.pallas_call(
        _up_kernel,
        grid_spec=pltpu.PrefetchScalarGridSpec(
            num_scalar_prefetch=1,
            grid=(f2 // tn, n_blocks),
            in_specs=[
                pl.BlockSpec((MOE_BLOCK, d), lambda j, i, e: (i, 0)),
                pl.BlockSpec((1, d, tn), lambda j, i, e: (e[i], 0, j)),
                pl.BlockSpec((1, 1, tn // 2), lambda j, i, e: (e[i], 0, j)),
                pl.BlockSpec((1, 1, tn // 2), lambda j, i, e: (e[i], 0, j)),
                pl.BlockSpec((V7X_MXU, V7X_MXU), lambda j, i, e: (0, 0)),
            ],
            out_specs=pl.BlockSpec((MOE_BLOCK, tn // 2), lambda j, i, e: (i, j)),
            scratch_shapes=[pltpu.VMEM((d, tn), BF16)],
        ),
        out_shape=jax.ShapeDtypeStruct((n_rows, f), BF16),
        compiler_params=_cparams(("arbitrary", "arbitrary")),
        name="moe_up",
    )(blk_e, xs, w1, b1g, b1l, perm)


def _down_kernel(e_ref, a_ref, w_ref, b_ref, g_ref, o_ref, wb_ref):
    i = pl.program_id(1)

    @pl.when(_expert_changed(e_ref, i))
    def _():
        wb_ref[...] = w_ref[0].astype(BF16)

    y = jnp.dot(a_ref[...], wb_ref[...], preferred_element_type=F32) + b_ref[0]
    o_ref[...] = y * g_ref[...]


def _moe_down(blk_e, act, w2, b2, g_buf, tn):
    n_rows, f = act.shape
    ne, _, d = w2.shape
    n_blocks = n_rows // MOE_BLOCK
    return pl.pallas_call(
        _down_kernel,
        grid_spec=pltpu.PrefetchScalarGridSpec(
            num_scalar_prefetch=1,
            grid=(d // tn, n_blocks),
            in_specs=[
                pl.BlockSpec((MOE_BLOCK, f), lambda j, i, e: (i, 0)),
                pl.BlockSpec((1, f, tn), lambda j, i, e: (e[i], 0, j)),
                pl.BlockSpec((1, 1, tn), lambda j, i, e: (e[i], 0, j)),
                pl.BlockSpec((MOE_BLOCK, 1), lambda j, i, e: (i, 0)),
            ],
            out_specs=pl.BlockSpec((MOE_BLOCK, tn), lambda j, i, e: (i, j)),
            scratch_shapes=[pltpu.VMEM((f, tn), BF16)],
        ),
        out_shape=jax.ShapeDtypeStruct((n_rows, d), F32),
        compiler_params=_cparams(("arbitrary", "arbitrary")),
        name="moe_down",
    )(blk_e, act, w2, b2.reshape(ne, 1, d), g_buf.reshape(n_rows, 1))


def _combine_kernel(pos_ref, y_hbm, x1_ref, ada_ref, g_ref, o_ref, buf, sem, *, tt):
    _row_gather(pos_ref, y_hbm, buf, sem, TOP_K * tt)
    f = buf[0:tt, :]
    for k in range(1, TOP_K):
        f = f + buf[k * tt:(k + 1) * tt, :]
    o_ref[...] = x1_ref[...] + ada_ref[0, 5:6, :] * _rms(f, g_ref[...])


def _moe_combine(pos, y_buf, x1, ada3, post_ffn_g, seq, tt):
    t, d = x1.shape
    per_b = seq // tt
    pos_t = pos.reshape(t // tt, tt, TOP_K).transpose(0, 2, 1).reshape(t // tt, 1, TOP_K * tt)
    return pl.pallas_call(
        functools.partial(_combine_kernel, tt=tt),
        grid=(t // tt,),
        in_specs=[
            pl.BlockSpec((1, 1, TOP_K * tt), lambda i: (i, 0, 0), memory_space=pltpu.SMEM),
            pl.BlockSpec(memory_space=pl.ANY),
            pl.BlockSpec((tt, d), lambda i: (i, 0)),
            pl.BlockSpec((1, N_ADA, d), lambda i: (i // per_b, 0, 0)),
            pl.BlockSpec((1, d), lambda i: (0, 0)),
        ],
        out_specs=pl.BlockSpec((tt, d), lambda i: (i, 0)),
        out_shape=jax.ShapeDtypeStruct((t, d), F32),
        scratch_shapes=[pltpu.VMEM((TOP_K * tt, d), F32), pltpu.SemaphoreType.DMA(())],
        compiler_params=_cparams(("arbitrary",)),
        name="moe_combine",
    )(pos_t, y_buf, x1, ada3, post_ffn_g.reshape(1, d))


def _route(logits, n_experts):
    n_tok = logits.shape[0]
    n_asg = n_tok * TOP_K
    top_val, top_idx = lax.top_k(logits, TOP_K)
    gates = jax.nn.softmax(top_val, axis=-1)
    flat_e = top_idx.reshape(n_asg).astype(jnp.int32)
    flat_tok = jnp.arange(n_asg, dtype=jnp.int32) // TOP_K
    order = jnp.argsort(flat_e)
    e_s = flat_e[order]
    counts = jnp.bincount(flat_e, length=n_experts).astype(jnp.int32)
    starts = jnp.cumsum(counts) - counts
    padded = (counts + MOE_BLOCK - 1) // MOE_BLOCK * MOE_BLOCK
    pends = jnp.cumsum(padded)
    pstarts = pends - padded
    dest = pstarts[e_s] + jnp.arange(n_asg, dtype=jnp.int32) - starts[e_s]
    n_blocks = -(-n_asg // MOE_BLOCK) + n_experts
    n_rows = n_blocks * MOE_BLOCK
    tok_buf = jnp.zeros((n_rows,), jnp.int32).at[dest].set(flat_tok[order])
    g_buf = jnp.zeros((n_rows,), F32).at[dest].set(gates.reshape(n_asg)[order])
    blk_start = jnp.arange(n_blocks, dtype=jnp.int32) * MOE_BLOCK
    blk_e = jnp.minimum(jnp.searchsorted(pends, blk_start, side='right'), n_experts - 1).astype(jnp.int32)
    pos = jnp.zeros((n_asg,), jnp.int32).at[order].set(dest)
    return tok_buf, g_buf, blk_e, pos, n_blocks


def _pick(n, pref):
    return pref if n % pref == 0 else n


def kernel(x, c, ada_w, ada_b, pre_mix_g, post_mix_g, pre_ffn_g, post_ffn_g, w_in, gate_b, dw_w, dw_b, cln_g, cln_b, cp_w, cp_b, s5_a_re, s5_a_im, s5_log_dt, s5_b_re, s5_b_im, s5_c_re, s5_c_im, s5_d, glu_wa, glu_wb, w_out, router_w, router_b, w1, b1, w2, b2):
    bsz, seq, d = x.shape
    t = bsz * seq
    depth = ada_w.shape[0]
    conv_ch = dw_w.shape[-1]
    ng, ns, nh = s5_b_re.shape[1:]
    s5_w = ng * nh
    ne = router_w.shape[-1]
    col_s5 = 2 * conv_ch
    col_gate = col_s5 + s5_w
    assert conv_ch == s5_w and seq % S5_CHUNK == 0
    tm = _pick(seq, 512)
    tm_out = _pick(seq, 256)
    tn_in = _pick(w_in.shape[-1], conv_ch)
    n_chunks = seq // S5_CHUNK
    cw = S5_CHUNK * nh
    bslots = -(-bsz // V7X_SUBLANES) * V7X_SUBLANES

    x2 = x.reshape(t, d)
    for l in range(depth):
        ada3 = _ada(c, ada_w[l], ada_b[l]).reshape(bsz, N_ADA, d)
        proj = _inproj(x2, ada3, pre_mix_g[l], w_in[l].astype(BF16), seq, tm, tn_in)
        vc = _conv_branch(proj, dw_w[l], dw_b[l], cln_g[l], cln_b[l], bsz, seq, tm)

        mats = _s5_matrices(s5_a_re[l], s5_a_im[l], s5_log_dt[l], s5_b_re[l], s5_b_im[l],
                            s5_c_re[l], s5_c_im[l])
        u = proj[:, col_s5:col_gate].reshape(bsz, n_chunks, S5_CHUNK, ng, nh)
        u_flat = jnp.pad(u.transpose(3, 1, 0, 2, 4), ((0, 0), (0, 0), (0, bslots - bsz), (0, 0), (0, 0)))
        u_flat = u_flat.reshape(ng, n_chunks * bslots, cw)
        d_tiled = jnp.tile(s5_d[l].reshape(ng, 1, nh), (1, S5_CHUNK, 1)).reshape(ng, 1, cw)
        ys = _s5_branch(u_flat, mats, d_tiled, bslots, _pick(ng, 8))
        ys = ys.reshape(ng, n_chunks, bslots, S5_CHUNK, nh)[:, :, :bsz]
        ys = ys.transpose(2, 1, 3, 0, 4).reshape(t, s5_w)

        merged = _merge(vc, ys, proj, cp_w[l], cp_b[l], glu_wa[l], glu_wb[l], gate_b[l], tm, col_gate)
        x1, h2, logits = _outproj(merged, x2, ada3, w_out[l], post_mix_g[l], pre_ffn_g[l],
                                  router_w[l], router_b[l], seq, tm_out)

        tok_buf, g_buf, blk_e, pos, n_blocks = _route(logits, ne)
        xs = _moe_gather(tok_buf, h2, n_blocks)
        act = _moe_up(blk_e, xs, w1[l], b1[l], _pick(w1.shape[-1], 1024))
        y_buf = _moe_down(blk_e, act, w2[l], b2[l], g_buf, _pick(d, 1024))
        x2 = _moe_combine(pos, y_buf, x1, ada3, post_ffn_g[l], seq, _pick(seq, 64))
    return x2.reshape(bsz, seq, d)
```

```python
import functools
import math

import jax
import jax.numpy as jnp
from jax import lax
from jax.experimental import pallas as pl
from jax.experimental.pallas import tpu as pltpu

EPS = 1e-6
N_ADA = 6
TOP_K = 4
MOE_BLOCK = 256
SWIGLU_ALPHA = 1.702
SWIGLU_LIMIT = 7.0
S5_CHUNK = 16
V7X_LANES = 128
V7X_SUBLANES = 8
V7X_MXU = 256
VMEM_LIMIT = 56 * 1024 * 1024

F32 = jnp.float32
BF16 = jnp.bfloat16


def _cparams(sem):
    return pltpu.CompilerParams(dimension_semantics=sem, vmem_limit_bytes=VMEM_LIMIT)


def _sigmoid(v):
    return 1.0 / (1.0 + jnp.exp(-v))


def _rms(v, g):
    return v * lax.rsqrt(jnp.mean(v * v, axis=-1, keepdims=True) + EPS) * g


def _ada_kernel(c_ref, w_ref, b_ref, o_ref):
    c = c_ref[...]
    s = (c * _sigmoid(c)).astype(BF16)
    o_ref[...] = jnp.dot(s, w_ref[...].astype(BF16), preferred_element_type=F32) + b_ref[...]


def _ada(c, ada_w, ada_b):
    bsz, d = c.shape
    n = ada_w.shape[1]
    tn = 1024 if n % 1024 == 0 else n
    return pl.pallas_call(
        _ada_kernel,
        grid=(n // tn,),
        in_specs=[
            pl.BlockSpec((bsz, d), lambda j: (0, 0)),
            pl.BlockSpec((d, tn), lambda j: (0, j)),
            pl.BlockSpec((1, tn), lambda j: (0, j)),
        ],
        out_specs=pl.BlockSpec((bsz, tn), lambda j: (0, j)),
        out_shape=jax.ShapeDtypeStruct((bsz, n), F32),
        compiler_params=_cparams(("arbitrary",)),
        name="ada",
    )(c, ada_w, ada_b.reshape(1, n))


def _inproj_kernel(x_ref, ada_ref, g_ref, w_ref, o_ref, h_ref):
    @pl.when(pl.program_id(1) == 0)
    def _():
        y = _rms(x_ref[...], g_ref[...])
        h = y * (1.0 + ada_ref[0, 1:2, :]) + ada_ref[0, 0:1, :]
        h_ref[...] = h.astype(BF16)

    o_ref[...] = jnp.dot(h_ref[...], w_ref[...], preferred_element_type=F32).astype(BF16)


def _inproj(x2, ada3, g, w_bf, seq, tm, tn):
    t, d = x2.shape
    n = w_bf.shape[1]
    per_b = seq // tm
    return pl.pallas_call(
        _inproj_kernel,
        grid=(t // tm, n // tn),
        in_specs=[
            pl.BlockSpec((tm, d), lambda i, j: (i, 0)),
            pl.BlockSpec((1, N_ADA, d), lambda i, j: (i // per_b, 0, 0)),
            pl.BlockSpec((1, d), lambda i, j: (0, 0)),
            pl.BlockSpec((d, tn), lambda i, j: (0, j)),
        ],
        out_specs=pl.BlockSpec((tm, tn), lambda i, j: (i, j)),
        out_shape=jax.ShapeDtypeStruct((t, n), BF16),
        scratch_shapes=[pltpu.VMEM((tm, d), BF16)],
        compiler_params=_cparams(("arbitrary", "arbitrary")),
        name="inproj",
    )(x2, ada3, g.reshape(1, d), w_bf)


CONV_HALO = 32
CONV_ROWS = 64


def _conv_kernel(pv_ref, pg_ref, w_ref, b_ref, g_ref, beta_ref, o_ref, vext, cbuf, *, taps, tl):
    t = pl.program_id(1)
    ch = cbuf.shape[1]

    @pl.when(t == 0)
    def _():
        vext[0:CONV_HALO, :] = jnp.zeros((CONV_HALO, ch), F32)

    @pl.when(t > 0)
    def _():
        vext[0:CONV_HALO, :] = vext[tl:tl + CONV_HALO, :]

    pv = pv_ref[...].astype(F32)
    pg = pg_ref[...].astype(F32)
    vext[CONV_HALO:CONV_HALO + tl, :] = pv * _sigmoid(pg)

    off = CONV_HALO - (taps - 1)
    for cb in range(ch // V7X_LANES):
        lanes = slice(cb * V7X_LANES, (cb + 1) * V7X_LANES)

        for rc in range(tl // CONV_ROWS):
            r0 = rc * CONV_ROWS
            acc = jnp.zeros((CONV_ROWS, V7X_LANES), F32)
            for k in range(taps):
                acc = acc + w_ref[k:k + 1, lanes] * vext[r0 + off + k:r0 + off + k + CONV_ROWS, lanes]
            cbuf[r0:r0 + CONV_ROWS, lanes] = acc + b_ref[:, lanes]

    v = cbuf[...]
    mu = jnp.mean(v, axis=-1, keepdims=True)
    xc = v - mu
    var = jnp.mean(xc * xc, axis=-1, keepdims=True)
    y = xc * lax.rsqrt(var + EPS) * g_ref[...] + beta_ref[...]
    o_ref[...] = (y * _sigmoid(y)).astype(BF16)


def _conv_branch(proj, dw_w, dw_b, cln_g, cln_b, bsz, seq, tl):
    taps, ch = dw_w.shape
    assert taps - 1 <= CONV_HALO and tl % CONV_ROWS == 0 and ch % V7X_LANES == 0
    per_b = seq // tl
    vec = lambda a: a.reshape(1, ch)
    return pl.pallas_call(
        functools.partial(_conv_kernel, taps=taps, tl=tl),
        grid=(bsz, per_b),
        in_specs=[
            pl.BlockSpec((tl, ch), lambda b, t: (b * per_b + t, 0)),
            pl.BlockSpec((tl, ch), lambda b, t: (b * per_b + t, 1)),
            pl.BlockSpec((taps, ch), lambda b, t: (0, 0)),
            pl.BlockSpec((1, ch), lambda b, t: (0, 0)),
            pl.BlockSpec((1, ch), lambda b, t: (0, 0)),
            pl.BlockSpec((1, ch), lambda b, t: (0, 0)),
        ],
        out_specs=pl.BlockSpec((tl, ch), lambda b, t: (b * per_b + t, 0)),
        out_shape=jax.ShapeDtypeStruct((bsz * seq, ch), BF16),
        scratch_shapes=[pltpu.VMEM((CONV_HALO + tl, ch), F32), pltpu.VMEM((tl, ch), F32)],
        compiler_params=_cparams(("arbitrary", "arbitrary")),
        name="conv",
    )(proj, proj, dw_w, vec(dw_b), vec(cln_g), vec(cln_b))


def _s5_matrices(a_re, a_im, log_dt, b_re, b_im, c_re, c_im):
    hp = lax.Precision.HIGHEST
    ng, ns = a_re.shape
    nh = b_re.shape[-1]
    tc = S5_CHUNK
    dt = jnp.exp(log_dt)[:, None]
    mag = jnp.exp(a_re * dt)
    abar_re = mag * jnp.cos(a_im * dt)
    abar_im = mag * jnp.sin(a_im * dt)
    den = a_re * a_re + a_im * a_im
    num_re = abar_re - 1.0
    coef_re = (num_re * a_re + abar_im * a_im) / den
    coef_im = (abar_im * a_re - num_re * a_im) / den
    bb_re = coef_re[:, :, None] * b_re - coef_im[:, :, None] * b_im
    bb_im = coef_re[:, :, None] * b_im + coef_im[:, :, None] * b_re
    p = jnp.arange(tc + 1, dtype=F32)[:, None, None]
    pmag = jnp.exp(p * (a_re * dt)[None])
    pw_re = pmag * jnp.cos(p * (a_im * dt)[None])
    pw_im = pmag * jnp.sin(p * (a_im * dt)[None])
    cp_re = c_re[None] * pw_re[:, :, None, :] - c_im[None] * pw_im[:, :, None, :]
    cp_im = c_re[None] * pw_im[:, :, None, :] + c_im[None] * pw_re[:, :, None, :]
    kern = (jnp.einsum('pgon,gni->pgoi', cp_re[:tc], bb_re, precision=hp)
            - jnp.einsum('pgon,gni->pgoi', cp_im[:tc], bb_im, precision=hp))
    s_idx = jnp.arange(tc)[:, None]
    t_idx = jnp.arange(tc)[None, :]
    tau = jnp.clip(t_idx - s_idx, 0, tc - 1)
    kst = jnp.where((t_idx >= s_idx)[:, :, None, None, None], kern[tau], 0.0)
    m_intra = kst.transpose(2, 0, 4, 1, 3).reshape(ng, tc * nh, tc * nh)
    rp_re = pw_re[:tc][::-1]
    rp_im = pw_im[:tc][::-1]
    ms_re = rp_re[:, :, :, None] * bb_re[None] - rp_im[:, :, :, None] * bb_im[None]
    ms_im = rp_re[:, :, :, None] * bb_im[None] + rp_im[:, :, :, None] * bb_re[None]
    ms_re = ms_re.transpose(1, 0, 3, 2).reshape(ng, tc * nh, ns)
    ms_im = ms_im.transpose(1, 0, 3, 2).reshape(ng, tc * nh, ns)
    mo_re = cp_re[1:].transpose(1, 3, 0, 2).reshape(ng, ns, tc * nh)
    mo_im = (-cp_im[1:]).transpose(1, 3, 0, 2).reshape(ng, ns, tc * nh)
    return m_intra, ms_re, ms_im, mo_re, mo_im, pw_re[tc], pw_im[tc]


def _s5_kernel(u_ref, mi_ref, msr_ref, msi_ref, mor_ref, moi_ref, lr_ref, li_ref, d_ref, o_ref,
               zr_ref, zi_ref, *, gb, ns, bsz, n_chunks):
    for g in range(gb):
        lanes = slice(g * ns, (g + 1) * ns)
        u = u_ref[g]
        zr_ref[:, lanes] = jnp.dot(u, msr_ref[g], preferred_element_type=F32)
        zi_ref[:, lanes] = jnp.dot(u, msi_ref[g], preferred_element_type=F32)

    lam_r = lr_ref[0]
    lam_i = li_ref[0]

    def step(c, carry):
        zr, zi = carry
        rows = pl.ds(pl.multiple_of(c * bsz, bsz), bsz)
        wr = zr_ref[rows, :]
        wi = zi_ref[rows, :]
        zr_ref[rows, :] = zr
        zi_ref[rows, :] = zi
        return (lam_r * zr - lam_i * zi + wr, lam_r * zi + lam_i * zr + wi)

    zero = jnp.zeros((bsz, gb * ns), F32)
    lax.fori_loop(0, n_chunks, step, (zero, zero))

    for g in range(gb):
        lanes = slice(g * ns, (g + 1) * ns)
        u = u_ref[g]
        y = jnp.dot(u, mi_ref[g], preferred_element_type=F32)
        y = y + jnp.dot(zr_ref[:, lanes].astype(BF16), mor_ref[g], preferred_element_type=F32)
        y = y + jnp.dot(zi_ref[:, lanes].astype(BF16), moi_ref[g], preferred_element_type=F32)
        y = y + d_ref[g] * u.astype(F32)
        y = 0.5 * y * (1.0 + jnp.tanh(math.sqrt(2.0 / math.pi) * (y + 0.044715 * (y * y * y))))
        o_ref[g] = y.astype(BF16)


def _s5_branch(u_flat, mats, d_tiled, bsz, gb):
    m_intra, ms_re, ms_im, mo_re, mo_im, lam_re, lam_im = mats
    ng, rows, w = u_flat.shape
    ns = ms_re.shape[-1]
    n_chunks = rows // bsz
    gspec = lambda shape: pl.BlockSpec((gb,) + shape, lambda i: (i, 0, 0))
    lam = lambda a: a.reshape(ng // gb, 1, gb * ns)
    return pl.pallas_call(
        functools.partial(_s5_kernel, gb=gb, ns=ns, bsz=bsz, n_chunks=n_chunks),
        grid=(ng // gb,),
        in_specs=[
            gspec((rows, w)),
            gspec((w, w)),
            gspec((w, ns)), gspec((w, ns)),
            gspec((ns, w)), gspec((ns, w)),
            pl.BlockSpec((1, 1, gb * ns), lambda i: (i, 0, 0)),
            pl.BlockSpec((1, 1, gb * ns), lambda i: (i, 0, 0)),
            gspec((1, w)),
        ],
        out_specs=gspec((rows, w)),
        out_shape=jax.ShapeDtypeStruct((ng, rows, w), BF16),
        scratch_shapes=[pltpu.VMEM((rows, gb * ns), F32), pltpu.VMEM((rows, gb * ns), F32)],
        compiler_params=_cparams(("arbitrary",)),
        name="s5",
    )(u_flat, m_intra.astype(BF16), ms_re.astype(BF16), ms_im.astype(BF16),
      mo_re.astype(BF16), mo_im.astype(BF16), lam(lam_re), lam(lam_im), d_tiled)


def _merge_kernel(vc_ref, ys_ref, l1_ref, l2_ref, cpw_ref, cpb_ref, wa_ref, wb_ref, gb1_ref, gb2_ref, o_ref):
    y_conv = jnp.dot(vc_ref[...], cpw_ref[...], preferred_element_type=F32) + cpb_ref[...]
    ys = ys_ref[...]
    a = jnp.dot(ys, wa_ref[...], preferred_element_type=F32)
    b = jnp.dot(ys, wb_ref[...], preferred_element_type=F32)
    y_s5 = a * _sigmoid(b)
    g1 = _sigmoid(l1_ref[...].astype(F32) + gb1_ref[...])
    g2 = _sigmoid(l2_ref[...].astype(F32) + gb2_ref[...])
    o_ref[...] = (g1 * y_conv + g2 * y_s5).astype(BF16)


def _merge(vc, ys, proj, cp_w, cp_b, wa, wb, gate_b, tm, col0):
    t, ch = vc.shape
    d = cp_w.shape[1]
    tn = ch
    assert col0 % tn == 0 and d % tn == 0
    cb, nj = col0 // tn, d // tn
    wcol = lambda rows: pl.BlockSpec((rows, tn), lambda j, i: (0, j))
    return pl.pallas_call(
        _merge_kernel,
        grid=(nj, t // tm),
        in_specs=[
            pl.BlockSpec((tm, ch), lambda j, i: (i, 0)),
            pl.BlockSpec((tm, ch), lambda j, i: (i, 0)),
            pl.BlockSpec((tm, tn), lambda j, i: (i, cb + j)),
            pl.BlockSpec((tm, tn), lambda j, i: (i, cb + nj + j)),
            wcol(ch), wcol(1), wcol(ch), wcol(ch), wcol(1), wcol(1),
        ],
        out_specs=pl.BlockSpec((tm, tn), lambda j, i: (i, j)),
        out_shape=jax.ShapeDtypeStruct((t, d), BF16),
        compiler_params=_cparams(("arbitrary", "arbitrary")),
        name="merge",
    )(vc, ys, proj, proj, cp_w.astype(BF16), cp_b.reshape(1, d), wa.astype(BF16), wb.astype(BF16),
      gate_b[:d].reshape(1, d), gate_b[d:].reshape(1, d))


def _split_bf16(v):
    hi = v.astype(BF16)
    lo = (v - hi.astype(F32)).astype(BF16)
    return hi, lo


def _bf16_bits_hi(v):
    b = lax.bitcast_convert_type(v, jnp.uint32)
    return (b + jnp.uint32(0x7FFF) + ((b >> 16) & jnp.uint32(1))) & jnp.uint32(0xFFFF0000)


def _pack_bf16_pairs(v):
    half = v.shape[-1] // 2
    return _bf16_bits_hi(v[:, half:]) | (_bf16_bits_hi(v[:, :half]) >> 16)


def _unpack_bf16_pairs(p):
    lo = lax.bitcast_convert_type(p << 16, F32).astype(BF16)
    hi = lax.bitcast_convert_type(p & jnp.uint32(0xFFFF0000), F32).astype(BF16)
    return lo, hi


def _outproj_kernel(m_ref, x_ref, ada_ref, wo_ref, g1_ref, g2_ref, rwh_ref, rwl_ref, rb_ref,
                    x1_ref, h2p_ref, idx_ref, gate_ref, *, ne):
    m = jnp.dot(m_ref[...], wo_ref[...], preferred_element_type=F32)
    x1 = x_ref[...] + ada_ref[0, 2:3, :] * _rms(m, g1_ref[...])
    x1_ref[...] = x1
    h2 = _rms(x1, g2_ref[...]) * (1.0 + ada_ref[0, 4:5, :]) + ada_ref[0, 3:4, :]
    h2p_ref[...] = _pack_bf16_pairs(h2)
    hh, hl = _split_bf16(h2)
    lg = jnp.dot(hh, rwh_ref[...], preferred_element_type=F32)
    lg = lg + jnp.dot(hl, rwh_ref[...], preferred_element_type=F32)
    lg = lg + jnp.dot(hh, rwl_ref[...], preferred_element_type=F32)
    lg = lg + rb_ref[...]
    lane = lax.broadcasted_iota(jnp.int32, lg.shape, 1)
    work = jnp.where(lane < ne, lg, -jnp.inf)
    vals, idxs = [], []
    for _ in range(TOP_K):
        mx = jnp.max(work, axis=-1, keepdims=True)
        ix = jnp.min(jnp.where(work == mx, lane, lg.shape[1]), axis=-1, keepdims=True)
        vals.append(mx)
        idxs.append(ix)
        work = jnp.where(lane == ix, -jnp.inf, work)
    ex = [jnp.exp(v - vals[0]) for v in vals]
    den = ex[0]
    for e in ex[1:]:
        den = den + e
    idx_out = jnp.zeros(lg.shape, jnp.int32)
    gate_out = jnp.zeros(lg.shape, F32)
    for k in range(TOP_K):
        idx_out = jnp.where(lane == k, idxs[k], idx_out)
        gate_out = jnp.where(lane == k, ex[k] / den, gate_out)
    idx_ref[...] = idx_out
    gate_ref[...] = gate_out


def _outproj(merged, x2, ada3, w_out, post_mix_g, pre_ffn_g, router_w, router_b, seq, tm):
    t, d = x2.shape
    ne = router_w.shape[1]
    nep = max(V7X_LANES, ne)
    rw = jnp.zeros((d, nep), F32).at[:, :ne].set(router_w)
    rw_hi, rw_lo = _split_bf16(rw)
    rb = jnp.zeros((1, nep), F32).at[0, :ne].set(router_b)
    per_b = seq // tm
    full = lambda shape: pl.BlockSpec(shape, lambda i: (0, 0))
    row = lambda w: pl.BlockSpec((tm, w), lambda i: (i, 0))
    x1, h2p, top_idx, gates = pl.pallas_call(
        functools.partial(_outproj_kernel, ne=ne),
        grid=(t // tm,),
        in_specs=[
            row(d), row(d),
            pl.BlockSpec((1, N_ADA, d), lambda i: (i // per_b, 0, 0)),
            full((d, d)), full((1, d)), full((1, d)), full((d, nep)), full((d, nep)), full((1, nep)),
        ],
        out_specs=[row(d), row(d // 2), row(nep), row(nep)],
        out_shape=[jax.ShapeDtypeStruct((t, d), F32), jax.ShapeDtypeStruct((t, d // 2), jnp.uint32),
                   jax.ShapeDtypeStruct((t, nep), jnp.int32), jax.ShapeDtypeStruct((t, nep), F32)],
        compiler_params=_cparams(("arbitrary",)),
        name="outproj",
    )(merged, x2, ada3, w_out.astype(BF16), post_mix_g.reshape(1, d), pre_ffn_g.reshape(1, d),
      rw_hi, rw_lo, rb)
    return x1, h2p, top_idx[:, :TOP_K], gates[:, :TOP_K]


GATHER_UNROLL = 8


def _row_copy(idx_ref, src_hbm, buf, sems, slot, r):
    return pltpu.make_async_copy(src_hbm.at[pl.ds(idx_ref[0, 0, r], 1), :],
                                 buf.at[slot, pl.ds(r, 1), :], sems.at[slot])


def _start_rows(idx_ref, src_hbm, buf, sems, slot, n):
    def start(r, carry):
        _row_copy(idx_ref, src_hbm, buf, sems, slot, r).start()
        return carry
    lax.fori_loop(0, n, start, 0, unroll=GATHER_UNROLL)


def _wait_rows(idx_ref, src_hbm, buf, sems, slot, n):
    def wait(r, carry):
        _row_copy(idx_ref, src_hbm, buf, sems, slot, r).wait()
        return carry
    lax.fori_loop(0, n, wait, 0, unroll=GATHER_UNROLL)


def _gather_kernel(tok_ref, h_hbm, o_ref, buf, sems):
    i = pl.program_id(0)
    n_blocks = pl.num_programs(0) - 1

    @pl.when(i < n_blocks)
    def _():
        _start_rows(tok_ref, h_hbm, buf, sems, i % 2, MOE_BLOCK)

    @pl.when(i > 0)
    def _():
        slot = (i - 1) % 2
        _wait_rows(tok_ref, h_hbm, buf, sems, slot, MOE_BLOCK)
        o_ref[...] = buf[slot]


def _moe_gather(tok_buf, h2p, n_blocks):
    t, w = h2p.shape
    return pl.pallas_call(
        _gather_kernel,
        grid=(n_blocks + 1,),
        in_specs=[
            pl.BlockSpec((1, 1, MOE_BLOCK), lambda i: (jnp.minimum(i, n_blocks - 1), 0, 0),
                         memory_space=pltpu.SMEM),
            pl.BlockSpec(memory_space=pl.ANY),
        ],
        out_specs=pl.BlockSpec((MOE_BLOCK, w), lambda i: (jnp.maximum(i - 1, 0), 0)),
        out_shape=jax.ShapeDtypeStruct((n_blocks * MOE_BLOCK, w), h2p.dtype),
        scratch_shapes=[pltpu.VMEM((2, MOE_BLOCK, w), h2p.dtype), pltpu.SemaphoreType.DMA((2,))],
        compiler_params=_cparams(("arbitrary",)),
        name="moe_gather",
    )(tok_buf.reshape(n_blocks, 1, MOE_BLOCK), h2p)


def _expert_changed(e_ref, i):
    return jnp.logical_or(i == 0, e_ref[i] != e_ref[jnp.maximum(i - 1, 0)])


def _up_kernel(e_ref, nv_ref, x_ref, w_ref, bg_ref, bl_ref, p_ref, o_ref, wp_ref):
    i = pl.program_id(1)
    tn = wp_ref.shape[1]
    half = V7X_MXU // 2
    hd = x_ref.shape[1]

    @pl.when(jnp.logical_and(_expert_changed(e_ref, i), nv_ref[i] > 0))
    def _():
        for cb in range(tn // V7X_MXU):
            cols = slice(cb * V7X_MXU, (cb + 1) * V7X_MXU)
            wp_ref[:, cols] = jnp.dot(w_ref[0, :, cols].astype(BF16), p_ref[...],
                                      preferred_element_type=F32).astype(BF16)

    @pl.when(nv_ref[i] > 0)
    def _():
        x_lo, x_hi = _unpack_bf16_pairs(x_ref[...])
        for cb in range(tn // V7X_MXU):
            cols = slice(cb * V7X_MXU, (cb + 1) * V7X_MXU)
            hb = (jnp.dot(x_lo, wp_ref[0:hd, cols], preferred_element_type=F32)
                  + jnp.dot(x_hi, wp_ref[hd:2 * hd, cols], preferred_element_type=F32))
            out_cols = slice(cb * half, (cb + 1) * half)
            x_glu = hb[:, :half] + bg_ref[0, :, out_cols]
            x_lin = hb[:, half:] + bl_ref[0, :, out_cols]
            x_glu = jnp.minimum(x_glu, SWIGLU_LIMIT)
            x_lin = jnp.clip(x_lin, -SWIGLU_LIMIT, SWIGLU_LIMIT)
            act = x_glu * _sigmoid(SWIGLU_ALPHA * x_glu) * (x_lin + 1.0)
            o_ref[:, out_cols] = act.astype(BF16)

    @pl.when(nv_ref[i] == 0)
    def _():
        o_ref[...] = jnp.zeros(o_ref.shape, o_ref.dtype)


def _moe_up(blk_e, nvalid, xs, w1, b1, tn):
    n_rows, hd = xs.shape
    ne, d, f2 = w1.shape
    assert d == 2 * hd
    f = f2 // 2
    n_blocks = n_rows // MOE_BLOCK
    half = V7X_MXU // 2
    c = jnp.arange(V7X_MXU)
    perm = jnp.zeros((V7X_MXU, V7X_MXU), BF16).at[c, (c % 2) * half + c // 2].set(1.0)
    b1g = b1[:, 0::2].reshape(ne, 1, f)
    b1l = b1[:, 1::2].reshape(ne, 1, f)
    return pl.pallas_call(
        _up_kernel,
        grid_spec=pltpu.PrefetchScalarGridSpec(
            num_scalar_prefetch=2,
            grid=(f2 // tn, n_blocks),
            in_specs=[
                pl.BlockSpec((MOE_BLOCK, hd), lambda j, i, e, nv: (i, 0)),
                pl.BlockSpec((1, d, tn), lambda j, i, e, nv: (e[i], 0, j)),
                pl.BlockSpec((1, 1, tn // 2), lambda j, i, e, nv: (e[i], 0, j)),
                pl.BlockSpec((1, 1, tn // 2), lambda j, i, e, nv: (e[i], 0, j)),
                pl.BlockSpec((V7X_MXU, V7X_MXU), lambda j, i, e, nv: (0, 0)),
            ],
            out_specs=pl.BlockSpec((MOE_BLOCK, tn // 2), lambda j, i, e, nv: (i, j)),
            scratch_shapes=[pltpu.VMEM((d, tn), BF16)],
        ),
        out_shape=jax.ShapeDtypeStruct((n_rows, f), BF16),
        compiler_params=_cparams(("arbitrary", "arbitrary")),
        name="moe_up",
    )(blk_e, nvalid, xs, w1, b1g, b1l, perm)


def _down_kernel(e_ref, nv_ref, a_ref, w_ref, b_ref, g_ref, o_ref, wb_ref):
    i = pl.program_id(0)

    @pl.when(nv_ref[i] > 0)
    def _():
        @pl.when(_expert_changed(e_ref, i))
        def _():
            wb_ref[...] = w_ref[0].astype(BF16)

        y = jnp.dot(a_ref[...], wb_ref[...], preferred_element_type=F32) + b_ref[0]
        o_ref[...] = y * g_ref[...]

    @pl.when(nv_ref[i] == 0)
    def _():
        o_ref[...] = jnp.zeros(o_ref.shape, o_ref.dtype)


def _moe_down(blk_e, nvalid, act, w2, b2, g_buf):
    n_rows, f = act.shape
    ne, _, d = w2.shape
    n_blocks = n_rows // MOE_BLOCK
    return pl.pallas_call(
        _down_kernel,
        grid_spec=pltpu.PrefetchScalarGridSpec(
            num_scalar_prefetch=2,
            grid=(n_blocks,),
            in_specs=[
                pl.BlockSpec((MOE_BLOCK, f), lambda i, e, nv: (i, 0)),
                pl.BlockSpec((1, f, d), lambda i, e, nv: (e[i], 0, 0)),
                pl.BlockSpec((1, 1, d), lambda i, e, nv: (e[i], 0, 0)),
                pl.BlockSpec((MOE_BLOCK, 1), lambda i, e, nv: (i, 0)),
            ],
            out_specs=pl.BlockSpec((MOE_BLOCK, d), lambda i, e, nv: (i, 0)),
            scratch_shapes=[pltpu.VMEM((f, d), BF16)],
        ),
        out_shape=jax.ShapeDtypeStruct((n_rows, d), F32),
        compiler_params=_cparams(("arbitrary",)),
        name="moe_down",
    )(blk_e, nvalid, act, w2, b2.reshape(ne, 1, d), g_buf.reshape(n_rows, 1))


def _combine_kernel(pos_ref, y_hbm, x1_ref, ada_ref, g_ref, o_ref, buf, sems, *, tt):
    i = pl.program_id(0)
    n_tiles = pl.num_programs(0) - 1

    @pl.when(i < n_tiles)
    def _():
        _start_rows(pos_ref, y_hbm, buf, sems, i % 2, TOP_K * tt)

    @pl.when(i > 0)
    def _():
        slot = (i - 1) % 2
        _wait_rows(pos_ref, y_hbm, buf, sems, slot, TOP_K * tt)
        f = buf[slot, 0:tt, :]
        for k in range(1, TOP_K):
            f = f + buf[slot, k * tt:(k + 1) * tt, :]
        o_ref[...] = x1_ref[...] + ada_ref[0, 5:6, :] * _rms(f, g_ref[...])


def _moe_combine(pos, y_buf, x1, ada3, post_ffn_g, seq, tt):
    t, d = x1.shape
    per_b = seq // tt
    nt = t // tt
    pos_t = pos.reshape(nt, tt, TOP_K).transpose(0, 2, 1).reshape(nt, 1, TOP_K * tt)
    prev = lambda i: jnp.maximum(i - 1, 0)
    return pl.pallas_call(
        functools.partial(_combine_kernel, tt=tt),
        grid=(nt + 1,),
        in_specs=[
            pl.BlockSpec((1, 1, TOP_K * tt), lambda i: (jnp.minimum(i, nt - 1), 0, 0), memory_space=pltpu.SMEM),
            pl.BlockSpec(memory_space=pl.ANY),
            pl.BlockSpec((tt, d), lambda i: (prev(i), 0)),
            pl.BlockSpec((1, N_ADA, d), lambda i: (prev(i) // per_b, 0, 0)),
            pl.BlockSpec((1, d), lambda i: (0, 0)),
        ],
        out_specs=pl.BlockSpec((tt, d), lambda i: (prev(i), 0)),
        out_shape=jax.ShapeDtypeStruct((t, d), F32),
        scratch_shapes=[pltpu.VMEM((2, TOP_K * tt, d), F32), pltpu.SemaphoreType.DMA((2,))],
        compiler_params=_cparams(("arbitrary",)),
        name="moe_combine",
    )(pos_t, y_buf, x1, ada3, post_ffn_g.reshape(1, d))


def _route(top_idx, gates, n_experts):
    n_tok = top_idx.shape[0]
    n_asg = n_tok * TOP_K
    i32 = jnp.int32
    flat_e = top_idx.reshape(n_asg)
    order = jnp.argsort(flat_e).astype(i32)
    counts = jnp.sum(flat_e[:, None] == jnp.arange(n_experts, dtype=i32)[None, :], axis=0, dtype=i32)
    starts = jnp.cumsum(counts) - counts
    padded = (counts + MOE_BLOCK - 1) // MOE_BLOCK * MOE_BLOCK
    pends = jnp.cumsum(padded)
    pstarts = pends - padded
    n_blocks = -(-n_asg // MOE_BLOCK) + n_experts
    blk_start = jnp.arange(n_blocks, dtype=i32) * MOE_BLOCK
    blk_e = jnp.minimum(jnp.sum(pends[None, :] <= blk_start[:, None], axis=1, dtype=i32), n_experts - 1)
    off = blk_start - pstarts[blk_e]
    nvalid = jnp.clip(counts[blk_e] - off, 0, MOE_BLOCK)
    j = off[:, None] + jnp.arange(MOE_BLOCK, dtype=i32)[None, :]
    valid = j < counts[blk_e][:, None]
    src = jnp.clip(starts[blk_e][:, None] + j, 0, n_asg - 1)
    asg = order[src]
    tok_buf = jnp.where(valid, asg // TOP_K, 0)
    g_buf = jnp.where(valid, gates.reshape(n_asg)[asg], 0.0)
    rank = jnp.argsort(order).astype(i32)
    pos = pstarts[flat_e] + rank - starts[flat_e]
    return tok_buf, g_buf, pos, blk_e, nvalid, n_blocks


def _pick(n, pref):
    return pref if n % pref == 0 else n


def kernel(x, c, ada_w, ada_b, pre_mix_g, post_mix_g, pre_ffn_g, post_ffn_g, w_in, gate_b, dw_w, dw_b, cln_g, cln_b, cp_w, cp_b, s5_a_re, s5_a_im, s5_log_dt, s5_b_re, s5_b_im, s5_c_re, s5_c_im, s5_d, glu_wa, glu_wb, w_out, router_w, router_b, w1, b1, w2, b2):
    bsz, seq, d = x.shape
    t = bsz * seq
    depth = ada_w.shape[0]
    conv_ch = dw_w.shape[-1]
    ng, ns, nh = s5_b_re.shape[1:]
    s5_w = ng * nh
    ne = router_w.shape[-1]
    col_s5 = 2 * conv_ch
    col_gate = col_s5 + s5_w
    assert conv_ch == s5_w and seq % S5_CHUNK == 0
    tm = _pick(seq, 512)
    tm_out = _pick(seq, 256)
    tn_in = _pick(w_in.shape[-1], conv_ch)
    n_chunks = seq // S5_CHUNK
    cw = S5_CHUNK * nh
    bslots = -(-bsz // V7X_SUBLANES) * V7X_SUBLANES

    x2 = x.reshape(t, d)
    for l in range(depth):
        ada3 = _ada(c, ada_w[l], ada_b[l]).reshape(bsz, N_ADA, d)
        proj = _inproj(x2, ada3, pre_mix_g[l], w_in[l].astype(BF16), seq, tm, tn_in)
        vc = _conv_branch(proj, dw_w[l], dw_b[l], cln_g[l], cln_b[l], bsz, seq, tm)

        mats = _s5_matrices(s5_a_re[l], s5_a_im[l], s5_log_dt[l], s5_b_re[l], s5_b_im[l],
                            s5_c_re[l], s5_c_im[l])
        u = proj[:, col_s5:col_gate].reshape(bsz, n_chunks, S5_CHUNK, ng, nh)
        u_flat = jnp.pad(u.transpose(3, 1, 0, 2, 4), ((0, 0), (0, 0), (0, bslots - bsz), (0, 0), (0, 0)))
        u_flat = u_flat.reshape(ng, n_chunks * bslots, cw)
        d_tiled = jnp.tile(s5_d[l].reshape(ng, 1, nh), (1, S5_CHUNK, 1)).reshape(ng, 1, cw)
        ys = _s5_branch(u_flat, mats, d_tiled, bslots, _pick(ng, 8))
        ys = ys.reshape(ng, n_chunks, bslots, S5_CHUNK, nh)[:, :, :bsz]
        ys = ys.transpose(2, 1, 3, 0, 4).reshape(t, s5_w)

        merged = _merge(vc, ys, proj, cp_w[l], cp_b[l], glu_wa[l], glu_wb[l], gate_b[l], tm, col_gate)
        x1, h2p, top_idx, gates = _outproj(merged, x2, ada3, w_out[l], post_mix_g[l], pre_ffn_g[l],
                                           router_w[l], router_b[l], seq, tm_out)

        tok_buf, g_buf, pos, blk_e, nvalid, n_blocks = _route(top_idx, gates, ne)
        xs = _moe_gather(tok_buf, h2p, n_blocks)
        act = _moe_up(blk_e, nvalid, xs, w1[l], b1[l], _pick(w1.shape[-1], 1024))
        y_buf = _moe_down(blk_e, nvalid, act, w2[l], b2[l], g_buf)
        x2 = _moe_combine(pos, y_buf, x1, ada3, post_ffn_g[l], seq, _pick(seq, 64))
    return x2.reshape(bsz, seq, d)
```

```python
import functools
import math

import jax
import jax.numpy as jnp
from jax import lax
from jax.experimental import pallas as pl
from jax.experimental.pallas import tpu as pltpu

EPS = 1e-6
N_ADA = 6
TOP_K = 4
MOE_BLOCK = 256
SWIGLU_ALPHA = 1.702
SWIGLU_LIMIT = 7.0
S5_CHUNK = 16
V7X_LANES = 128
V7X_SUBLANES = 8
V7X_MXU = 256
VMEM_LIMIT = 56 * 1024 * 1024

F32 = jnp.float32
BF16 = jnp.bfloat16


def _cparams(sem):
    return pltpu.CompilerParams(dimension_semantics=sem, vmem_limit_bytes=VMEM_LIMIT)


def _sigmoid(v):
    return 1.0 / (1.0 + jnp.exp(-v))


def _rms(v, g):
    return v * lax.rsqrt(jnp.mean(v * v, axis=-1, keepdims=True) + EPS) * g


def _ada_kernel(c_ref, w_ref, b_ref, o_ref):
    c = c_ref[...]
    s = (c * _sigmoid(c)).astype(BF16)
    o_ref[...] = jnp.dot(s, w_ref[...].astype(BF16), preferred_element_type=F32) + b_ref[...]


def _ada(c, ada_w, ada_b):
    bsz, d = c.shape
    n = ada_w.shape[1]
    tn = 1024 if n % 1024 == 0 else n
    return pl.pallas_call(
        _ada_kernel,
        grid=(n // tn,),
        in_specs=[
            pl.BlockSpec((bsz, d), lambda j: (0, 0)),
            pl.BlockSpec((d, tn), lambda j: (0, j)),
            pl.BlockSpec((1, tn), lambda j: (0, j)),
        ],
        out_specs=pl.BlockSpec((bsz, tn), lambda j: (0, j)),
        out_shape=jax.ShapeDtypeStruct((bsz, n), F32),
        compiler_params=_cparams(("arbitrary",)),
        name="ada",
    )(c, ada_w, ada_b.reshape(1, n))


def _inproj_kernel(x_ref, ada_ref, g_ref, w_ref, o_ref, h_ref):
    @pl.when(pl.program_id(1) == 0)
    def _():
        y = _rms(x_ref[...], g_ref[...])
        h = y * (1.0 + ada_ref[0, 1:2, :]) + ada_ref[0, 0:1, :]
        h_ref[...] = h.astype(BF16)

    o_ref[...] = jnp.dot(h_ref[...], w_ref[...], preferred_element_type=F32).astype(BF16)


def _inproj(x2, ada3, g, w_bf, seq, tm, tn):
    t, d = x2.shape
    n = w_bf.shape[1]
    per_b = seq // tm
    return pl.pallas_call(
        _inproj_kernel,
        grid=(t // tm, n // tn),
        in_specs=[
            pl.BlockSpec((tm, d), lambda i, j: (i, 0)),
            pl.BlockSpec((1, N_ADA, d), lambda i, j: (i // per_b, 0, 0)),
            pl.BlockSpec((1, d), lambda i, j: (0, 0)),
            pl.BlockSpec((d, tn), lambda i, j: (0, j)),
        ],
        out_specs=pl.BlockSpec((tm, tn), lambda i, j: (i, j)),
        out_shape=jax.ShapeDtypeStruct((t, n), BF16),
        scratch_shapes=[pltpu.VMEM((tm, d), BF16)],
        compiler_params=_cparams(("arbitrary", "arbitrary")),
        name="inproj",
    )(x2, ada3, g.reshape(1, d), w_bf)


CONV_HALO = 32
CONV_ROWS = 64


def _conv_kernel(pv_ref, pg_ref, w_ref, b_ref, g_ref, beta_ref, o_ref, vext, cbuf, *, taps, tl):
    t = pl.program_id(1)
    ch = cbuf.shape[1]

    @pl.when(t == 0)
    def _():
        vext[0:CONV_HALO, :] = jnp.zeros((CONV_HALO, ch), F32)

    @pl.when(t > 0)
    def _():
        vext[0:CONV_HALO, :] = vext[tl:tl + CONV_HALO, :]

    pv = pv_ref[...].astype(F32)
    pg = pg_ref[...].astype(F32)
    vext[CONV_HALO:CONV_HALO + tl, :] = pv * _sigmoid(pg)

    off = CONV_HALO - (taps - 1)
    for cb in range(ch // V7X_LANES):
        lanes = slice(cb * V7X_LANES, (cb + 1) * V7X_LANES)

        for rc in range(tl // CONV_ROWS):
            r0 = rc * CONV_ROWS
            acc = jnp.zeros((CONV_ROWS, V7X_LANES), F32)
            for k in range(taps):
                acc = acc + w_ref[k:k + 1, lanes] * vext[r0 + off + k:r0 + off + k + CONV_ROWS, lanes]
            cbuf[r0:r0 + CONV_ROWS, lanes] = acc + b_ref[:, lanes]

    v = cbuf[...]
    mu = jnp.mean(v, axis=-1, keepdims=True)
    xc = v - mu
    var = jnp.mean(xc * xc, axis=-1, keepdims=True)
    y = xc * lax.rsqrt(var + EPS) * g_ref[...] + beta_ref[...]
    o_ref[...] = (y * _sigmoid(y)).astype(BF16)


def _conv_branch(proj, dw_w, dw_b, cln_g, cln_b, bsz, seq, tl):
    taps, ch = dw_w.shape
    assert taps - 1 <= CONV_HALO and tl % CONV_ROWS == 0 and ch % V7X_LANES == 0
    per_b = seq // tl
    vec = lambda a: a.reshape(1, ch)
    return pl.pallas_call(
        functools.partial(_conv_kernel, taps=taps, tl=tl),
        grid=(bsz, per_b),
        in_specs=[
            pl.BlockSpec((tl, ch), lambda b, t: (b * per_b + t, 0)),
            pl.BlockSpec((tl, ch), lambda b, t: (b * per_b + t, 1)),
            pl.BlockSpec((taps, ch), lambda b, t: (0, 0)),
            pl.BlockSpec((1, ch), lambda b, t: (0, 0)),
            pl.BlockSpec((1, ch), lambda b, t: (0, 0)),
            pl.BlockSpec((1, ch), lambda b, t: (0, 0)),
        ],
        out_specs=pl.BlockSpec((tl, ch), lambda b, t: (b * per_b + t, 0)),
        out_shape=jax.ShapeDtypeStruct((bsz * seq, ch), BF16),
        scratch_shapes=[pltpu.VMEM((CONV_HALO + tl, ch), F32), pltpu.VMEM((tl, ch), F32)],
        compiler_params=_cparams(("arbitrary", "arbitrary")),
        name="conv",
    )(proj, proj, dw_w, vec(dw_b), vec(cln_g), vec(cln_b))


S5_SUPER = 16


def _s5_params(a_re, a_im, log_dt, b_re, b_im, c_re, c_im, d_skip):
    ng, ns = a_re.shape
    nh = b_re.shape[-1]
    gt = V7X_MXU // nh
    nj = ng // gt
    dt = jnp.exp(log_dt)[:, None]
    mag = jnp.exp(a_re * dt)
    abar_re = mag * jnp.cos(a_im * dt)
    abar_im = mag * jnp.sin(a_im * dt)
    den = a_re * a_re + a_im * a_im
    num_re = abar_re - 1.0
    coef_re = (num_re * a_re + abar_im * a_im) / den
    coef_im = (abar_im * a_re - num_re * a_im) / den
    bb_re = coef_re[:, :, None] * b_re - coef_im[:, :, None] * b_im
    bb_im = coef_re[:, :, None] * b_im + coef_im[:, :, None] * b_re
    p = jnp.asarray([1.0, S5_CHUNK, S5_CHUNK * S5_SUPER] + [S5_CHUNK * r for r in range(S5_SUPER)], F32)
    pmag = jnp.exp(p[:, None, None] * (a_re * dt)[None])
    pw_re = pmag * jnp.cos(p[:, None, None] * (a_im * dt)[None])
    pw_im = pmag * jnp.sin(p[:, None, None] * (a_im * dt)[None])
    tile = lambda a: a.reshape(-1, nj, gt * ns).transpose(1, 0, 2)
    apow = jnp.concatenate([tile(pw_re), tile(pw_im)], axis=1)
    eye = jnp.eye(gt, dtype=F32)

    def bdiag_in(bb):
        return jnp.einsum('jgnh,gk->jghkn', bb.reshape(nj, gt, ns, nh), eye).reshape(nj, gt * nh, gt * ns)

    def bdiag_out(cc):
        return jnp.einsum('jghn,gk->jgnkh', cc.reshape(nj, gt, nh, ns), eye).reshape(nj, gt * ns, gt * nh)

    bdb = jnp.stack([bdiag_in(bb_re), bdiag_in(bb_im)], axis=1)
    bdc = jnp.stack([bdiag_out(c_re), bdiag_out(-c_im)], axis=1)
    return bdb.astype(BF16), bdc.astype(BF16), apow, d_skip.reshape(nj, 1, gt * nh)


def _cmul_add(ar, ai, xr, xi, vr, vi):
    return ar * xr - ai * xi + vr, ar * xi + ai * xr + vi


def _s5_kernel(u_ref, bdb_ref, bdc_ref, ap_ref, d_ref, p1_ref, p1t_ref, p2_ref, p2t_ref, o_ref,
               u2n_ref, u2_ref, hr_ref, hi_ref, zr_ref, zi_ref, gr_ref, gi_ref, y2_ref, y2n_ref,
               *, n_super):
    tc, nr, nm = S5_CHUNK, S5_SUPER, n_super
    tile = tc * nr
    np_ = 3 + nr
    apow = lambda k: (ap_ref[0, k:k + 1, :], ap_ref[0, np_ + k:np_ + k + 1, :])
    ar, ai = apow(0)

    for m in range(nm):
        pu = jnp.dot(p1_ref[...], u_ref[m * tile:(m + 1) * tile, :], preferred_element_type=F32).astype(BF16)
        for s in range(tc):
            u2n_ref[s, m * nr:(m + 1) * nr, :] = pu[s * nr:(s + 1) * nr, :]
    for s in range(tc):
        u2_ref[s] = jnp.dot(p2_ref[...], u2n_ref[s], preferred_element_type=F32).astype(BF16)

    def drive(s):
        u = u2_ref[s]
        return (jnp.dot(u, bdb_ref[0, 0], preferred_element_type=F32),
                jnp.dot(u, bdb_ref[0, 1], preferred_element_type=F32))

    hr_ref[...] = jnp.zeros(hr_ref.shape, F32)
    hi_ref[...] = jnp.zeros(hi_ref.shape, F32)

    def pass1(s, carry):
        vr, vi = drive(s)
        hr, hi = _cmul_add(ar, ai, hr_ref[...], hi_ref[...], vr, vi)
        hr_ref[...] = hr
        hi_ref[...] = hi
        return carry

    lax.fori_loop(0, tc, pass1, 0)

    a16r, a16i = apow(1)
    qr = jnp.zeros((nm, hr_ref.shape[1]), F32)
    qi = qr
    for r in range(nr):
        rows = slice(r * nm, (r + 1) * nm)
        zr_ref[rows, :] = qr
        zi_ref[rows, :] = qi
        qr, qi = _cmul_add(a16r, a16i, qr, qi, hr_ref[rows, :], hi_ref[rows, :])
    a256r, a256i = apow(2)
    gr = jnp.zeros((1, hr_ref.shape[1]), F32)
    gi = gr
    for m in range(nm):
        gr_ref[m:m + 1, :] = gr
        gi_ref[m:m + 1, :] = gi
        gr, gi = _cmul_add(a256r, a256i, gr, gi, qr[m:m + 1, :], qi[m:m + 1, :])
    gpr = gr_ref[...]
    gpi = gi_ref[...]
    for r in range(nr):
        rows = slice(r * nm, (r + 1) * nm)
        pr, pi = apow(3 + r)
        zr, zi = _cmul_add(pr, pi, gpr, gpi, zr_ref[rows, :], zi_ref[rows, :])
        zr_ref[rows, :] = zr
        zi_ref[rows, :] = zi

    def pass2(t, carry):
        vr, vi = drive(t)
        sr, si = _cmul_add(ar, ai, zr_ref[...], zi_ref[...], vr, vi)
        zr_ref[...] = sr
        zi_ref[...] = si
        y = (jnp.dot(sr.astype(BF16), bdc_ref[0, 0], preferred_element_type=F32)
             + jnp.dot(si.astype(BF16), bdc_ref[0, 1], preferred_element_type=F32))
        y = y + d_ref[0] * u2_ref[t].astype(F32)
        y = 0.5 * y * (1.0 + jnp.tanh(math.sqrt(2.0 / math.pi) * (y + 0.044715 * (y * y * y))))
        y2_ref[t] = y.astype(BF16)
        return carry

    lax.fori_loop(0, tc, pass2, 0)

    for t in range(tc):
        yn = jnp.dot(p2t_ref[...], y2_ref[t], preferred_element_type=F32).astype(BF16)
        for m in range(nm):
            y2n_ref[m * tile + t * nr:m * tile + (t + 1) * nr, :] = yn[m * nr:(m + 1) * nr, :]
    for m in range(nm):
        rows = slice(m * tile, (m + 1) * tile)
        o_ref[rows, :] = jnp.dot(p1t_ref[...], y2n_ref[rows, :], preferred_element_type=F32).astype(BF16)


def _s5_branch(proj, col0, params, bsz, seq):
    bdb, bdc, apow, dskip = params
    nj, _, w, sl = bdb.shape
    tile = S5_CHUNK * S5_SUPER
    assert seq % tile == 0 and col0 % w == 0
    nm = seq // tile
    assert nm % V7X_SUBLANES == 0, "row slabs of the chunk recurrence must be whole sublane tiles"
    rows = S5_SUPER * nm
    i1 = jnp.arange(tile)
    p1 = jnp.zeros((tile, tile), BF16).at[(i1 % S5_CHUNK) * S5_SUPER + i1 // S5_CHUNK, i1].set(1.0)
    i2 = jnp.arange(rows)
    p2 = jnp.zeros((rows, rows), BF16).at[(i2 % S5_SUPER) * nm + i2 // S5_SUPER, i2].set(1.0)
    const = lambda a: pl.BlockSpec(a.shape, lambda b, j: (0,) * a.ndim)
    per_j = lambda a: pl.BlockSpec((1,) + a.shape[1:], lambda b, j: (j,) + (0,) * (a.ndim - 1))
    return pl.pallas_call(
        functools.partial(_s5_kernel, n_super=nm),
        grid=(bsz, nj),
        in_specs=[
            pl.BlockSpec((seq, w), lambda b, j: (b, col0 // w + j)),
            per_j(bdb), per_j(bdc), per_j(apow), per_j(dskip),
            const(p1), const(p1), const(p2), const(p2),
        ],
        out_specs=pl.BlockSpec((seq, w), lambda b, j: (b, j)),
        out_shape=jax.ShapeDtypeStruct((bsz * seq, nj * w), BF16),
        scratch_shapes=[
            pltpu.VMEM((S5_CHUNK, rows, w), BF16), pltpu.VMEM((S5_CHUNK, rows, w), BF16),
            pltpu.VMEM((rows, sl), F32), pltpu.VMEM((rows, sl), F32),
            pltpu.VMEM((rows, sl), F32), pltpu.VMEM((rows, sl), F32),
            pltpu.VMEM((nm, sl), F32), pltpu.VMEM((nm, sl), F32),
            pltpu.VMEM((S5_CHUNK, rows, w), BF16), pltpu.VMEM((seq, w), BF16),
        ],
        compiler_params=_cparams(("arbitrary", "arbitrary")),
        name="s5",
    )(proj, bdb, bdc, apow, dskip, p1, p1.T, p2, p2.T)


def _merge_kernel(vc_ref, ys_ref, l1_ref, l2_ref, cpw_ref, cpb_ref, wa_ref, wb_ref, gb1_ref, gb2_ref, o_ref):
    y_conv = jnp.dot(vc_ref[...], cpw_ref[...], preferred_element_type=F32) + cpb_ref[...]
    ys = ys_ref[...]
    a = jnp.dot(ys, wa_ref[...], preferred_element_type=F32)
    b = jnp.dot(ys, wb_ref[...], preferred_element_type=F32)
    y_s5 = a * _sigmoid(b)
    g1 = _sigmoid(l1_ref[...].astype(F32) + gb1_ref[...])
    g2 = _sigmoid(l2_ref[...].astype(F32) + gb2_ref[...])
    o_ref[...] = (g1 * y_conv + g2 * y_s5).astype(BF16)


def _merge(vc, ys, proj, cp_w, cp_b, wa, wb, gate_b, tm, col0):
    t, ch = vc.shape
    d = cp_w.shape[1]
    tn = ch
    assert col0 % tn == 0 and d % tn == 0
    cb, nj = col0 // tn, d // tn
    wcol = lambda rows: pl.BlockSpec((rows, tn), lambda j, i: (0, j))
    return pl.pallas_call(
        _merge_kernel,
        grid=(nj, t // tm),
        in_specs=[
            pl.BlockSpec((tm, ch), lambda j, i: (i, 0)),
            pl.BlockSpec((tm, ch), lambda j, i: (i, 0)),
            pl.BlockSpec((tm, tn), lambda j, i: (i, cb + j)),
            pl.BlockSpec((tm, tn), lambda j, i: (i, cb + nj + j)),
            wcol(ch), wcol(1), wcol(ch), wcol(ch), wcol(1), wcol(1),
        ],
        out_specs=pl.BlockSpec((tm, tn), lambda j, i: (i, j)),
        out_shape=jax.ShapeDtypeStruct((t, d), BF16),
        compiler_params=_cparams(("arbitrary", "arbitrary")),
        name="merge",
    )(vc, ys, proj, proj, cp_w.astype(BF16), cp_b.reshape(1, d), wa.astype(BF16), wb.astype(BF16),
      gate_b[:d].reshape(1, d), gate_b[d:].reshape(1, d))


def _split_bf16(v):
    hi = v.astype(BF16)
    lo = (v - hi.astype(F32)).astype(BF16)
    return hi, lo


def _bf16_bits_hi(v):
    b = lax.bitcast_convert_type(v, jnp.uint32)
    return (b + jnp.uint32(0x7FFF) + ((b >> 16) & jnp.uint32(1))) & jnp.uint32(0xFFFF0000)


def _pack_bf16_pairs(v):
    half = v.shape[-1] // 2
    return _bf16_bits_hi(v[:, half:]) | (_bf16_bits_hi(v[:, :half]) >> 16)


def _unpack_bf16_pairs(p):
    lo = lax.bitcast_convert_type(p << 16, F32).astype(BF16)
    hi = lax.bitcast_convert_type(p & jnp.uint32(0xFFFF0000), F32).astype(BF16)
    return lo, hi


def _outproj_kernel(m_ref, x_ref, ada_ref, wo_ref, g1_ref, g2_ref, rwh_ref, rwl_ref, rb_ref,
                    x1_ref, h2p_ref, idx_ref, gate_ref, *, ne):
    m = jnp.dot(m_ref[...], wo_ref[...], preferred_element_type=F32)
    x1 = x_ref[...] + ada_ref[0, 2:3, :] * _rms(m, g1_ref[...])
    x1_ref[...] = x1
    h2 = _rms(x1, g2_ref[...]) * (1.0 + ada_ref[0, 4:5, :]) + ada_ref[0, 3:4, :]
    h2p_ref[...] = _pack_bf16_pairs(h2)
    hh, hl = _split_bf16(h2)
    lg = jnp.dot(hh, rwh_ref[...], preferred_element_type=F32)
    lg = lg + jnp.dot(hl, rwh_ref[...], preferred_element_type=F32)
    lg = lg + jnp.dot(hh, rwl_ref[...], preferred_element_type=F32)
    lg = lg + rb_ref[...]
    lane = lax.broadcasted_iota(jnp.int32, lg.shape, 1)
    work = jnp.where(lane < ne, lg, -jnp.inf)
    vals, idxs = [], []
    for _ in range(TOP_K):
        mx = jnp.max(work, axis=-1, keepdims=True)
        ix = jnp.min(jnp.where(work == mx, lane, lg.shape[1]), axis=-1, keepdims=True)
        vals.append(mx)
        idxs.append(ix)
        work = jnp.where(lane == ix, -jnp.inf, work)
    ex = [jnp.exp(v - vals[0]) for v in vals]
    den = ex[0]
    for e in ex[1:]:
        den = den + e
    idx_out = jnp.zeros(lg.shape, jnp.int32)
    gate_out = jnp.zeros(lg.shape, F32)
    for k in range(TOP_K):
        idx_out = jnp.where(lane == k, idxs[k], idx_out)
        gate_out = jnp.where(lane == k, ex[k] / den, gate_out)
    idx_ref[...] = idx_out
    gate_ref[...] = gate_out


def _outproj(merged, x2, ada3, w_out, post_mix_g, pre_ffn_g, router_w, router_b, seq, tm):
    t, d = x2.shape
    ne = router_w.shape[1]
    nep = max(V7X_LANES, ne)
    rw = jnp.zeros((d, nep), F32).at[:, :ne].set(router_w)
    rw_hi, rw_lo = _split_bf16(rw)
    rb = jnp.zeros((1, nep), F32).at[0, :ne].set(router_b)
    per_b = seq // tm
    full = lambda shape: pl.BlockSpec(shape, lambda i: (0, 0))
    row = lambda w: pl.BlockSpec((tm, w), lambda i: (i, 0))
    x1, h2p, top_idx, gates = pl.pallas_call(
        functools.partial(_outproj_kernel, ne=ne),
        grid=(t // tm,),
        in_specs=[
            row(d), row(d),
            pl.BlockSpec((1, N_ADA, d), lambda i: (i // per_b, 0, 0)),
            full((d, d)), full((1, d)), full((1, d)), full((d, nep)), full((d, nep)), full((1, nep)),
        ],
        out_specs=[row(d), row(d // 2), row(nep), row(nep)],
        out_shape=[jax.ShapeDtypeStruct((t, d), F32), jax.ShapeDtypeStruct((t, d // 2), jnp.uint32),
                   jax.ShapeDtypeStruct((t, nep), jnp.int32), jax.ShapeDtypeStruct((t, nep), F32)],
        compiler_params=_cparams(("arbitrary",)),
        name="outproj",
    )(merged, x2, ada3, w_out.astype(BF16), post_mix_g.reshape(1, d), pre_ffn_g.reshape(1, d),
      rw_hi, rw_lo, rb)
    return x1, h2p, top_idx[:, :TOP_K], gates[:, :TOP_K]


GATHER_UNROLL = 8


def _row_copy(idx_ref, src_hbm, buf, sems, slot, r):
    return pltpu.make_async_copy(src_hbm.at[pl.ds(idx_ref[0, 0, r], 1), :],
                                 buf.at[slot, pl.ds(r, 1), :], sems.at[slot])


def _start_rows(idx_ref, src_hbm, buf, sems, slot, n):
    def start(r, carry):
        _row_copy(idx_ref, src_hbm, buf, sems, slot, r).start()
        return carry
    lax.fori_loop(0, n, start, 0, unroll=GATHER_UNROLL)


def _wait_rows(idx_ref, src_hbm, buf, sems, slot, n):
    def wait(r, carry):
        _row_copy(idx_ref, src_hbm, buf, sems, slot, r).wait()
        return carry
    lax.fori_loop(0, n, wait, 0, unroll=GATHER_UNROLL)


def _gather_kernel(tok_ref, h_hbm, o_ref, buf, sems):
    i = pl.program_id(0)
    n_blocks = pl.num_programs(0) - 1

    @pl.when(i < n_blocks)
    def _():
        _start_rows(tok_ref, h_hbm, buf, sems, i % 2, MOE_BLOCK)

    @pl.when(i > 0)
    def _():
        slot = (i - 1) % 2
        _wait_rows(tok_ref, h_hbm, buf, sems, slot, MOE_BLOCK)
        o_ref[...] = buf[slot]


def _moe_gather(tok_buf, h2p, n_blocks):
    t, w = h2p.shape
    return pl.pallas_call(
        _gather_kernel,
        grid=(n_blocks + 1,),
        in_specs=[
            pl.BlockSpec((1, 1, MOE_BLOCK), lambda i: (jnp.minimum(i, n_blocks - 1), 0, 0),
                         memory_space=pltpu.SMEM),
            pl.BlockSpec(memory_space=pl.ANY),
        ],
        out_specs=pl.BlockSpec((MOE_BLOCK, w), lambda i: (jnp.maximum(i - 1, 0), 0)),
        out_shape=jax.ShapeDtypeStruct((n_blocks * MOE_BLOCK, w), h2p.dtype),
        scratch_shapes=[pltpu.VMEM((2, MOE_BLOCK, w), h2p.dtype), pltpu.SemaphoreType.DMA((2,))],
        compiler_params=_cparams(("arbitrary",)),
        name="moe_gather",
    )(tok_buf.reshape(n_blocks, 1, MOE_BLOCK), h2p)


def _expert_changed(e_ref, i):
    return jnp.logical_or(i == 0, e_ref[i] != e_ref[jnp.maximum(i - 1, 0)])


def _up_kernel(e_ref, nv_ref, x_ref, w_ref, bg_ref, bl_ref, p_ref, o_ref, wp_ref):
    i = pl.program_id(1)
    tn = wp_ref.shape[1]
    half = V7X_MXU // 2
    hd = x_ref.shape[1]

    @pl.when(jnp.logical_and(_expert_changed(e_ref, i), nv_ref[i] > 0))
    def _():
        for cb in range(tn // V7X_MXU):
            cols = slice(cb * V7X_MXU, (cb + 1) * V7X_MXU)
            wp_ref[:, cols] = jnp.dot(w_ref[0, :, cols].astype(BF16), p_ref[...],
                                      preferred_element_type=F32).astype(BF16)

    @pl.when(nv_ref[i] > 0)
    def _():
        x_lo, x_hi = _unpack_bf16_pairs(x_ref[...])
        for cb in range(tn // V7X_MXU):
            cols = slice(cb * V7X_MXU, (cb + 1) * V7X_MXU)
            hb = (jnp.dot(x_lo, wp_ref[0:hd, cols], preferred_element_type=F32)
                  + jnp.dot(x_hi, wp_ref[hd:2 * hd, cols], preferred_element_type=F32))
            out_cols = slice(cb * half, (cb + 1) * half)
            x_glu = hb[:, :half] + bg_ref[0, :, out_cols]
            x_lin = hb[:, half:] + bl_ref[0, :, out_cols]
            x_glu = jnp.minimum(x_glu, SWIGLU_LIMIT)
            x_lin = jnp.clip(x_lin, -SWIGLU_LIMIT, SWIGLU_LIMIT)
            act = x_glu * _sigmoid(SWIGLU_ALPHA * x_glu) * (x_lin + 1.0)
            o_ref[:, out_cols] = act.astype(BF16)

    @pl.when(nv_ref[i] == 0)
    def _():
        o_ref[...] = jnp.zeros(o_ref.shape, o_ref.dtype)


def _moe_up(blk_e, nvalid, xs, w1, b1, tn):
    n_rows, hd = xs.shape
    ne, d, f2 = w1.shape
    assert d == 2 * hd
    f = f2 // 2
    n_blocks = n_rows // MOE_BLOCK
    half = V7X_MXU // 2
    c = jnp.arange(V7X_MXU)
    perm = jnp.zeros((V7X_MXU, V7X_MXU), BF16).at[c, (c % 2) * half + c // 2].set(1.0)
    b1g = b1[:, 0::2].reshape(ne, 1, f)
    b1l = b1[:, 1::2].reshape(ne, 1, f)
    return pl.pallas_call(
        _up_kernel,
        grid_spec=pltpu.PrefetchScalarGridSpec(
            num_scalar_prefetch=2,
            grid=(f2 // tn, n_blocks),
            in_specs=[
                pl.BlockSpec((MOE_BLOCK, hd), lambda j, i, e, nv: (i, 0)),
                pl.BlockSpec((1, d, tn), lambda j, i, e, nv: (e[i], 0, j)),
                pl.BlockSpec((1, 1, tn // 2), lambda j, i, e, nv: (e[i], 0, j)),
                pl.BlockSpec((1, 1, tn // 2), lambda j, i, e, nv: (e[i], 0, j)),
                pl.BlockSpec((V7X_MXU, V7X_MXU), lambda j, i, e, nv: (0, 0)),
            ],
            out_specs=pl.BlockSpec((MOE_BLOCK, tn // 2), lambda j, i, e, nv: (i, j)),
            scratch_shapes=[pltpu.VMEM((d, tn), BF16)],
        ),
        out_shape=jax.ShapeDtypeStruct((n_rows, f), BF16),
        compiler_params=_cparams(("arbitrary", "arbitrary")),
        name="moe_up",
    )(blk_e, nvalid, xs, w1, b1g, b1l, perm)


def _down_kernel(e_ref, nv_ref, a_ref, w_ref, b_ref, g_ref, o_ref, wb_ref):
    i = pl.program_id(0)

    @pl.when(nv_ref[i] > 0)
    def _():
        @pl.when(_expert_changed(e_ref, i))
        def _():
            wb_ref[...] = w_ref[0].astype(BF16)

        y = jnp.dot(a_ref[...], wb_ref[...], preferred_element_type=F32) + b_ref[0]
        o_ref[...] = y * g_ref[...]

    @pl.when(nv_ref[i] == 0)
    def _():
        o_ref[...] = jnp.zeros(o_ref.shape, o_ref.dtype)


def _moe_down(blk_e, nvalid, act, w2, b2, g_buf):
    n_rows, f = act.shape
    ne, _, d = w2.shape
    n_blocks = n_rows // MOE_BLOCK
    return pl.pallas_call(
        _down_kernel,
        grid_spec=pltpu.PrefetchScalarGridSpec(
            num_scalar_prefetch=2,
            grid=(n_blocks,),
            in_specs=[
                pl.BlockSpec((MOE_BLOCK, f), lambda i, e, nv: (i, 0)),
                pl.BlockSpec((1, f, d), lambda i, e, nv: (e[i], 0, 0)),
                pl.BlockSpec((1, 1, d), lambda i, e, nv: (e[i], 0, 0)),
                pl.BlockSpec((MOE_BLOCK, 1), lambda i, e, nv: (i, 0)),
            ],
            out_specs=pl.BlockSpec((MOE_BLOCK, d), lambda i, e, nv: (i, 0)),
            scratch_shapes=[pltpu.VMEM((f, d), BF16)],
        ),
        out_shape=jax.ShapeDtypeStruct((n_rows, d), F32),
        compiler_params=_cparams(("arbitrary",)),
        name="moe_down",
    )(blk_e, nvalid, act, w2, b2.reshape(ne, 1, d), g_buf.reshape(n_rows, 1))


def _combine_kernel(pos_ref, y_hbm, x1_ref, ada_ref, g_ref, o_ref, buf, sems, *, tt):
    i = pl.program_id(0)
    n_tiles = pl.num_programs(0) - 1

    @pl.when(i < n_tiles)
    def _():
        _start_rows(pos_ref, y_hbm, buf, sems, i % 2, TOP_K * tt)

    @pl.when(i > 0)
    def _():
        slot = (i - 1) % 2
        _wait_rows(pos_ref, y_hbm, buf, sems, slot, TOP_K * tt)
        f = buf[slot, 0:tt, :]
        for k in range(1, TOP_K):
            f = f + buf[slot, k * tt:(k + 1) * tt, :]
        o_ref[...] = x1_ref[...] + ada_ref[0, 5:6, :] * _rms(f, g_ref[...])


def _moe_combine(pos, y_buf, x1, ada3, post_ffn_g, seq, tt):
    t, d = x1.shape
    per_b = seq // tt
    nt = t // tt
    pos_t = pos.reshape(nt, tt, TOP_K).transpose(0, 2, 1).reshape(nt, 1, TOP_K * tt)
    prev = lambda i: jnp.maximum(i - 1, 0)
    return pl.pallas_call(
        functools.partial(_combine_kernel, tt=tt),
        grid=(nt + 1,),
        in_specs=[
            pl.BlockSpec((1, 1, TOP_K * tt), lambda i: (jnp.minimum(i, nt - 1), 0, 0), memory_space=pltpu.SMEM),
            pl.BlockSpec(memory_space=pl.ANY),
            pl.BlockSpec((tt, d), lambda i: (prev(i), 0)),
            pl.BlockSpec((1, N_ADA, d), lambda i: (prev(i) // per_b, 0, 0)),
            pl.BlockSpec((1, d), lambda i: (0, 0)),
        ],
        out_specs=pl.BlockSpec((tt, d), lambda i: (prev(i), 0)),
        out_shape=jax.ShapeDtypeStruct((t, d), F32),
        scratch_shapes=[pltpu.VMEM((2, TOP_K * tt, d), F32), pltpu.SemaphoreType.DMA((2,))],
        compiler_params=_cparams(("arbitrary",)),
        name="moe_combine",
    )(pos_t, y_buf, x1, ada3, post_ffn_g.reshape(1, d))


def _route(top_idx, gates, n_experts):
    n_tok = top_idx.shape[0]
    n_asg = n_tok * TOP_K
    i32 = jnp.int32
    flat_e = top_idx.reshape(n_asg)
    order = jnp.argsort(flat_e).astype(i32)
    counts = jnp.sum(flat_e[:, None] == jnp.arange(n_experts, dtype=i32)[None, :], axis=0, dtype=i32)
    starts = jnp.cumsum(counts) - counts
    padded = (counts + MOE_BLOCK - 1) // MOE_BLOCK * MOE_BLOCK
    pends = jnp.cumsum(padded)
    pstarts = pends - padded
    n_blocks = -(-n_asg // MOE_BLOCK) + n_experts
    blk_start = jnp.arange(n_blocks, dtype=i32) * MOE_BLOCK
    blk_e = jnp.minimum(jnp.sum(pends[None, :] <= blk_start[:, None], axis=1, dtype=i32), n_experts - 1)
    off = blk_start - pstarts[blk_e]
    nvalid = jnp.clip(counts[blk_e] - off, 0, MOE_BLOCK)
    j = off[:, None] + jnp.arange(MOE_BLOCK, dtype=i32)[None, :]
    valid = j < counts[blk_e][:, None]
    src = jnp.clip(starts[blk_e][:, None] + j, 0, n_asg - 1)
    asg = order[src]
    tok_buf = jnp.where(valid, asg // TOP_K, 0)
    g_buf = jnp.where(valid, gates.reshape(n_asg)[asg], 0.0)
    rank = jnp.argsort(order).astype(i32)
    pos = pstarts[flat_e] + rank - starts[flat_e]
    return tok_buf, g_buf, pos, blk_e, nvalid, n_blocks


def _pick(n, pref):
    return pref if n % pref == 0 else n


def kernel(x, c, ada_w, ada_b, pre_mix_g, post_mix_g, pre_ffn_g, post_ffn_g, w_in, gate_b, dw_w, dw_b, cln_g, cln_b, cp_w, cp_b, s5_a_re, s5_a_im, s5_log_dt, s5_b_re, s5_b_im, s5_c_re, s5_c_im, s5_d, glu_wa, glu_wb, w_out, router_w, router_b, w1, b1, w2, b2):
    bsz, seq, d = x.shape
    t = bsz * seq
    depth = ada_w.shape[0]
    conv_ch = dw_w.shape[-1]
    ng, ns, nh = s5_b_re.shape[1:]
    s5_w = ng * nh
    ne = router_w.shape[-1]
    col_s5 = 2 * conv_ch
    col_gate = col_s5 + s5_w
    assert conv_ch == s5_w and V7X_MXU % nh == 0 and ng % (V7X_MXU // nh) == 0
    tm = _pick(seq, 512)
    tm_out = _pick(seq, 256)
    tn_in = _pick(w_in.shape[-1], conv_ch)

    x2 = x.reshape(t, d)
    for l in range(depth):
        ada3 = _ada(c, ada_w[l], ada_b[l]).reshape(bsz, N_ADA, d)
        proj = _inproj(x2, ada3, pre_mix_g[l], w_in[l].astype(BF16), seq, _pick(seq, 1024), tn_in)
        vc = _conv_branch(proj, dw_w[l], dw_b[l], cln_g[l], cln_b[l], bsz, seq, tm)

        s5_params = _s5_params(s5_a_re[l], s5_a_im[l], s5_log_dt[l], s5_b_re[l], s5_b_im[l],
                               s5_c_re[l], s5_c_im[l], s5_d[l])
        ys = _s5_branch(proj, col_s5, s5_params, bsz, seq)

        merged = _merge(vc, ys, proj, cp_w[l], cp_b[l], glu_wa[l], glu_wb[l], gate_b[l], tm, col_gate)
        x1, h2p, top_idx, gates = _outproj(merged, x2, ada3, w_out[l], post_mix_g[l], pre_ffn_g[l],
                                           router_w[l], router_b[l], seq, tm_out)

        tok_buf, g_buf, pos, blk_e, nvalid, n_blocks = _route(top_idx, gates, ne)
        xs = _moe_gather(tok_buf, h2p, n_blocks)
        act = _moe_up(blk_e, nvalid, xs, w1[l], b1[l], _pick(w1.shape[-1], 2048))
        y_buf = _moe_down(blk_e, nvalid, act, w2[l], b2[l], g_buf)
        x2 = _moe_combine(pos, y_buf, x1, ada3, post_ffn_g[l], seq, _pick(seq, 128))
    return x2.reshape(bsz, seq, d)
```

```python
import functools
import math

import jax
import jax.numpy as jnp
from jax import lax
from jax.experimental import pallas as pl
from jax.experimental.pallas import tpu as pltpu

EPS = 1e-6
N_ADA = 6
TOP_K = 4
MOE_BLOCK = 256
SWIGLU_ALPHA = 1.702
SWIGLU_LIMIT = 7.0
S5_CHUNK = 16
V7X_LANES = 128
V7X_SUBLANES = 8
V7X_MXU = 256
VMEM_LIMIT = 56 * 1024 * 1024

F32 = jnp.float32
BF16 = jnp.bfloat16


def _cparams(sem):
    return pltpu.CompilerParams(dimension_semantics=sem, vmem_limit_bytes=VMEM_LIMIT)


def _sigmoid(v):
    return 1.0 / (1.0 + jnp.exp(-v))


def _rms(v, g):
    return v * lax.rsqrt(jnp.mean(v * v, axis=-1, keepdims=True) + EPS) * g


def _ada_kernel(c_ref, w_ref, b_ref, o_ref):
    c = c_ref[...]
    s = (c * _sigmoid(c)).astype(BF16)
    o_ref[...] = jnp.dot(s, w_ref[...].astype(BF16), preferred_element_type=F32) + b_ref[...]


def _ada(c, ada_w, ada_b):
    bsz, d = c.shape
    n = ada_w.shape[1]
    tn = 1024 if n % 1024 == 0 else n
    return pl.pallas_call(
        _ada_kernel,
        grid=(n // tn,),
        in_specs=[
            pl.BlockSpec((bsz, d), lambda j: (0, 0)),
            pl.BlockSpec((d, tn), lambda j: (0, j)),
            pl.BlockSpec((1, tn), lambda j: (0, j)),
        ],
        out_specs=pl.BlockSpec((bsz, tn), lambda j: (0, j)),
        out_shape=jax.ShapeDtypeStruct((bsz, n), F32),
        compiler_params=_cparams(("arbitrary",)),
        name="ada",
    )(c, ada_w, ada_b.reshape(1, n))


def _inproj_kernel(x_ref, ada_ref, g_ref, w_ref, o_ref, h_ref):
    @pl.when(pl.program_id(1) == 0)
    def _():
        y = _rms(x_ref[...], g_ref[...])
        h = y * (1.0 + ada_ref[0, 1:2, :]) + ada_ref[0, 0:1, :]
        h_ref[...] = h.astype(BF16)

    o_ref[...] = jnp.dot(h_ref[...], w_ref[...], preferred_element_type=F32).astype(BF16)


def _inproj(x2, ada3, g, w_bf, seq, tm, tn):
    t, d = x2.shape
    n = w_bf.shape[1]
    per_b = seq // tm
    return pl.pallas_call(
        _inproj_kernel,
        grid=(t // tm, n // tn),
        in_specs=[
            pl.BlockSpec((tm, d), lambda i, j: (i, 0)),
            pl.BlockSpec((1, N_ADA, d), lambda i, j: (i // per_b, 0, 0)),
            pl.BlockSpec((1, d), lambda i, j: (0, 0)),
            pl.BlockSpec((d, tn), lambda i, j: (0, j)),
        ],
        out_specs=pl.BlockSpec((tm, tn), lambda i, j: (i, j)),
        out_shape=jax.ShapeDtypeStruct((t, n), BF16),
        scratch_shapes=[pltpu.VMEM((tm, d), BF16)],
        compiler_params=_cparams(("arbitrary", "arbitrary")),
        name="inproj",
    )(x2, ada3, g.reshape(1, d), w_bf)


CONV_HALO = 32
CONV_ROWS = 64


def _conv_kernel(pv_ref, pg_ref, w_ref, b_ref, g_ref, beta_ref, o_ref, vext, cbuf, *, taps, tl):
    t = pl.program_id(1)
    ch = cbuf.shape[1]

    @pl.when(t == 0)
    def _():
        vext[0:CONV_HALO, :] = jnp.zeros((CONV_HALO, ch), F32)

    @pl.when(t > 0)
    def _():
        vext[0:CONV_HALO, :] = vext[tl:tl + CONV_HALO, :]

    pv = pv_ref[...].astype(F32)
    pg = pg_ref[...].astype(F32)
    vext[CONV_HALO:CONV_HALO + tl, :] = pv * _sigmoid(pg)

    off = CONV_HALO - (taps - 1)
    for cb in range(ch // V7X_LANES):
        lanes = slice(cb * V7X_LANES, (cb + 1) * V7X_LANES)

        for rc in range(tl // CONV_ROWS):
            r0 = rc * CONV_ROWS
            acc = jnp.zeros((CONV_ROWS, V7X_LANES), F32)
            for k in range(taps):
                acc = acc + w_ref[k:k + 1, lanes] * vext[r0 + off + k:r0 + off + k + CONV_ROWS, lanes]
            cbuf[r0:r0 + CONV_ROWS, lanes] = acc + b_ref[:, lanes]

    v = cbuf[...]
    mu = jnp.mean(v, axis=-1, keepdims=True)
    xc = v - mu
    var = jnp.mean(xc * xc, axis=-1, keepdims=True)
    y = xc * lax.rsqrt(var + EPS) * g_ref[...] + beta_ref[...]
    o_ref[...] = (y * _sigmoid(y)).astype(BF16)


def _conv_branch(proj, dw_w, dw_b, cln_g, cln_b, bsz, seq, tl):
    taps, ch = dw_w.shape
    assert taps - 1 <= CONV_HALO and tl % CONV_ROWS == 0 and ch % V7X_LANES == 0
    per_b = seq // tl
    vec = lambda a: a.reshape(1, ch)
    return pl.pallas_call(
        functools.partial(_conv_kernel, taps=taps, tl=tl),
        grid=(bsz, per_b),
        in_specs=[
            pl.BlockSpec((tl, ch), lambda b, t: (b * per_b + t, 0)),
            pl.BlockSpec((tl, ch), lambda b, t: (b * per_b + t, 1)),
            pl.BlockSpec((taps, ch), lambda b, t: (0, 0)),
            pl.BlockSpec((1, ch), lambda b, t: (0, 0)),
            pl.BlockSpec((1, ch), lambda b, t: (0, 0)),
            pl.BlockSpec((1, ch), lambda b, t: (0, 0)),
        ],
        out_specs=pl.BlockSpec((tl, ch), lambda b, t: (b * per_b + t, 0)),
        out_shape=jax.ShapeDtypeStruct((bsz * seq, ch), BF16),
        scratch_shapes=[pltpu.VMEM((CONV_HALO + tl, ch), F32), pltpu.VMEM((tl, ch), F32)],
        compiler_params=_cparams(("arbitrary", "arbitrary")),
        name="conv",
    )(proj, proj, dw_w, vec(dw_b), vec(cln_g), vec(cln_b))


S5_SUPER = 16


def _s5_params(a_re, a_im, log_dt, b_re, b_im, c_re, c_im, d_skip):
    ng, ns = a_re.shape
    nh = b_re.shape[-1]
    gt = V7X_MXU // nh
    nj = ng // gt
    dt = jnp.exp(log_dt)[:, None]
    mag = jnp.exp(a_re * dt)
    abar_re = mag * jnp.cos(a_im * dt)
    abar_im = mag * jnp.sin(a_im * dt)
    den = a_re * a_re + a_im * a_im
    num_re = abar_re - 1.0
    coef_re = (num_re * a_re + abar_im * a_im) / den
    coef_im = (abar_im * a_re - num_re * a_im) / den
    bb_re = coef_re[:, :, None] * b_re - coef_im[:, :, None] * b_im
    bb_im = coef_re[:, :, None] * b_im + coef_im[:, :, None] * b_re
    p = jnp.asarray([1.0, S5_CHUNK, S5_CHUNK * S5_SUPER] + [S5_CHUNK * r for r in range(S5_SUPER)], F32)
    pmag = jnp.exp(p[:, None, None] * (a_re * dt)[None])
    pw_re = pmag * jnp.cos(p[:, None, None] * (a_im * dt)[None])
    pw_im = pmag * jnp.sin(p[:, None, None] * (a_im * dt)[None])
    tile = lambda a: a.reshape(-1, nj, gt * ns).transpose(1, 0, 2)
    apow = jnp.concatenate([tile(pw_re), tile(pw_im)], axis=1)
    eye = jnp.eye(gt, dtype=F32)

    def bdiag_in(bb):
        return jnp.einsum('jgnh,gk->jghkn', bb.reshape(nj, gt, ns, nh), eye).reshape(nj, gt * nh, gt * ns)

    def bdiag_out(cc):
        return jnp.einsum('jghn,gk->jgnkh', cc.reshape(nj, gt, nh, ns), eye).reshape(nj, gt * ns, gt * nh)

    bdb = jnp.stack([bdiag_in(bb_re), bdiag_in(bb_im)], axis=1)
    bdc = jnp.stack([bdiag_out(c_re), bdiag_out(-c_im)], axis=1)
    return bdb.astype(BF16), bdc.astype(BF16), apow, d_skip.reshape(nj, 1, gt * nh)


def _cmul_add(ar, ai, xr, xi, vr, vi):
    return ar * xr - ai * xi + vr, ar * xi + ai * xr + vi


def _s5_kernel(u_ref, bdb_ref, bdc_ref, ap_ref, d_ref, p1_ref, p1t_ref, p2_ref, p2t_ref, o_ref,
               u2n_ref, u2_ref, hr_ref, hi_ref, zr_ref, zi_ref, gr_ref, gi_ref, y2_ref, y2n_ref,
               *, n_super):
    tc, nr, nm = S5_CHUNK, S5_SUPER, n_super
    tile = tc * nr
    np_ = 3 + nr
    apow = lambda k: (ap_ref[0, k:k + 1, :], ap_ref[0, np_ + k:np_ + k + 1, :])
    ar, ai = apow(0)

    for m in range(nm):
        pu = jnp.dot(p1_ref[...], u_ref[m * tile:(m + 1) * tile, :], preferred_element_type=F32).astype(BF16)
        for s in range(tc):
            u2n_ref[s, m * nr:(m + 1) * nr, :] = pu[s * nr:(s + 1) * nr, :]
    for s in range(tc):
        u2_ref[s] = jnp.dot(p2_ref[...], u2n_ref[s], preferred_element_type=F32).astype(BF16)

    def drive(s):
        u = u2_ref[s]
        return (jnp.dot(u, bdb_ref[0, 0], preferred_element_type=F32),
                jnp.dot(u, bdb_ref[0, 1], preferred_element_type=F32))

    hr_ref[...] = jnp.zeros(hr_ref.shape, F32)
    hi_ref[...] = jnp.zeros(hi_ref.shape, F32)

    def pass1(s, carry):
        vr, vi = drive(s)
        hr, hi = _cmul_add(ar, ai, hr_ref[...], hi_ref[...], vr, vi)
        hr_ref[...] = hr
        hi_ref[...] = hi
        return carry

    lax.fori_loop(0, tc, pass1, 0)

    a16r, a16i = apow(1)
    qr = jnp.zeros((nm, hr_ref.shape[1]), F32)
    qi = qr
    for r in range(nr):
        rows = slice(r * nm, (r + 1) * nm)
        zr_ref[rows, :] = qr
        zi_ref[rows, :] = qi
        qr, qi = _cmul_add(a16r, a16i, qr, qi, hr_ref[rows, :], hi_ref[rows, :])
    a256r, a256i = apow(2)
    gr = jnp.zeros((1, hr_ref.shape[1]), F32)
    gi = gr
    for m in range(nm):
        gr_ref[m:m + 1, :] = gr
        gi_ref[m:m + 1, :] = gi
        gr, gi = _cmul_add(a256r, a256i, gr, gi, qr[m:m + 1, :], qi[m:m + 1, :])
    gpr = gr_ref[...]
    gpi = gi_ref[...]
    for r in range(nr):
        rows = slice(r * nm, (r + 1) * nm)
        pr, pi = apow(3 + r)
        zr, zi = _cmul_add(pr, pi, gpr, gpi, zr_ref[rows, :], zi_ref[rows, :])
        zr_ref[rows, :] = zr
        zi_ref[rows, :] = zi

    def pass2(t, carry):
        vr, vi = drive(t)
        sr, si = _cmul_add(ar, ai, zr_ref[...], zi_ref[...], vr, vi)
        zr_ref[...] = sr
        zi_ref[...] = si
        y = (jnp.dot(sr.astype(BF16), bdc_ref[0, 0], preferred_element_type=F32)
             + jnp.dot(si.astype(BF16), bdc_ref[0, 1], preferred_element_type=F32))
        y = y + d_ref[0] * u2_ref[t].astype(F32)
        y = 0.5 * y * (1.0 + jnp.tanh(math.sqrt(2.0 / math.pi) * (y + 0.044715 * (y * y * y))))
        y2_ref[t] = y.astype(BF16)
        return carry

    lax.fori_loop(0, tc, pass2, 0)

    for t in range(tc):
        yn = jnp.dot(p2t_ref[...], y2_ref[t], preferred_element_type=F32).astype(BF16)
        for m in range(nm):
            y2n_ref[m * tile + t * nr:m * tile + (t + 1) * nr, :] = yn[m * nr:(m + 1) * nr, :]
    for m in range(nm):
        rows = slice(m * tile, (m + 1) * tile)
        o_ref[rows, :] = jnp.dot(p1t_ref[...], y2n_ref[rows, :], preferred_element_type=F32).astype(BF16)


def _s5_branch(proj, col0, params, bsz, seq):
    bdb, bdc, apow, dskip = params
    nj, _, w, sl = bdb.shape
    tile = S5_CHUNK * S5_SUPER
    assert seq % tile == 0 and col0 % w == 0
    nm = seq // tile
    assert nm % V7X_SUBLANES == 0, "row slabs of the chunk recurrence must be whole sublane tiles"
    rows = S5_SUPER * nm
    i1 = jnp.arange(tile)
    p1 = jnp.zeros((tile, tile), BF16).at[(i1 % S5_CHUNK) * S5_SUPER + i1 // S5_CHUNK, i1].set(1.0)
    i2 = jnp.arange(rows)
    p2 = jnp.zeros((rows, rows), BF16).at[(i2 % S5_SUPER) * nm + i2 // S5_SUPER, i2].set(1.0)
    const = lambda a: pl.BlockSpec(a.shape, lambda b, j: (0,) * a.ndim)
    per_j = lambda a: pl.BlockSpec((1,) + a.shape[1:], lambda b, j: (j,) + (0,) * (a.ndim - 1))
    return pl.pallas_call(
        functools.partial(_s5_kernel, n_super=nm),
        grid=(bsz, nj),
        in_specs=[
            pl.BlockSpec((seq, w), lambda b, j: (b, col0 // w + j)),
            per_j(bdb), per_j(bdc), per_j(apow), per_j(dskip),
            const(p1), const(p1), const(p2), const(p2),
        ],
        out_specs=pl.BlockSpec((seq, w), lambda b, j: (b, j)),
        out_shape=jax.ShapeDtypeStruct((bsz * seq, nj * w), BF16),
        scratch_shapes=[
            pltpu.VMEM((S5_CHUNK, rows, w), BF16), pltpu.VMEM((S5_CHUNK, rows, w), BF16),
            pltpu.VMEM((rows, sl), F32), pltpu.VMEM((rows, sl), F32),
            pltpu.VMEM((rows, sl), F32), pltpu.VMEM((rows, sl), F32),
            pltpu.VMEM((nm, sl), F32), pltpu.VMEM((nm, sl), F32),
            pltpu.VMEM((S5_CHUNK, rows, w), BF16), pltpu.VMEM((seq, w), BF16),
        ],
        compiler_params=_cparams(("arbitrary", "arbitrary")),
        name="s5",
    )(proj, bdb, bdc, apow, dskip, p1, p1.T, p2, p2.T)


def _merge_kernel(vc_ref, ys_ref, l1_ref, l2_ref, cpw_ref, cpb_ref, wa_ref, wb_ref, gb1_ref, gb2_ref, o_ref):
    y_conv = jnp.dot(vc_ref[...], cpw_ref[...], preferred_element_type=F32) + cpb_ref[...]
    ys = ys_ref[...]
    a = jnp.dot(ys, wa_ref[...], preferred_element_type=F32)
    b = jnp.dot(ys, wb_ref[...], preferred_element_type=F32)
    y_s5 = a * _sigmoid(b)
    g1 = _sigmoid(l1_ref[...].astype(F32) + gb1_ref[...])
    g2 = _sigmoid(l2_ref[...].astype(F32) + gb2_ref[...])
    o_ref[...] = (g1 * y_conv + g2 * y_s5).astype(BF16)


def _merge(vc, ys, proj, cp_w, cp_b, wa, wb, gate_b, tm, col0):
    t, ch = vc.shape
    d = cp_w.shape[1]
    tn = ch
    assert col0 % tn == 0 and d % tn == 0
    cb, nj = col0 // tn, d // tn
    wcol = lambda rows: pl.BlockSpec((rows, tn), lambda j, i: (0, j))
    return pl.pallas_call(
        _merge_kernel,
        grid=(nj, t // tm),
        in_specs=[
            pl.BlockSpec((tm, ch), lambda j, i: (i, 0)),
            pl.BlockSpec((tm, ch), lambda j, i: (i, 0)),
            pl.BlockSpec((tm, tn), lambda j, i: (i, cb + j)),
            pl.BlockSpec((tm, tn), lambda j, i: (i, cb + nj + j)),
            wcol(ch), wcol(1), wcol(ch), wcol(ch), wcol(1), wcol(1),
        ],
        out_specs=pl.BlockSpec((tm, tn), lambda j, i: (i, j)),
        out_shape=jax.ShapeDtypeStruct((t, d), BF16),
        compiler_params=_cparams(("arbitrary", "arbitrary")),
        name="merge",
    )(vc, ys, proj, proj, cp_w.astype(BF16), cp_b.reshape(1, d), wa.astype(BF16), wb.astype(BF16),
      gate_b[:d].reshape(1, d), gate_b[d:].reshape(1, d))


def _split_bf16(v):
    hi = v.astype(BF16)
    lo = (v - hi.astype(F32)).astype(BF16)
    return hi, lo


def _bf16_bits_hi(v):
    b = lax.bitcast_convert_type(v, jnp.uint32)
    return (b + jnp.uint32(0x7FFF) + ((b >> 16) & jnp.uint32(1))) & jnp.uint32(0xFFFF0000)


def _pack_bf16_pairs(v):
    half = v.shape[-1] // 2
    return _bf16_bits_hi(v[:, half:]) | (_bf16_bits_hi(v[:, :half]) >> 16)


def _unpack_bf16_pairs(p):
    lo = lax.bitcast_convert_type(p << 16, F32).astype(BF16)
    hi = lax.bitcast_convert_type(p & jnp.uint32(0xFFFF0000), F32).astype(BF16)
    return lo, hi


def _outproj_kernel(m_ref, x_ref, ada_ref, wo_ref, g1_ref, g2_ref, rwh_ref, rwl_ref, rb_ref,
                    x1_ref, h2p_ref, idx_ref, gate_ref, *, ne):
    m = jnp.dot(m_ref[...], wo_ref[...], preferred_element_type=F32)
    x1 = x_ref[...] + ada_ref[0, 2:3, :] * _rms(m, g1_ref[...])
    x1_ref[...] = x1
    h2 = _rms(x1, g2_ref[...]) * (1.0 + ada_ref[0, 4:5, :]) + ada_ref[0, 3:4, :]
    h2p_ref[...] = _pack_bf16_pairs(h2)
    hh, hl = _split_bf16(h2)
    lg = jnp.dot(hh, rwh_ref[...], preferred_element_type=F32)
    lg = lg + jnp.dot(hl, rwh_ref[...], preferred_element_type=F32)
    lg = lg + jnp.dot(hh, rwl_ref[...], preferred_element_type=F32)
    lg = lg + rb_ref[...]
    lane = lax.broadcasted_iota(jnp.int32, lg.shape, 1)
    work = jnp.where(lane < ne, lg, -jnp.inf)
    vals, idxs = [], []
    for _ in range(TOP_K):
        mx = jnp.max(work, axis=-1, keepdims=True)
        ix = jnp.min(jnp.where(work == mx, lane, lg.shape[1]), axis=-1, keepdims=True)
        vals.append(mx)
        idxs.append(ix)
        work = jnp.where(lane == ix, -jnp.inf, work)
    ex = [jnp.exp(v - vals[0]) for v in vals]
    den = ex[0]
    for e in ex[1:]:
        den = den + e
    idx_out = jnp.zeros(lg.shape, jnp.int32)
    gate_out = jnp.zeros(lg.shape, F32)
    for k in range(TOP_K):
        idx_out = jnp.where(lane == k, idxs[k], idx_out)
        gate_out = jnp.where(lane == k, ex[k] / den, gate_out)
    idx_ref[...] = idx_out
    gate_ref[...] = gate_out


def _outproj(merged, x2, ada3, w_out, post_mix_g, pre_ffn_g, router_w, router_b, seq, tm):
    t, d = x2.shape
    ne = router_w.shape[1]
    nep = max(V7X_LANES, ne)
    rw = jnp.zeros((d, nep), F32).at[:, :ne].set(router_w)
    rw_hi, rw_lo = _split_bf16(rw)
    rb = jnp.zeros((1, nep), F32).at[0, :ne].set(router_b)
    per_b = seq // tm
    full = lambda shape: pl.BlockSpec(shape, lambda i: (0, 0))
    row = lambda w: pl.BlockSpec((tm, w), lambda i: (i, 0))
    x1, h2p, top_idx, gates = pl.pallas_call(
        functools.partial(_outproj_kernel, ne=ne),
        grid=(t // tm,),
        in_specs=[
            row(d), row(d),
            pl.BlockSpec((1, N_ADA, d), lambda i: (i // per_b, 0, 0)),
            full((d, d)), full((1, d)), full((1, d)), full((d, nep)), full((d, nep)), full((1, nep)),
        ],
        out_specs=[row(d), row(d // 2), row(nep), row(nep)],
        out_shape=[jax.ShapeDtypeStruct((t, d), F32), jax.ShapeDtypeStruct((t, d // 2), jnp.uint32),
                   jax.ShapeDtypeStruct((t, nep), jnp.int32), jax.ShapeDtypeStruct((t, nep), F32)],
        compiler_params=_cparams(("arbitrary",)),
        name="outproj",
    )(merged, x2, ada3, w_out.astype(BF16), post_mix_g.reshape(1, d), pre_ffn_g.reshape(1, d),
      rw_hi, rw_lo, rb)
    return x1, h2p, top_idx[:, :TOP_K], gates[:, :TOP_K]


GATHER_UNROLL = 8


def _row_copy(idx_ref, src_hbm, buf, sems, slot, r):
    return pltpu.make_async_copy(src_hbm.at[pl.ds(idx_ref[0, 0, r], 1), :],
                                 buf.at[slot, pl.ds(r, 1), :], sems.at[slot])


def _start_rows(idx_ref, src_hbm, buf, sems, slot, n, unrolled=False):
    if unrolled:
        for r in range(n):
            _row_copy(idx_ref, src_hbm, buf, sems, slot, r).start()
        return

    def start(r, carry):
        _row_copy(idx_ref, src_hbm, buf, sems, slot, r).start()
        return carry
    lax.fori_loop(0, n, start, 0, unroll=GATHER_UNROLL)


def _wait_rows(idx_ref, src_hbm, buf, sems, slot, n):
    def wait(r, carry):
        _row_copy(idx_ref, src_hbm, buf, sems, slot, r).wait()
        return carry
    lax.fori_loop(0, n, wait, 0, unroll=GATHER_UNROLL)


def _expert_changed(e_ref, i):
    return jnp.logical_or(i == 0, e_ref[i] != e_ref[jnp.maximum(i - 1, 0)])


def _up_kernel(e_ref, nv_ref, tok_ref, tokn_ref, h_hbm, w_ref, bg_ref, bl_ref, p_ref, o_ref,
               wp_ref, xbuf, sems):
    j, i = pl.program_id(0), pl.program_id(1)
    n_blocks = pl.num_programs(1)
    step = j * n_blocks + i
    slot = step % 2
    last = step == pl.num_programs(0) * n_blocks - 1
    tn = wp_ref.shape[1]
    half = V7X_MXU // 2
    hd = xbuf.shape[2]

    @pl.when(step == 0)
    def _():
        _start_rows(tok_ref, h_hbm, xbuf, sems, slot, MOE_BLOCK)

    _wait_rows(tok_ref, h_hbm, xbuf, sems, slot, MOE_BLOCK)

    @pl.when(jnp.logical_and(_expert_changed(e_ref, i), nv_ref[i] > 0))
    def _():
        for cb in range(tn // V7X_MXU):
            cols = slice(cb * V7X_MXU, (cb + 1) * V7X_MXU)
            wp_ref[:, cols] = jnp.dot(w_ref[0, :, cols].astype(BF16), p_ref[...],
                                      preferred_element_type=F32).astype(BF16)

    @pl.when(nv_ref[i] > 0)
    def _():
        _start_rows(tokn_ref, h_hbm, xbuf, sems, 1 - slot, MOE_BLOCK, unrolled=True)
        x_lo, x_hi = _unpack_bf16_pairs(xbuf[slot])
        for cb in range(tn // V7X_MXU):
            cols = slice(cb * V7X_MXU, (cb + 1) * V7X_MXU)
            hb = (jnp.dot(x_lo, wp_ref[0:hd, cols], preferred_element_type=F32)
                  + jnp.dot(x_hi, wp_ref[hd:2 * hd, cols], preferred_element_type=F32))
            out_cols = slice(cb * half, (cb + 1) * half)
            x_glu = hb[:, :half] + bg_ref[0, :, out_cols]
            x_lin = hb[:, half:] + bl_ref[0, :, out_cols]
            x_glu = jnp.minimum(x_glu, SWIGLU_LIMIT)
            x_lin = jnp.clip(x_lin, -SWIGLU_LIMIT, SWIGLU_LIMIT)
            act = x_glu * _sigmoid(SWIGLU_ALPHA * x_glu) * (x_lin + 1.0)
            o_ref[:, out_cols] = act.astype(BF16)

    @pl.when(nv_ref[i] == 0)
    def _():
        @pl.when(jnp.logical_not(last))
        def _():
            _start_rows(tokn_ref, h_hbm, xbuf, sems, 1 - slot, MOE_BLOCK)

        o_ref[...] = jnp.zeros(o_ref.shape, o_ref.dtype)


def _moe_up(blk_e, nvalid, tok_buf, h2p, w1, b1, tn):
    n_blocks = tok_buf.shape[0]
    hd = h2p.shape[1]
    ne, d, f2 = w1.shape
    assert d == 2 * hd
    f = f2 // 2
    half = V7X_MXU // 2
    c = jnp.arange(V7X_MXU)
    perm = jnp.zeros((V7X_MXU, V7X_MXU), BF16).at[c, (c % 2) * half + c // 2].set(1.0)
    b1g = b1[:, 0::2].reshape(ne, 1, f)
    b1l = b1[:, 1::2].reshape(ne, 1, f)
    tok3 = tok_buf.reshape(n_blocks, 1, MOE_BLOCK)
    tok_spec = lambda shift: pl.BlockSpec((1, 1, MOE_BLOCK), lambda j, i, e, nv: ((i + shift) % n_blocks, 0, 0),
                                          memory_space=pltpu.SMEM)
    return pl.pallas_call(
        _up_kernel,
        grid_spec=pltpu.PrefetchScalarGridSpec(
            num_scalar_prefetch=2,
            grid=(f2 // tn, n_blocks),
            in_specs=[
                tok_spec(0), tok_spec(1),
                pl.BlockSpec(memory_space=pl.ANY),
                pl.BlockSpec((1, d, tn), lambda j, i, e, nv: (e[i], 0, j)),
                pl.BlockSpec((1, 1, tn // 2), lambda j, i, e, nv: (e[i], 0, j)),
                pl.BlockSpec((1, 1, tn // 2), lambda j, i, e, nv: (e[i], 0, j)),
                pl.BlockSpec((V7X_MXU, V7X_MXU), lambda j, i, e, nv: (0, 0)),
            ],
            out_specs=pl.BlockSpec((MOE_BLOCK, tn // 2), lambda j, i, e, nv: (i, j)),
            scratch_shapes=[pltpu.VMEM((d, tn), BF16), pltpu.VMEM((2, MOE_BLOCK, hd), h2p.dtype),
                            pltpu.SemaphoreType.DMA((2,))],
        ),
        out_shape=jax.ShapeDtypeStruct((n_blocks * MOE_BLOCK, f), BF16),
        compiler_params=_cparams(("arbitrary", "arbitrary")),
        name="moe_up",
    )(blk_e, nvalid, tok3, tok3, h2p, w1, b1g, b1l, perm)


def _down_kernel(e_ref, nv_ref, a_ref, w_ref, b_ref, g_ref, o_ref, wb_ref):
    i = pl.program_id(0)

    @pl.when(nv_ref[i] > 0)
    def _():
        @pl.when(_expert_changed(e_ref, i))
        def _():
            wb_ref[...] = w_ref[0].astype(BF16)

        y = jnp.dot(a_ref[...], wb_ref[...], preferred_element_type=F32) + b_ref[0]
        o_ref[...] = y * g_ref[...]

    @pl.when(nv_ref[i] == 0)
    def _():
        o_ref[...] = jnp.zeros(o_ref.shape, o_ref.dtype)


def _moe_down(blk_e, nvalid, act, w2, b2, g_buf):
    n_rows, f = act.shape
    ne, _, d = w2.shape
    n_blocks = n_rows // MOE_BLOCK
    return pl.pallas_call(
        _down_kernel,
        grid_spec=pltpu.PrefetchScalarGridSpec(
            num_scalar_prefetch=2,
            grid=(n_blocks,),
            in_specs=[
                pl.BlockSpec((MOE_BLOCK, f), lambda i, e, nv: (i, 0)),
                pl.BlockSpec((1, f, d), lambda i, e, nv: (e[i], 0, 0)),
                pl.BlockSpec((1, 1, d), lambda i, e, nv: (e[i], 0, 0)),
                pl.BlockSpec((MOE_BLOCK, 1), lambda i, e, nv: (i, 0)),
            ],
            out_specs=pl.BlockSpec((MOE_BLOCK, d), lambda i, e, nv: (i, 0)),
            scratch_shapes=[pltpu.VMEM((f, d), BF16)],
        ),
        out_shape=jax.ShapeDtypeStruct((n_rows, d), F32),
        compiler_params=_cparams(("arbitrary",)),
        name="moe_down",
    )(blk_e, nvalid, act, w2, b2.reshape(ne, 1, d), g_buf.reshape(n_rows, 1))


def _combine_kernel(pos_ref, y_hbm, x1_ref, ada_ref, g_ref, o_ref, buf, sems, *, tt):
    i = pl.program_id(0)
    n_tiles = pl.num_programs(0) - 1
    n_rows = TOP_K * tt

    def finish(slot):
        f = buf[slot, 0:tt, :]
        for k in range(1, TOP_K):
            f = f + buf[slot, k * tt:(k + 1) * tt, :]
        o_ref[...] = x1_ref[...] + ada_ref[0, 5:6, :] * _rms(f, g_ref[...])

    @pl.when(i == 0)
    def _():
        _start_rows(pos_ref, y_hbm, buf, sems, 0, n_rows)

    @pl.when(jnp.logical_and(i > 0, i < n_tiles))
    def _():
        _wait_rows(pos_ref, y_hbm, buf, sems, (i - 1) % 2, n_rows)
        _start_rows(pos_ref, y_hbm, buf, sems, i % 2, n_rows, unrolled=True)
        finish((i - 1) % 2)

    @pl.when(i == n_tiles)
    def _():
        _wait_rows(pos_ref, y_hbm, buf, sems, (i - 1) % 2, n_rows)
        finish((i - 1) % 2)


def _moe_combine(pos, y_buf, x1, ada3, post_ffn_g, seq, tt):
    t, d = x1.shape
    per_b = seq // tt
    nt = t // tt
    pos_t = pos.reshape(nt, tt, TOP_K).transpose(0, 2, 1).reshape(nt, 1, TOP_K * tt)
    prev = lambda i: jnp.maximum(i - 1, 0)
    return pl.pallas_call(
        functools.partial(_combine_kernel, tt=tt),
        grid=(nt + 1,),
        in_specs=[
            pl.BlockSpec((1, 1, TOP_K * tt), lambda i: (jnp.minimum(i, nt - 1), 0, 0), memory_space=pltpu.SMEM),
            pl.BlockSpec(memory_space=pl.ANY),
            pl.BlockSpec((tt, d), lambda i: (prev(i), 0)),
            pl.BlockSpec((1, N_ADA, d), lambda i: (prev(i) // per_b, 0, 0)),
            pl.BlockSpec((1, d), lambda i: (0, 0)),
        ],
        out_specs=pl.BlockSpec((tt, d), lambda i: (prev(i), 0)),
        out_shape=jax.ShapeDtypeStruct((t, d), F32),
        scratch_shapes=[pltpu.VMEM((2, TOP_K * tt, d), F32), pltpu.SemaphoreType.DMA((2,))],
        compiler_params=_cparams(("arbitrary",)),
        name="moe_combine",
    )(pos_t, y_buf, x1, ada3, post_ffn_g.reshape(1, d))


def _route(top_idx, gates, n_experts):
    n_tok = top_idx.shape[0]
    n_asg = n_tok * TOP_K
    i32 = jnp.int32
    flat_e = top_idx.reshape(n_asg)
    order = jnp.argsort(flat_e).astype(i32)
    counts = jnp.sum(flat_e[:, None] == jnp.arange(n_experts, dtype=i32)[None, :], axis=0, dtype=i32)
    starts = jnp.cumsum(counts) - counts
    padded = (counts + MOE_BLOCK - 1) // MOE_BLOCK * MOE_BLOCK
    pends = jnp.cumsum(padded)
    pstarts = pends - padded
    n_blocks = -(-n_asg // MOE_BLOCK) + n_experts + 1
    blk_start = jnp.arange(n_blocks, dtype=i32) * MOE_BLOCK
    blk_e = jnp.minimum(jnp.sum(pends[None, :] <= blk_start[:, None], axis=1, dtype=i32), n_experts - 1)
    off = blk_start - pstarts[blk_e]
    nvalid = jnp.clip(counts[blk_e] - off, 0, MOE_BLOCK)
    j = off[:, None] + jnp.arange(MOE_BLOCK, dtype=i32)[None, :]
    valid = j < counts[blk_e][:, None]
    src = jnp.clip(starts[blk_e][:, None] + j, 0, n_asg - 1)
    asg = order[src]
    tok_buf = jnp.where(valid, asg // TOP_K, 0)
    g_buf = jnp.where(valid, gates.reshape(n_asg)[asg], 0.0)
    rank = jnp.argsort(order).astype(i32)
    pos = pstarts[flat_e] + rank - starts[flat_e]
    return tok_buf, g_buf, pos, blk_e, nvalid, n_blocks


def _pick(n, pref):
    return pref if n % pref == 0 else n


def kernel(x, c, ada_w, ada_b, pre_mix_g, post_mix_g, pre_ffn_g, post_ffn_g, w_in, gate_b, dw_w, dw_b, cln_g, cln_b, cp_w, cp_b, s5_a_re, s5_a_im, s5_log_dt, s5_b_re, s5_b_im, s5_c_re, s5_c_im, s5_d, glu_wa, glu_wb, w_out, router_w, router_b, w1, b1, w2, b2):
    bsz, seq, d = x.shape
    t = bsz * seq
    depth = ada_w.shape[0]
    conv_ch = dw_w.shape[-1]
    ng, ns, nh = s5_b_re.shape[1:]
    s5_w = ng * nh
    ne = router_w.shape[-1]
    col_s5 = 2 * conv_ch
    col_gate = col_s5 + s5_w
    assert conv_ch == s5_w and V7X_MXU % nh == 0 and ng % (V7X_MXU // nh) == 0
    tm = _pick(seq, 512)
    tm_out = _pick(seq, 256)
    tn_in = _pick(w_in.shape[-1], conv_ch)

    x2 = x.reshape(t, d)
    for l in range(depth):
        ada3 = _ada(c, ada_w[l], ada_b[l]).reshape(bsz, N_ADA, d)
        proj = _inproj(x2, ada3, pre_mix_g[l], w_in[l].astype(BF16), seq, _pick(seq, 1024), tn_in)
        vc = _conv_branch(proj, dw_w[l], dw_b[l], cln_g[l], cln_b[l], bsz, seq, tm)

        s5_params = _s5_params(s5_a_re[l], s5_a_im[l], s5_log_dt[l], s5_b_re[l], s5_b_im[l],
                               s5_c_re[l], s5_c_im[l], s5_d[l])
        ys = _s5_branch(proj, col_s5, s5_params, bsz, seq)

        merged = _merge(vc, ys, proj, cp_w[l], cp_b[l], glu_wa[l], glu_wb[l], gate_b[l], tm, col_gate)
        x1, h2p, top_idx, gates = _outproj(merged, x2, ada3, w_out[l], post_mix_g[l], pre_ffn_g[l],
                                           router_w[l], router_b[l], seq, tm_out)

        tok_buf, g_buf, pos, blk_e, nvalid, n_blocks = _route(top_idx, gates, ne)
        act = _moe_up(blk_e, nvalid, tok_buf, h2p, w1[l], b1[l], _pick(w1.shape[-1], 2048))
        y_buf = _moe_down(blk_e, nvalid, act, w2[l], b2[l], g_buf)
        x2 = _moe_combine(pos, y_buf, x1, ada3, post_ffn_g[l], seq, _pick(seq, 128))
    return x2.reshape(bsz, seq, d)
```

```python
import functools
import math

import jax
import jax.numpy as jnp
from jax import lax
from jax.experimental import pallas as pl
from jax.experimental.pallas import tpu as pltpu

EPS = 1e-6
N_ADA = 6
TOP_K = 4
MOE_BLOCK = 256
SWIGLU_ALPHA = 1.702
SWIGLU_LIMIT = 7.0
S5_CHUNK = 16
V7X_LANES = 128
V7X_SUBLANES = 8
V7X_MXU = 256
VMEM_LIMIT = 56 * 1024 * 1024

F32 = jnp.float32
BF16 = jnp.bfloat16


def _cparams(sem):
    return pltpu.CompilerParams(dimension_semantics=sem, vmem_limit_bytes=VMEM_LIMIT)


def _sigmoid(v):
    return 1.0 / (1.0 + jnp.exp(-v))


def _rms(v, g):
    return v * lax.rsqrt(jnp.mean(v * v, axis=-1, keepdims=True) + EPS) * g


def _ada_kernel(c_ref, w_ref, b_ref, o_ref):
    c = c_ref[...]
    s = (c * _sigmoid(c)).astype(BF16)
    o_ref[...] = jnp.dot(s, w_ref[...].astype(BF16), preferred_element_type=F32) + b_ref[...]


def _ada(c, ada_w, ada_b):
    bsz, d = c.shape
    n = ada_w.shape[1]
    tn = 1024 if n % 1024 == 0 else n
    return pl.pallas_call(
        _ada_kernel,
        grid=(n // tn,),
        in_specs=[
            pl.BlockSpec((bsz, d), lambda j: (0, 0)),
            pl.BlockSpec((d, tn), lambda j: (0, j)),
            pl.BlockSpec((1, tn), lambda j: (0, j)),
        ],
        out_specs=pl.BlockSpec((bsz, tn), lambda j: (0, j)),
        out_shape=jax.ShapeDtypeStruct((bsz, n), F32),
        compiler_params=_cparams(("arbitrary",)),
        name="ada",
    )(c, ada_w, ada_b.reshape(1, n))


def _inproj_kernel(x_ref, ada_ref, g_ref, w_ref, o_ref, h_ref):
    @pl.when(pl.program_id(1) == 0)
    def _():
        y = _rms(x_ref[...], g_ref[...])
        h = y * (1.0 + ada_ref[0, 1:2, :]) + ada_ref[0, 0:1, :]
        h_ref[...] = h.astype(BF16)

    o_ref[...] = jnp.dot(h_ref[...], w_ref[...], preferred_element_type=F32).astype(BF16)


def _inproj(x2, ada3, g, w_bf, seq, tm, tn):
    t, d = x2.shape
    n = w_bf.shape[1]
    per_b = seq // tm
    return pl.pallas_call(
        _inproj_kernel,
        grid=(t // tm, n // tn),
        in_specs=[
            pl.BlockSpec((tm, d), lambda i, j: (i, 0)),
            pl.BlockSpec((1, N_ADA, d), lambda i, j: (i // per_b, 0, 0)),
            pl.BlockSpec((1, d), lambda i, j: (0, 0)),
            pl.BlockSpec((d, tn), lambda i, j: (0, j)),
        ],
        out_specs=pl.BlockSpec((tm, tn), lambda i, j: (i, j)),
        out_shape=jax.ShapeDtypeStruct((t, n), BF16),
        scratch_shapes=[pltpu.VMEM((tm, d), BF16)],
        compiler_params=_cparams(("arbitrary", "arbitrary")),
        name="inproj",
    )(x2, ada3, g.reshape(1, d), w_bf)


CONV_HALO = 32
CONV_ROWS = 64


def _conv_kernel(pv_ref, pg_ref, w_ref, b_ref, g_ref, beta_ref, o_ref, vext, vsh, cbuf, *, taps, tl):
    t = pl.program_id(1)
    ch = cbuf.shape[1]

    @pl.when(t == 0)
    def _():
        vext[0:CONV_HALO, :] = jnp.zeros((CONV_HALO, ch), F32)

    @pl.when(t > 0)
    def _():
        vext[0:CONV_HALO, :] = vext[tl:tl + CONV_HALO, :]

    pv = pv_ref[...].astype(F32)
    pg = pg_ref[...].astype(F32)
    vext[CONV_HALO:CONV_HALO + tl, :] = pv * _sigmoid(pg)

    n_sh = vsh.shape[1]
    for p in range(1, V7X_SUBLANES):
        vsh[p - 1] = vext[p:p + n_sh, :]

    off = CONV_HALO - (taps - 1)
    for cb in range(ch // V7X_LANES):
        lanes = slice(cb * V7X_LANES, (cb + 1) * V7X_LANES)

        for rc in range(tl // CONV_ROWS):
            r0 = rc * CONV_ROWS
            acc = jnp.zeros((CONV_ROWS, V7X_LANES), F32)
            for k in range(taps):
                p = (off + k) % V7X_SUBLANES
                base = r0 + off + k - p
                win = vext[base:base + CONV_ROWS, lanes] if p == 0 else vsh[p - 1, base:base + CONV_ROWS, lanes]
                acc = acc + w_ref[k:k + 1, lanes] * win
            cbuf[r0:r0 + CONV_ROWS, lanes] = acc + b_ref[:, lanes]

    v = cbuf[...]
    mu = jnp.mean(v, axis=-1, keepdims=True)
    xc = v - mu
    var = jnp.mean(xc * xc, axis=-1, keepdims=True)
    y = xc * lax.rsqrt(var + EPS) * g_ref[...] + beta_ref[...]
    o_ref[...] = (y * _sigmoid(y)).astype(BF16)


def _conv_branch(proj, dw_w, dw_b, cln_g, cln_b, bsz, seq, tl):
    taps, ch = dw_w.shape
    assert taps - 1 <= CONV_HALO and tl % CONV_ROWS == 0 and ch % V7X_LANES == 0
    per_b = seq // tl
    vec = lambda a: a.reshape(1, ch)
    return pl.pallas_call(
        functools.partial(_conv_kernel, taps=taps, tl=tl),
        grid=(bsz, per_b),
        in_specs=[
            pl.BlockSpec((tl, ch), lambda b, t: (b * per_b + t, 0)),
            pl.BlockSpec((tl, ch), lambda b, t: (b * per_b + t, 1)),
            pl.BlockSpec((taps, ch), lambda b, t: (0, 0)),
            pl.BlockSpec((1, ch), lambda b, t: (0, 0)),
            pl.BlockSpec((1, ch), lambda b, t: (0, 0)),
            pl.BlockSpec((1, ch), lambda b, t: (0, 0)),
        ],
        out_specs=pl.BlockSpec((tl, ch), lambda b, t: (b * per_b + t, 0)),
        out_shape=jax.ShapeDtypeStruct((bsz * seq, ch), BF16),
        scratch_shapes=[pltpu.VMEM((CONV_HALO + tl, ch), F32),
                        pltpu.VMEM((V7X_SUBLANES - 1, CONV_HALO + tl - V7X_SUBLANES, ch), F32),
                        pltpu.VMEM((tl, ch), F32)],
        compiler_params=_cparams(("arbitrary", "arbitrary")),
        name="conv",
    )(proj, proj, dw_w, vec(dw_b), vec(cln_g), vec(cln_b))


S5_SUPER = 16


def _s5_params(a_re, a_im, log_dt, b_re, b_im, c_re, c_im, d_skip):
    ng, ns = a_re.shape
    nh = b_re.shape[-1]
    gt = V7X_MXU // nh
    nj = ng // gt
    dt = jnp.exp(log_dt)[:, None]
    mag = jnp.exp(a_re * dt)
    abar_re = mag * jnp.cos(a_im * dt)
    abar_im = mag * jnp.sin(a_im * dt)
    den = a_re * a_re + a_im * a_im
    num_re = abar_re - 1.0
    coef_re = (num_re * a_re + abar_im * a_im) / den
    coef_im = (abar_im * a_re - num_re * a_im) / den
    bb_re = coef_re[:, :, None] * b_re - coef_im[:, :, None] * b_im
    bb_im = coef_re[:, :, None] * b_im + coef_im[:, :, None] * b_re
    p = jnp.asarray([1.0, S5_CHUNK, S5_CHUNK * S5_SUPER] + [S5_CHUNK * r for r in range(S5_SUPER)], F32)
    pmag = jnp.exp(p[:, None, None] * (a_re * dt)[None])
    pw_re = pmag * jnp.cos(p[:, None, None] * (a_im * dt)[None])
    pw_im = pmag * jnp.sin(p[:, None, None] * (a_im * dt)[None])
    tile = lambda a: a.reshape(-1, nj, gt * ns).transpose(1, 0, 2)
    apow = jnp.concatenate([tile(pw_re), tile(pw_im)], axis=1)
    eye = jnp.eye(gt, dtype=F32)

    def bdiag_in(bb):
        return jnp.einsum('jgnh,gk->jghkn', bb.reshape(nj, gt, ns, nh), eye).reshape(nj, gt * nh, gt * ns)

    def bdiag_out(cc):
        return jnp.einsum('jghn,gk->jgnkh', cc.reshape(nj, gt, nh, ns), eye).reshape(nj, gt * ns, gt * nh)

    bdb = jnp.stack([bdiag_in(bb_re), bdiag_in(bb_im)], axis=1)
    bdc = jnp.stack([bdiag_out(c_re), bdiag_out(-c_im)], axis=1)
    return bdb.astype(BF16), bdc.astype(BF16), apow, d_skip.reshape(nj, 1, gt * nh)


def _cmul_add(ar, ai, xr, xi, vr, vi):
    return ar * xr - ai * xi + vr, ar * xi + ai * xr + vi


def _s5_kernel(u_ref, bdb_ref, bdc_ref, ap_ref, d_ref, p1_ref, p1t_ref, p2_ref, p2t_ref, o_ref,
               u2n_ref, u2_ref, hr_ref, hi_ref, zr_ref, zi_ref, gr_ref, gi_ref, y2_ref, y2n_ref,
               *, n_super):
    tc, nr, nm = S5_CHUNK, S5_SUPER, n_super
    tile = tc * nr
    np_ = 3 + nr
    apow = lambda k: (ap_ref[0, k:k + 1, :], ap_ref[0, np_ + k:np_ + k + 1, :])
    ar, ai = apow(0)

    for m in range(nm):
        pu = jnp.dot(p1_ref[...], u_ref[m * tile:(m + 1) * tile, :], preferred_element_type=F32).astype(BF16)
        for s in range(tc):
            u2n_ref[s, m * nr:(m + 1) * nr, :] = pu[s * nr:(s + 1) * nr, :]
    for s in range(tc):
        u2_ref[s] = jnp.dot(p2_ref[...], u2n_ref[s], preferred_element_type=F32).astype(BF16)

    def drive(s):
        u = u2_ref[s]
        return (jnp.dot(u, bdb_ref[0, 0], preferred_element_type=F32),
                jnp.dot(u, bdb_ref[0, 1], preferred_element_type=F32))

    hr_ref[...] = jnp.zeros(hr_ref.shape, F32)
    hi_ref[...] = jnp.zeros(hi_ref.shape, F32)

    def pass1(s, carry):
        vr, vi = drive(s)
        hr, hi = _cmul_add(ar, ai, hr_ref[...], hi_ref[...], vr, vi)
        hr_ref[...] = hr
        hi_ref[...] = hi
        return carry

    lax.fori_loop(0, tc, pass1, 0, unroll=2)

    a16r, a16i = apow(1)
    qr = jnp.zeros((nm, hr_ref.shape[1]), F32)
    qi = qr
    for r in range(nr):
        rows = slice(r * nm, (r + 1) * nm)
        zr_ref[rows, :] = qr
        zi_ref[rows, :] = qi
        qr, qi = _cmul_add(a16r, a16i, qr, qi, hr_ref[rows, :], hi_ref[rows, :])
    a256r, a256i = apow(2)
    gr = jnp.zeros((1, hr_ref.shape[1]), F32)
    gi = gr
    for m in range(nm):
        gr_ref[m:m + 1, :] = gr
        gi_ref[m:m + 1, :] = gi
        gr, gi = _cmul_add(a256r, a256i, gr, gi, qr[m:m + 1, :], qi[m:m + 1, :])
    gpr = gr_ref[...]
    gpi = gi_ref[...]
    for r in range(nr):
        rows = slice(r * nm, (r + 1) * nm)
        pr, pi = apow(3 + r)
        zr, zi = _cmul_add(pr, pi, gpr, gpi, zr_ref[rows, :], zi_ref[rows, :])
        zr_ref[rows, :] = zr
        zi_ref[rows, :] = zi

    def pass2(t, carry):
        vr, vi = drive(t)
        sr, si = _cmul_add(ar, ai, zr_ref[...], zi_ref[...], vr, vi)
        zr_ref[...] = sr
        zi_ref[...] = si
        y = (jnp.dot(sr.astype(BF16), bdc_ref[0, 0], preferred_element_type=F32)
             + jnp.dot(si.astype(BF16), bdc_ref[0, 1], preferred_element_type=F32))
        y = y + d_ref[0] * u2_ref[t].astype(F32)
        y = 0.5 * y * (1.0 + jnp.tanh(math.sqrt(2.0 / math.pi) * (y + 0.044715 * (y * y * y))))
        y2_ref[t] = y.astype(BF16)
        return carry

    lax.fori_loop(0, tc, pass2, 0, unroll=2)

    for t in range(tc):
        yn = jnp.dot(p2t_ref[...], y2_ref[t], preferred_element_type=F32).astype(BF16)
        for m in range(nm):
            y2n_ref[m * tile + t * nr:m * tile + (t + 1) * nr, :] = yn[m * nr:(m + 1) * nr, :]
    for m in range(nm):
        rows = slice(m * tile, (m + 1) * tile)
        o_ref[rows, :] = jnp.dot(p1t_ref[...], y2n_ref[rows, :], preferred_element_type=F32).astype(BF16)


def _s5_branch(proj, col0, params, bsz, seq):
    bdb, bdc, apow, dskip = params
    nj, _, w, sl = bdb.shape
    tile = S5_CHUNK * S5_SUPER
    assert seq % tile == 0 and col0 % w == 0
    nm = seq // tile
    assert nm % V7X_SUBLANES == 0, "row slabs of the chunk recurrence must be whole sublane tiles"
    rows = S5_SUPER * nm
    i1 = jnp.arange(tile)
    p1 = jnp.zeros((tile, tile), BF16).at[(i1 % S5_CHUNK) * S5_SUPER + i1 // S5_CHUNK, i1].set(1.0)
    i2 = jnp.arange(rows)
    p2 = jnp.zeros((rows, rows), BF16).at[(i2 % S5_SUPER) * nm + i2 // S5_SUPER, i2].set(1.0)
    const = lambda a: pl.BlockSpec(a.shape, lambda b, j: (0,) * a.ndim)
    per_j = lambda a: pl.BlockSpec((1,) + a.shape[1:], lambda b, j: (j,) + (0,) * (a.ndim - 1))
    return pl.pallas_call(
        functools.partial(_s5_kernel, n_super=nm),
        grid=(bsz, nj),
        in_specs=[
            pl.BlockSpec((seq, w), lambda b, j: (b, col0 // w + j)),
            per_j(bdb), per_j(bdc), per_j(apow), per_j(dskip),
            const(p1), const(p1), const(p2), const(p2),
        ],
        out_specs=pl.BlockSpec((seq, w), lambda b, j: (b, j)),
        out_shape=jax.ShapeDtypeStruct((bsz * seq, nj * w), BF16),
        scratch_shapes=[
            pltpu.VMEM((S5_CHUNK, rows, w), BF16), pltpu.VMEM((S5_CHUNK, rows, w), BF16),
            pltpu.VMEM((rows, sl), F32), pltpu.VMEM((rows, sl), F32),
            pltpu.VMEM((rows, sl), F32), pltpu.VMEM((rows, sl), F32),
            pltpu.VMEM((nm, sl), F32), pltpu.VMEM((nm, sl), F32),
            pltpu.VMEM((S5_CHUNK, rows, w), BF16), pltpu.VMEM((seq, w), BF16),
        ],
        compiler_params=_cparams(("arbitrary", "arbitrary")),
        name="s5",
    )(proj, bdb, bdc, apow, dskip, p1, p1.T, p2, p2.T)


def _merge_kernel(vc_ref, ys_ref, l1_ref, l2_ref, cpw_ref, cpb_ref, wa_ref, wb_ref, gb1_ref, gb2_ref, o_ref):
    y_conv = jnp.dot(vc_ref[...], cpw_ref[...], preferred_element_type=F32) + cpb_ref[...]
    ys = ys_ref[...]
    a = jnp.dot(ys, wa_ref[...], preferred_element_type=F32)
    b = jnp.dot(ys, wb_ref[...], preferred_element_type=F32)
    y_s5 = a * _sigmoid(b)
    g1 = _sigmoid(l1_ref[...].astype(F32) + gb1_ref[...])
    g2 = _sigmoid(l2_ref[...].astype(F32) + gb2_ref[...])
    o_ref[...] = (g1 * y_conv + g2 * y_s5).astype(BF16)


def _merge(vc, ys, proj, cp_w, cp_b, wa, wb, gate_b, tm, col0):
    t, ch = vc.shape
    d = cp_w.shape[1]
    tn = ch
    assert col0 % tn == 0 and d % tn == 0
    cb, nj = col0 // tn, d // tn
    wcol = lambda rows: pl.BlockSpec((rows, tn), lambda j, i: (0, j))
    return pl.pallas_call(
        _merge_kernel,
        grid=(nj, t // tm),
        in_specs=[
            pl.BlockSpec((tm, ch), lambda j, i: (i, 0)),
            pl.BlockSpec((tm, ch), lambda j, i: (i, 0)),
            pl.BlockSpec((tm, tn), lambda j, i: (i, cb + j)),
            pl.BlockSpec((tm, tn), lambda j, i: (i, cb + nj + j)),
            wcol(ch), wcol(1), wcol(ch), wcol(ch), wcol(1), wcol(1),
        ],
        out_specs=pl.BlockSpec((tm, tn), lambda j, i: (i, j)),
        out_shape=jax.ShapeDtypeStruct((t, d), BF16),
        compiler_params=_cparams(("arbitrary", "arbitrary")),
        name="merge",
    )(vc, ys, proj, proj, cp_w.astype(BF16), cp_b.reshape(1, d), wa.astype(BF16), wb.astype(BF16),
      gate_b[:d].reshape(1, d), gate_b[d:].reshape(1, d))


def _split_bf16(v):
    hi = v.astype(BF16)
    lo = (v - hi.astype(F32)).astype(BF16)
    return hi, lo


def _bf16_bits_hi(v):
    b = lax.bitcast_convert_type(v, jnp.uint32)
    return (b + jnp.uint32(0x7FFF) + ((b >> 16) & jnp.uint32(1))) & jnp.uint32(0xFFFF0000)


def _pack_bf16_pairs(v):
    half = v.shape[-1] // 2
    return _bf16_bits_hi(v[:, half:]) | (_bf16_bits_hi(v[:, :half]) >> 16)


def _unpack_bf16_pairs(p):
    lo = lax.bitcast_convert_type(p << 16, F32).astype(BF16)
    hi = lax.bitcast_convert_type(p & jnp.uint32(0xFFFF0000), F32).astype(BF16)
    return lo, hi


OUTPROJ_ROWS = 128


def _outproj_kernel(m_ref, x_ref, ada_ref, wo_ref, g1_ref, g2_ref, rwh_ref, rwl_ref, rb_ref,
                    x1_ref, h2p_ref, idx_ref, gate_ref, *, ne):
    for rb in range(m_ref.shape[0] // OUTPROJ_ROWS):
        rows = slice(rb * OUTPROJ_ROWS, (rb + 1) * OUTPROJ_ROWS)
        _outproj_rows(m_ref[rows, :], x_ref[rows, :], ada_ref, wo_ref, g1_ref, g2_ref, rwh_ref, rwl_ref, rb_ref,
                      x1_ref.at[rows, :], h2p_ref.at[rows, :], idx_ref.at[rows, :], gate_ref.at[rows, :], ne)


def _outproj_rows(m_in, x_in, ada_ref, wo_ref, g1_ref, g2_ref, rwh_ref, rwl_ref, rb_ref,
                  x1_ref, h2p_ref, idx_ref, gate_ref, ne):
    m = jnp.dot(m_in, wo_ref[...], preferred_element_type=F32)
    x1 = x_in + ada_ref[0, 2:3, :] * _rms(m, g1_ref[...])
    x1_ref[...] = x1
    h2 = _rms(x1, g2_ref[...]) * (1.0 + ada_ref[0, 4:5, :]) + ada_ref[0, 3:4, :]
    h2p_ref[...] = _pack_bf16_pairs(h2)
    hh, hl = _split_bf16(h2)
    lg = jnp.dot(hh, rwh_ref[...], preferred_element_type=F32)
    lg = lg + jnp.dot(hl, rwh_ref[...], preferred_element_type=F32)
    lg = lg + jnp.dot(hh, rwl_ref[...], preferred_element_type=F32)
    lg = lg + rb_ref[...]
    lane = lax.broadcasted_iota(jnp.int32, lg.shape, 1)
    work = jnp.where(lane < ne, lg, -jnp.inf)
    vals, idxs = [], []
    for _ in range(TOP_K):
        mx = jnp.max(work, axis=-1, keepdims=True)
        ix = jnp.min(jnp.where(work == mx, lane, lg.shape[1]), axis=-1, keepdims=True)
        vals.append(mx)
        idxs.append(ix)
        work = jnp.where(lane == ix, -jnp.inf, work)
    ex = [jnp.exp(v - vals[0]) for v in vals]
    den = ex[0]
    for e in ex[1:]:
        den = den + e
    idx_out = jnp.zeros(lg.shape, jnp.int32)
    gate_out = jnp.zeros(lg.shape, F32)
    for k in range(TOP_K):
        idx_out = jnp.where(lane == k, idxs[k], idx_out)
        gate_out = jnp.where(lane == k, ex[k] / den, gate_out)
    idx_ref[...] = idx_out
    gate_ref[...] = gate_out


def _outproj(merged, x2, ada3, w_out, post_mix_g, pre_ffn_g, router_w, router_b, seq, tm):
    t, d = x2.shape
    ne = router_w.shape[1]
    nep = max(V7X_LANES, ne)
    rw = jnp.zeros((d, nep), F32).at[:, :ne].set(router_w)
    rw_hi, rw_lo = _split_bf16(rw)
    rb = jnp.zeros((1, nep), F32).at[0, :ne].set(router_b)
    per_b = seq // tm
    full = lambda shape: pl.BlockSpec(shape, lambda i: (0, 0))
    row = lambda w: pl.BlockSpec((tm, w), lambda i: (i, 0))
    x1, h2p, top_idx, gates = pl.pallas_call(
        functools.partial(_outproj_kernel, ne=ne),
        grid=(t // tm,),
        in_specs=[
            row(d), row(d),
            pl.BlockSpec((1, N_ADA, d), lambda i: (i // per_b, 0, 0)),
            full((d, d)), full((1, d)), full((1, d)), full((d, nep)), full((d, nep)), full((1, nep)),
        ],
        out_specs=[row(d), row(d // 2), row(nep), row(nep)],
        out_shape=[jax.ShapeDtypeStruct((t, d), F32), jax.ShapeDtypeStruct((t, d // 2), jnp.uint32),
                   jax.ShapeDtypeStruct((t, nep), jnp.int32), jax.ShapeDtypeStruct((t, nep), F32)],
        compiler_params=_cparams(("arbitrary",)),
        name="outproj",
    )(merged, x2, ada3, w_out.astype(BF16), post_mix_g.reshape(1, d), pre_ffn_g.reshape(1, d),
      rw_hi, rw_lo, rb)
    return x1, h2p, top_idx[:, :TOP_K], gates[:, :TOP_K]


GATHER_UNROLL = 8


def _row_copy(idx_ref, src_hbm, buf, sems, slot, r):
    return pltpu.make_async_copy(src_hbm.at[pl.ds(idx_ref[0, 0, r], 1), :],
                                 buf.at[slot, pl.ds(r, 1), :], sems.at[slot])


def _start_rows(idx_ref, src_hbm, buf, sems, slot, n, unrolled=False):
    if unrolled:
        for r in range(n):
            _row_copy(idx_ref, src_hbm, buf, sems, slot, r).start()
        return

    def start(r, carry):
        _row_copy(idx_ref, src_hbm, buf, sems, slot, r).start()
        return carry
    lax.fori_loop(0, n, start, 0, unroll=GATHER_UNROLL)


def _wait_rows(idx_ref, src_hbm, buf, sems, slot, n):
    def wait(r, carry):
        _row_copy(idx_ref, src_hbm, buf, sems, slot, r).wait()
        return carry
    lax.fori_loop(0, n, wait, 0, unroll=GATHER_UNROLL)


def _expert_changed(e_ref, i):
    return jnp.logical_or(i == 0, e_ref[i] != e_ref[jnp.maximum(i - 1, 0)])


def _gather_kernel(tok_ref, h_hbm, o_ref, buf, sems):
    i = pl.program_id(0)
    n_blocks = pl.num_programs(0) - 1

    @pl.when(i < n_blocks)
    def _():
        _start_rows(tok_ref, h_hbm, buf, sems, i % 2, MOE_BLOCK, unrolled=True)

    @pl.when(i > 0)
    def _():
        slot = (i - 1) % 2
        _wait_rows(tok_ref, h_hbm, buf, sems, slot, MOE_BLOCK)
        o_ref[...] = buf[slot]


def _moe_gather(tok_buf, h2p):
    n_blocks = tok_buf.shape[0]
    t, w = h2p.shape
    return pl.pallas_call(
        _gather_kernel,
        grid=(n_blocks + 1,),
        in_specs=[
            pl.BlockSpec((1, 1, MOE_BLOCK), lambda i: (jnp.minimum(i, n_blocks - 1), 0, 0),
                         memory_space=pltpu.SMEM),
            pl.BlockSpec(memory_space=pl.ANY),
        ],
        out_specs=pl.BlockSpec((MOE_BLOCK, w), lambda i: (jnp.maximum(i - 1, 0), 0)),
        out_shape=jax.ShapeDtypeStruct((n_blocks * MOE_BLOCK, w), h2p.dtype),
        scratch_shapes=[pltpu.VMEM((2, MOE_BLOCK, w), h2p.dtype), pltpu.SemaphoreType.DMA((2,))],
        compiler_params=_cparams(("arbitrary",)),
        name="moe_gather",
    )(tok_buf.reshape(n_blocks, 1, MOE_BLOCK), h2p)


def _up_kernel(e_ref, nv_ref, x_ref, w_ref, bg_ref, bl_ref, p_ref, o_ref, wp_ref):
    i = pl.program_id(1)
    tn = wp_ref.shape[1]
    half = V7X_MXU // 2
    hd = x_ref.shape[1]

    @pl.when(jnp.logical_and(_expert_changed(e_ref, i), nv_ref[i] > 0))
    def _():
        for cb in range(tn // V7X_MXU):
            cols = slice(cb * V7X_MXU, (cb + 1) * V7X_MXU)
            wp_ref[:, cols] = jnp.dot(w_ref[0, :, cols].astype(BF16), p_ref[...],
                                      preferred_element_type=F32).astype(BF16)

    @pl.when(nv_ref[i] > 0)
    def _():
        x_lo, x_hi = _unpack_bf16_pairs(x_ref[...])
        for cb in range(tn // V7X_MXU):
            cols = slice(cb * V7X_MXU, (cb + 1) * V7X_MXU)
            hb = (jnp.dot(x_lo, wp_ref[0:hd, cols], preferred_element_type=F32)
                  + jnp.dot(x_hi, wp_ref[hd:2 * hd, cols], preferred_element_type=F32))
            out_cols = slice(cb * half, (cb + 1) * half)
            x_glu = hb[:, :half] + bg_ref[0, :, out_cols]
            x_lin = hb[:, half:] + bl_ref[0, :, out_cols]
            x_glu = jnp.minimum(x_glu, SWIGLU_LIMIT)
            x_lin = jnp.clip(x_lin, -SWIGLU_LIMIT, SWIGLU_LIMIT)
            act = x_glu * _sigmoid(SWIGLU_ALPHA * x_glu) * (x_lin + 1.0)
            o_ref[:, out_cols] = act.astype(BF16)

    @pl.when(nv_ref[i] == 0)
    def _():
        o_ref[...] = jnp.zeros(o_ref.shape, o_ref.dtype)


def _moe_up(blk_e, nvalid, xs, w1, b1, tn):
    n_rows, hd = xs.shape
    ne, d, f2 = w1.shape
    assert d == 2 * hd
    f = f2 // 2
    n_blocks = n_rows // MOE_BLOCK
    half = V7X_MXU // 2
    c = jnp.arange(V7X_MXU)
    perm = jnp.zeros((V7X_MXU, V7X_MXU), BF16).at[c, (c % 2) * half + c // 2].set(1.0)
    b1g = b1[:, 0::2].reshape(ne, 1, f)
    b1l = b1[:, 1::2].reshape(ne, 1, f)
    return pl.pallas_call(
        _up_kernel,
        grid_spec=pltpu.PrefetchScalarGridSpec(
            num_scalar_prefetch=2,
            grid=(f2 // tn, n_blocks),
            in_specs=[
                pl.BlockSpec((MOE_BLOCK, hd), lambda j, i, e, nv: (i, 0)),
                pl.BlockSpec((1, d, tn), lambda j, i, e, nv: (e[i], 0, j)),
                pl.BlockSpec((1, 1, tn // 2), lambda j, i, e, nv: (e[i], 0, j)),
                pl.BlockSpec((1, 1, tn // 2), lambda j, i, e, nv: (e[i], 0, j)),
                pl.BlockSpec((V7X_MXU, V7X_MXU), lambda j, i, e, nv: (0, 0)),
            ],
            out_specs=pl.BlockSpec((MOE_BLOCK, tn // 2), lambda j, i, e, nv: (i, j)),
            scratch_shapes=[pltpu.VMEM((d, tn), BF16)],
        ),
        out_shape=jax.ShapeDtypeStruct((n_rows, f), BF16),
        compiler_params=_cparams(("arbitrary", "arbitrary")),
        name="moe_up",
    )(blk_e, nvalid, xs, w1, b1g, b1l, perm)


def _down_kernel(e_ref, nv_ref, a_ref, w_ref, b_ref, g_ref, o_ref, wb_ref):
    i = pl.program_id(0)

    @pl.when(nv_ref[i] > 0)
    def _():
        @pl.when(_expert_changed(e_ref, i))
        def _():
            wb_ref[...] = w_ref[0].astype(BF16)

        y = jnp.dot(a_ref[...], wb_ref[...], preferred_element_type=F32) + b_ref[0]
        o_ref[...] = y * g_ref[...]

    @pl.when(nv_ref[i] == 0)
    def _():
        o_ref[...] = jnp.zeros(o_ref.shape, o_ref.dtype)


def _moe_down(blk_e, nvalid, act, w2, b2, g_buf):
    n_rows, f = act.shape
    ne, _, d = w2.shape
    n_blocks = n_rows // MOE_BLOCK
    return pl.pallas_call(
        _down_kernel,
        grid_spec=pltpu.PrefetchScalarGridSpec(
            num_scalar_prefetch=2,
            grid=(n_blocks,),
            in_specs=[
                pl.BlockSpec((MOE_BLOCK, f), lambda i, e, nv: (i, 0)),
                pl.BlockSpec((1, f, d), lambda i, e, nv: (e[i], 0, 0)),
                pl.BlockSpec((1, 1, d), lambda i, e, nv: (e[i], 0, 0)),
                pl.BlockSpec((MOE_BLOCK, 1), lambda i, e, nv: (i, 0)),
            ],
            out_specs=pl.BlockSpec((MOE_BLOCK, d), lambda i, e, nv: (i, 0)),
            scratch_shapes=[pltpu.VMEM((f, d), BF16)],
        ),
        out_shape=jax.ShapeDtypeStruct((n_rows, d), F32),
        compiler_params=_cparams(("arbitrary",)),
        name="moe_down",
    )(blk_e, nvalid, act, w2, b2.reshape(ne, 1, d), g_buf.reshape(n_rows, 1))


def _combine_kernel(pos_ref, y_hbm, x1_ref, ada_ref, g_ref, o_ref, buf, sems, *, tt):
    i = pl.program_id(0)
    n_tiles = pl.num_programs(0) - 1
    n_rows = TOP_K * tt

    def finish(slot):
        f = buf[slot, 0:tt, :]
        for k in range(1, TOP_K):
            f = f + buf[slot, k * tt:(k + 1) * tt, :]
        o_ref[...] = x1_ref[...] + ada_ref[0, 5:6, :] * _rms(f, g_ref[...])

    @pl.when(i == 0)
    def _():
        _start_rows(pos_ref, y_hbm, buf, sems, 0, n_rows)

    @pl.when(jnp.logical_and(i > 0, i < n_tiles))
    def _():
        _wait_rows(pos_ref, y_hbm, buf, sems, (i - 1) % 2, n_rows)
        _start_rows(pos_ref, y_hbm, buf, sems, i % 2, n_rows, unrolled=True)
        finish((i - 1) % 2)

    @pl.when(i == n_tiles)
    def _():
        _wait_rows(pos_ref, y_hbm, buf, sems, (i - 1) % 2, n_rows)
        finish((i - 1) % 2)


def _moe_combine(pos, y_buf, x1, ada3, post_ffn_g, seq, tt):
    t, d = x1.shape
    per_b = seq // tt
    nt = t // tt
    pos_t = pos.reshape(nt, tt, TOP_K).transpose(0, 2, 1).reshape(nt, 1, TOP_K * tt)
    prev = lambda i: jnp.maximum(i - 1, 0)
    return pl.pallas_call(
        functools.partial(_combine_kernel, tt=tt),
        grid=(nt + 1,),
        in_specs=[
            pl.BlockSpec((1, 1, TOP_K * tt), lambda i: (jnp.minimum(i, nt - 1), 0, 0), memory_space=pltpu.SMEM),
            pl.BlockSpec(memory_space=pl.ANY),
            pl.BlockSpec((tt, d), lambda i: (prev(i), 0)),
            pl.BlockSpec((1, N_ADA, d), lambda i: (prev(i) // per_b, 0, 0)),
            pl.BlockSpec((1, d), lambda i: (0, 0)),
        ],
        out_specs=pl.BlockSpec((tt, d), lambda i: (prev(i), 0)),
        out_shape=jax.ShapeDtypeStruct((t, d), F32),
        scratch_shapes=[pltpu.VMEM((2, TOP_K * tt, d), F32), pltpu.SemaphoreType.DMA((2,))],
        compiler_params=_cparams(("arbitrary",)),
        name="moe_combine",
    )(pos_t, y_buf, x1, ada3, post_ffn_g.reshape(1, d))


def _route(top_idx, gates, n_experts):
    n_tok = top_idx.shape[0]
    n_asg = n_tok * TOP_K
    i32 = jnp.int32
    flat_e = top_idx.reshape(n_asg)
    order = jnp.argsort(flat_e).astype(i32)
    counts = jnp.sum(flat_e[:, None] == jnp.arange(n_experts, dtype=i32)[None, :], axis=0, dtype=i32)
    starts = jnp.cumsum(counts) - counts
    padded = (counts + MOE_BLOCK - 1) // MOE_BLOCK * MOE_BLOCK
    pends = jnp.cumsum(padded)
    pstarts = pends - padded
    n_blocks = -(-n_asg // MOE_BLOCK) + n_experts
    blk_start = jnp.arange(n_blocks, dtype=i32) * MOE_BLOCK
    blk_e = jnp.minimum(jnp.sum(pends[None, :] <= blk_start[:, None], axis=1, dtype=i32), n_experts - 1)
    off = blk_start - pstarts[blk_e]
    nvalid = jnp.clip(counts[blk_e] - off, 0, MOE_BLOCK)
    j = off[:, None] + jnp.arange(MOE_BLOCK, dtype=i32)[None, :]
    valid = j < counts[blk_e][:, None]
    src = jnp.clip(starts[blk_e][:, None] + j, 0, n_asg - 1)
    asg = order[src]
    tok_buf = jnp.where(valid, asg // TOP_K, 0)
    g_buf = jnp.where(valid, gates.reshape(n_asg)[asg], 0.0)
    rank = jnp.argsort(order).astype(i32)
    pos = pstarts[flat_e] + rank - starts[flat_e]
    return tok_buf, g_buf, pos, blk_e, nvalid, n_blocks


def _pick(n, pref):
    return pref if n % pref == 0 else n


def kernel(x, c, ada_w, ada_b, pre_mix_g, post_mix_g, pre_ffn_g, post_ffn_g, w_in, gate_b, dw_w, dw_b, cln_g, cln_b, cp_w, cp_b, s5_a_re, s5_a_im, s5_log_dt, s5_b_re, s5_b_im, s5_c_re, s5_c_im, s5_d, glu_wa, glu_wb, w_out, router_w, router_b, w1, b1, w2, b2):
    bsz, seq, d = x.shape
    t = bsz * seq
    depth = ada_w.shape[0]
    conv_ch = dw_w.shape[-1]
    ng, ns, nh = s5_b_re.shape[1:]
    s5_w = ng * nh
    ne = router_w.shape[-1]
    col_s5 = 2 * conv_ch
    col_gate = col_s5 + s5_w
    assert conv_ch == s5_w and V7X_MXU % nh == 0 and ng % (V7X_MXU // nh) == 0
    tm = _pick(seq, 512)
    tm_out = _pick(seq, 256)
    tn_in = _pick(w_in.shape[-1], conv_ch)

    x2 = x.reshape(t, d)
    for l in range(depth):
        ada3 = _ada(c, ada_w[l], ada_b[l]).reshape(bsz, N_ADA, d)
        proj = _inproj(x2, ada3, pre_mix_g[l], w_in[l].astype(BF16), seq, _pick(seq, 1024), tn_in)
        vc = _conv_branch(proj, dw_w[l], dw_b[l], cln_g[l], cln_b[l], bsz, seq, tm)

        s5_params = _s5_params(s5_a_re[l], s5_a_im[l], s5_log_dt[l], s5_b_re[l], s5_b_im[l],
                               s5_c_re[l], s5_c_im[l], s5_d[l])
        ys = _s5_branch(proj, col_s5, s5_params, bsz, seq)

        merged = _merge(vc, ys, proj, cp_w[l], cp_b[l], glu_wa[l], glu_wb[l], gate_b[l], tm, col_gate)
        x1, h2p, top_idx, gates = _outproj(merged, x2, ada3, w_out[l], post_mix_g[l], pre_ffn_g[l],
                                           router_w[l], router_b[l], seq, tm_out)

        tok_buf, g_buf, pos, blk_e, nvalid, n_blocks = _route(top_idx, gates, ne)
        xs = _moe_gather(tok_buf, h2p)
        act = _moe_up(blk_e, nvalid, xs, w1[l], b1[l], _pick(w1.shape[-1], 2048))
        y_buf = _moe_down(blk_e, nvalid, act, w2[l], b2[l], g_buf)
        x2 = _moe_combine(pos, y_buf, x1, ada3, post_ffn_g[l], seq, _pick(seq, 128))
    return x2.reshape(bsz, seq, d)
```

```python
import functools
import math

import jax
import jax.numpy as jnp
from jax import lax
from jax.experimental import pallas as pl
from jax.experimental.pallas import tpu as pltpu

EPS = 1e-6
N_ADA = 6
TOP_K = 4
MOE_BLOCK = 256
SWIGLU_ALPHA = 1.702
SWIGLU_LIMIT = 7.0
S5_CHUNK = 16
V7X_LANES = 128
V7X_SUBLANES = 8
V7X_MXU = 256
VMEM_LIMIT = 56 * 1024 * 1024

F32 = jnp.float32
BF16 = jnp.bfloat16


def _cparams(sem):
    return pltpu.CompilerParams(dimension_semantics=sem, vmem_limit_bytes=VMEM_LIMIT)


def _sigmoid(v):
    return 1.0 / (1.0 + jnp.exp(-v))


def _rms(v, g):
    return v * lax.rsqrt(jnp.mean(v * v, axis=-1, keepdims=True) + EPS) * g


def _ada_kernel(c_ref, w_ref, b_ref, o_ref):
    c = c_ref[...]
    s = (c * _sigmoid(c)).astype(BF16)
    o_ref[...] = jnp.dot(s, w_ref[...].astype(BF16), preferred_element_type=F32) + b_ref[...]


def _ada(c, ada_w, ada_b):
    bsz, d = c.shape
    n = ada_w.shape[1]
    tn = 1024 if n % 1024 == 0 else n
    return pl.pallas_call(
        _ada_kernel,
        grid=(n // tn,),
        in_specs=[
            pl.BlockSpec((bsz, d), lambda j: (0, 0)),
            pl.BlockSpec((d, tn), lambda j: (0, j)),
            pl.BlockSpec((1, tn), lambda j: (0, j)),
        ],
        out_specs=pl.BlockSpec((bsz, tn), lambda j: (0, j)),
        out_shape=jax.ShapeDtypeStruct((bsz, n), F32),
        compiler_params=_cparams(("arbitrary",)),
        name="ada",
    )(c, ada_w, ada_b.reshape(1, n))


def _inproj_kernel(x_ref, ada_ref, g_ref, w_ref, o_ref, h_ref):
    @pl.when(pl.program_id(1) == 0)
    def _():
        y = _rms(x_ref[...], g_ref[...])
        h = y * (1.0 + ada_ref[0, 1:2, :]) + ada_ref[0, 0:1, :]
        h_ref[...] = h.astype(BF16)

    o_ref[...] = jnp.dot(h_ref[...], w_ref[...], preferred_element_type=F32).astype(BF16)


def _inproj(x2, ada3, g, w_bf, seq, tm, tn):
    t, d = x2.shape
    n = w_bf.shape[1]
    per_b = seq // tm
    return pl.pallas_call(
        _inproj_kernel,
        grid=(t // tm, n // tn),
        in_specs=[
            pl.BlockSpec((tm, d), lambda i, j: (i, 0)),
            pl.BlockSpec((1, N_ADA, d), lambda i, j: (i // per_b, 0, 0)),
            pl.BlockSpec((1, d), lambda i, j: (0, 0)),
            pl.BlockSpec((d, tn), lambda i, j: (0, j)),
        ],
        out_specs=pl.BlockSpec((tm, tn), lambda i, j: (i, j)),
        out_shape=jax.ShapeDtypeStruct((t, n), BF16),
        scratch_shapes=[pltpu.VMEM((tm, d), BF16)],
        compiler_params=_cparams(("arbitrary", "arbitrary")),
        name="inproj",
    )(x2, ada3, g.reshape(1, d), w_bf)


CONV_HALO = 32
CONV_ROWS = 64


def _conv_kernel(pv_ref, pg_ref, w_ref, b_ref, g_ref, beta_ref, o_ref, vext, vsh, cbuf, *, taps, tl):
    t = pl.program_id(1)
    ch = cbuf.shape[1]

    @pl.when(t == 0)
    def _():
        vext[0:CONV_HALO, :] = jnp.zeros((CONV_HALO, ch), F32)

    @pl.when(t > 0)
    def _():
        vext[0:CONV_HALO, :] = vext[tl:tl + CONV_HALO, :]

    pv = pv_ref[...].astype(F32)
    pg = pg_ref[...].astype(F32)
    vext[CONV_HALO:CONV_HALO + tl, :] = pv * _sigmoid(pg)

    n_sh = vsh.shape[1]
    for p in range(1, V7X_SUBLANES):
        vsh[p - 1] = vext[p:p + n_sh, :]

    off = CONV_HALO - (taps - 1)
    for cb in range(ch // V7X_LANES):
        lanes = slice(cb * V7X_LANES, (cb + 1) * V7X_LANES)

        for rc in range(tl // CONV_ROWS):
            r0 = rc * CONV_ROWS
            acc = jnp.zeros((CONV_ROWS, V7X_LANES), F32)
            for k in range(taps):
                p = (off + k) % V7X_SUBLANES
                base = r0 + off + k - p
                win = vext[base:base + CONV_ROWS, lanes] if p == 0 else vsh[p - 1, base:base + CONV_ROWS, lanes]
                acc = acc + w_ref[k:k + 1, lanes] * win
            cbuf[r0:r0 + CONV_ROWS, lanes] = acc + b_ref[:, lanes]

    v = cbuf[...]
    mu = jnp.mean(v, axis=-1, keepdims=True)
    xc = v - mu
    var = jnp.mean(xc * xc, axis=-1, keepdims=True)
    y = xc * lax.rsqrt(var + EPS) * g_ref[...] + beta_ref[...]
    o_ref[...] = (y * _sigmoid(y)).astype(BF16)


def _conv_branch(proj, dw_w, dw_b, cln_g, cln_b, bsz, seq, tl):
    taps, ch = dw_w.shape
    assert taps - 1 <= CONV_HALO and tl % CONV_ROWS == 0 and ch % V7X_LANES == 0
    per_b = seq // tl
    vec = lambda a: a.reshape(1, ch)
    return pl.pallas_call(
        functools.partial(_conv_kernel, taps=taps, tl=tl),
        grid=(bsz, per_b),
        in_specs=[
            pl.BlockSpec((tl, ch), lambda b, t: (b * per_b + t, 0)),
            pl.BlockSpec((tl, ch), lambda b, t: (b * per_b + t, 1)),
            pl.BlockSpec((taps, ch), lambda b, t: (0, 0)),
            pl.BlockSpec((1, ch), lambda b, t: (0, 0)),
            pl.BlockSpec((1, ch), lambda b, t: (0, 0)),
            pl.BlockSpec((1, ch), lambda b, t: (0, 0)),
        ],
        out_specs=pl.BlockSpec((tl, ch), lambda b, t: (b * per_b + t, 0)),
        out_shape=jax.ShapeDtypeStruct((bsz * seq, ch), BF16),
        scratch_shapes=[pltpu.VMEM((CONV_HALO + tl, ch), F32),
                        pltpu.VMEM((V7X_SUBLANES - 1, CONV_HALO + tl - V7X_SUBLANES, ch), F32),
                        pltpu.VMEM((tl, ch), F32)],
        compiler_params=_cparams(("arbitrary", "arbitrary")),
        name="conv",
    )(proj, proj, dw_w, vec(dw_b), vec(cln_g), vec(cln_b))


S5_SUPER = 16


def _s5_params(a_re, a_im, log_dt, b_re, b_im, c_re, c_im, d_skip):
    ng, ns = a_re.shape
    nh = b_re.shape[-1]
    gt = V7X_MXU // nh
    nj = ng // gt
    dt = jnp.exp(log_dt)[:, None]
    mag = jnp.exp(a_re * dt)
    abar_re = mag * jnp.cos(a_im * dt)
    abar_im = mag * jnp.sin(a_im * dt)
    den = a_re * a_re + a_im * a_im
    num_re = abar_re - 1.0
    coef_re = (num_re * a_re + abar_im * a_im) / den
    coef_im = (abar_im * a_re - num_re * a_im) / den
    bb_re = coef_re[:, :, None] * b_re - coef_im[:, :, None] * b_im
    bb_im = coef_re[:, :, None] * b_im + coef_im[:, :, None] * b_re
    p = jnp.asarray([1.0, S5_CHUNK, S5_CHUNK * S5_SUPER] + [S5_CHUNK * r for r in range(S5_SUPER)], F32)
    pmag = jnp.exp(p[:, None, None] * (a_re * dt)[None])
    pw_re = pmag * jnp.cos(p[:, None, None] * (a_im * dt)[None])
    pw_im = pmag * jnp.sin(p[:, None, None] * (a_im * dt)[None])
    tile = lambda a: a.reshape(-1, nj, gt * ns).transpose(1, 0, 2)
    apow = jnp.concatenate([tile(pw_re), tile(pw_im)], axis=1)
    eye = jnp.eye(gt, dtype=F32)

    def bdiag_in(bb):
        return jnp.einsum('jgnh,gk->jghkn', bb.reshape(nj, gt, ns, nh), eye).reshape(nj, gt * nh, gt * ns)

    def bdiag_out(cc):
        return jnp.einsum('jghn,gk->jgnkh', cc.reshape(nj, gt, nh, ns), eye).reshape(nj, gt * ns, gt * nh)

    bdb = jnp.stack([bdiag_in(bb_re), bdiag_in(bb_im)], axis=1)
    bdc = jnp.stack([bdiag_out(c_re), bdiag_out(-c_im)], axis=1)
    return bdb.astype(BF16), bdc.astype(BF16), apow, d_skip.reshape(nj, 1, gt * nh)


def _cmul_add(ar, ai, xr, xi, vr, vi):
    return ar * xr - ai * xi + vr, ar * xi + ai * xr + vi


def _s5_kernel(u_ref, bdb_ref, bdc_ref, ap_ref, d_ref, p1_ref, p1t_ref, p2_ref, p2t_ref, o_ref,
               u2n_ref, u2_ref, hr_ref, hi_ref, zr_ref, zi_ref, gr_ref, gi_ref, y2_ref, y2n_ref,
               *, n_super):
    tc, nr, nm = S5_CHUNK, S5_SUPER, n_super
    tile = tc * nr
    np_ = 3 + nr
    apow = lambda k: (ap_ref[0, k:k + 1, :], ap_ref[0, np_ + k:np_ + k + 1, :])
    ar, ai = apow(0)

    for m in range(nm):
        pu = jnp.dot(p1_ref[...], u_ref[m * tile:(m + 1) * tile, :], preferred_element_type=F32).astype(BF16)
        for s in range(tc):
            u2n_ref[s, m * nr:(m + 1) * nr, :] = pu[s * nr:(s + 1) * nr, :]
    for s in range(tc):
        u2_ref[s] = jnp.dot(p2_ref[...], u2n_ref[s], preferred_element_type=F32).astype(BF16)

    def drive(s):
        u = u2_ref[s]
        return (jnp.dot(u, bdb_ref[0, 0], preferred_element_type=F32),
                jnp.dot(u, bdb_ref[0, 1], preferred_element_type=F32))

    hr_ref[...] = jnp.zeros(hr_ref.shape, F32)
    hi_ref[...] = jnp.zeros(hi_ref.shape, F32)

    def pass1(s, carry):
        vr, vi = drive(s)
        hr, hi = _cmul_add(ar, ai, hr_ref[...], hi_ref[...], vr, vi)
        hr_ref[...] = hr
        hi_ref[...] = hi
        return carry

    lax.fori_loop(0, tc, pass1, 0, unroll=4)

    a16r, a16i = apow(1)
    qr = jnp.zeros((nm, hr_ref.shape[1]), F32)
    qi = qr
    for r in range(nr):
        rows = slice(r * nm, (r + 1) * nm)
        zr_ref[rows, :] = qr
        zi_ref[rows, :] = qi
        qr, qi = _cmul_add(a16r, a16i, qr, qi, hr_ref[rows, :], hi_ref[rows, :])
    a256r, a256i = apow(2)
    gr = jnp.zeros((1, hr_ref.shape[1]), F32)
    gi = gr
    for m in range(nm):
        gr_ref[m:m + 1, :] = gr
        gi_ref[m:m + 1, :] = gi
        gr, gi = _cmul_add(a256r, a256i, gr, gi, qr[m:m + 1, :], qi[m:m + 1, :])
    gpr = gr_ref[...]
    gpi = gi_ref[...]
    for r in range(nr):
        rows = slice(r * nm, (r + 1) * nm)
        pr, pi = apow(3 + r)
        zr, zi = _cmul_add(pr, pi, gpr, gpi, zr_ref[rows, :], zi_ref[rows, :])
        zr_ref[rows, :] = zr
        zi_ref[rows, :] = zi

    def pass2(t, carry):
        vr, vi = drive(t)
        sr, si = _cmul_add(ar, ai, zr_ref[...], zi_ref[...], vr, vi)
        zr_ref[...] = sr
        zi_ref[...] = si
        y = (jnp.dot(sr.astype(BF16), bdc_ref[0, 0], preferred_element_type=F32)
             + jnp.dot(si.astype(BF16), bdc_ref[0, 1], preferred_element_type=F32))
        y = y + d_ref[0] * u2_ref[t].astype(F32)
        y = 0.5 * y * (1.0 + jnp.tanh(math.sqrt(2.0 / math.pi) * (y + 0.044715 * (y * y * y))))
        y2_ref[t] = y.astype(BF16)
        return carry

    lax.fori_loop(0, tc, pass2, 0, unroll=4)

    for t in range(tc):
        yn = jnp.dot(p2t_ref[...], y2_ref[t], preferred_element_type=F32).astype(BF16)
        for m in range(nm):
            y2n_ref[m * tile + t * nr:m * tile + (t + 1) * nr, :] = yn[m * nr:(m + 1) * nr, :]
    for m in range(nm):
        rows = slice(m * tile, (m + 1) * tile)
        o_ref[rows, :] = jnp.dot(p1t_ref[...], y2n_ref[rows, :], preferred_element_type=F32).astype(BF16)


def _s5_branch(proj, col0, params, bsz, seq):
    bdb, bdc, apow, dskip = params
    nj, _, w, sl = bdb.shape
    tile = S5_CHUNK * S5_SUPER
    assert seq % tile == 0 and col0 % w == 0
    nm = seq // tile
    assert nm % V7X_SUBLANES == 0, "row slabs of the chunk recurrence must be whole sublane tiles"
    rows = S5_SUPER * nm
    i1 = jnp.arange(tile)
    p1 = jnp.zeros((tile, tile), BF16).at[(i1 % S5_CHUNK) * S5_SUPER + i1 // S5_CHUNK, i1].set(1.0)
    i2 = jnp.arange(rows)
    p2 = jnp.zeros((rows, rows), BF16).at[(i2 % S5_SUPER) * nm + i2 // S5_SUPER, i2].set(1.0)
    const = lambda a: pl.BlockSpec(a.shape, lambda b, j: (0,) * a.ndim)
    per_j = lambda a: pl.BlockSpec((1,) + a.shape[1:], lambda b, j: (j,) + (0,) * (a.ndim - 1))
    return pl.pallas_call(
        functools.partial(_s5_kernel, n_super=nm),
        grid=(bsz, nj),
        in_specs=[
            pl.BlockSpec((seq, w), lambda b, j: (b, col0 // w + j)),
            per_j(bdb), per_j(bdc), per_j(apow), per_j(dskip),
            const(p1), const(p1), const(p2), const(p2),
        ],
        out_specs=pl.BlockSpec((seq, w), lambda b, j: (b, j)),
        out_shape=jax.ShapeDtypeStruct((bsz * seq, nj * w), BF16),
        scratch_shapes=[
            pltpu.VMEM((S5_CHUNK, rows, w), BF16), pltpu.VMEM((S5_CHUNK, rows, w), BF16),
            pltpu.VMEM((rows, sl), F32), pltpu.VMEM((rows, sl), F32),
            pltpu.VMEM((rows, sl), F32), pltpu.VMEM((rows, sl), F32),
            pltpu.VMEM((nm, sl), F32), pltpu.VMEM((nm, sl), F32),
            pltpu.VMEM((S5_CHUNK, rows, w), BF16), pltpu.VMEM((seq, w), BF16),
        ],
        compiler_params=_cparams(("arbitrary", "arbitrary")),
        name="s5",
    )(proj, bdb, bdc, apow, dskip, p1, p1.T, p2, p2.T)


def _merge_kernel(vc_ref, ys_ref, l1_ref, l2_ref, cpw_ref, cpb_ref, wa_ref, wb_ref, gb1_ref, gb2_ref, o_ref):
    y_conv = jnp.dot(vc_ref[...], cpw_ref[...], preferred_element_type=F32) + cpb_ref[...]
    ys = ys_ref[...]
    a = jnp.dot(ys, wa_ref[...], preferred_element_type=F32)
    b = jnp.dot(ys, wb_ref[...], preferred_element_type=F32)
    y_s5 = a * _sigmoid(b)
    g1 = _sigmoid(l1_ref[...].astype(F32) + gb1_ref[...])
    g2 = _sigmoid(l2_ref[...].astype(F32) + gb2_ref[...])
    o_ref[...] = (g1 * y_conv + g2 * y_s5).astype(BF16)


def _merge(vc, ys, proj, cp_w, cp_b, wa, wb, gate_b, tm, col0):
    t, ch = vc.shape
    d = cp_w.shape[1]
    tn = ch
    assert col0 % tn == 0 and d % tn == 0
    cb, nj = col0 // tn, d // tn
    wcol = lambda rows: pl.BlockSpec((rows, tn), lambda j, i: (0, j))
    return pl.pallas_call(
        _merge_kernel,
        grid=(nj, t // tm),
        in_specs=[
            pl.BlockSpec((tm, ch), lambda j, i: (i, 0)),
            pl.BlockSpec((tm, ch), lambda j, i: (i, 0)),
            pl.BlockSpec((tm, tn), lambda j, i: (i, cb + j)),
            pl.BlockSpec((tm, tn), lambda j, i: (i, cb + nj + j)),
            wcol(ch), wcol(1), wcol(ch), wcol(ch), wcol(1), wcol(1),
        ],
        out_specs=pl.BlockSpec((tm, tn), lambda j, i: (i, j)),
        out_shape=jax.ShapeDtypeStruct((t, d), BF16),
        compiler_params=_cparams(("arbitrary", "arbitrary")),
        name="merge",
    )(vc, ys, proj, proj, cp_w.astype(BF16), cp_b.reshape(1, d), wa.astype(BF16), wb.astype(BF16),
      gate_b[:d].reshape(1, d), gate_b[d:].reshape(1, d))


def _split_bf16(v):
    hi = v.astype(BF16)
    lo = (v - hi.astype(F32)).astype(BF16)
    return hi, lo


def _bf16_bits_hi(v):
    b = lax.bitcast_convert_type(v, jnp.uint32)
    return (b + jnp.uint32(0x7FFF) + ((b >> 16) & jnp.uint32(1))) & jnp.uint32(0xFFFF0000)


def _pack_bf16_pairs(v):
    half = v.shape[-1] // 2
    return _bf16_bits_hi(v[:, half:]) | (_bf16_bits_hi(v[:, :half]) >> 16)


def _unpack_bf16_pairs(p):
    lo = lax.bitcast_convert_type(p << 16, F32).astype(BF16)
    hi = lax.bitcast_convert_type(p & jnp.uint32(0xFFFF0000), F32).astype(BF16)
    return lo, hi


OUTPROJ_ROWS = 256


def _outproj_kernel(m_ref, x_ref, ada_ref, wo_ref, g1_ref, g2_ref, rwh_ref, rwl_ref, rb_ref,
                    x1_ref, h2p_ref, idx_ref, gate_ref, *, ne):
    for rb in range(m_ref.shape[0] // OUTPROJ_ROWS):
        rows = slice(rb * OUTPROJ_ROWS, (rb + 1) * OUTPROJ_ROWS)
        _outproj_rows(m_ref[rows, :], x_ref[rows, :], ada_ref, wo_ref, g1_ref, g2_ref, rwh_ref, rwl_ref, rb_ref,
                      x1_ref.at[rows, :], h2p_ref.at[rows, :], idx_ref.at[rows, :], gate_ref.at[rows, :], ne)


def _outproj_rows(m_in, x_in, ada_ref, wo_ref, g1_ref, g2_ref, rwh_ref, rwl_ref, rb_ref,
                  x1_ref, h2p_ref, idx_ref, gate_ref, ne):
    m = jnp.dot(m_in, wo_ref[...], preferred_element_type=F32)
    x1 = x_in + ada_ref[0, 2:3, :] * _rms(m, g1_ref[...])
    x1_ref[...] = x1
    h2 = _rms(x1, g2_ref[...]) * (1.0 + ada_ref[0, 4:5, :]) + ada_ref[0, 3:4, :]
    h2p_ref[...] = _pack_bf16_pairs(h2)
    hh, hl = _split_bf16(h2)
    lg = jnp.dot(hh, rwh_ref[...], preferred_element_type=F32)
    lg = lg + jnp.dot(hl, rwh_ref[...], preferred_element_type=F32)
    lg = lg + jnp.dot(hh, rwl_ref[...], preferred_element_type=F32)
    lg = lg + rb_ref[...]
    lane = lax.broadcasted_iota(jnp.int32, lg.shape, 1)
    work = jnp.where(lane < ne, lg, -jnp.inf)
    vals, idxs = [], []
    for _ in range(TOP_K):
        mx = jnp.max(work, axis=-1, keepdims=True)
        ix = jnp.min(jnp.where(work == mx, lane, lg.shape[1]), axis=-1, keepdims=True)
        vals.append(mx)
        idxs.append(ix)
        work = jnp.where(lane == ix, -jnp.inf, work)
    ex = [jnp.exp(v - vals[0]) for v in vals]
    den = ex[0]
    for e in ex[1:]:
        den = den + e
    idx_out = jnp.zeros(lg.shape, jnp.int32)
    gate_out = jnp.zeros(lg.shape, F32)
    for k in range(TOP_K):
        idx_out = jnp.where(lane == k, idxs[k], idx_out)
        gate_out = jnp.where(lane == k, ex[k] / den, gate_out)
    idx_ref[...] = idx_out
    gate_ref[...] = gate_out


def _outproj(merged, x2, ada3, w_out, post_mix_g, pre_ffn_g, router_w, router_b, seq, tm):
    t, d = x2.shape
    ne = router_w.shape[1]
    nep = max(V7X_LANES, ne)
    rw = jnp.zeros((d, nep), F32).at[:, :ne].set(router_w)
    rw_hi, rw_lo = _split_bf16(rw)
    rb = jnp.zeros((1, nep), F32).at[0, :ne].set(router_b)
    per_b = seq // tm
    full = lambda shape: pl.BlockSpec(shape, lambda i: (0, 0))
    row = lambda w: pl.BlockSpec((tm, w), lambda i: (i, 0))
    x1, h2p, top_idx, gates = pl.pallas_call(
        functools.partial(_outproj_kernel, ne=ne),
        grid=(t // tm,),
        in_specs=[
            row(d), row(d),
            pl.BlockSpec((1, N_ADA, d), lambda i: (i // per_b, 0, 0)),
            full((d, d)), full((1, d)), full((1, d)), full((d, nep)), full((d, nep)), full((1, nep)),
        ],
        out_specs=[row(d), row(d // 2), row(nep), row(nep)],
        out_shape=[jax.ShapeDtypeStruct((t, d), F32), jax.ShapeDtypeStruct((t, d // 2), jnp.uint32),
                   jax.ShapeDtypeStruct((t, nep), jnp.int32), jax.ShapeDtypeStruct((t, nep), F32)],
        compiler_params=_cparams(("arbitrary",)),
        name="outproj",
    )(merged, x2, ada3, w_out.astype(BF16), post_mix_g.reshape(1, d), pre_ffn_g.reshape(1, d),
      rw_hi, rw_lo, rb)
    return x1, h2p, top_idx[:, :TOP_K], gates[:, :TOP_K]


GATHER_UNROLL = 8


def _row_copy(idx_ref, src_hbm, buf, sems, slot, r):
    return pltpu.make_async_copy(src_hbm.at[pl.ds(idx_ref[0, 0, r], 1), :],
                                 buf.at[slot, pl.ds(r, 1), :], sems.at[slot])


def _start_rows(idx_ref, src_hbm, buf, sems, slot, n, unrolled=False):
    if unrolled:
        for r in range(n):
            _row_copy(idx_ref, src_hbm, buf, sems, slot, r).start(priority=r % 2)
        return

    def start(r, carry):
        _row_copy(idx_ref, src_hbm, buf, sems, slot, r).start()
        return carry
    lax.fori_loop(0, n, start, 0, unroll=GATHER_UNROLL)


def _wait_rows(idx_ref, src_hbm, buf, sems, slot, n):
    def wait(r, carry):
        _row_copy(idx_ref, src_hbm, buf, sems, slot, r).wait()
        return carry
    lax.fori_loop(0, n, wait, 0, unroll=GATHER_UNROLL)


def _expert_changed(e_ref, i):
    return jnp.logical_or(i == 0, e_ref[i] != e_ref[jnp.maximum(i - 1, 0)])


def _gather_kernel(tok_ref, h_hbm, o_ref, buf, sems):
    i = pl.program_id(0)
    n_blocks = pl.num_programs(0) - 1

    @pl.when(i < n_blocks)
    def _():
        _start_rows(tok_ref, h_hbm, buf, sems, i % 2, MOE_BLOCK, unrolled=True)

    @pl.when(i > 0)
    def _():
        slot = (i - 1) % 2
        _wait_rows(tok_ref, h_hbm, buf, sems, slot, MOE_BLOCK)
        o_ref[...] = buf[slot]


def _moe_gather(tok_buf, h2p):
    n_blocks = tok_buf.shape[0]
    t, w = h2p.shape
    return pl.pallas_call(
        _gather_kernel,
        grid=(n_blocks + 1,),
        in_specs=[
            pl.BlockSpec((1, 1, MOE_BLOCK), lambda i: (jnp.minimum(i, n_blocks - 1), 0, 0),
                         memory_space=pltpu.SMEM),
            pl.BlockSpec(memory_space=pl.ANY),
        ],
        out_specs=pl.BlockSpec((MOE_BLOCK, w), lambda i: (jnp.maximum(i - 1, 0), 0)),
        out_shape=jax.ShapeDtypeStruct((n_blocks * MOE_BLOCK, w), h2p.dtype),
        scratch_shapes=[pltpu.VMEM((2, MOE_BLOCK, w), h2p.dtype), pltpu.SemaphoreType.DMA((2,))],
        compiler_params=_cparams(("arbitrary",)),
        name="moe_gather",
    )(tok_buf.reshape(n_blocks, 1, MOE_BLOCK), h2p)


def _up_kernel(e_ref, nv_ref, x_ref, w_ref, bg_ref, bl_ref, p_ref, o_ref, wp_ref):
    i = pl.program_id(1)
    tn = wp_ref.shape[1]
    half = V7X_MXU // 2
    hd = x_ref.shape[1]

    @pl.when(jnp.logical_and(_expert_changed(e_ref, i), nv_ref[i] > 0))
    def _():
        for cb in range(tn // V7X_MXU):
            cols = slice(cb * V7X_MXU, (cb + 1) * V7X_MXU)
            wp_ref[:, cols] = jnp.dot(w_ref[0, :, cols].astype(BF16), p_ref[...],
                                      preferred_element_type=F32).astype(BF16)

    @pl.when(nv_ref[i] > 0)
    def _():
        x_lo, x_hi = _unpack_bf16_pairs(x_ref[...])
        for cb in range(tn // V7X_MXU):
            cols = slice(cb * V7X_MXU, (cb + 1) * V7X_MXU)
            hb = (jnp.dot(x_lo, wp_ref[0:hd, cols], preferred_element_type=F32)
                  + jnp.dot(x_hi, wp_ref[hd:2 * hd, cols], preferred_element_type=F32))
            out_cols = slice(cb * half, (cb + 1) * half)
            x_glu = hb[:, :half] + bg_ref[0, :, out_cols]
            x_lin = hb[:, half:] + bl_ref[0, :, out_cols]
            x_glu = jnp.minimum(x_glu, SWIGLU_LIMIT)
            x_lin = jnp.clip(x_lin, -SWIGLU_LIMIT, SWIGLU_LIMIT)
            act = x_glu * _sigmoid(SWIGLU_ALPHA * x_glu) * (x_lin + 1.0)
            o_ref[:, out_cols] = act.astype(BF16)

    @pl.when(nv_ref[i] == 0)
    def _():
        o_ref[...] = jnp.zeros(o_ref.shape, o_ref.dtype)


def _moe_up(blk_e, nvalid, xs, w1, b1, tn):
    n_rows, hd = xs.shape
    ne, d, f2 = w1.shape
    assert d == 2 * hd
    f = f2 // 2
    n_blocks = n_rows // MOE_BLOCK
    half = V7X_MXU // 2
    c = jnp.arange(V7X_MXU)
    perm = jnp.zeros((V7X_MXU, V7X_MXU), BF16).at[c, (c % 2) * half + c // 2].set(1.0)
    b1g = b1[:, 0::2].reshape(ne, 1, f)
    b1l = b1[:, 1::2].reshape(ne, 1, f)
    return pl.pallas_call(
        _up_kernel,
        grid_spec=pltpu.PrefetchScalarGridSpec(
            num_scalar_prefetch=2,
            grid=(f2 // tn, n_blocks),
            in_specs=[
                pl.BlockSpec((MOE_BLOCK, hd), lambda j, i, e, nv: (i, 0)),
                pl.BlockSpec((1, d, tn), lambda j, i, e, nv: (e[i], 0, j)),
                pl.BlockSpec((1, 1, tn // 2), lambda j, i, e, nv: (e[i], 0, j)),
                pl.BlockSpec((1, 1, tn // 2), lambda j, i, e, nv: (e[i], 0, j)),
                pl.BlockSpec((V7X_MXU, V7X_MXU), lambda j, i, e, nv: (0, 0)),
            ],
            out_specs=pl.BlockSpec((MOE_BLOCK, tn // 2), lambda j, i, e, nv: (i, j)),
            scratch_shapes=[pltpu.VMEM((d, tn), BF16)],
        ),
        out_shape=jax.ShapeDtypeStruct((n_rows, f), BF16),
        compiler_params=_cparams(("arbitrary", "arbitrary")),
        name="moe_up",
    )(blk_e, nvalid, xs, w1, b1g, b1l, perm)


def _down_kernel(e_ref, nv_ref, a_ref, w_ref, b_ref, g_ref, o_ref, wb_ref):
    i = pl.program_id(0)

    @pl.when(nv_ref[i] > 0)
    def _():
        @pl.when(_expert_changed(e_ref, i))
        def _():
            wb_ref[...] = w_ref[0].astype(BF16)

        y = jnp.dot(a_ref[...], wb_ref[...], preferred_element_type=F32) + b_ref[0]
        o_ref[...] = y * g_ref[...]

    @pl.when(nv_ref[i] == 0)
    def _():
        o_ref[...] = jnp.zeros(o_ref.shape, o_ref.dtype)


def _moe_down(blk_e, nvalid, act, w2, b2, g_buf):
    n_rows, f = act.shape
    ne, _, d = w2.shape
    n_blocks = n_rows // MOE_BLOCK
    return pl.pallas_call(
        _down_kernel,
        grid_spec=pltpu.PrefetchScalarGridSpec(
            num_scalar_prefetch=2,
            grid=(n_blocks,),
            in_specs=[
                pl.BlockSpec((MOE_BLOCK, f), lambda i, e, nv: (i, 0)),
                pl.BlockSpec((1, f, d), lambda i, e, nv: (e[i], 0, 0)),
                pl.BlockSpec((1, 1, d), lambda i, e, nv: (e[i], 0, 0)),
                pl.BlockSpec((MOE_BLOCK, 1), lambda i, e, nv: (i, 0)),
            ],
            out_specs=pl.BlockSpec((MOE_BLOCK, d), lambda i, e, nv: (i, 0)),
            scratch_shapes=[pltpu.VMEM((f, d), BF16)],
        ),
        out_shape=jax.ShapeDtypeStruct((n_rows, d), F32),
        compiler_params=_cparams(("arbitrary",)),
        name="moe_down",
    )(blk_e, nvalid, act, w2, b2.reshape(ne, 1, d), g_buf.reshape(n_rows, 1))


def _combine_kernel(pos_ref, y_hbm, x1_ref, ada_ref, g_ref, o_ref, buf, sems, *, tt):
    i = pl.program_id(0)
    n_tiles = pl.num_programs(0) - 1
    n_rows = TOP_K * tt

    def finish(slot):
        f = buf[slot, 0:tt, :]
        for k in range(1, TOP_K):
            f = f + buf[slot, k * tt:(k + 1) * tt, :]
        o_ref[...] = x1_ref[...] + ada_ref[0, 5:6, :] * _rms(f, g_ref[...])

    @pl.when(i == 0)
    def _():
        _start_rows(pos_ref, y_hbm, buf, sems, 0, n_rows)

    @pl.when(jnp.logical_and(i > 0, i < n_tiles))
    def _():
        _wait_rows(pos_ref, y_hbm, buf, sems, (i - 1) % 2, n_rows)
        _start_rows(pos_ref, y_hbm, buf, sems, i % 2, n_rows, unrolled=True)
        finish((i - 1) % 2)

    @pl.when(i == n_tiles)
    def _():
        _wait_rows(pos_ref, y_hbm, buf, sems, (i - 1) % 2, n_rows)
        finish((i - 1) % 2)


def _moe_combine(pos, y_buf, x1, ada3, post_ffn_g, seq, tt):
    t, d = x1.shape
    per_b = seq // tt
    nt = t // tt
    pos_t = pos.reshape(nt, tt, TOP_K).transpose(0, 2, 1).reshape(nt, 1, TOP_K * tt)
    prev = lambda i: jnp.maximum(i - 1, 0)
    return pl.pallas_call(
        functools.partial(_combine_kernel, tt=tt),
        grid=(nt + 1,),
        in_specs=[
            pl.BlockSpec((1, 1, TOP_K * tt), lambda i: (jnp.minimum(i, nt - 1), 0, 0), memory_space=pltpu.SMEM),
            pl.BlockSpec(memory_space=pl.ANY),
            pl.BlockSpec((tt, d), lambda i: (prev(i), 0)),
            pl.BlockSpec((1, N_ADA, d), lambda i: (prev(i) // per_b, 0, 0)),
            pl.BlockSpec((1, d), lambda i: (0, 0)),
        ],
        out_specs=pl.BlockSpec((tt, d), lambda i: (prev(i), 0)),
        out_shape=jax.ShapeDtypeStruct((t, d), F32),
        scratch_shapes=[pltpu.VMEM((2, TOP_K * tt, d), F32), pltpu.SemaphoreType.DMA((2,))],
        compiler_params=_cparams(("arbitrary",)),
        name="moe_combine",
    )(pos_t, y_buf, x1, ada3, post_ffn_g.reshape(1, d))


def _route(top_idx, gates, n_experts):
    n_tok = top_idx.shape[0]
    n_asg = n_tok * TOP_K
    i32 = jnp.int32
    flat_e = top_idx.reshape(n_asg)
    order = jnp.argsort(flat_e).astype(i32)
    counts = jnp.sum(flat_e[:, None] == jnp.arange(n_experts, dtype=i32)[None, :], axis=0, dtype=i32)
    starts = jnp.cumsum(counts) - counts
    padded = (counts + MOE_BLOCK - 1) // MOE_BLOCK * MOE_BLOCK
    pends = jnp.cumsum(padded)
    pstarts = pends - padded
    n_blocks = -(-n_asg // MOE_BLOCK) + n_experts
    blk_start = jnp.arange(n_blocks, dtype=i32) * MOE_BLOCK
    blk_e = jnp.minimum(jnp.sum(pends[None, :] <= blk_start[:, None], axis=1, dtype=i32), n_experts - 1)
    off = blk_start - pstarts[blk_e]
    nvalid = jnp.clip(counts[blk_e] - off, 0, MOE_BLOCK)
    j = off[:, None] + jnp.arange(MOE_BLOCK, dtype=i32)[None, :]
    valid = j < counts[blk_e][:, None]
    src = jnp.clip(starts[blk_e][:, None] + j, 0, n_asg - 1)
    asg = order[src]
    tok_buf = jnp.where(valid, asg // TOP_K, 0)
    g_buf = jnp.where(valid, gates.reshape(n_asg)[asg], 0.0)
    rank = jnp.argsort(order).astype(i32)
    pos = pstarts[flat_e] + rank - starts[flat_e]
    return tok_buf, g_buf, pos, blk_e, nvalid, n_blocks


def _pick(n, pref):
    return pref if n % pref == 0 else n


def kernel(x, c, ada_w, ada_b, pre_mix_g, post_mix_g, pre_ffn_g, post_ffn_g, w_in, gate_b, dw_w, dw_b, cln_g, cln_b, cp_w, cp_b, s5_a_re, s5_a_im, s5_log_dt, s5_b_re, s5_b_im, s5_c_re, s5_c_im, s5_d, glu_wa, glu_wb, w_out, router_w, router_b, w1, b1, w2, b2):
    bsz, seq, d = x.shape
    t = bsz * seq
    depth = ada_w.shape[0]
    conv_ch = dw_w.shape[-1]
    ng, ns, nh = s5_b_re.shape[1:]
    s5_w = ng * nh
    ne = router_w.shape[-1]
    col_s5 = 2 * conv_ch
    col_gate = col_s5 + s5_w
    assert conv_ch == s5_w and V7X_MXU % nh == 0 and ng % (V7X_MXU // nh) == 0
    tm = _pick(seq, 512)
    tm_out = _pick(seq, 256)
    tn_in = _pick(w_in.shape[-1], conv_ch)

    x2 = x.reshape(t, d)
    for l in range(depth):
        ada3 = _ada(c, ada_w[l], ada_b[l]).reshape(bsz, N_ADA, d)
        proj = _inproj(x2, ada3, pre_mix_g[l], w_in[l].astype(BF16), seq, _pick(seq, 1024), tn_in)
        vc = _conv_branch(proj, dw_w[l], dw_b[l], cln_g[l], cln_b[l], bsz, seq, tm)

        s5_params = _s5_params(s5_a_re[l], s5_a_im[l], s5_log_dt[l], s5_b_re[l], s5_b_im[l],
                               s5_c_re[l], s5_c_im[l], s5_d[l])
        ys = _s5_branch(proj, col_s5, s5_params, bsz, seq)

        merged = _merge(vc, ys, proj, cp_w[l], cp_b[l], glu_wa[l], glu_wb[l], gate_b[l], tm, col_gate)
        x1, h2p, top_idx, gates = _outproj(merged, x2, ada3, w_out[l], post_mix_g[l], pre_ffn_g[l],
                                           router_w[l], router_b[l], seq, tm_out)

        tok_buf, g_buf, pos, blk_e, nvalid, n_blocks = _route(top_idx, gates, ne)
        xs = _moe_gather(tok_buf, h2p)
        act = _moe_up(blk_e, nvalid, xs, w1[l], b1[l], _pick(w1.shape[-1], 2048))
        y_buf = _moe_down(blk_e, nvalid, act, w2[l], b2[l], g_buf)
        x2 = _moe_combine(pos, y_buf, x1, ada3, post_ffn_g[l], seq, _pick(seq, 128))
    return x2.reshape(bsz, seq, d)
```

```python
import functools
import math

import jax
import jax.numpy as jnp
from jax import lax
from jax.experimental import pallas as pl
from jax.experimental.pallas import tpu as pltpu

EPS = 1e-6
N_ADA = 6
TOP_K = 4
MOE_BLOCK = 256
SWIGLU_ALPHA = 1.702
SWIGLU_LIMIT = 7.0
S5_CHUNK = 16
V7X_LANES = 128
V7X_SUBLANES = 8
V7X_MXU = 256
VMEM_LIMIT = 56 * 1024 * 1024

F32 = jnp.float32
BF16 = jnp.bfloat16


def _cparams(sem):
    return pltpu.CompilerParams(dimension_semantics=sem, vmem_limit_bytes=VMEM_LIMIT)


def _sigmoid(v):
    return 1.0 / (1.0 + jnp.exp(-v))


def _rms(v, g):
    return v * lax.rsqrt(jnp.mean(v * v, axis=-1, keepdims=True) + EPS) * g


def _ada_kernel(c_ref, w_ref, b_ref, o_ref):
    c = c_ref[...]
    s = (c * _sigmoid(c)).astype(BF16)
    o_ref[...] = jnp.dot(s, w_ref[...].astype(BF16), preferred_element_type=F32) + b_ref[...]


def _ada(c, ada_w, ada_b):
    bsz, d = c.shape
    n = ada_w.shape[1]
    tn = 1024 if n % 1024 == 0 else n
    return pl.pallas_call(
        _ada_kernel,
        grid=(n // tn,),
        in_specs=[
            pl.BlockSpec((bsz, d), lambda j: (0, 0)),
            pl.BlockSpec((d, tn), lambda j: (0, j)),
            pl.BlockSpec((1, tn), lambda j: (0, j)),
        ],
        out_specs=pl.BlockSpec((bsz, tn), lambda j: (0, j)),
        out_shape=jax.ShapeDtypeStruct((bsz, n), F32),
        compiler_params=_cparams(("arbitrary",)),
        name="ada",
    )(c, ada_w, ada_b.reshape(1, n))


def _inproj_kernel(x_ref, ada_ref, g_ref, w_ref, o_ref, h_ref):
    @pl.when(pl.program_id(1) == 0)
    def _():
        y = _rms(x_ref[...], g_ref[...])
        h = y * (1.0 + ada_ref[0, 1:2, :]) + ada_ref[0, 0:1, :]
        h_ref[...] = h.astype(BF16)

    o_ref[...] = jnp.dot(h_ref[...], w_ref[...], preferred_element_type=F32).astype(BF16)


def _inproj(x2, ada3, g, w_bf, seq, tm, tn):
    t, d = x2.shape
    n = w_bf.shape[1]
    per_b = seq // tm
    return pl.pallas_call(
        _inproj_kernel,
        grid=(t // tm, n // tn),
        in_specs=[
            pl.BlockSpec((tm, d), lambda i, j: (i, 0)),
            pl.BlockSpec((1, N_ADA, d), lambda i, j: (i // per_b, 0, 0)),
            pl.BlockSpec((1, d), lambda i, j: (0, 0)),
            pl.BlockSpec((d, tn), lambda i, j: (0, j)),
        ],
        out_specs=pl.BlockSpec((tm, tn), lambda i, j: (i, j)),
        out_shape=jax.ShapeDtypeStruct((t, n), BF16),
        scratch_shapes=[pltpu.VMEM((tm, d), BF16)],
        compiler_params=_cparams(("arbitrary", "arbitrary")),
        name="inproj",
    )(x2, ada3, g.reshape(1, d), w_bf)


CONV_HALO = 32
CONV_ROWS = 64


def _conv_kernel(pv_ref, pg_ref, w_ref, b_ref, g_ref, beta_ref, o_ref, vext, vsh, cbuf, *, taps, tl):
    t = pl.program_id(1)
    ch = cbuf.shape[1]

    @pl.when(t == 0)
    def _():
        vext[0:CONV_HALO, :] = jnp.zeros((CONV_HALO, ch), F32)

    @pl.when(t > 0)
    def _():
        vext[0:CONV_HALO, :] = vext[tl:tl + CONV_HALO, :]

    pv = pv_ref[...].astype(F32)
    pg = pg_ref[...].astype(F32)
    vext[CONV_HALO:CONV_HALO + tl, :] = pv * _sigmoid(pg)

    n_sh = vsh.shape[1]
    for p in range(1, V7X_SUBLANES):
        vsh[p - 1] = vext[p:p + n_sh, :]

    off = CONV_HALO - (taps - 1)
    for cb in range(ch // V7X_LANES):
        lanes = slice(cb * V7X_LANES, (cb + 1) * V7X_LANES)

        for rc in range(tl // CONV_ROWS):
            r0 = rc * CONV_ROWS
            acc = jnp.zeros((CONV_ROWS, V7X_LANES), F32)
            for k in range(taps):
                p = (off + k) % V7X_SUBLANES
                base = r0 + off + k - p
                win = vext[base:base + CONV_ROWS, lanes] if p == 0 else vsh[p - 1, base:base + CONV_ROWS, lanes]
                acc = acc + w_ref[k:k + 1, lanes] * win
            cbuf[r0:r0 + CONV_ROWS, lanes] = acc + b_ref[:, lanes]

    v = cbuf[...]
    mu = jnp.mean(v, axis=-1, keepdims=True)
    xc = v - mu
    var = jnp.mean(xc * xc, axis=-1, keepdims=True)
    y = xc * lax.rsqrt(var + EPS) * g_ref[...] + beta_ref[...]
    o_ref[...] = (y * _sigmoid(y)).astype(BF16)


def _conv_branch(proj, dw_w, dw_b, cln_g, cln_b, bsz, seq, tl):
    taps, ch = dw_w.shape
    assert taps - 1 <= CONV_HALO and tl % CONV_ROWS == 0 and ch % V7X_LANES == 0
    per_b = seq // tl
    vec = lambda a: a.reshape(1, ch)
    return pl.pallas_call(
        functools.partial(_conv_kernel, taps=taps, tl=tl),
        grid=(bsz, per_b),
        in_specs=[
            pl.BlockSpec((tl, ch), lambda b, t: (b * per_b + t, 0)),
            pl.BlockSpec((tl, ch), lambda b, t: (b * per_b + t, 1)),
            pl.BlockSpec((taps, ch), lambda b, t: (0, 0)),
            pl.BlockSpec((1, ch), lambda b, t: (0, 0)),
            pl.BlockSpec((1, ch), lambda b, t: (0, 0)),
            pl.BlockSpec((1, ch), lambda b, t: (0, 0)),
        ],
        out_specs=pl.BlockSpec((tl, ch), lambda b, t: (b * per_b + t, 0)),
        out_shape=jax.ShapeDtypeStruct((bsz * seq, ch), BF16),
        scratch_shapes=[pltpu.VMEM((CONV_HALO + tl, ch), F32),
                        pltpu.VMEM((V7X_SUBLANES - 1, CONV_HALO + tl - V7X_SUBLANES, ch), F32),
                        pltpu.VMEM((tl, ch), F32)],
        compiler_params=_cparams(("arbitrary", "arbitrary")),
        name="conv",
    )(proj, proj, dw_w, vec(dw_b), vec(cln_g), vec(cln_b))


S5_SUPER = 16


def _s5_params(a_re, a_im, log_dt, b_re, b_im, c_re, c_im, d_skip):
    ng, ns = a_re.shape
    nh = b_re.shape[-1]
    gt = V7X_MXU // nh
    nj = ng // gt
    dt = jnp.exp(log_dt)[:, None]
    mag = jnp.exp(a_re * dt)
    abar_re = mag * jnp.cos(a_im * dt)
    abar_im = mag * jnp.sin(a_im * dt)
    den = a_re * a_re + a_im * a_im
    num_re = abar_re - 1.0
    coef_re = (num_re * a_re + abar_im * a_im) / den
    coef_im = (abar_im * a_re - num_re * a_im) / den
    bb_re = coef_re[:, :, None] * b_re - coef_im[:, :, None] * b_im
    bb_im = coef_re[:, :, None] * b_im + coef_im[:, :, None] * b_re
    p = jnp.asarray([1.0, S5_CHUNK, S5_CHUNK * S5_SUPER] + [S5_CHUNK * r for r in range(S5_SUPER)], F32)
    pmag = jnp.exp(p[:, None, None] * (a_re * dt)[None])
    pw_re = pmag * jnp.cos(p[:, None, None] * (a_im * dt)[None])
    pw_im = pmag * jnp.sin(p[:, None, None] * (a_im * dt)[None])
    tile = lambda a: a.reshape(-1, nj, gt * ns).transpose(1, 0, 2)
    apow = jnp.concatenate([tile(pw_re), tile(pw_im)], axis=1)
    eye = jnp.eye(gt, dtype=F32)

    def bdiag_in(bb):
        return jnp.einsum('jgnh,gk->jghkn', bb.reshape(nj, gt, ns, nh), eye).reshape(nj, gt * nh, gt * ns)

    def bdiag_out(cc):
        return jnp.einsum('jghn,gk->jgnkh', cc.reshape(nj, gt, nh, ns), eye).reshape(nj, gt * ns, gt * nh)

    bdb = jnp.stack([bdiag_in(bb_re), bdiag_in(bb_im)], axis=1)
    bdc = jnp.stack([bdiag_out(c_re), bdiag_out(-c_im)], axis=1)
    return bdb.astype(BF16), bdc.astype(BF16), apow, d_skip.reshape(nj, 1, gt * nh)


def _cmul_add(ar, ai, xr, xi, vr, vi):
    return ar * xr - ai * xi + vr, ar * xi + ai * xr + vi


def _s5_kernel(u_ref, bdb_ref, bdc_ref, ap_ref, d_ref, p1_ref, p1t_ref, p2_ref, p2t_ref, o_ref,
               u2n_ref, u2_ref, hr_ref, hi_ref, zr_ref, zi_ref, gr_ref, gi_ref, y2_ref, y2n_ref,
               *, n_super):
    tc, nr, nm = S5_CHUNK, S5_SUPER, n_super
    tile = tc * nr
    np_ = 3 + nr
    apow = lambda k: (ap_ref[0, k:k + 1, :], ap_ref[0, np_ + k:np_ + k + 1, :])
    ar, ai = apow(0)

    for m in range(nm):
        pu = jnp.dot(p1_ref[...], u_ref[m * tile:(m + 1) * tile, :], preferred_element_type=F32).astype(BF16)
        for s in range(tc):
            u2n_ref[s, m * nr:(m + 1) * nr, :] = pu[s * nr:(s + 1) * nr, :]
    for s in range(tc):
        u2_ref[s] = jnp.dot(p2_ref[...], u2n_ref[s], preferred_element_type=F32).astype(BF16)

    def drive(s):
        u = u2_ref[s]
        return (jnp.dot(u, bdb_ref[0, 0], preferred_element_type=F32),
                jnp.dot(u, bdb_ref[0, 1], preferred_element_type=F32))

    hr_ref[...] = jnp.zeros(hr_ref.shape, F32)
    hi_ref[...] = jnp.zeros(hi_ref.shape, F32)

    def pass1(s, carry):
        vr, vi = drive(s)
        hr, hi = _cmul_add(ar, ai, hr_ref[...], hi_ref[...], vr, vi)
        hr_ref[...] = hr
        hi_ref[...] = hi
        return carry

    lax.fori_loop(0, tc, pass1, 0, unroll=4)

    a16r, a16i = apow(1)
    qr = jnp.zeros((nm, hr_ref.shape[1]), F32)
    qi = qr
    for r in range(nr):
        rows = slice(r * nm, (r + 1) * nm)
        zr_ref[rows, :] = qr
        zi_ref[rows, :] = qi
        qr, qi = _cmul_add(a16r, a16i, qr, qi, hr_ref[rows, :], hi_ref[rows, :])
    a256r, a256i = apow(2)
    gr = jnp.zeros((1, hr_ref.shape[1]), F32)
    gi = gr
    for m in range(nm):
        gr_ref[m:m + 1, :] = gr
        gi_ref[m:m + 1, :] = gi
        gr, gi = _cmul_add(a256r, a256i, gr, gi, qr[m:m + 1, :], qi[m:m + 1, :])
    gpr = gr_ref[...]
    gpi = gi_ref[...]
    for r in range(nr):
        rows = slice(r * nm, (r + 1) * nm)
        pr, pi = apow(3 + r)
        zr, zi = _cmul_add(pr, pi, gpr, gpi, zr_ref[rows, :], zi_ref[rows, :])
        zr_ref[rows, :] = zr
        zi_ref[rows, :] = zi

    def pass2(t, carry):
        vr, vi = drive(t)
        sr, si = _cmul_add(ar, ai, zr_ref[...], zi_ref[...], vr, vi)
        zr_ref[...] = sr
        zi_ref[...] = si
        y = (jnp.dot(sr.astype(BF16), bdc_ref[0, 0], preferred_element_type=F32)
             + jnp.dot(si.astype(BF16), bdc_ref[0, 1], preferred_element_type=F32))
        y = y + d_ref[0] * u2_ref[t].astype(F32)
        y = 0.5 * y * (1.0 + jnp.tanh(math.sqrt(2.0 / math.pi) * (y + 0.044715 * (y * y * y))))
        y2_ref[t] = y.astype(BF16)
        return carry

    lax.fori_loop(0, tc, pass2, 0, unroll=4)

    for t in range(tc):
        yn = jnp.dot(p2t_ref[...], y2_ref[t], preferred_element_type=F32).astype(BF16)
        for m in range(nm):
            y2n_ref[m * tile + t * nr:m * tile + (t + 1) * nr, :] = yn[m * nr:(m + 1) * nr, :]
    for m in range(nm):
        rows = slice(m * tile, (m + 1) * tile)
        o_ref[rows, :] = jnp.dot(p1t_ref[...], y2n_ref[rows, :], preferred_element_type=F32).astype(BF16)


def _s5_branch(proj, col0, params, bsz, seq):
    bdb, bdc, apow, dskip = params
    nj, _, w, sl = bdb.shape
    tile = S5_CHUNK * S5_SUPER
    assert seq % tile == 0 and col0 % w == 0
    nm = seq // tile
    assert nm % V7X_SUBLANES == 0, "row slabs of the chunk recurrence must be whole sublane tiles"
    rows = S5_SUPER * nm
    i1 = jnp.arange(tile)
    p1 = jnp.zeros((tile, tile), BF16).at[(i1 % S5_CHUNK) * S5_SUPER + i1 // S5_CHUNK, i1].set(1.0)
    i2 = jnp.arange(rows)
    p2 = jnp.zeros((rows, rows), BF16).at[(i2 % S5_SUPER) * nm + i2 // S5_SUPER, i2].set(1.0)
    const = lambda a: pl.BlockSpec(a.shape, lambda b, j: (0,) * a.ndim)
    per_j = lambda a: pl.BlockSpec((1,) + a.shape[1:], lambda b, j: (j,) + (0,) * (a.ndim - 1))
    return pl.pallas_call(
        functools.partial(_s5_kernel, n_super=nm),
        grid=(bsz, nj),
        in_specs=[
            pl.BlockSpec((seq, w), lambda b, j: (b, col0 // w + j)),
            per_j(bdb), per_j(bdc), per_j(apow), per_j(dskip),
            const(p1), const(p1), const(p2), const(p2),
        ],
        out_specs=pl.BlockSpec((seq, w), lambda b, j: (b, j)),
        out_shape=jax.ShapeDtypeStruct((bsz * seq, nj * w), BF16),
        scratch_shapes=[
            pltpu.VMEM((S5_CHUNK, rows, w), BF16), pltpu.VMEM((S5_CHUNK, rows, w), BF16),
            pltpu.VMEM((rows, sl), F32), pltpu.VMEM((rows, sl), F32),
            pltpu.VMEM((rows, sl), F32), pltpu.VMEM((rows, sl), F32),
            pltpu.VMEM((nm, sl), F32), pltpu.VMEM((nm, sl), F32),
            pltpu.VMEM((S5_CHUNK, rows, w), BF16), pltpu.VMEM((seq, w), BF16),
        ],
        compiler_params=_cparams(("arbitrary", "arbitrary")),
        name="s5",
    )(proj, bdb, bdc, apow, dskip, p1, p1.T, p2, p2.T)


def _merge_kernel(vc_ref, ys_ref, l1_ref, l2_ref, cpw_ref, cpb_ref, wa_ref, wb_ref, gb1_ref, gb2_ref, o_ref):
    y_conv = jnp.dot(vc_ref[...], cpw_ref[...], preferred_element_type=F32) + cpb_ref[...]
    ys = ys_ref[...]
    a = jnp.dot(ys, wa_ref[...], preferred_element_type=F32)
    b = jnp.dot(ys, wb_ref[...], preferred_element_type=F32)
    y_s5 = a * _sigmoid(b)
    g1 = _sigmoid(l1_ref[...].astype(F32) + gb1_ref[...])
    g2 = _sigmoid(l2_ref[...].astype(F32) + gb2_ref[...])
    o_ref[...] = (g1 * y_conv + g2 * y_s5).astype(BF16)


def _merge(vc, ys, proj, cp_w, cp_b, wa, wb, gate_b, tm, col0):
    t, ch = vc.shape
    d = cp_w.shape[1]
    tn = ch
    assert col0 % tn == 0 and d % tn == 0
    cb, nj = col0 // tn, d // tn
    wcol = lambda rows: pl.BlockSpec((rows, tn), lambda j, i: (0, j))
    return pl.pallas_call(
        _merge_kernel,
        grid=(nj, t // tm),
        in_specs=[
            pl.BlockSpec((tm, ch), lambda j, i: (i, 0)),
            pl.BlockSpec((tm, ch), lambda j, i: (i, 0)),
            pl.BlockSpec((tm, tn), lambda j, i: (i, cb + j)),
            pl.BlockSpec((tm, tn), lambda j, i: (i, cb + nj + j)),
            wcol(ch), wcol(1), wcol(ch), wcol(ch), wcol(1), wcol(1),
        ],
        out_specs=pl.BlockSpec((tm, tn), lambda j, i: (i, j)),
        out_shape=jax.ShapeDtypeStruct((t, d), BF16),
        compiler_params=_cparams(("arbitrary", "arbitrary")),
        name="merge",
    )(vc, ys, proj, proj, cp_w.astype(BF16), cp_b.reshape(1, d), wa.astype(BF16), wb.astype(BF16),
      gate_b[:d].reshape(1, d), gate_b[d:].reshape(1, d))


def _split_bf16(v):
    hi = v.astype(BF16)
    lo = (v - hi.astype(F32)).astype(BF16)
    return hi, lo


def _bf16_bits_hi(v):
    b = lax.bitcast_convert_type(v, jnp.uint32)
    return (b + jnp.uint32(0x7FFF) + ((b >> 16) & jnp.uint32(1))) & jnp.uint32(0xFFFF0000)


def _pack_bf16_pairs(v):
    half = v.shape[-1] // 2
    return _bf16_bits_hi(v[:, half:]) | (_bf16_bits_hi(v[:, :half]) >> 16)


def _unpack_bf16_pairs(p):
    lo = lax.bitcast_convert_type(p << 16, F32).astype(BF16)
    hi = lax.bitcast_convert_type(p & jnp.uint32(0xFFFF0000), F32).astype(BF16)
    return lo, hi


OUTPROJ_ROWS = 256


def _outproj_kernel(m_ref, x_ref, ada_ref, wo_ref, g1_ref, g2_ref, rwh_ref, rwl_ref, rb_ref,
                    x1_ref, h2p_ref, idx_ref, gate_ref, *, ne):
    for rb in range(m_ref.shape[0] // OUTPROJ_ROWS):
        rows = slice(rb * OUTPROJ_ROWS, (rb + 1) * OUTPROJ_ROWS)
        _outproj_rows(m_ref[rows, :], x_ref[rows, :], ada_ref, wo_ref, g1_ref, g2_ref, rwh_ref, rwl_ref, rb_ref,
                      x1_ref.at[rows, :], h2p_ref.at[rows, :], idx_ref.at[rows, :], gate_ref.at[rows, :], ne)


def _outproj_rows(m_in, x_in, ada_ref, wo_ref, g1_ref, g2_ref, rwh_ref, rwl_ref, rb_ref,
                  x1_ref, h2p_ref, idx_ref, gate_ref, ne):
    m = jnp.dot(m_in, wo_ref[...], preferred_element_type=F32)
    x1 = x_in + ada_ref[0, 2:3, :] * _rms(m, g1_ref[...])
    x1_ref[...] = x1
    h2 = _rms(x1, g2_ref[...]) * (1.0 + ada_ref[0, 4:5, :]) + ada_ref[0, 3:4, :]
    h2p_ref[...] = _pack_bf16_pairs(h2)
    hh, hl = _split_bf16(h2)
    lg = jnp.dot(hh, rwh_ref[...], preferred_element_type=F32)
    lg = lg + jnp.dot(hl, rwh_ref[...], preferred_element_type=F32)
    lg = lg + jnp.dot(hh, rwl_ref[...], preferred_element_type=F32)
    lg = lg + rb_ref[...]
    lane = lax.broadcasted_iota(jnp.int32, lg.shape, 1)
    work = jnp.where(lane < ne, lg, -jnp.inf)
    vals, idxs = [], []
    for _ in range(TOP_K):
        mx = jnp.max(work, axis=-1, keepdims=True)
        ix = jnp.min(jnp.where(work == mx, lane, lg.shape[1]), axis=-1, keepdims=True)
        vals.append(mx)
        idxs.append(ix)
        work = jnp.where(lane == ix, -jnp.inf, work)
    ex = [jnp.exp(v - vals[0]) for v in vals]
    den = ex[0]
    for e in ex[1:]:
        den = den + e
    idx_out = jnp.zeros(lg.shape, jnp.int32)
    gate_out = jnp.zeros(lg.shape, F32)
    for k in range(TOP_K):
        idx_out = jnp.where(lane == k, idxs[k], idx_out)
        gate_out = jnp.where(lane == k, ex[k] / den, gate_out)
    idx_ref[...] = idx_out
    gate_ref[...] = gate_out


def _outproj(merged, x2, ada3, w_out, post_mix_g, pre_ffn_g, router_w, router_b, seq, tm):
    t, d = x2.shape
    ne = router_w.shape[1]
    nep = max(V7X_LANES, ne)
    rw = jnp.zeros((d, nep), F32).at[:, :ne].set(router_w)
    rw_hi, rw_lo = _split_bf16(rw)
    rb = jnp.zeros((1, nep), F32).at[0, :ne].set(router_b)
    per_b = seq // tm
    full = lambda shape: pl.BlockSpec(shape, lambda i: (0, 0))
    row = lambda w: pl.BlockSpec((tm, w), lambda i: (i, 0))
    x1, h2p, top_idx, gates = pl.pallas_call(
        functools.partial(_outproj_kernel, ne=ne),
        grid=(t // tm,),
        in_specs=[
            row(d), row(d),
            pl.BlockSpec((1, N_ADA, d), lambda i: (i // per_b, 0, 0)),
            full((d, d)), full((1, d)), full((1, d)), full((d, nep)), full((d, nep)), full((1, nep)),
        ],
        out_specs=[row(d), row(d // 2), row(nep), row(nep)],
        out_shape=[jax.ShapeDtypeStruct((t, d), F32), jax.ShapeDtypeStruct((t, d // 2), jnp.uint32),
                   jax.ShapeDtypeStruct((t, nep), jnp.int32), jax.ShapeDtypeStruct((t, nep), F32)],
        compiler_params=_cparams(("arbitrary",)),
        name="outproj",
    )(merged, x2, ada3, w_out.astype(BF16), post_mix_g.reshape(1, d), pre_ffn_g.reshape(1, d),
      rw_hi, rw_lo, rb)
    return x1, h2p, top_idx[:, :TOP_K], gates[:, :TOP_K]


GATHER_UNROLL = 8


def _row_copy(idx_ref, src_hbm, buf, sems, slot, r):
    return pltpu.make_async_copy(src_hbm.at[pl.ds(idx_ref[0, 0, r], 1), :],
                                 buf.at[slot, pl.ds(r, 1), :], sems.at[slot])


def _start_rows(idx_ref, src_hbm, buf, sems, slot, n, unrolled=False):
    if unrolled:
        for r in range(n):
            _row_copy(idx_ref, src_hbm, buf, sems, slot, r).start(priority=r % 2)
        return

    def start(r, carry):
        _row_copy(idx_ref, src_hbm, buf, sems, slot, r).start()
        return carry
    lax.fori_loop(0, n, start, 0, unroll=GATHER_UNROLL)


def _wait_rows(src_hbm, buf, sems, slot):
    n = buf.shape[1]
    pltpu.make_async_copy(src_hbm.at[pl.ds(0, n), :], buf.at[slot], sems.at[slot]).wait()


def _gather_kernel(tok_ref, h_hbm, o_ref, buf, sems):
    i = pl.program_id(0)
    n_blocks = pl.num_programs(0) - 1

    @pl.when(i < n_blocks)
    def _():
        _start_rows(tok_ref, h_hbm, buf, sems, i % 2, MOE_BLOCK, unrolled=True)

    @pl.when(i > 0)
    def _():
        slot = (i - 1) % 2
        _wait_rows(h_hbm, buf, sems, slot)
        o_ref[...] = buf[slot]


def _moe_gather(tok_buf, h2p):
    n_blocks = tok_buf.shape[0]
    t, w = h2p.shape
    return pl.pallas_call(
        _gather_kernel,
        grid=(n_blocks + 1,),
        in_specs=[
            pl.BlockSpec((1, 1, MOE_BLOCK), lambda i: (jnp.minimum(i, n_blocks - 1), 0, 0),
                         memory_space=pltpu.SMEM),
            pl.BlockSpec(memory_space=pl.ANY),
        ],
        out_specs=pl.BlockSpec((MOE_BLOCK, w), lambda i: (jnp.maximum(i - 1, 0), 0)),
        out_shape=jax.ShapeDtypeStruct((n_blocks * MOE_BLOCK, w), h2p.dtype),
        scratch_shapes=[pltpu.VMEM((2, MOE_BLOCK, w), h2p.dtype), pltpu.SemaphoreType.DMA((2,))],
        compiler_params=_cparams(("arbitrary",)),
        name="moe_gather",
    )(tok_buf.reshape(n_blocks, 1, MOE_BLOCK), h2p)


def _group_weights(gf_ref, gn_ref, e_ref, i, first_step, more_passes, copy_of, gcnt):
    @pl.when(first_step)
    def _():
        gcnt[0] = 0
        copy_of(e_ref[0], False, 0).start()

    slot = gcnt[0] % 2

    @pl.when(gf_ref[i] == 1)
    def _():
        copy_of(e_ref[0], False, slot).wait()
        nxt = gn_ref[i]

        @pl.when(nxt >= 0)
        def _():
            copy_of(nxt, False, 1 - slot).start()

        @pl.when(jnp.logical_and(nxt < 0, more_passes))
        def _():
            copy_of(e_ref[0], True, 1 - slot).start()

        gcnt[0] = gcnt[0] + 1

    return slot


def _up_kernel(e_ref, nv_ref, gf_ref, gn_ref, x_ref, w_hbm, bg_ref, bl_ref, p_ref, o_ref,
               wp_ref, wbuf, wsem, gcnt):
    j, i = pl.program_id(0), pl.program_id(1)
    tn = wp_ref.shape[1]
    half = V7X_MXU // 2
    hd = x_ref.shape[1]

    def copy_of(expert, next_pass, slot):
        col = (j + 1) * tn if next_pass else j * tn
        return pltpu.make_async_copy(w_hbm.at[expert, :, pl.ds(col, tn)], wbuf.at[slot], wsem.at[slot])

    slot = _group_weights(gf_ref, gn_ref, e_ref, i, jnp.logical_and(j == 0, i == 0),
                          j < pl.num_programs(0) - 1, copy_of, gcnt)

    @pl.when(gf_ref[i] == 1)
    def _():
        for cb in range(tn // V7X_MXU):
            cols = slice(cb * V7X_MXU, (cb + 1) * V7X_MXU)
            wp_ref[:, cols] = jnp.dot(wbuf[slot, :, cols].astype(BF16), p_ref[...],
                                      preferred_element_type=F32).astype(BF16)

    @pl.when(nv_ref[i] > 0)
    def _():
        x_lo, x_hi = _unpack_bf16_pairs(x_ref[...])
        for cb in range(tn // V7X_MXU):
            cols = slice(cb * V7X_MXU, (cb + 1) * V7X_MXU)
            hb = (jnp.dot(x_lo, wp_ref[0:hd, cols], preferred_element_type=F32)
                  + jnp.dot(x_hi, wp_ref[hd:2 * hd, cols], preferred_element_type=F32))
            out_cols = slice(cb * half, (cb + 1) * half)
            x_glu = hb[:, :half] + bg_ref[0, :, out_cols]
            x_lin = hb[:, half:] + bl_ref[0, :, out_cols]
            x_glu = jnp.minimum(x_glu, SWIGLU_LIMIT)
            x_lin = jnp.clip(x_lin, -SWIGLU_LIMIT, SWIGLU_LIMIT)
            act = x_glu * _sigmoid(SWIGLU_ALPHA * x_glu) * (x_lin + 1.0)
            o_ref[:, out_cols] = act.astype(BF16)

    @pl.when(nv_ref[i] == 0)
    def _():
        o_ref[...] = jnp.zeros(o_ref.shape, o_ref.dtype)


def _moe_up(blk_e, nvalid, gfirst, gnext, xs, w1, b1, tn):
    n_rows, hd = xs.shape
    ne, d, f2 = w1.shape
    assert d == 2 * hd
    f = f2 // 2
    n_blocks = n_rows // MOE_BLOCK
    half = V7X_MXU // 2
    c = jnp.arange(V7X_MXU)
    perm = jnp.zeros((V7X_MXU, V7X_MXU), BF16).at[c, (c % 2) * half + c // 2].set(1.0)
    b1g = b1[:, 0::2].reshape(ne, 1, f)
    b1l = b1[:, 1::2].reshape(ne, 1, f)
    return pl.pallas_call(
        _up_kernel,
        grid_spec=pltpu.PrefetchScalarGridSpec(
            num_scalar_prefetch=4,
            grid=(f2 // tn, n_blocks),
            in_specs=[
                pl.BlockSpec((MOE_BLOCK, hd), lambda j, i, e, nv, gf, gn: (i, 0)),
                pl.BlockSpec(memory_space=pl.ANY),
                pl.BlockSpec((1, 1, tn // 2), lambda j, i, e, nv, gf, gn: (e[i], 0, j)),
                pl.BlockSpec((1, 1, tn // 2), lambda j, i, e, nv, gf, gn: (e[i], 0, j)),
                pl.BlockSpec((V7X_MXU, V7X_MXU), lambda j, i, e, nv, gf, gn: (0, 0)),
            ],
            out_specs=pl.BlockSpec((MOE_BLOCK, tn // 2), lambda j, i, e, nv, gf, gn: (i, j)),
            scratch_shapes=[pltpu.VMEM((d, tn), BF16), pltpu.VMEM((2, d, tn), F32),
                            pltpu.SemaphoreType.DMA((2,)), pltpu.SMEM((1,), jnp.int32)],
        ),
        out_shape=jax.ShapeDtypeStruct((n_rows, f), BF16),
        compiler_params=_cparams(("arbitrary", "arbitrary")),
        name="moe_up",
    )(blk_e, nvalid, gfirst, gnext, xs, w1, b1g, b1l, perm)


def _down_kernel(e_ref, nv_ref, gf_ref, gn_ref, a_ref, w_hbm, b_ref, g_ref, o_ref, wb_ref, wbuf, wsem, gcnt):
    i = pl.program_id(0)

    def copy_of(expert, next_pass, slot):
        return pltpu.make_async_copy(w_hbm.at[expert], wbuf.at[slot], wsem.at[slot])

    slot = _group_weights(gf_ref, gn_ref, e_ref, i, i == 0, False, copy_of, gcnt)

    @pl.when(gf_ref[i] == 1)
    def _():
        wb_ref[...] = wbuf[slot].astype(BF16)

    @pl.when(nv_ref[i] > 0)
    def _():
        y = jnp.dot(a_ref[...], wb_ref[...], preferred_element_type=F32) + b_ref[0]
        o_ref[...] = y * g_ref[...]

    @pl.when(nv_ref[i] == 0)
    def _():
        o_ref[...] = jnp.zeros(o_ref.shape, o_ref.dtype)


def _moe_down(blk_e, nvalid, gfirst, gnext, act, w2, b2, g_buf):
    n_rows, f = act.shape
    ne, _, d = w2.shape
    n_blocks = n_rows // MOE_BLOCK
    return pl.pallas_call(
        _down_kernel,
        grid_spec=pltpu.PrefetchScalarGridSpec(
            num_scalar_prefetch=4,
            grid=(n_blocks,),
            in_specs=[
                pl.BlockSpec((MOE_BLOCK, f), lambda i, e, nv, gf, gn: (i, 0)),
                pl.BlockSpec(memory_space=pl.ANY),
                pl.BlockSpec((1, 1, d), lambda i, e, nv, gf, gn: (e[i], 0, 0)),
                pl.BlockSpec((MOE_BLOCK, 1), lambda i, e, nv, gf, gn: (i, 0)),
            ],
            out_specs=pl.BlockSpec((MOE_BLOCK, d), lambda i, e, nv, gf, gn: (i, 0)),
            scratch_shapes=[pltpu.VMEM((f, d), BF16), pltpu.VMEM((2, f, d), F32),
                            pltpu.SemaphoreType.DMA((2,)), pltpu.SMEM((1,), jnp.int32)],
        ),
        out_shape=jax.ShapeDtypeStruct((n_rows, d), F32),
        compiler_params=_cparams(("arbitrary",)),
        name="moe_down",
    )(blk_e, nvalid, gfirst, gnext, act, w2, b2.reshape(ne, 1, d), g_buf.reshape(n_rows, 1))


def _combine_kernel(pos_ref, y_hbm, x1_ref, ada_ref, g_ref, o_ref, buf, sems, *, tt):
    i = pl.program_id(0)
    n_tiles = pl.num_programs(0) - 1
    n_rows = TOP_K * tt

    def finish(slot):
        f = buf[slot, 0:tt, :]
        for k in range(1, TOP_K):
            f = f + buf[slot, k * tt:(k + 1) * tt, :]
        o_ref[...] = x1_ref[...] + ada_ref[0, 5:6, :] * _rms(f, g_ref[...])

    @pl.when(i == 0)
    def _():
        _start_rows(pos_ref, y_hbm, buf, sems, 0, n_rows)

    @pl.when(jnp.logical_and(i > 0, i < n_tiles))
    def _():
        _wait_rows(y_hbm, buf, sems, (i - 1) % 2)
        _start_rows(pos_ref, y_hbm, buf, sems, i % 2, n_rows, unrolled=True)
        finish((i - 1) % 2)

    @pl.when(i == n_tiles)
    def _():
        _wait_rows(y_hbm, buf, sems, (i - 1) % 2)
        finish((i - 1) % 2)


def _moe_combine(pos, y_buf, x1, ada3, post_ffn_g, seq, tt):
    t, d = x1.shape
    per_b = seq // tt
    nt = t // tt
    pos_t = pos.reshape(nt, tt, TOP_K).transpose(0, 2, 1).reshape(nt, 1, TOP_K * tt)
    prev = lambda i: jnp.maximum(i - 1, 0)
    return pl.pallas_call(
        functools.partial(_combine_kernel, tt=tt),
        grid=(nt + 1,),
        in_specs=[
            pl.BlockSpec((1, 1, TOP_K * tt), lambda i: (jnp.minimum(i, nt - 1), 0, 0), memory_space=pltpu.SMEM),
            pl.BlockSpec(memory_space=pl.ANY),
            pl.BlockSpec((tt, d), lambda i: (prev(i), 0)),
            pl.BlockSpec((1, N_ADA, d), lambda i: (prev(i) // per_b, 0, 0)),
            pl.BlockSpec((1, d), lambda i: (0, 0)),
        ],
        out_specs=pl.BlockSpec((tt, d), lambda i: (prev(i), 0)),
        out_shape=jax.ShapeDtypeStruct((t, d), F32),
        scratch_shapes=[pltpu.VMEM((2, TOP_K * tt, d), F32), pltpu.SemaphoreType.DMA((2,))],
        compiler_params=_cparams(("arbitrary",)),
        name="moe_combine",
    )(pos_t, y_buf, x1, ada3, post_ffn_g.reshape(1, d))


def _route(top_idx, gates, n_experts):
    n_tok = top_idx.shape[0]
    n_asg = n_tok * TOP_K
    i32 = jnp.int32
    flat_e = top_idx.reshape(n_asg)
    order = jnp.argsort(flat_e).astype(i32)
    counts = jnp.sum(flat_e[:, None] == jnp.arange(n_experts, dtype=i32)[None, :], axis=0, dtype=i32)
    starts = jnp.cumsum(counts) - counts
    padded = (counts + MOE_BLOCK - 1) // MOE_BLOCK * MOE_BLOCK
    pends = jnp.cumsum(padded)
    pstarts = pends - padded
    n_blocks = -(-n_asg // MOE_BLOCK) + n_experts
    blk_start = jnp.arange(n_blocks, dtype=i32) * MOE_BLOCK
    blk_e = jnp.minimum(jnp.sum(pends[None, :] <= blk_start[:, None], axis=1, dtype=i32), n_experts - 1)
    off = blk_start - pstarts[blk_e]
    nvalid = jnp.clip(counts[blk_e] - off, 0, MOE_BLOCK)
    j = off[:, None] + jnp.arange(MOE_BLOCK, dtype=i32)[None, :]
    valid = j < counts[blk_e][:, None]
    src = jnp.clip(starts[blk_e][:, None] + j, 0, n_asg - 1)
    asg = order[src]
    tok_buf = jnp.where(valid, asg // TOP_K, 0)
    g_buf = jnp.where(valid, gates.reshape(n_asg)[asg], 0.0)
    rank = jnp.argsort(order).astype(i32)
    pos = pstarts[flat_e] + rank - starts[flat_e]
    blk = jnp.arange(n_blocks, dtype=i32)
    prev_e = jnp.concatenate([jnp.full((1,), -1, i32), blk_e[:-1]])
    gfirst = jnp.logical_and(nvalid > 0, blk_e != prev_e).astype(i32)
    cand = jnp.where(gfirst == 1, blk, n_blocks)
    later = jnp.concatenate([lax.cummin(cand[::-1])[::-1][1:], jnp.full((1,), n_blocks, i32)])
    gnext = jnp.where(later < n_blocks, blk_e[jnp.minimum(later, n_blocks - 1)], -1)
    return tok_buf, g_buf, pos, blk_e, nvalid, gfirst, gnext


def _pick(n, pref):
    return pref if n % pref == 0 else n


def kernel(x, c, ada_w, ada_b, pre_mix_g, post_mix_g, pre_ffn_g, post_ffn_g, w_in, gate_b, dw_w, dw_b, cln_g, cln_b, cp_w, cp_b, s5_a_re, s5_a_im, s5_log_dt, s5_b_re, s5_b_im, s5_c_re, s5_c_im, s5_d, glu_wa, glu_wb, w_out, router_w, router_b, w1, b1, w2, b2):
    bsz, seq, d = x.shape
    t = bsz * seq
    depth = ada_w.shape[0]
    conv_ch = dw_w.shape[-1]
    ng, ns, nh = s5_b_re.shape[1:]
    s5_w = ng * nh
    ne = router_w.shape[-1]
    col_s5 = 2 * conv_ch
    col_gate = col_s5 + s5_w
    assert conv_ch == s5_w and V7X_MXU % nh == 0 and ng % (V7X_MXU // nh) == 0
    tm = _pick(seq, 512)
    tm_out = _pick(seq, 256)
    tn_in = _pick(w_in.shape[-1], conv_ch)

    x2 = x.reshape(t, d)
    for l in range(depth):
        ada3 = _ada(c, ada_w[l], ada_b[l]).reshape(bsz, N_ADA, d)
        proj = _inproj(x2, ada3, pre_mix_g[l], w_in[l].astype(BF16), seq, _pick(seq, 1024), tn_in)
        vc = _conv_branch(proj, dw_w[l], dw_b[l], cln_g[l], cln_b[l], bsz, seq, tm)

        s5_params = _s5_params(s5_a_re[l], s5_a_im[l], s5_log_dt[l], s5_b_re[l], s5_b_im[l],
                               s5_c_re[l], s5_c_im[l], s5_d[l])
        ys = _s5_branch(proj, col_s5, s5_params, bsz, seq)

        merged = _merge(vc, ys, proj, cp_w[l], cp_b[l], glu_wa[l], glu_wb[l], gate_b[l], tm, col_gate)
        x1, h2p, top_idx, gates = _outproj(merged, x2, ada3, w_out[l], post_mix_g[l], pre_ffn_g[l],
                                           router_w[l], router_b[l], seq, tm_out)

        tok_buf, g_buf, pos, blk_e, nvalid, gfirst, gnext = _route(top_idx, gates, ne)
        xs = _moe_gather(tok_buf, h2p)
        act = _moe_up(blk_e, nvalid, gfirst, gnext, xs, w1[l], b1[l], _pick(w1.shape[-1], 2048))
        y_buf = _moe_down(blk_e, nvalid, gfirst, gnext, act, w2[l], b2[l], g_buf)
        x2 = _moe_combine(pos, y_buf, x1, ada3, post_ffn_g[l], seq, _pick(seq, 128))
    return x2.reshape(bsz, seq, d)
```

```python
import functools
import math

import jax
import jax.numpy as jnp
from jax import lax
from jax.experimental import pallas as pl
from jax.experimental.pallas import tpu as pltpu

EPS = 1e-6
N_ADA = 6
TOP_K = 4
MOE_BLOCK = 256
SWIGLU_ALPHA = 1.702
SWIGLU_LIMIT = 7.0
S5_CHUNK = 16
V7X_LANES = 128
V7X_SUBLANES = 8
V7X_MXU = 256
VMEM_LIMIT = 56 * 1024 * 1024

F32 = jnp.float32
BF16 = jnp.bfloat16


def _cparams(sem):
    return pltpu.CompilerParams(dimension_semantics=sem, vmem_limit_bytes=VMEM_LIMIT)


def _sigmoid(v):
    return 1.0 / (1.0 + jnp.exp(-v))


def _rms(v, g):
    return v * lax.rsqrt(jnp.mean(v * v, axis=-1, keepdims=True) + EPS) * g


def _ada_kernel(c_ref, w_ref, b_ref, o_ref):
    c = c_ref[...]
    s = (c * _sigmoid(c)).astype(BF16)
    o_ref[...] = jnp.dot(s, w_ref[...].astype(BF16), preferred_element_type=F32) + b_ref[...]


def _ada(c, ada_w, ada_b):
    bsz, d = c.shape
    n = ada_w.shape[1]
    tn = 1024 if n % 1024 == 0 else n
    return pl.pallas_call(
        _ada_kernel,
        grid=(n // tn,),
        in_specs=[
            pl.BlockSpec((bsz, d), lambda j: (0, 0)),
            pl.BlockSpec((d, tn), lambda j: (0, j)),
            pl.BlockSpec((1, tn), lambda j: (0, j)),
        ],
        out_specs=pl.BlockSpec((bsz, tn), lambda j: (0, j)),
        out_shape=jax.ShapeDtypeStruct((bsz, n), F32),
        compiler_params=_cparams(("arbitrary",)),
        name="ada",
    )(c, ada_w, ada_b.reshape(1, n))


def _inproj_kernel(x_ref, ada_ref, g_ref, w_ref, o_ref, h_ref):
    @pl.when(pl.program_id(1) == 0)
    def _():
        y = _rms(x_ref[...], g_ref[...])
        h = y * (1.0 + ada_ref[0, 1:2, :]) + ada_ref[0, 0:1, :]
        h_ref[...] = h.astype(BF16)

    o_ref[...] = jnp.dot(h_ref[...], w_ref[...].astype(BF16), preferred_element_type=F32).astype(BF16)


def _inproj(x2, ada3, g, w_bf, seq, tm, tn):
    t, d = x2.shape
    n = w_bf.shape[1]
    per_b = seq // tm
    return pl.pallas_call(
        _inproj_kernel,
        grid=(t // tm, n // tn),
        in_specs=[
            pl.BlockSpec((tm, d), lambda i, j: (i, 0)),
            pl.BlockSpec((1, N_ADA, d), lambda i, j: (i // per_b, 0, 0)),
            pl.BlockSpec((1, d), lambda i, j: (0, 0)),
            pl.BlockSpec((d, tn), lambda i, j: (0, j)),
        ],
        out_specs=pl.BlockSpec((tm, tn), lambda i, j: (i, j)),
        out_shape=jax.ShapeDtypeStruct((t, n), BF16),
        scratch_shapes=[pltpu.VMEM((tm, d), BF16)],
        compiler_params=_cparams(("arbitrary", "arbitrary")),
        name="inproj",
    )(x2, ada3, g.reshape(1, d), w_bf)


CONV_HALO = 32
CONV_ROWS = 64


def _conv_kernel(pv_ref, pg_ref, w_ref, b_ref, g_ref, beta_ref, o_ref, vext, vsh, cbuf, *, taps, tl):
    t = pl.program_id(1)
    ch = cbuf.shape[1]

    @pl.when(t == 0)
    def _():
        vext[0:CONV_HALO, :] = jnp.zeros((CONV_HALO, ch), F32)

    @pl.when(t > 0)
    def _():
        vext[0:CONV_HALO, :] = vext[tl:tl + CONV_HALO, :]

    pv = pv_ref[...].astype(F32)
    pg = pg_ref[...].astype(F32)
    vext[CONV_HALO:CONV_HALO + tl, :] = pv * _sigmoid(pg)

    n_sh = vsh.shape[1]
    for p in range(1, V7X_SUBLANES):
        vsh[p - 1] = vext[p:p + n_sh, :]

    off = CONV_HALO - (taps - 1)
    for cb in range(ch // V7X_LANES):
        lanes = slice(cb * V7X_LANES, (cb + 1) * V7X_LANES)

        for rc in range(tl // CONV_ROWS):
            r0 = rc * CONV_ROWS
            acc = jnp.zeros((CONV_ROWS, V7X_LANES), F32)
            for k in range(taps):
                p = (off + k) % V7X_SUBLANES
                base = r0 + off + k - p
                win = vext[base:base + CONV_ROWS, lanes] if p == 0 else vsh[p - 1, base:base + CONV_ROWS, lanes]
                acc = acc + w_ref[k:k + 1, lanes] * win
            cbuf[r0:r0 + CONV_ROWS, lanes] = acc + b_ref[:, lanes]

    v = cbuf[...]
    mu = jnp.mean(v, axis=-1, keepdims=True)
    xc = v - mu
    var = jnp.mean(xc * xc, axis=-1, keepdims=True)
    y = xc * lax.rsqrt(var + EPS) * g_ref[...] + beta_ref[...]
    o_ref[...] = (y * _sigmoid(y)).astype(BF16)


def _conv_branch(proj, dw_w, dw_b, cln_g, cln_b, bsz, seq, tl):
    taps, ch = dw_w.shape
    assert taps - 1 <= CONV_HALO and tl % CONV_ROWS == 0 and ch % V7X_LANES == 0
    per_b = seq // tl
    vec = lambda a: a.reshape(1, ch)
    return pl.pallas_call(
        functools.partial(_conv_kernel, taps=taps, tl=tl),
        grid=(bsz, per_b),
        in_specs=[
            pl.BlockSpec((tl, ch), lambda b, t: (b * per_b + t, 0)),
            pl.BlockSpec((tl, ch), lambda b, t: (b * per_b + t, 1)),
            pl.BlockSpec((taps, ch), lambda b, t: (0, 0)),
            pl.BlockSpec((1, ch), lambda b, t: (0, 0)),
            pl.BlockSpec((1, ch), lambda b, t: (0, 0)),
            pl.BlockSpec((1, ch), lambda b, t: (0, 0)),
        ],
        out_specs=pl.BlockSpec((tl, ch), lambda b, t: (b * per_b + t, 0)),
        out_shape=jax.ShapeDtypeStruct((bsz * seq, ch), BF16),
        scratch_shapes=[pltpu.VMEM((CONV_HALO + tl, ch), F32),
                        pltpu.VMEM((V7X_SUBLANES - 1, CONV_HALO + tl - V7X_SUBLANES, ch), F32),
                        pltpu.VMEM((tl, ch), F32)],
        compiler_params=_cparams(("arbitrary", "arbitrary")),
        name="conv",
    )(proj, proj, dw_w, vec(dw_b), vec(cln_g), vec(cln_b))


S5_SUPER = 16


def _s5_params(a_re, a_im, log_dt, b_re, b_im, c_re, c_im, d_skip):
    ng, ns = a_re.shape
    nh = b_re.shape[-1]
    gt = V7X_MXU // nh
    nj = ng // gt
    dt = jnp.exp(log_dt)[:, None]
    mag = jnp.exp(a_re * dt)
    abar_re = mag * jnp.cos(a_im * dt)
    abar_im = mag * jnp.sin(a_im * dt)
    den = a_re * a_re + a_im * a_im
    num_re = abar_re - 1.0
    coef_re = (num_re * a_re + abar_im * a_im) / den
    coef_im = (abar_im * a_re - num_re * a_im) / den
    bb_re = coef_re[:, :, None] * b_re - coef_im[:, :, None] * b_im
    bb_im = coef_re[:, :, None] * b_im + coef_im[:, :, None] * b_re
    p = jnp.asarray([1.0, S5_CHUNK, S5_CHUNK * S5_SUPER] + [S5_CHUNK * r for r in range(S5_SUPER)], F32)
    pmag = jnp.exp(p[:, None, None] * (a_re * dt)[None])
    pw_re = pmag * jnp.cos(p[:, None, None] * (a_im * dt)[None])
    pw_im = pmag * jnp.sin(p[:, None, None] * (a_im * dt)[None])
    tile = lambda a: a.reshape(-1, nj, gt * ns).transpose(1, 0, 2)
    apow = jnp.concatenate([tile(pw_re), tile(pw_im)], axis=1)
    eye = jnp.eye(gt, dtype=F32)

    def bdiag_in(bb):
        return jnp.einsum('jgnh,gk->jghkn', bb.reshape(nj, gt, ns, nh), eye).reshape(nj, gt * nh, gt * ns)

    def bdiag_out(cc):
        return jnp.einsum('jghn,gk->jgnkh', cc.reshape(nj, gt, nh, ns), eye).reshape(nj, gt * ns, gt * nh)

    bdb = jnp.stack([bdiag_in(bb_re), bdiag_in(bb_im)], axis=1)
    bdc = jnp.stack([bdiag_out(c_re), bdiag_out(-c_im)], axis=1)
    return bdb.astype(BF16), bdc.astype(BF16), apow, d_skip.reshape(nj, 1, gt * nh)


def _cmul_add(ar, ai, xr, xi, vr, vi):
    return ar * xr - ai * xi + vr, ar * xi + ai * xr + vi


def _s5_kernel(u_ref, bdb_ref, bdc_ref, ap_ref, d_ref, p1_ref, p1t_ref, p2_ref, p2t_ref, o_ref,
               u2n_ref, u2_ref, hr_ref, hi_ref, zr_ref, zi_ref, gr_ref, gi_ref, y2_ref, y2n_ref,
               *, n_super):
    tc, nr, nm = S5_CHUNK, S5_SUPER, n_super
    tile = tc * nr
    np_ = 3 + nr
    apow = lambda k: (ap_ref[0, k:k + 1, :], ap_ref[0, np_ + k:np_ + k + 1, :])
    ar, ai = apow(0)

    for m in range(nm):
        pu = jnp.dot(p1_ref[...], u_ref[m * tile:(m + 1) * tile, :], preferred_element_type=F32).astype(BF16)
        for s in range(tc):
            u2n_ref[s, m * nr:(m + 1) * nr, :] = pu[s * nr:(s + 1) * nr, :]
    for s in range(tc):
        u2_ref[s] = jnp.dot(p2_ref[...], u2n_ref[s], preferred_element_type=F32).astype(BF16)

    def drive(s):
        u = u2_ref[s]
        return (jnp.dot(u, bdb_ref[0, 0], preferred_element_type=F32),
                jnp.dot(u, bdb_ref[0, 1], preferred_element_type=F32))

    hr_ref[...] = jnp.zeros(hr_ref.shape, F32)
    hi_ref[...] = jnp.zeros(hi_ref.shape, F32)

    def pass1(s, carry):
        vr, vi = drive(s)
        hr, hi = _cmul_add(ar, ai, hr_ref[...], hi_ref[...], vr, vi)
        hr_ref[...] = hr
        hi_ref[...] = hi
        return carry

    lax.fori_loop(0, tc, pass1, 0, unroll=4)

    a16r, a16i = apow(1)
    qr = jnp.zeros((nm, hr_ref.shape[1]), F32)
    qi = qr
    for r in range(nr):
        rows = slice(r * nm, (r + 1) * nm)
        zr_ref[rows, :] = qr
        zi_ref[rows, :] = qi
        qr, qi = _cmul_add(a16r, a16i, qr, qi, hr_ref[rows, :], hi_ref[rows, :])
    a256r, a256i = apow(2)
    gr = jnp.zeros((1, hr_ref.shape[1]), F32)
    gi = gr
    for m in range(nm):
        gr_ref[m:m + 1, :] = gr
        gi_ref[m:m + 1, :] = gi
        gr, gi = _cmul_add(a256r, a256i, gr, gi, qr[m:m + 1, :], qi[m:m + 1, :])
    gpr = gr_ref[...]
    gpi = gi_ref[...]
    for r in range(nr):
        rows = slice(r * nm, (r + 1) * nm)
        pr, pi = apow(3 + r)
        zr, zi = _cmul_add(pr, pi, gpr, gpi, zr_ref[rows, :], zi_ref[rows, :])
        zr_ref[rows, :] = zr
        zi_ref[rows, :] = zi

    def pass2(t, carry):
        vr, vi = drive(t)
        sr, si = _cmul_add(ar, ai, zr_ref[...], zi_ref[...], vr, vi)
        zr_ref[...] = sr
        zi_ref[...] = si
        y = (jnp.dot(sr.astype(BF16), bdc_ref[0, 0], preferred_element_type=F32)
             + jnp.dot(si.astype(BF16), bdc_ref[0, 1], preferred_element_type=F32))
        y = y + d_ref[0] * u2_ref[t].astype(F32)
        y = 0.5 * y * (1.0 + jnp.tanh(math.sqrt(2.0 / math.pi) * (y + 0.044715 * (y * y * y))))
        y2_ref[t] = y.astype(BF16)
        return carry

    lax.fori_loop(0, tc, pass2, 0, unroll=4)

    for t in range(tc):
        yn = jnp.dot(p2t_ref[...], y2_ref[t], preferred_element_type=F32).astype(BF16)
        for m in range(nm):
            y2n_ref[m * tile + t * nr:m * tile + (t + 1) * nr, :] = yn[m * nr:(m + 1) * nr, :]
    for m in range(nm):
        rows = slice(m * tile, (m + 1) * tile)
        o_ref[rows, :] = jnp.dot(p1t_ref[...], y2n_ref[rows, :], preferred_element_type=F32).astype(BF16)


def _s5_branch(proj, col0, params, bsz, seq):
    bdb, bdc, apow, dskip = params
    nj, _, w, sl = bdb.shape
    tile = S5_CHUNK * S5_SUPER
    assert seq % tile == 0 and col0 % w == 0
    nm = seq // tile
    assert nm % V7X_SUBLANES == 0, "row slabs of the chunk recurrence must be whole sublane tiles"
    rows = S5_SUPER * nm
    i1 = jnp.arange(tile)
    p1 = jnp.zeros((tile, tile), BF16).at[(i1 % S5_CHUNK) * S5_SUPER + i1 // S5_CHUNK, i1].set(1.0)
    i2 = jnp.arange(rows)
    p2 = jnp.zeros((rows, rows), BF16).at[(i2 % S5_SUPER) * nm + i2 // S5_SUPER, i2].set(1.0)
    const = lambda a: pl.BlockSpec(a.shape, lambda b, j: (0,) * a.ndim)
    per_j = lambda a: pl.BlockSpec((1,) + a.shape[1:], lambda b, j: (j,) + (0,) * (a.ndim - 1))
    return pl.pallas_call(
        functools.partial(_s5_kernel, n_super=nm),
        grid=(bsz, nj),
        in_specs=[
            pl.BlockSpec((seq, w), lambda b, j: (b, col0 // w + j)),
            per_j(bdb), per_j(bdc), per_j(apow), per_j(dskip),
            const(p1), const(p1), const(p2), const(p2),
        ],
        out_specs=pl.BlockSpec((seq, w), lambda b, j: (b, j)),
        out_shape=jax.ShapeDtypeStruct((bsz * seq, nj * w), BF16),
        scratch_shapes=[
            pltpu.VMEM((S5_CHUNK, rows, w), BF16), pltpu.VMEM((S5_CHUNK, rows, w), BF16),
            pltpu.VMEM((rows, sl), F32), pltpu.VMEM((rows, sl), F32),
            pltpu.VMEM((rows, sl), F32), pltpu.VMEM((rows, sl), F32),
            pltpu.VMEM((nm, sl), F32), pltpu.VMEM((nm, sl), F32),
            pltpu.VMEM((S5_CHUNK, rows, w), BF16), pltpu.VMEM((seq, w), BF16),
        ],
        compiler_params=_cparams(("arbitrary", "arbitrary")),
        name="s5",
    )(proj, bdb, bdc, apow, dskip, p1, p1.T, p2, p2.T)


def _merge_kernel(vc_ref, ys_ref, l1_ref, l2_ref, cpw_ref, cpb_ref, wa_ref, wb_ref, gb1_ref, gb2_ref, o_ref):
    y_conv = jnp.dot(vc_ref[...], cpw_ref[...], preferred_element_type=F32) + cpb_ref[...]
    ys = ys_ref[...]
    a = jnp.dot(ys, wa_ref[...], preferred_element_type=F32)
    b = jnp.dot(ys, wb_ref[...], preferred_element_type=F32)
    y_s5 = a * _sigmoid(b)
    g1 = _sigmoid(l1_ref[...].astype(F32) + gb1_ref[...])
    g2 = _sigmoid(l2_ref[...].astype(F32) + gb2_ref[...])
    o_ref[...] = (g1 * y_conv + g2 * y_s5).astype(BF16)


def _merge(vc, ys, proj, cp_w, cp_b, wa, wb, gate_b, tm, col0):
    t, ch = vc.shape
    d = cp_w.shape[1]
    tn = ch
    assert col0 % tn == 0 and d % tn == 0
    cb, nj = col0 // tn, d // tn
    wcol = lambda rows: pl.BlockSpec((rows, tn), lambda j, i: (0, j))
    return pl.pallas_call(
        _merge_kernel,
        grid=(nj, t // tm),
        in_specs=[
            pl.BlockSpec((tm, ch), lambda j, i: (i, 0)),
            pl.BlockSpec((tm, ch), lambda j, i: (i, 0)),
            pl.BlockSpec((tm, tn), lambda j, i: (i, cb + j)),
            pl.BlockSpec((tm, tn), lambda j, i: (i, cb + nj + j)),
            wcol(ch), wcol(1), wcol(ch), wcol(ch), wcol(1), wcol(1),
        ],
        out_specs=pl.BlockSpec((tm, tn), lambda j, i: (i, j)),
        out_shape=jax.ShapeDtypeStruct((t, d), BF16),
        compiler_params=_cparams(("arbitrary", "arbitrary")),
        name="merge",
    )(vc, ys, proj, proj, cp_w.astype(BF16), cp_b.reshape(1, d), wa.astype(BF16), wb.astype(BF16),
      gate_b[:d].reshape(1, d), gate_b[d:].reshape(1, d))


def _split_bf16(v):
    hi = v.astype(BF16)
    lo = (v - hi.astype(F32)).astype(BF16)
    return hi, lo


def _bf16_bits_hi(v):
    b = lax.bitcast_convert_type(v, jnp.uint32)
    return (b + jnp.uint32(0x7FFF) + ((b >> 16) & jnp.uint32(1))) & jnp.uint32(0xFFFF0000)


def _pack_bf16_pairs(v):
    half = v.shape[-1] // 2
    return _bf16_bits_hi(v[:, half:]) | (_bf16_bits_hi(v[:, :half]) >> 16)


def _unpack_bf16_pairs(p):
    lo = lax.bitcast_convert_type(p << 16, F32).astype(BF16)
    hi = lax.bitcast_convert_type(p & jnp.uint32(0xFFFF0000), F32).astype(BF16)
    return lo, hi


def _outproj_kernel(m_ref, x_ref, ada_ref, wo_ref, g1_ref, g2_ref, rwh_ref, rwl_ref, rb_ref,
                    x1_ref, h2p_ref, idx_ref, gate_ref, acc_a, acc_b, *, ne):
    i = pl.program_id(0)
    n = pl.num_programs(0) - 1
    even = i % 2 == 0

    def matmul(acc):
        acc[...] = jnp.dot(m_ref[...], wo_ref[...], preferred_element_type=F32)

    def epilogue(acc):
        _outproj_epilogue(acc[...], x_ref[...], ada_ref, g1_ref, g2_ref, rwh_ref, rwl_ref, rb_ref,
                          x1_ref, h2p_ref, idx_ref, gate_ref, ne)

    @pl.when(i == 0)
    def _():
        matmul(acc_a)

    for parity, cur, prev in ((True, acc_a, acc_b), (False, acc_b, acc_a)):
        @pl.when(jnp.logical_and(jnp.logical_and(i > 0, i < n), even == parity))
        def _():
            matmul(cur)
            epilogue(prev)

        @pl.when(jnp.logical_and(i == n, even == parity))
        def _():
            epilogue(prev)


def _outproj_epilogue(m, x_in, ada_ref, g1_ref, g2_ref, rwh_ref, rwl_ref, rb_ref,
                      x1_ref, h2p_ref, idx_ref, gate_ref, ne):
    x1 = x_in + ada_ref[0, 2:3, :] * _rms(m, g1_ref[...])
    x1_ref[...] = x1
    h2 = _rms(x1, g2_ref[...]) * (1.0 + ada_ref[0, 4:5, :]) + ada_ref[0, 3:4, :]
    h2p_ref[...] = _pack_bf16_pairs(h2)
    hh, hl = _split_bf16(h2)
    lg = jnp.dot(hh, rwh_ref[...], preferred_element_type=F32)
    lg = lg + jnp.dot(hl, rwh_ref[...], preferred_element_type=F32)
    lg = lg + jnp.dot(hh, rwl_ref[...], preferred_element_type=F32)
    lg = lg + rb_ref[...]
    lane = lax.broadcasted_iota(jnp.int32, lg.shape, 1)
    work = jnp.where(lane < ne, lg, -jnp.inf)
    vals, idxs = [], []
    for _ in range(TOP_K):
        mx = jnp.max(work, axis=-1, keepdims=True)
        ix = jnp.min(jnp.where(work == mx, lane, lg.shape[1]), axis=-1, keepdims=True)
        vals.append(mx)
        idxs.append(ix)
        work = jnp.where(lane == ix, -jnp.inf, work)
    ex = [jnp.exp(v - vals[0]) for v in vals]
    den = ex[0]
    for e in ex[1:]:
        den = den + e
    idx_out = jnp.zeros(lg.shape, jnp.int32)
    gate_out = jnp.zeros(lg.shape, F32)
    for k in range(TOP_K):
        idx_out = jnp.where(lane == k, idxs[k], idx_out)
        gate_out = jnp.where(lane == k, ex[k] / den, gate_out)
    idx_ref[...] = idx_out
    gate_ref[...] = gate_out


def _outproj(merged, x2, ada3, w_out, post_mix_g, pre_ffn_g, router_w, router_b, seq, tm):
    t, d = x2.shape
    ne = router_w.shape[1]
    nep = max(V7X_LANES, ne)
    rw = jnp.zeros((d, nep), F32).at[:, :ne].set(router_w)
    rw_hi, rw_lo = _split_bf16(rw)
    rb = jnp.zeros((1, nep), F32).at[0, :ne].set(router_b)
    per_b = seq // tm
    n = t // tm
    full = lambda shape: pl.BlockSpec(shape, lambda i: (0, 0))
    prev = lambda i: jnp.maximum(i - 1, 0)
    row = lambda w: pl.BlockSpec((tm, w), lambda i: (prev(i), 0))
    x1, h2p, top_idx, gates = pl.pallas_call(
        functools.partial(_outproj_kernel, ne=ne),
        grid=(n + 1,),
        in_specs=[
            pl.BlockSpec((tm, d), lambda i: (jnp.minimum(i, n - 1), 0)), row(d),
            pl.BlockSpec((1, N_ADA, d), lambda i: (prev(i) // per_b, 0, 0)),
            full((d, d)), full((1, d)), full((1, d)), full((d, nep)), full((d, nep)), full((1, nep)),
        ],
        out_specs=[row(d), row(d // 2), row(nep), row(nep)],
        out_shape=[jax.ShapeDtypeStruct((t, d), F32), jax.ShapeDtypeStruct((t, d // 2), jnp.uint32),
                   jax.ShapeDtypeStruct((t, nep), jnp.int32), jax.ShapeDtypeStruct((t, nep), F32)],
        scratch_shapes=[pltpu.VMEM((tm, d), F32), pltpu.VMEM((tm, d), F32)],
        compiler_params=_cparams(("arbitrary",)),
        name="outproj",
    )(merged, x2, ada3, w_out.astype(BF16), post_mix_g.reshape(1, d), pre_ffn_g.reshape(1, d),
      rw_hi, rw_lo, rb)
    return x1, h2p, top_idx[:, :TOP_K], gates[:, :TOP_K]


GATHER_UNROLL = 8


def _row_copy(idx_ref, src_hbm, buf, sems, slot, r):
    return pltpu.make_async_copy(src_hbm.at[pl.ds(idx_ref[0, 0, r], 1), :],
                                 buf.at[slot, pl.ds(r, 1), :], sems.at[slot])


def _start_rows(idx_ref, src_hbm, buf, sems, slot, n, unrolled=False):
    if unrolled:
        for r in range(n):
            _row_copy(idx_ref, src_hbm, buf, sems, slot, r).start(priority=r % 2)
        return

    def start(r, carry):
        _row_copy(idx_ref, src_hbm, buf, sems, slot, r).start()
        return carry
    lax.fori_loop(0, n, start, 0, unroll=GATHER_UNROLL)


def _wait_rows(src_hbm, buf, sems, slot):
    n = buf.shape[1]
    pltpu.make_async_copy(src_hbm.at[pl.ds(0, n), :], buf.at[slot], sems.at[slot]).wait()


def _gather_kernel(tok_ref, h_hbm, o_ref, buf, sems):
    i = pl.program_id(0)
    n_steps = pl.num_programs(0) - 1

    @pl.when(i < n_steps)
    def _():
        _start_rows(tok_ref, h_hbm, buf, sems, i % 2, buf.shape[1], unrolled=True)

    @pl.when(i > 0)
    def _():
        slot = (i - 1) % 2
        _wait_rows(h_hbm, buf, sems, slot)
        o_ref[...] = buf[slot]


def _moe_gather(tok_buf, h2p, rows):
    n_rows = tok_buf.size
    assert n_rows % rows == 0
    n_steps = n_rows // rows
    t, w = h2p.shape
    return pl.pallas_call(
        _gather_kernel,
        grid=(n_steps + 1,),
        in_specs=[
            pl.BlockSpec((1, 1, rows), lambda i: (jnp.minimum(i, n_steps - 1), 0, 0), memory_space=pltpu.SMEM),
            pl.BlockSpec(memory_space=pl.ANY),
        ],
        out_specs=pl.BlockSpec((rows, w), lambda i: (jnp.maximum(i - 1, 0), 0)),
        out_shape=jax.ShapeDtypeStruct((n_rows, w), h2p.dtype),
        scratch_shapes=[pltpu.VMEM((2, rows, w), h2p.dtype), pltpu.SemaphoreType.DMA((2,))],
        compiler_params=_cparams(("arbitrary",)),
        name="moe_gather",
    )(tok_buf.reshape(n_steps, 1, rows), h2p)


def _group_weights(gf_ref, gn_ref, e_ref, i, first_step, more_passes, copy_of, gcnt):
    @pl.when(first_step)
    def _():
        gcnt[0] = 0
        copy_of(e_ref[0], False, 0).start()

    slot = gcnt[0] % 2

    @pl.when(gf_ref[i] == 1)
    def _():
        copy_of(e_ref[0], False, slot).wait()
        nxt = gn_ref[i]

        @pl.when(nxt >= 0)
        def _():
            copy_of(nxt, False, 1 - slot).start()

        @pl.when(jnp.logical_and(nxt < 0, more_passes))
        def _():
            copy_of(e_ref[0], True, 1 - slot).start()

        gcnt[0] = gcnt[0] + 1

    return slot


def _up_kernel(e_ref, nv_ref, gf_ref, gn_ref, x_ref, w_hbm, bg_ref, bl_ref, p_ref, o_ref,
               wp_ref, wbuf, wsem, gcnt):
    j, i = pl.program_id(0), pl.program_id(1)
    tn = wp_ref.shape[1]
    half = V7X_MXU // 2
    hd = x_ref.shape[1]

    def copy_of(expert, next_pass, slot):
        col = (j + 1) * tn if next_pass else j * tn
        return pltpu.make_async_copy(w_hbm.at[expert, :, pl.ds(col, tn)], wbuf.at[slot], wsem.at[slot])

    slot = _group_weights(gf_ref, gn_ref, e_ref, i, jnp.logical_and(j == 0, i == 0),
                          j < pl.num_programs(0) - 1, copy_of, gcnt)

    @pl.when(gf_ref[i] == 1)
    def _():
        for cb in range(tn // V7X_MXU):
            cols = slice(cb * V7X_MXU, (cb + 1) * V7X_MXU)
            wp_ref[:, cols] = jnp.dot(wbuf[slot, :, cols].astype(BF16), p_ref[...],
                                      preferred_element_type=F32).astype(BF16)

    @pl.when(nv_ref[i] > 0)
    def _():
        x_lo, x_hi = _unpack_bf16_pairs(x_ref[...])
        for cb in range(tn // V7X_MXU):
            cols = slice(cb * V7X_MXU, (cb + 1) * V7X_MXU)
            hb = (jnp.dot(x_lo, wp_ref[0:hd, cols], preferred_element_type=F32)
                  + jnp.dot(x_hi, wp_ref[hd:2 * hd, cols], preferred_element_type=F32))
            out_cols = slice(cb * half, (cb + 1) * half)
            x_glu = hb[:, :half] + bg_ref[0, :, out_cols]
            x_lin = hb[:, half:] + bl_ref[0, :, out_cols]
            x_glu = jnp.minimum(x_glu, SWIGLU_LIMIT)
            x_lin = jnp.clip(x_lin, -SWIGLU_LIMIT, SWIGLU_LIMIT)
            act = x_glu * _sigmoid(SWIGLU_ALPHA * x_glu) * (x_lin + 1.0)
            o_ref[:, out_cols] = act.astype(BF16)

    @pl.when(nv_ref[i] == 0)
    def _():
        o_ref[...] = jnp.zeros(o_ref.shape, o_ref.dtype)


def _moe_up(blk_e, nvalid, gfirst, gnext, xs, w1, b1, tn):
    n_rows, hd = xs.shape
    ne, d, f2 = w1.shape
    assert d == 2 * hd
    f = f2 // 2
    n_blocks = n_rows // MOE_BLOCK
    half = V7X_MXU // 2
    c = jnp.arange(V7X_MXU)
    perm = jnp.zeros((V7X_MXU, V7X_MXU), BF16).at[c, (c % 2) * half + c // 2].set(1.0)
    b1g = b1[:, 0::2].reshape(ne, 1, f)
    b1l = b1[:, 1::2].reshape(ne, 1, f)
    return pl.pallas_call(
        _up_kernel,
        grid_spec=pltpu.PrefetchScalarGridSpec(
            num_scalar_prefetch=4,
            grid=(f2 // tn, n_blocks),
            in_specs=[
                pl.BlockSpec((MOE_BLOCK, hd), lambda j, i, e, nv, gf, gn: (i, 0)),
                pl.BlockSpec(memory_space=pl.ANY),
                pl.BlockSpec((1, 1, tn // 2), lambda j, i, e, nv, gf, gn: (e[i], 0, j)),
                pl.BlockSpec((1, 1, tn // 2), lambda j, i, e, nv, gf, gn: (e[i], 0, j)),
                pl.BlockSpec((V7X_MXU, V7X_MXU), lambda j, i, e, nv, gf, gn: (0, 0)),
            ],
            out_specs=pl.BlockSpec((MOE_BLOCK, tn // 2), lambda j, i, e, nv, gf, gn: (i, j)),
            scratch_shapes=[pltpu.VMEM((d, tn), BF16), pltpu.VMEM((2, d, tn), F32),
                            pltpu.SemaphoreType.DMA((2,)), pltpu.SMEM((1,), jnp.int32)],
        ),
        out_shape=jax.ShapeDtypeStruct((n_rows, f), BF16),
        compiler_params=_cparams(("arbitrary", "arbitrary")),
        name="moe_up",
    )(blk_e, nvalid, gfirst, gnext, xs, w1, b1g, b1l, perm)


def _down_kernel(e_ref, nv_ref, gf_ref, gn_ref, a_ref, w_hbm, b_ref, g_ref, o_ref, wb_ref, wbuf, wsem, gcnt):
    i = pl.program_id(0)

    def copy_of(expert, next_pass, slot):
        return pltpu.make_async_copy(w_hbm.at[expert], wbuf.at[slot], wsem.at[slot])

    slot = _group_weights(gf_ref, gn_ref, e_ref, i, i == 0, False, copy_of, gcnt)

    @pl.when(gf_ref[i] == 1)
    def _():
        wb_ref[...] = wbuf[slot].astype(BF16)

    @pl.when(nv_ref[i] > 0)
    def _():
        y = jnp.dot(a_ref[...], wb_ref[...], preferred_element_type=F32) + b_ref[0]
        o_ref[...] = y * g_ref[...]

    @pl.when(nv_ref[i] == 0)
    def _():
        o_ref[...] = jnp.zeros(o_ref.shape, o_ref.dtype)


def _moe_down(blk_e, nvalid, gfirst, gnext, act, w2, b2, g_buf):
    n_rows, f = act.shape
    ne, _, d = w2.shape
    n_blocks = n_rows // MOE_BLOCK
    return pl.pallas_call(
        _down_kernel,
        grid_spec=pltpu.PrefetchScalarGridSpec(
            num_scalar_prefetch=4,
            grid=(n_blocks,),
            in_specs=[
                pl.BlockSpec((MOE_BLOCK, f), lambda i, e, nv, gf, gn: (i, 0)),
                pl.BlockSpec(memory_space=pl.ANY),
                pl.BlockSpec((1, 1, d), lambda i, e, nv, gf, gn: (e[i], 0, 0)),
                pl.BlockSpec((MOE_BLOCK, 1), lambda i, e, nv, gf, gn: (i, 0)),
            ],
            out_specs=pl.BlockSpec((MOE_BLOCK, d), lambda i, e, nv, gf, gn: (i, 0)),
            scratch_shapes=[pltpu.VMEM((f, d), BF16), pltpu.VMEM((2, f, d), F32),
                            pltpu.SemaphoreType.DMA((2,)), pltpu.SMEM((1,), jnp.int32)],
        ),
        out_shape=jax.ShapeDtypeStruct((n_rows, d), F32),
        compiler_params=_cparams(("arbitrary",)),
        name="moe_down",
    )(blk_e, nvalid, gfirst, gnext, act, w2, b2.reshape(ne, 1, d), g_buf.reshape(n_rows, 1))


def _combine_kernel(pos_ref, y_hbm, x1_ref, ada_ref, g_ref, o_ref, buf, sems, *, tt):
    i = pl.program_id(0)
    n_tiles = pl.num_programs(0) - 1
    n_rows = TOP_K * tt

    def finish(slot):
        f = buf[slot, 0:tt, :]
        for k in range(1, TOP_K):
            f = f + buf[slot, k * tt:(k + 1) * tt, :]
        o_ref[...] = x1_ref[...] + ada_ref[0, 5:6, :] * _rms(f, g_ref[...])

    @pl.when(i == 0)
    def _():
        _start_rows(pos_ref, y_hbm, buf, sems, 0, n_rows)

    @pl.when(jnp.logical_and(i > 0, i < n_tiles))
    def _():
        _wait_rows(y_hbm, buf, sems, (i - 1) % 2)
        _start_rows(pos_ref, y_hbm, buf, sems, i % 2, n_rows, unrolled=True)
        finish((i - 1) % 2)

    @pl.when(i == n_tiles)
    def _():
        _wait_rows(y_hbm, buf, sems, (i - 1) % 2)
        finish((i - 1) % 2)


def _moe_combine(pos, y_buf, x1, ada3, post_ffn_g, seq, tt):
    t, d = x1.shape
    per_b = seq // tt
    nt = t // tt
    pos_t = pos.reshape(nt, tt, TOP_K).transpose(0, 2, 1).reshape(nt, 1, TOP_K * tt)
    prev = lambda i: jnp.maximum(i - 1, 0)
    return pl.pallas_call(
        functools.partial(_combine_kernel, tt=tt),
        grid=(nt + 1,),
        in_specs=[
            pl.BlockSpec((1, 1, TOP_K * tt), lambda i: (jnp.minimum(i, nt - 1), 0, 0), memory_space=pltpu.SMEM),
            pl.BlockSpec(memory_space=pl.ANY),
            pl.BlockSpec((tt, d), lambda i: (prev(i), 0)),
            pl.BlockSpec((1, N_ADA, d), lambda i: (prev(i) // per_b, 0, 0)),
            pl.BlockSpec((1, d), lambda i: (0, 0)),
        ],
        out_specs=pl.BlockSpec((tt, d), lambda i: (prev(i), 0)),
        out_shape=jax.ShapeDtypeStruct((t, d), F32),
        scratch_shapes=[pltpu.VMEM((2, TOP_K * tt, d), F32), pltpu.SemaphoreType.DMA((2,))],
        compiler_params=_cparams(("arbitrary",)),
        name="moe_combine",
    )(pos_t, y_buf, x1, ada3, post_ffn_g.reshape(1, d))


def _route(top_idx, gates, n_experts):
    n_tok = top_idx.shape[0]
    n_asg = n_tok * TOP_K
    i32 = jnp.int32
    flat_e = top_idx.reshape(n_asg)
    order = jnp.argsort(flat_e).astype(i32)
    counts = jnp.sum(flat_e[:, None] == jnp.arange(n_experts, dtype=i32)[None, :], axis=0, dtype=i32)
    starts = jnp.cumsum(counts) - counts
    padded = (counts + MOE_BLOCK - 1) // MOE_BLOCK * MOE_BLOCK
    pends = jnp.cumsum(padded)
    pstarts = pends - padded
    n_blocks = -(-n_asg // MOE_BLOCK) + n_experts
    blk_start = jnp.arange(n_blocks, dtype=i32) * MOE_BLOCK
    blk_e = jnp.minimum(jnp.sum(pends[None, :] <= blk_start[:, None], axis=1, dtype=i32), n_experts - 1)
    off = blk_start - pstarts[blk_e]
    nvalid = jnp.clip(counts[blk_e] - off, 0, MOE_BLOCK)
    j = off[:, None] + jnp.arange(MOE_BLOCK, dtype=i32)[None, :]
    valid = j < counts[blk_e][:, None]
    src = jnp.clip(starts[blk_e][:, None] + j, 0, n_asg - 1)
    asg = order[src]
    tok_buf = jnp.where(valid, asg // TOP_K, 0)
    g_buf = jnp.where(valid, gates.reshape(n_asg)[asg], 0.0)
    rank = jnp.argsort(order).astype(i32)
    pos = pstarts[flat_e] + rank - starts[flat_e]
    blk = jnp.arange(n_blocks, dtype=i32)
    prev_e = jnp.concatenate([jnp.full((1,), -1, i32), blk_e[:-1]])
    gfirst = jnp.logical_and(nvalid > 0, blk_e != prev_e).astype(i32)
    cand = jnp.where(gfirst == 1, blk, n_blocks)
    later = jnp.concatenate([lax.cummin(cand[::-1])[::-1][1:], jnp.full((1,), n_blocks, i32)])
    gnext = jnp.where(later < n_blocks, blk_e[jnp.minimum(later, n_blocks - 1)], -1)
    return tok_buf, g_buf, pos, blk_e, nvalid, gfirst, gnext


def _pick(n, pref):
    return pref if n % pref == 0 else n


def kernel(x, c, ada_w, ada_b, pre_mix_g, post_mix_g, pre_ffn_g, post_ffn_g, w_in, gate_b, dw_w, dw_b, cln_g, cln_b, cp_w, cp_b, s5_a_re, s5_a_im, s5_log_dt, s5_b_re, s5_b_im, s5_c_re, s5_c_im, s5_d, glu_wa, glu_wb, w_out, router_w, router_b, w1, b1, w2, b2):
    bsz, seq, d = x.shape
    t = bsz * seq
    depth = ada_w.shape[0]
    conv_ch = dw_w.shape[-1]
    ng, ns, nh = s5_b_re.shape[1:]
    s5_w = ng * nh
    ne = router_w.shape[-1]
    col_s5 = 2 * conv_ch
    col_gate = col_s5 + s5_w
    assert conv_ch == s5_w and V7X_MXU % nh == 0 and ng % (V7X_MXU // nh) == 0
    tm = _pick(seq, 512)
    tm_out = _pick(seq, 256)
    tn_in = _pick(w_in.shape[-1], conv_ch)

    x2 = x.reshape(t, d)
    for l in range(depth):
        ada3 = _ada(c, ada_w[l], ada_b[l]).reshape(bsz, N_ADA, d)
        proj = _inproj(x2, ada3, pre_mix_g[l], w_in[l], seq, _pick(seq, 1024), tn_in)
        vc = _conv_branch(proj, dw_w[l], dw_b[l], cln_g[l], cln_b[l], bsz, seq, tm)

        s5_params = _s5_params(s5_a_re[l], s5_a_im[l], s5_log_dt[l], s5_b_re[l], s5_b_im[l],
                               s5_c_re[l], s5_c_im[l], s5_d[l])
        ys = _s5_branch(proj, col_s5, s5_params, bsz, seq)

        merged = _merge(vc, ys, proj, cp_w[l], cp_b[l], glu_wa[l], glu_wb[l], gate_b[l], tm, col_gate)
        x1, h2p, top_idx, gates = _outproj(merged, x2, ada3, w_out[l], post_mix_g[l], pre_ffn_g[l],
                                           router_w[l], router_b[l], seq, tm_out)

        tok_buf, g_buf, pos, blk_e, nvalid, gfirst, gnext = _route(top_idx, gates, ne)
        xs = _moe_gather(tok_buf, h2p, 2 * MOE_BLOCK if tok_buf.shape[0] % 2 == 0 else MOE_BLOCK)
        act = _moe_up(blk_e, nvalid, gfirst, gnext, xs, w1[l], b1[l], _pick(w1.shape[-1], 2048))
        y_buf = _moe_down(blk_e, nvalid, gfirst, gnext, act, w2[l], b2[l], g_buf)
        x2 = _moe_combine(pos, y_buf, x1, ada3, post_ffn_g[l], seq, _pick(seq, 256))
    return x2.reshape(bsz, seq, d)
```

```python
import functools
import math

import jax
import jax.numpy as jnp
from jax import lax
from jax.experimental import pallas as pl
from jax.experimental.pallas import tpu as pltpu

EPS = 1e-6
N_ADA = 6
TOP_K = 4
MOE_BLOCK = 256
SWIGLU_ALPHA = 1.702
SWIGLU_LIMIT = 7.0
S5_CHUNK = 16
V7X_LANES = 128
V7X_SUBLANES = 8
V7X_MXU = 256
VMEM_LIMIT = 56 * 1024 * 1024

F32 = jnp.float32
BF16 = jnp.bfloat16


def _cparams(sem):
    return pltpu.CompilerParams(dimension_semantics=sem, vmem_limit_bytes=VMEM_LIMIT)


def _sigmoid(v):
    return 1.0 / (1.0 + jnp.exp(-v))


def _rms(v, g):
    return v * lax.rsqrt(jnp.mean(v * v, axis=-1, keepdims=True) + EPS) * g


def _ada_kernel(c_ref, w_ref, b_ref, o_ref):
    c = c_ref[...]
    s = (c * _sigmoid(c)).astype(BF16)
    o_ref[...] = jnp.dot(s, w_ref[...].astype(BF16), preferred_element_type=F32) + b_ref[...]


def _ada(c, ada_w, ada_b):
    bsz, d = c.shape
    n = ada_w.shape[1]
    tn = 1024 if n % 1024 == 0 else n
    return pl.pallas_call(
        _ada_kernel,
        grid=(n // tn,),
        in_specs=[
            pl.BlockSpec((bsz, d), lambda j: (0, 0)),
            pl.BlockSpec((d, tn), lambda j: (0, j)),
            pl.BlockSpec((1, tn), lambda j: (0, j)),
        ],
        out_specs=pl.BlockSpec((bsz, tn), lambda j: (0, j)),
        out_shape=jax.ShapeDtypeStruct((bsz, n), F32),
        compiler_params=_cparams(("arbitrary",)),
        name="ada",
    )(c, ada_w, ada_b.reshape(1, n))


def _inproj_kernel(x_ref, ada_ref, g_ref, w_ref, o_ref, h_ref):
    @pl.when(pl.program_id(1) == 0)
    def _():
        y = _rms(x_ref[...], g_ref[...])
        h = y * (1.0 + ada_ref[0, 1:2, :]) + ada_ref[0, 0:1, :]
        h_ref[...] = h.astype(BF16)

    o_ref[...] = jnp.dot(h_ref[...], w_ref[...].astype(BF16), preferred_element_type=F32).astype(BF16)


def _inproj(x2, ada3, g, w_bf, seq, tm, tn):
    t, d = x2.shape
    n = w_bf.shape[1]
    per_b = seq // tm
    return pl.pallas_call(
        _inproj_kernel,
        grid=(t // tm, n // tn),
        in_specs=[
            pl.BlockSpec((tm, d), lambda i, j: (i, 0)),
            pl.BlockSpec((1, N_ADA, d), lambda i, j: (i // per_b, 0, 0)),
            pl.BlockSpec((1, d), lambda i, j: (0, 0)),
            pl.BlockSpec((d, tn), lambda i, j: (0, j)),
        ],
        out_specs=pl.BlockSpec((tm, tn), lambda i, j: (i, j)),
        out_shape=jax.ShapeDtypeStruct((t, n), BF16),
        scratch_shapes=[pltpu.VMEM((tm, d), BF16)],
        compiler_params=_cparams(("arbitrary", "arbitrary")),
        name="inproj",
    )(x2, ada3, g.reshape(1, d), w_bf)


CONV_HALO = 32
CONV_ROWS = 64


def _conv_kernel(pv_ref, pg_ref, w_ref, b_ref, g_ref, beta_ref, o_ref, vext, vsh, cbuf, *, taps, tl):
    t = pl.program_id(1)
    ch = cbuf.shape[1]

    @pl.when(t == 0)
    def _():
        vext[0:CONV_HALO, :] = jnp.zeros((CONV_HALO, ch), F32)

    @pl.when(t > 0)
    def _():
        vext[0:CONV_HALO, :] = vext[tl:tl + CONV_HALO, :]

    pv = pv_ref[...].astype(F32)
    pg = pg_ref[...].astype(F32)
    vext[CONV_HALO:CONV_HALO + tl, :] = pv * _sigmoid(pg)

    n_sh = vsh.shape[1]
    for p in range(1, V7X_SUBLANES):
        vsh[p - 1] = vext[p:p + n_sh, :]

    off = CONV_HALO - (taps - 1)
    for cb in range(ch // V7X_LANES):
        lanes = slice(cb * V7X_LANES, (cb + 1) * V7X_LANES)

        for rc in range(tl // CONV_ROWS):
            r0 = rc * CONV_ROWS
            acc = jnp.zeros((CONV_ROWS, V7X_LANES), F32)
            for k in range(taps):
                p = (off + k) % V7X_SUBLANES
                base = r0 + off + k - p
                win = vext[base:base + CONV_ROWS, lanes] if p == 0 else vsh[p - 1, base:base + CONV_ROWS, lanes]
                acc = acc + w_ref[k:k + 1, lanes] * win
            cbuf[r0:r0 + CONV_ROWS, lanes] = acc + b_ref[:, lanes]

    v = cbuf[...]
    mu = jnp.mean(v, axis=-1, keepdims=True)
    xc = v - mu
    var = jnp.mean(xc * xc, axis=-1, keepdims=True)
    y = xc * lax.rsqrt(var + EPS) * g_ref[...] + beta_ref[...]
    o_ref[...] = (y * _sigmoid(y)).astype(BF16)


def _conv_branch(proj, dw_w, dw_b, cln_g, cln_b, bsz, seq, tl):
    taps, ch = dw_w.shape
    assert taps - 1 <= CONV_HALO and tl % CONV_ROWS == 0 and ch % V7X_LANES == 0
    per_b = seq // tl
    vec = lambda a: a.reshape(1, ch)
    return pl.pallas_call(
        functools.partial(_conv_kernel, taps=taps, tl=tl),
        grid=(bsz, per_b),
        in_specs=[
            pl.BlockSpec((tl, ch), lambda b, t: (b * per_b + t, 0)),
            pl.BlockSpec((tl, ch), lambda b, t: (b * per_b + t, 1)),
            pl.BlockSpec((taps, ch), lambda b, t: (0, 0)),
            pl.BlockSpec((1, ch), lambda b, t: (0, 0)),
            pl.BlockSpec((1, ch), lambda b, t: (0, 0)),
            pl.BlockSpec((1, ch), lambda b, t: (0, 0)),
        ],
        out_specs=pl.BlockSpec((tl, ch), lambda b, t: (b * per_b + t, 0)),
        out_shape=jax.ShapeDtypeStruct((bsz * seq, ch), BF16),
        scratch_shapes=[pltpu.VMEM((CONV_HALO + tl, ch), F32),
                        pltpu.VMEM((V7X_SUBLANES - 1, CONV_HALO + tl - V7X_SUBLANES, ch), F32),
                        pltpu.VMEM((tl, ch), F32)],
        compiler_params=_cparams(("arbitrary", "arbitrary")),
        name="conv",
    )(proj, proj, dw_w, vec(dw_b), vec(cln_g), vec(cln_b))


S5_SUPER = 16


def _s5_params(a_re, a_im, log_dt, b_re, b_im, c_re, c_im, d_skip):
    ng, ns = a_re.shape
    nh = b_re.shape[-1]
    gt = V7X_MXU // nh
    nj = ng // gt
    dt = jnp.exp(log_dt)[:, None]
    mag = jnp.exp(a_re * dt)
    abar_re = mag * jnp.cos(a_im * dt)
    abar_im = mag * jnp.sin(a_im * dt)
    den = a_re * a_re + a_im * a_im
    num_re = abar_re - 1.0
    coef_re = (num_re * a_re + abar_im * a_im) / den
    coef_im = (abar_im * a_re - num_re * a_im) / den
    bb_re = coef_re[:, :, None] * b_re - coef_im[:, :, None] * b_im
    bb_im = coef_re[:, :, None] * b_im + coef_im[:, :, None] * b_re
    p = jnp.asarray([1.0, S5_CHUNK, S5_CHUNK * S5_SUPER] + [S5_CHUNK * r for r in range(S5_SUPER)], F32)
    pmag = jnp.exp(p[:, None, None] * (a_re * dt)[None])
    pw_re = pmag * jnp.cos(p[:, None, None] * (a_im * dt)[None])
    pw_im = pmag * jnp.sin(p[:, None, None] * (a_im * dt)[None])
    tile = lambda a: a.reshape(-1, nj, gt * ns).transpose(1, 0, 2)
    apow = jnp.concatenate([tile(pw_re), tile(pw_im)], axis=1)
    eye = jnp.eye(gt, dtype=F32)

    def bdiag_in(bb):
        return jnp.einsum('jgnh,gk->jghkn', bb.reshape(nj, gt, ns, nh), eye).reshape(nj, gt * nh, gt * ns)

    def bdiag_out(cc):
        return jnp.einsum('jghn,gk->jgnkh', cc.reshape(nj, gt, nh, ns), eye).reshape(nj, gt * ns, gt * nh)

    bdb = jnp.stack([bdiag_in(bb_re), bdiag_in(bb_im)], axis=1)
    bdc = jnp.stack([bdiag_out(c_re), bdiag_out(-c_im)], axis=1)
    return bdb.astype(BF16), bdc.astype(BF16), apow, d_skip.reshape(nj, 1, gt * nh)


def _cmul_add(ar, ai, xr, xi, vr, vi):
    return ar * xr - ai * xi + vr, ar * xi + ai * xr + vi


def _s5_kernel(u_ref, bdb_ref, bdc_ref, ap_ref, d_ref, p1_ref, p1t_ref, p2_ref, p2t_ref, o_ref,
               u2n_ref, u2_ref, hr_ref, hi_ref, zr_ref, zi_ref, gr_ref, gi_ref, y2_ref, y2n_ref,
               *, n_super):
    tc, nr, nm = S5_CHUNK, S5_SUPER, n_super
    tile = tc * nr
    np_ = 3 + nr
    apow = lambda k: (ap_ref[0, k:k + 1, :], ap_ref[0, np_ + k:np_ + k + 1, :])
    ar, ai = apow(0)

    for m in range(nm):
        pu = jnp.dot(p1_ref[...], u_ref[m * tile:(m + 1) * tile, :], preferred_element_type=F32).astype(BF16)
        for s in range(tc):
            u2n_ref[s, m * nr:(m + 1) * nr, :] = pu[s * nr:(s + 1) * nr, :]
    for s in range(tc):
        u2_ref[s] = jnp.dot(p2_ref[...], u2n_ref[s], preferred_element_type=F32).astype(BF16)

    def drive(s):
        u = u2_ref[s]
        return (jnp.dot(u, bdb_ref[0, 0], preferred_element_type=F32),
                jnp.dot(u, bdb_ref[0, 1], preferred_element_type=F32))

    hr_ref[...] = jnp.zeros(hr_ref.shape, F32)
    hi_ref[...] = jnp.zeros(hi_ref.shape, F32)

    def pass1(s, carry):
        vr, vi = drive(s)
        hr, hi = _cmul_add(ar, ai, hr_ref[...], hi_ref[...], vr, vi)
        hr_ref[...] = hr
        hi_ref[...] = hi
        return carry

    lax.fori_loop(0, tc, pass1, 0, unroll=4)

    a16r, a16i = apow(1)
    qr = jnp.zeros((nm, hr_ref.shape[1]), F32)
    qi = qr
    for r in range(nr):
        rows = slice(r * nm, (r + 1) * nm)
        zr_ref[rows, :] = qr
        zi_ref[rows, :] = qi
        qr, qi = _cmul_add(a16r, a16i, qr, qi, hr_ref[rows, :], hi_ref[rows, :])
    a256r, a256i = apow(2)
    gr = jnp.zeros((1, hr_ref.shape[1]), F32)
    gi = gr
    for m in range(nm):
        gr_ref[m:m + 1, :] = gr
        gi_ref[m:m + 1, :] = gi
        gr, gi = _cmul_add(a256r, a256i, gr, gi, qr[m:m + 1, :], qi[m:m + 1, :])
    gpr = gr_ref[...]
    gpi = gi_ref[...]
    for r in range(nr):
        rows = slice(r * nm, (r + 1) * nm)
        pr, pi = apow(3 + r)
        zr, zi = _cmul_add(pr, pi, gpr, gpi, zr_ref[rows, :], zi_ref[rows, :])
        zr_ref[rows, :] = zr
        zi_ref[rows, :] = zi

    def pass2(t, carry):
        vr, vi = drive(t)
        sr, si = _cmul_add(ar, ai, zr_ref[...], zi_ref[...], vr, vi)
        zr_ref[...] = sr
        zi_ref[...] = si
        y = (jnp.dot(sr.astype(BF16), bdc_ref[0, 0], preferred_element_type=F32)
             + jnp.dot(si.astype(BF16), bdc_ref[0, 1], preferred_element_type=F32))
        y = y + d_ref[0] * u2_ref[t].astype(F32)
        y = 0.5 * y * (1.0 + jnp.tanh(math.sqrt(2.0 / math.pi) * (y + 0.044715 * (y * y * y))))
        y2_ref[t] = y.astype(BF16)
        return carry

    lax.fori_loop(0, tc, pass2, 0, unroll=4)

    for t in range(tc):
        yn = jnp.dot(p2t_ref[...], y2_ref[t], preferred_element_type=F32).astype(BF16)
        for m in range(nm):
            y2n_ref[m * tile + t * nr:m * tile + (t + 1) * nr, :] = yn[m * nr:(m + 1) * nr, :]
    for m in range(nm):
        rows = slice(m * tile, (m + 1) * tile)
        o_ref[rows, :] = jnp.dot(p1t_ref[...], y2n_ref[rows, :], preferred_element_type=F32).astype(BF16)


def _s5_branch(proj, col0, params, bsz, seq):
    bdb, bdc, apow, dskip = params
    nj, _, w, sl = bdb.shape
    tile = S5_CHUNK * S5_SUPER
    assert seq % tile == 0 and col0 % w == 0
    nm = seq // tile
    assert nm % V7X_SUBLANES == 0, "row slabs of the chunk recurrence must be whole sublane tiles"
    rows = S5_SUPER * nm
    i1 = jnp.arange(tile)
    p1 = jnp.zeros((tile, tile), BF16).at[(i1 % S5_CHUNK) * S5_SUPER + i1 // S5_CHUNK, i1].set(1.0)
    i2 = jnp.arange(rows)
    p2 = jnp.zeros((rows, rows), BF16).at[(i2 % S5_SUPER) * nm + i2 // S5_SUPER, i2].set(1.0)
    const = lambda a: pl.BlockSpec(a.shape, lambda b, j: (0,) * a.ndim)
    per_j = lambda a: pl.BlockSpec((1,) + a.shape[1:], lambda b, j: (j,) + (0,) * (a.ndim - 1))
    return pl.pallas_call(
        functools.partial(_s5_kernel, n_super=nm),
        grid=(bsz, nj),
        in_specs=[
            pl.BlockSpec((seq, w), lambda b, j: (b, col0 // w + j)),
            per_j(bdb), per_j(bdc), per_j(apow), per_j(dskip),
            const(p1), const(p1), const(p2), const(p2),
        ],
        out_specs=pl.BlockSpec((seq, w), lambda b, j: (b, j)),
        out_shape=jax.ShapeDtypeStruct((bsz * seq, nj * w), BF16),
        scratch_shapes=[
            pltpu.VMEM((S5_CHUNK, rows, w), BF16), pltpu.VMEM((S5_CHUNK, rows, w), BF16),
            pltpu.VMEM((rows, sl), F32), pltpu.VMEM((rows, sl), F32),
            pltpu.VMEM((rows, sl), F32), pltpu.VMEM((rows, sl), F32),
            pltpu.VMEM((nm, sl), F32), pltpu.VMEM((nm, sl), F32),
            pltpu.VMEM((S5_CHUNK, rows, w), BF16), pltpu.VMEM((seq, w), BF16),
        ],
        compiler_params=_cparams(("arbitrary", "arbitrary")),
        name="s5",
    )(proj, bdb, bdc, apow, dskip, p1, p1.T, p2, p2.T)


def _merge_kernel(vc_ref, ys_ref, l1_ref, l2_ref, cpw_ref, cpb_ref, wa_ref, wb_ref, gb1_ref, gb2_ref, o_ref):
    y_conv = jnp.dot(vc_ref[...], cpw_ref[...], preferred_element_type=F32) + cpb_ref[...]
    ys = ys_ref[...]
    a = jnp.dot(ys, wa_ref[...], preferred_element_type=F32)
    b = jnp.dot(ys, wb_ref[...], preferred_element_type=F32)
    y_s5 = a * _sigmoid(b)
    g1 = _sigmoid(l1_ref[...].astype(F32) + gb1_ref[...])
    g2 = _sigmoid(l2_ref[...].astype(F32) + gb2_ref[...])
    o_ref[...] = (g1 * y_conv + g2 * y_s5).astype(BF16)


def _merge(vc, ys, proj, cp_w, cp_b, wa, wb, gate_b, tm, col0):
    t, ch = vc.shape
    d = cp_w.shape[1]
    tn = ch
    assert col0 % tn == 0 and d % tn == 0
    cb, nj = col0 // tn, d // tn
    wcol = lambda rows: pl.BlockSpec((rows, tn), lambda j, i: (0, j))
    return pl.pallas_call(
        _merge_kernel,
        grid=(nj, t // tm),
        in_specs=[
            pl.BlockSpec((tm, ch), lambda j, i: (i, 0)),
            pl.BlockSpec((tm, ch), lambda j, i: (i, 0)),
            pl.BlockSpec((tm, tn), lambda j, i: (i, cb + j)),
            pl.BlockSpec((tm, tn), lambda j, i: (i, cb + nj + j)),
            wcol(ch), wcol(1), wcol(ch), wcol(ch), wcol(1), wcol(1),
        ],
        out_specs=pl.BlockSpec((tm, tn), lambda j, i: (i, j)),
        out_shape=jax.ShapeDtypeStruct((t, d), BF16),
        compiler_params=_cparams(("arbitrary", "arbitrary")),
        name="merge",
    )(vc, ys, proj, proj, cp_w.astype(BF16), cp_b.reshape(1, d), wa.astype(BF16), wb.astype(BF16),
      gate_b[:d].reshape(1, d), gate_b[d:].reshape(1, d))


def _split_bf16(v):
    hi = v.astype(BF16)
    lo = (v - hi.astype(F32)).astype(BF16)
    return hi, lo


def _bf16_bits_hi(v):
    b = lax.bitcast_convert_type(v, jnp.uint32)
    return (b + jnp.uint32(0x7FFF) + ((b >> 16) & jnp.uint32(1))) & jnp.uint32(0xFFFF0000)


def _pack_bf16_pairs(v):
    half = v.shape[-1] // 2
    return _bf16_bits_hi(v[:, half:]) | (_bf16_bits_hi(v[:, :half]) >> 16)


def _unpack_bf16_pairs(p):
    lo = lax.bitcast_convert_type(p << 16, F32).astype(BF16)
    hi = lax.bitcast_convert_type(p & jnp.uint32(0xFFFF0000), F32).astype(BF16)
    return lo, hi


def _outproj_kernel(m_ref, x_ref, ada_ref, wo_ref, g1_ref, g2_ref, rwh_ref, rwl_ref, rb_ref,
                    x1_ref, h2p_ref, idx_ref, gate_ref, *, ne):
    m = jnp.dot(m_ref[...], wo_ref[...], preferred_element_type=F32)
    x1 = x_ref[...] + ada_ref[0, 2:3, :] * _rms(m, g1_ref[...])
    x1_ref[...] = x1
    h2 = _rms(x1, g2_ref[...]) * (1.0 + ada_ref[0, 4:5, :]) + ada_ref[0, 3:4, :]
    h2p_ref[...] = _pack_bf16_pairs(h2)
    hh, hl = _split_bf16(h2)
    lg = jnp.dot(hh, rwh_ref[...], preferred_element_type=F32)
    lg = lg + jnp.dot(hl, rwh_ref[...], preferred_element_type=F32)
    lg = lg + jnp.dot(hh, rwl_ref[...], preferred_element_type=F32)
    lg = lg + rb_ref[...]
    lane = lax.broadcasted_iota(jnp.int32, lg.shape, 1)
    work = jnp.where(lane < ne, lg, -jnp.inf)
    vals, idxs = [], []
    for _ in range(TOP_K):
        mx = jnp.max(work, axis=-1, keepdims=True)
        ix = jnp.min(jnp.where(work == mx, lane, lg.shape[1]), axis=-1, keepdims=True)
        vals.append(mx)
        idxs.append(ix)
        work = jnp.where(lane == ix, -jnp.inf, work)
    ex = [jnp.exp(v - vals[0]) for v in vals]
    den = ex[0]
    for e in ex[1:]:
        den = den + e
    idx_out = jnp.zeros(lg.shape, jnp.int32)
    gate_out = jnp.zeros(lg.shape, F32)
    for k in range(TOP_K):
        idx_out = jnp.where(lane == k, idxs[k], idx_out)
        gate_out = jnp.where(lane == k, ex[k] / den, gate_out)
    idx_ref[...] = idx_out
    gate_ref[...] = gate_out


def _outproj(merged, x2, ada3, w_out, post_mix_g, pre_ffn_g, router_w, router_b, seq, tm):
    t, d = x2.shape
    ne = router_w.shape[1]
    nep = max(V7X_LANES, ne)
    rw = jnp.zeros((d, nep), F32).at[:, :ne].set(router_w)
    rw_hi, rw_lo = _split_bf16(rw)
    rb = jnp.zeros((1, nep), F32).at[0, :ne].set(router_b)
    per_b = seq // tm
    full = lambda shape: pl.BlockSpec(shape, lambda i: (0, 0))
    row = lambda w: pl.BlockSpec((tm, w), lambda i: (i, 0))
    x1, h2p, top_idx, gates = pl.pallas_call(
        functools.partial(_outproj_kernel, ne=ne),
        grid=(t // tm,),
        in_specs=[
            row(d), row(d),
            pl.BlockSpec((1, N_ADA, d), lambda i: (i // per_b, 0, 0)),
            full((d, d)), full((1, d)), full((1, d)), full((d, nep)), full((d, nep)), full((1, nep)),
        ],
        out_specs=[row(d), row(d // 2), row(nep), row(nep)],
        out_shape=[jax.ShapeDtypeStruct((t, d), F32), jax.ShapeDtypeStruct((t, d // 2), jnp.uint32),
                   jax.ShapeDtypeStruct((t, nep), jnp.int32), jax.ShapeDtypeStruct((t, nep), F32)],
        compiler_params=_cparams(("arbitrary",)),
        name="outproj",
    )(merged, x2, ada3, w_out.astype(BF16), post_mix_g.reshape(1, d), pre_ffn_g.reshape(1, d),
      rw_hi, rw_lo, rb)
    return x1, h2p, top_idx[:, :TOP_K], gates[:, :TOP_K]


GATHER_UNROLL = 8


def _row_copy(idx_ref, src_hbm, buf, sems, slot, r):
    return pltpu.make_async_copy(src_hbm.at[pl.ds(idx_ref[0, 0, r], 1), :],
                                 buf.at[slot, pl.ds(r, 1), :], sems.at[slot])


def _start_rows(idx_ref, src_hbm, buf, sems, slot, n, unrolled=False):
    if unrolled:
        for r in range(n):
            _row_copy(idx_ref, src_hbm, buf, sems, slot, r).start(priority=r % 2)
        return

    def start(r, carry):
        _row_copy(idx_ref, src_hbm, buf, sems, slot, r).start()
        return carry
    lax.fori_loop(0, n, start, 0, unroll=GATHER_UNROLL)


def _wait_rows(src_hbm, buf, sems, slot):
    n = buf.shape[1]
    pltpu.make_async_copy(src_hbm.at[pl.ds(0, n), :], buf.at[slot], sems.at[slot]).wait()


def _gather_kernel(tok_ref, h_hbm, o_ref, buf, sems):
    i = pl.program_id(0)
    n_steps = pl.num_programs(0) - 1

    @pl.when(i < n_steps)
    def _():
        _start_rows(tok_ref, h_hbm, buf, sems, i % 2, buf.shape[1], unrolled=True)

    @pl.when(i > 0)
    def _():
        slot = (i - 1) % 2
        _wait_rows(h_hbm, buf, sems, slot)
        o_ref[...] = buf[slot]


def _moe_gather(tok_buf, h2p, rows):
    n_rows = tok_buf.size
    assert n_rows % rows == 0
    n_steps = n_rows // rows
    t, w = h2p.shape
    return pl.pallas_call(
        _gather_kernel,
        grid=(n_steps + 1,),
        in_specs=[
            pl.BlockSpec((1, 1, rows), lambda i: (jnp.minimum(i, n_steps - 1), 0, 0), memory_space=pltpu.SMEM),
            pl.BlockSpec(memory_space=pl.ANY),
        ],
        out_specs=pl.BlockSpec((rows, w), lambda i: (jnp.maximum(i - 1, 0), 0)),
        out_shape=jax.ShapeDtypeStruct((n_rows, w), h2p.dtype),
        scratch_shapes=[pltpu.VMEM((2, rows, w), h2p.dtype), pltpu.SemaphoreType.DMA((2,))],
        compiler_params=_cparams(("arbitrary",)),
        name="moe_gather",
    )(tok_buf.reshape(n_steps, 1, rows), h2p)


def _group_weights(gf_ref, gn_ref, e_ref, i, first_step, more_passes, copy_of, gcnt):
    @pl.when(first_step)
    def _():
        gcnt[0] = 0
        copy_of(e_ref[0], False, 0).start()

    slot = gcnt[0] % 2

    @pl.when(gf_ref[i] == 1)
    def _():
        copy_of(e_ref[0], False, slot).wait()
        nxt = gn_ref[i]

        @pl.when(nxt >= 0)
        def _():
            copy_of(nxt, False, 1 - slot).start()

        @pl.when(jnp.logical_and(nxt < 0, more_passes))
        def _():
            copy_of(e_ref[0], True, 1 - slot).start()

        gcnt[0] = gcnt[0] + 1

    return slot


def _up_kernel(e_ref, nv_ref, gf_ref, gn_ref, x_ref, w_hbm, bg_ref, bl_ref, p_ref, o_ref,
               wp_ref, wbuf, wsem, gcnt):
    j, i = pl.program_id(0), pl.program_id(1)
    tn = wp_ref.shape[1]
    half = V7X_MXU // 2
    hd = x_ref.shape[1]

    def copy_of(expert, next_pass, slot):
        col = (j + 1) * tn if next_pass else j * tn
        return pltpu.make_async_copy(w_hbm.at[expert, :, pl.ds(col, tn)], wbuf.at[slot], wsem.at[slot])

    slot = _group_weights(gf_ref, gn_ref, e_ref, i, jnp.logical_and(j == 0, i == 0),
                          j < pl.num_programs(0) - 1, copy_of, gcnt)

    @pl.when(gf_ref[i] == 1)
    def _():
        for cb in range(tn // V7X_MXU):
            cols = slice(cb * V7X_MXU, (cb + 1) * V7X_MXU)
            wp_ref[:, cols] = jnp.dot(wbuf[slot, :, cols].astype(BF16), p_ref[...],
                                      preferred_element_type=F32).astype(BF16)

    @pl.when(nv_ref[i] > 0)
    def _():
        x_lo, x_hi = _unpack_bf16_pairs(x_ref[...])
        for cb in range(tn // V7X_MXU):
            cols = slice(cb * V7X_MXU, (cb + 1) * V7X_MXU)
            hb = (jnp.dot(x_lo, wp_ref[0:hd, cols], preferred_element_type=F32)
                  + jnp.dot(x_hi, wp_ref[hd:2 * hd, cols], preferred_element_type=F32))
            out_cols = slice(cb * half, (cb + 1) * half)
            x_glu = hb[:, :half] + bg_ref[0, :, out_cols]
            x_lin = hb[:, half:] + bl_ref[0, :, out_cols]
            x_glu = jnp.minimum(x_glu, SWIGLU_LIMIT)
            x_lin = jnp.clip(x_lin, -SWIGLU_LIMIT, SWIGLU_LIMIT)
            act = x_glu * _sigmoid(SWIGLU_ALPHA * x_glu) * (x_lin + 1.0)
            o_ref[:, out_cols] = act.astype(BF16)

    @pl.when(nv_ref[i] == 0)
    def _():
        o_ref[...] = jnp.zeros(o_ref.shape, o_ref.dtype)


def _moe_up(blk_e, nvalid, gfirst, gnext, xs, w1, b1, tn):
    n_rows, hd = xs.shape
    ne, d, f2 = w1.shape
    assert d == 2 * hd
    f = f2 // 2
    n_blocks = n_rows // MOE_BLOCK
    half = V7X_MXU // 2
    c = jnp.arange(V7X_MXU)
    perm = jnp.zeros((V7X_MXU, V7X_MXU), BF16).at[c, (c % 2) * half + c // 2].set(1.0)
    b1g = b1[:, 0::2].reshape(ne, 1, f)
    b1l = b1[:, 1::2].reshape(ne, 1, f)
    return pl.pallas_call(
        _up_kernel,
        grid_spec=pltpu.PrefetchScalarGridSpec(
            num_scalar_prefetch=4,
            grid=(f2 // tn, n_blocks),
            in_specs=[
                pl.BlockSpec((MOE_BLOCK, hd), lambda j, i, e, nv, gf, gn: (i, 0)),
                pl.BlockSpec(memory_space=pl.ANY),
                pl.BlockSpec((1, 1, tn // 2), lambda j, i, e, nv, gf, gn: (e[i], 0, j)),
                pl.BlockSpec((1, 1, tn // 2), lambda j, i, e, nv, gf, gn: (e[i], 0, j)),
                pl.BlockSpec((V7X_MXU, V7X_MXU), lambda j, i, e, nv, gf, gn: (0, 0)),
            ],
            out_specs=pl.BlockSpec((MOE_BLOCK, tn // 2), lambda j, i, e, nv, gf, gn: (i, j)),
            scratch_shapes=[pltpu.VMEM((d, tn), BF16), pltpu.VMEM((2, d, tn), F32),
                            pltpu.SemaphoreType.DMA((2,)), pltpu.SMEM((1,), jnp.int32)],
        ),
        out_shape=jax.ShapeDtypeStruct((n_rows, f), BF16),
        compiler_params=_cparams(("arbitrary", "arbitrary")),
        name="moe_up",
    )(blk_e, nvalid, gfirst, gnext, xs, w1, b1g, b1l, perm)


def _down_kernel(e_ref, nv_ref, gf_ref, gn_ref, a_ref, w_hbm, b_ref, g_ref, o_ref, wb_ref, wbuf, wsem, gcnt):
    i = pl.program_id(0)

    def copy_of(expert, next_pass, slot):
        return pltpu.make_async_copy(w_hbm.at[expert], wbuf.at[slot], wsem.at[slot])

    slot = _group_weights(gf_ref, gn_ref, e_ref, i, i == 0, False, copy_of, gcnt)

    @pl.when(gf_ref[i] == 1)
    def _():
        wb_ref[...] = wbuf[slot].astype(BF16)

    @pl.when(nv_ref[i] > 0)
    def _():
        y = jnp.dot(a_ref[...], wb_ref[...], preferred_element_type=F32) + b_ref[0]
        o_ref[...] = y * g_ref[...]

    @pl.when(nv_ref[i] == 0)
    def _():
        o_ref[...] = jnp.zeros(o_ref.shape, o_ref.dtype)


def _moe_down(blk_e, nvalid, gfirst, gnext, act, w2, b2, g_buf):
    n_rows, f = act.shape
    ne, _, d = w2.shape
    n_blocks = n_rows // MOE_BLOCK
    return pl.pallas_call(
        _down_kernel,
        grid_spec=pltpu.PrefetchScalarGridSpec(
            num_scalar_prefetch=4,
            grid=(n_blocks,),
            in_specs=[
                pl.BlockSpec((MOE_BLOCK, f), lambda i, e, nv, gf, gn: (i, 0)),
                pl.BlockSpec(memory_space=pl.ANY),
                pl.BlockSpec((1, 1, d), lambda i, e, nv, gf, gn: (e[i], 0, 0)),
                pl.BlockSpec((MOE_BLOCK, 1), lambda i, e, nv, gf, gn: (i, 0)),
            ],
            out_specs=pl.BlockSpec((MOE_BLOCK, d), lambda i, e, nv, gf, gn: (i, 0)),
            scratch_shapes=[pltpu.VMEM((f, d), BF16), pltpu.VMEM((2, f, d), F32),
                            pltpu.SemaphoreType.DMA((2,)), pltpu.SMEM((1,), jnp.int32)],
        ),
        out_shape=jax.ShapeDtypeStruct((n_rows, d), F32),
        compiler_params=_cparams(("arbitrary",)),
        name="moe_down",
    )(blk_e, nvalid, gfirst, gnext, act, w2, b2.reshape(ne, 1, d), g_buf.reshape(n_rows, 1))


def _combine_kernel(pos_ref, y_hbm, x1_ref, ada_ref, g_ref, o_ref, buf, sems, *, tt):
    i = pl.program_id(0)
    n_tiles = pl.num_programs(0) - 1
    n_rows = TOP_K * tt

    def finish(slot):
        f = buf[slot, 0:tt, :]
        for k in range(1, TOP_K):
            f = f + buf[slot, k * tt:(k + 1) * tt, :]
        o_ref[...] = x1_ref[...] + ada_ref[0, 5:6, :] * _rms(f, g_ref[...])

    @pl.when(i == 0)
    def _():
        _start_rows(pos_ref, y_hbm, buf, sems, 0, n_rows)

    @pl.when(jnp.logical_and(i > 0, i < n_tiles))
    def _():
        _wait_rows(y_hbm, buf, sems, (i - 1) % 2)
        _start_rows(pos_ref, y_hbm, buf, sems, i % 2, n_rows, unrolled=True)
        finish((i - 1) % 2)

    @pl.when(i == n_tiles)
    def _():
        _wait_rows(y_hbm, buf, sems, (i - 1) % 2)
        finish((i - 1) % 2)


def _moe_combine(pos, y_buf, x1, ada3, post_ffn_g, seq, tt):
    t, d = x1.shape
    per_b = seq // tt
    nt = t // tt
    pos_t = pos.reshape(nt, tt, TOP_K).transpose(0, 2, 1).reshape(nt, 1, TOP_K * tt)
    prev = lambda i: jnp.maximum(i - 1, 0)
    return pl.pallas_call(
        functools.partial(_combine_kernel, tt=tt),
        grid=(nt + 1,),
        in_specs=[
            pl.BlockSpec((1, 1, TOP_K * tt), lambda i: (jnp.minimum(i, nt - 1), 0, 0), memory_space=pltpu.SMEM),
            pl.BlockSpec(memory_space=pl.ANY),
            pl.BlockSpec((tt, d), lambda i: (prev(i), 0)),
            pl.BlockSpec((1, N_ADA, d), lambda i: (prev(i) // per_b, 0, 0)),
            pl.BlockSpec((1, d), lambda i: (0, 0)),
        ],
        out_specs=pl.BlockSpec((tt, d), lambda i: (prev(i), 0)),
        out_shape=jax.ShapeDtypeStruct((t, d), F32),
        scratch_shapes=[pltpu.VMEM((2, TOP_K * tt, d), F32), pltpu.SemaphoreType.DMA((2,))],
        compiler_params=_cparams(("arbitrary",)),
        name="moe_combine",
    )(pos_t, y_buf, x1, ada3, post_ffn_g.reshape(1, d))


def _route(top_idx, gates, n_experts):
    n_tok = top_idx.shape[0]
    n_asg = n_tok * TOP_K
    i32 = jnp.int32
    flat_e = top_idx.reshape(n_asg)
    order = jnp.argsort(flat_e).astype(i32)
    counts = jnp.sum(flat_e[:, None] == jnp.arange(n_experts, dtype=i32)[None, :], axis=0, dtype=i32)
    starts = jnp.cumsum(counts) - counts
    padded = (counts + MOE_BLOCK - 1) // MOE_BLOCK * MOE_BLOCK
    pends = jnp.cumsum(padded)
    pstarts = pends - padded
    n_blocks = -(-n_asg // MOE_BLOCK) + n_experts
    blk_start = jnp.arange(n_blocks, dtype=i32) * MOE_BLOCK
    blk_e = jnp.minimum(jnp.sum(pends[None, :] <= blk_start[:, None], axis=1, dtype=i32), n_experts - 1)
    off = blk_start - pstarts[blk_e]
    nvalid = jnp.clip(counts[blk_e] - off, 0, MOE_BLOCK)
    j = off[:, None] + jnp.arange(MOE_BLOCK, dtype=i32)[None, :]
    valid = j < counts[blk_e][:, None]
    src = jnp.clip(starts[blk_e][:, None] + j, 0, n_asg - 1)
    asg = order[src]
    row_id = blk_start[:, None] + jnp.arange(MOE_BLOCK, dtype=i32)[None, :]
    tok_buf = jnp.where(valid, asg // TOP_K, row_id % n_tok)
    g_buf = jnp.where(valid, gates.reshape(n_asg)[asg], 0.0)
    rank = jnp.argsort(order).astype(i32)
    pos = pstarts[flat_e] + rank - starts[flat_e]
    blk = jnp.arange(n_blocks, dtype=i32)
    prev_e = jnp.concatenate([jnp.full((1,), -1, i32), blk_e[:-1]])
    gfirst = jnp.logical_and(nvalid > 0, blk_e != prev_e).astype(i32)
    cand = jnp.where(gfirst == 1, blk, n_blocks)
    later = jnp.concatenate([lax.cummin(cand[::-1])[::-1][1:], jnp.full((1,), n_blocks, i32)])
    gnext = jnp.where(later < n_blocks, blk_e[jnp.minimum(later, n_blocks - 1)], -1)
    return tok_buf, g_buf, pos, blk_e, nvalid, gfirst, gnext


def _pick(n, pref):
    return pref if n % pref == 0 else n


def kernel(x, c, ada_w, ada_b, pre_mix_g, post_mix_g, pre_ffn_g, post_ffn_g, w_in, gate_b, dw_w, dw_b, cln_g, cln_b, cp_w, cp_b, s5_a_re, s5_a_im, s5_log_dt, s5_b_re, s5_b_im, s5_c_re, s5_c_im, s5_d, glu_wa, glu_wb, w_out, router_w, router_b, w1, b1, w2, b2):
    bsz, seq, d = x.shape
    t = bsz * seq
    depth = ada_w.shape[0]
    conv_ch = dw_w.shape[-1]
    ng, ns, nh = s5_b_re.shape[1:]
    s5_w = ng * nh
    ne = router_w.shape[-1]
    col_s5 = 2 * conv_ch
    col_gate = col_s5 + s5_w
    assert conv_ch == s5_w and V7X_MXU % nh == 0 and ng % (V7X_MXU // nh) == 0
    tm = _pick(seq, 512)
    tm_out = _pick(seq, 256)
    tn_in = _pick(w_in.shape[-1], conv_ch)

    x2 = x.reshape(t, d)
    for l in range(depth):
        ada3 = _ada(c, ada_w[l], ada_b[l]).reshape(bsz, N_ADA, d)
        proj = _inproj(x2, ada3, pre_mix_g[l], w_in[l], seq, _pick(seq, 1024), tn_in)
        vc = _conv_branch(proj, dw_w[l], dw_b[l], cln_g[l], cln_b[l], bsz, seq, tm)

        s5_params = _s5_params(s5_a_re[l], s5_a_im[l], s5_log_dt[l], s5_b_re[l], s5_b_im[l],
                               s5_c_re[l], s5_c_im[l], s5_d[l])
        ys = _s5_branch(proj, col_s5, s5_params, bsz, seq)

        merged = _merge(vc, ys, proj, cp_w[l], cp_b[l], glu_wa[l], glu_wb[l], gate_b[l], tm, col_gate)
        x1, h2p, top_idx, gates = _outproj(merged, x2, ada3, w_out[l], post_mix_g[l], pre_ffn_g[l],
                                           router_w[l], router_b[l], seq, tm_out)

        tok_buf, g_buf, pos, blk_e, nvalid, gfirst, gnext = _route(top_idx, gates, ne)
        nb = tok_buf.shape[0]
        xs = _moe_gather(tok_buf, h2p, MOE_BLOCK * next(k for k in (4, 2, 1) if nb % k == 0))
        act = _moe_up(blk_e, nvalid, gfirst, gnext, xs, w1[l], b1[l], _pick(w1.shape[-1], 2048))
        y_buf = _moe_down(blk_e, nvalid, gfirst, gnext, act, w2[l], b2[l], g_buf)
        x2 = _moe_combine(pos, y_buf, x1, ada3, post_ffn_g[l], seq, _pick(seq, 256))
    return x2.reshape(bsz, seq, d)
```

```python
import functools
import math

import jax
import jax.numpy as jnp
from jax import lax
from jax.experimental import pallas as pl
from jax.experimental.pallas import tpu as pltpu

EPS = 1e-6
N_ADA = 6
TOP_K = 4
MOE_BLOCK = 256
SWIGLU_ALPHA = 1.702
SWIGLU_LIMIT = 7.0
S5_CHUNK = 16
V7X_LANES = 128
V7X_SUBLANES = 8
V7X_MXU = 256
VMEM_LIMIT = 56 * 1024 * 1024

F32 = jnp.float32
BF16 = jnp.bfloat16


def _cparams(sem):
    return pltpu.CompilerParams(dimension_semantics=sem, vmem_limit_bytes=VMEM_LIMIT)


def _sigmoid(v):
    return 1.0 / (1.0 + jnp.exp(-v))


def _rms(v, g):
    return v * lax.rsqrt(jnp.mean(v * v, axis=-1, keepdims=True) + EPS) * g


def _ada_kernel(c_ref, w_ref, b_ref, o_ref):
    c = c_ref[...]
    s = (c * _sigmoid(c)).astype(BF16)
    o_ref[...] = jnp.dot(s, w_ref[...].astype(BF16), preferred_element_type=F32) + b_ref[...]


def _ada(c, ada_w, ada_b):
    bsz, d = c.shape
    n = ada_w.shape[1]
    tn = 1024 if n % 1024 == 0 else n
    return pl.pallas_call(
        _ada_kernel,
        grid=(n // tn,),
        in_specs=[
            pl.BlockSpec((bsz, d), lambda j: (0, 0)),
            pl.BlockSpec((d, tn), lambda j: (0, j)),
            pl.BlockSpec((1, tn), lambda j: (0, j)),
        ],
        out_specs=pl.BlockSpec((bsz, tn), lambda j: (0, j)),
        out_shape=jax.ShapeDtypeStruct((bsz, n), F32),
        compiler_params=_cparams(("arbitrary",)),
        name="ada",
    )(c, ada_w, ada_b.reshape(1, n))


def _inproj_kernel(x_ref, ada_ref, g_ref, w_ref, o_ref, h_ref):
    @pl.when(pl.program_id(1) == 0)
    def _():
        y = _rms(x_ref[...], g_ref[...])
        h = y * (1.0 + ada_ref[0, 1:2, :]) + ada_ref[0, 0:1, :]
        h_ref[...] = h.astype(BF16)

    o_ref[...] = jnp.dot(h_ref[...], w_ref[...].astype(BF16), preferred_element_type=F32).astype(BF16)


def _inproj(x2, ada3, g, w_bf, seq, tm, tn):
    t, d = x2.shape
    n = w_bf.shape[1]
    per_b = seq // tm
    return pl.pallas_call(
        _inproj_kernel,
        grid=(t // tm, n // tn),
        in_specs=[
            pl.BlockSpec((tm, d), lambda i, j: (i, 0)),
            pl.BlockSpec((1, N_ADA, d), lambda i, j: (i // per_b, 0, 0)),
            pl.BlockSpec((1, d), lambda i, j: (0, 0)),
            pl.BlockSpec((d, tn), lambda i, j: (0, j)),
        ],
        out_specs=pl.BlockSpec((tm, tn), lambda i, j: (i, j)),
        out_shape=jax.ShapeDtypeStruct((t, n), BF16),
        scratch_shapes=[pltpu.VMEM((tm, d), BF16)],
        compiler_params=_cparams(("arbitrary", "arbitrary")),
        name="inproj",
    )(x2, ada3, g.reshape(1, d), w_bf)


CONV_HALO = 32
CONV_ROWS = 64


def _conv_kernel(pv_ref, pg_ref, w_ref, b_ref, g_ref, beta_ref, o_ref, vext, vsh, cbuf, *, taps, tl):
    t = pl.program_id(1)
    ch = cbuf.shape[1]

    @pl.when(t == 0)
    def _():
        vext[0:CONV_HALO, :] = jnp.zeros((CONV_HALO, ch), F32)

    @pl.when(t > 0)
    def _():
        vext[0:CONV_HALO, :] = vext[tl:tl + CONV_HALO, :]

    pv = pv_ref[...].astype(F32)
    pg = pg_ref[...].astype(F32)
    vext[CONV_HALO:CONV_HALO + tl, :] = pv * _sigmoid(pg)

    n_sh = vsh.shape[1]
    for p in range(1, V7X_SUBLANES):
        vsh[p - 1] = vext[p:p + n_sh, :]

    off = CONV_HALO - (taps - 1)
    for cb in range(ch // V7X_LANES):
        lanes = slice(cb * V7X_LANES, (cb + 1) * V7X_LANES)

        for rc in range(tl // CONV_ROWS):
            r0 = rc * CONV_ROWS
            acc = jnp.zeros((CONV_ROWS, V7X_LANES), F32)
            for k in range(taps):
                p = (off + k) % V7X_SUBLANES
                base = r0 + off + k - p
                win = vext[base:base + CONV_ROWS, lanes] if p == 0 else vsh[p - 1, base:base + CONV_ROWS, lanes]
                acc = acc + w_ref[k:k + 1, lanes] * win
            cbuf[r0:r0 + CONV_ROWS, lanes] = acc + b_ref[:, lanes]

    v = cbuf[...]
    mu = jnp.mean(v, axis=-1, keepdims=True)
    xc = v - mu
    var = jnp.mean(xc * xc, axis=-1, keepdims=True)
    y = xc * lax.rsqrt(var + EPS) * g_ref[...] + beta_ref[...]
    o_ref[...] = (y * _sigmoid(y)).astype(BF16)


def _conv_branch(proj, dw_w, dw_b, cln_g, cln_b, bsz, seq, tl):
    taps, ch = dw_w.shape
    assert taps - 1 <= CONV_HALO and tl % CONV_ROWS == 0 and ch % V7X_LANES == 0
    per_b = seq // tl
    vec = lambda a: a.reshape(1, ch)
    return pl.pallas_call(
        functools.partial(_conv_kernel, taps=taps, tl=tl),
        grid=(bsz, per_b),
        in_specs=[
            pl.BlockSpec((tl, ch), lambda b, t: (b * per_b + t, 0)),
            pl.BlockSpec((tl, ch), lambda b, t: (b * per_b + t, 1)),
            pl.BlockSpec((taps, ch), lambda b, t: (0, 0)),
            pl.BlockSpec((1, ch), lambda b, t: (0, 0)),
            pl.BlockSpec((1, ch), lambda b, t: (0, 0)),
            pl.BlockSpec((1, ch), lambda b, t: (0, 0)),
        ],
        out_specs=pl.BlockSpec((tl, ch), lambda b, t: (b * per_b + t, 0)),
        out_shape=jax.ShapeDtypeStruct((bsz * seq, ch), BF16),
        scratch_shapes=[pltpu.VMEM((CONV_HALO + tl, ch), F32),
                        pltpu.VMEM((V7X_SUBLANES - 1, CONV_HALO + tl - V7X_SUBLANES, ch), F32),
                        pltpu.VMEM((tl, ch), F32)],
        compiler_params=_cparams(("arbitrary", "arbitrary")),
        name="conv",
    )(proj, proj, dw_w, vec(dw_b), vec(cln_g), vec(cln_b))


S5_SUPER = 16


def _s5_params(a_re, a_im, log_dt, b_re, b_im, c_re, c_im, d_skip):
    ng, ns = a_re.shape
    nh = b_re.shape[-1]
    gt = V7X_MXU // nh
    nj = ng // gt
    dt = jnp.exp(log_dt)[:, None]
    mag = jnp.exp(a_re * dt)
    abar_re = mag * jnp.cos(a_im * dt)
    abar_im = mag * jnp.sin(a_im * dt)
    den = a_re * a_re + a_im * a_im
    num_re = abar_re - 1.0
    coef_re = (num_re * a_re + abar_im * a_im) / den
    coef_im = (abar_im * a_re - num_re * a_im) / den
    bb_re = coef_re[:, :, None] * b_re - coef_im[:, :, None] * b_im
    bb_im = coef_re[:, :, None] * b_im + coef_im[:, :, None] * b_re
    p = jnp.asarray([1.0, S5_CHUNK, S5_CHUNK * S5_SUPER] + [S5_CHUNK * r for r in range(S5_SUPER)], F32)
    pmag = jnp.exp(p[:, None, None] * (a_re * dt)[None])
    pw_re = pmag * jnp.cos(p[:, None, None] * (a_im * dt)[None])
    pw_im = pmag * jnp.sin(p[:, None, None] * (a_im * dt)[None])
    tile = lambda a: a.reshape(-1, nj, gt * ns).transpose(1, 0, 2)
    apow = jnp.concatenate([tile(pw_re), tile(pw_im)], axis=1)
    eye = jnp.eye(gt, dtype=F32)

    def bdiag_in(bb):
        return jnp.einsum('jgnh,gk->jghkn', bb.reshape(nj, gt, ns, nh), eye).reshape(nj, gt * nh, gt * ns)

    def bdiag_out(cc):
        return jnp.einsum('jghn,gk->jgnkh', cc.reshape(nj, gt, nh, ns), eye).reshape(nj, gt * ns, gt * nh)

    bdb = jnp.stack([bdiag_in(bb_re), bdiag_in(bb_im)], axis=1)
    bdc = jnp.stack([bdiag_out(c_re), bdiag_out(-c_im)], axis=1)
    return bdb.astype(BF16), bdc.astype(BF16), apow, d_skip.reshape(nj, 1, gt * nh)


def _cmul_add(ar, ai, xr, xi, vr, vi):
    return ar * xr - ai * xi + vr, ar * xi + ai * xr + vi


def _s5_kernel(u_ref, bdb_ref, bdc_ref, ap_ref, d_ref, p1_ref, p1t_ref, p2_ref, p2t_ref, o_ref,
               u2n_ref, u2_ref, hr_ref, hi_ref, zr_ref, zi_ref, gr_ref, gi_ref, y2_ref, y2n_ref,
               *, n_super):
    tc, nr, nm = S5_CHUNK, S5_SUPER, n_super
    tile = tc * nr
    np_ = 3 + nr
    apow = lambda k: (ap_ref[0, k:k + 1, :], ap_ref[0, np_ + k:np_ + k + 1, :])
    ar, ai = apow(0)

    for m in range(nm):
        pu = jnp.dot(p1_ref[...], u_ref[m * tile:(m + 1) * tile, :], preferred_element_type=F32).astype(BF16)
        for s in range(tc):
            u2n_ref[s, m * nr:(m + 1) * nr, :] = pu[s * nr:(s + 1) * nr, :]
    for s in range(tc):
        u2_ref[s] = jnp.dot(p2_ref[...], u2n_ref[s], preferred_element_type=F32).astype(BF16)

    def drive(s):
        u = u2_ref[s]
        return (jnp.dot(u, bdb_ref[0, 0], preferred_element_type=F32),
                jnp.dot(u, bdb_ref[0, 1], preferred_element_type=F32))

    hr_ref[...] = jnp.zeros(hr_ref.shape, F32)
    hi_ref[...] = jnp.zeros(hi_ref.shape, F32)

    def pass1(s, carry):
        vr, vi = drive(s)
        hr, hi = _cmul_add(ar, ai, hr_ref[...], hi_ref[...], vr, vi)
        hr_ref[...] = hr
        hi_ref[...] = hi
        return carry

    lax.fori_loop(0, tc, pass1, 0, unroll=4)

    a16r, a16i = apow(1)
    qr = jnp.zeros((nm, hr_ref.shape[1]), F32)
    qi = qr
    for r in range(nr):
        rows = slice(r * nm, (r + 1) * nm)
        zr_ref[rows, :] = qr
        zi_ref[rows, :] = qi
        qr, qi = _cmul_add(a16r, a16i, qr, qi, hr_ref[rows, :], hi_ref[rows, :])
    a256r, a256i = apow(2)
    gr = jnp.zeros((1, hr_ref.shape[1]), F32)
    gi = gr
    for m in range(nm):
        gr_ref[m:m + 1, :] = gr
        gi_ref[m:m + 1, :] = gi
        gr, gi = _cmul_add(a256r, a256i, gr, gi, qr[m:m + 1, :], qi[m:m + 1, :])
    gpr = gr_ref[...]
    gpi = gi_ref[...]
    for r in range(nr):
        rows = slice(r * nm, (r + 1) * nm)
        pr, pi = apow(3 + r)
        zr, zi = _cmul_add(pr, pi, gpr, gpi, zr_ref[rows, :], zi_ref[rows, :])
        zr_ref[rows, :] = zr
        zi_ref[rows, :] = zi

    def pass2(t, carry):
        vr, vi = drive(t)
        sr, si = _cmul_add(ar, ai, zr_ref[...], zi_ref[...], vr, vi)
        zr_ref[...] = sr
        zi_ref[...] = si
        y = (jnp.dot(sr.astype(BF16), bdc_ref[0, 0], preferred_element_type=F32)
             + jnp.dot(si.astype(BF16), bdc_ref[0, 1], preferred_element_type=F32))
        y = y + d_ref[0] * u2_ref[t].astype(F32)
        y = 0.5 * y * (1.0 + jnp.tanh(math.sqrt(2.0 / math.pi) * (y + 0.044715 * (y * y * y))))
        y2_ref[t] = y.astype(BF16)
        return carry

    lax.fori_loop(0, tc, pass2, 0, unroll=4)

    for t in range(tc):
        yn = jnp.dot(p2t_ref[...], y2_ref[t], preferred_element_type=F32).astype(BF16)
        for m in range(nm):
            y2n_ref[m * tile + t * nr:m * tile + (t + 1) * nr, :] = yn[m * nr:(m + 1) * nr, :]
    for m in range(nm):
        rows = slice(m * tile, (m + 1) * tile)
        o_ref[rows, :] = jnp.dot(p1t_ref[...], y2n_ref[rows, :], preferred_element_type=F32).astype(BF16)


def _s5_branch(proj, col0, params, bsz, seq):
    bdb, bdc, apow, dskip = params
    nj, _, w, sl = bdb.shape
    tile = S5_CHUNK * S5_SUPER
    assert seq % tile == 0 and col0 % w == 0
    nm = seq // tile
    assert nm % V7X_SUBLANES == 0, "row slabs of the chunk recurrence must be whole sublane tiles"
    rows = S5_SUPER * nm
    i1 = jnp.arange(tile)
    p1 = (i1[:, None] == ((i1 % S5_CHUNK) * S5_SUPER + i1 // S5_CHUNK)[None, :]).astype(BF16)
    i2 = jnp.arange(rows)
    p2 = (i2[:, None] == ((i2 % S5_SUPER) * nm + i2 // S5_SUPER)[None, :]).astype(BF16)
    const = lambda a: pl.BlockSpec(a.shape, lambda b, j: (0,) * a.ndim)
    per_j = lambda a: pl.BlockSpec((1,) + a.shape[1:], lambda b, j: (j,) + (0,) * (a.ndim - 1))
    return pl.pallas_call(
        functools.partial(_s5_kernel, n_super=nm),
        grid=(bsz, nj),
        in_specs=[
            pl.BlockSpec((seq, w), lambda b, j: (b, col0 // w + j)),
            per_j(bdb), per_j(bdc), per_j(apow), per_j(dskip),
            const(p1), const(p1), const(p2), const(p2),
        ],
        out_specs=pl.BlockSpec((seq, w), lambda b, j: (b, j)),
        out_shape=jax.ShapeDtypeStruct((bsz * seq, nj * w), BF16),
        scratch_shapes=[
            pltpu.VMEM((S5_CHUNK, rows, w), BF16), pltpu.VMEM((S5_CHUNK, rows, w), BF16),
            pltpu.VMEM((rows, sl), F32), pltpu.VMEM((rows, sl), F32),
            pltpu.VMEM((rows, sl), F32), pltpu.VMEM((rows, sl), F32),
            pltpu.VMEM((nm, sl), F32), pltpu.VMEM((nm, sl), F32),
            pltpu.VMEM((S5_CHUNK, rows, w), BF16), pltpu.VMEM((seq, w), BF16),
        ],
        compiler_params=_cparams(("arbitrary", "arbitrary")),
        name="s5",
    )(proj, bdb, bdc, apow, dskip, p1, p1.T, p2, p2.T)


def _merge_kernel(vc_ref, ys_ref, l1_ref, l2_ref, cpw_ref, cpb_ref, wa_ref, wb_ref, gb1_ref, gb2_ref, o_ref):
    y_conv = jnp.dot(vc_ref[...], cpw_ref[...], preferred_element_type=F32) + cpb_ref[...]
    ys = ys_ref[...]
    a = jnp.dot(ys, wa_ref[...], preferred_element_type=F32)
    b = jnp.dot(ys, wb_ref[...], preferred_element_type=F32)
    y_s5 = a * _sigmoid(b)
    g1 = _sigmoid(l1_ref[...].astype(F32) + gb1_ref[...])
    g2 = _sigmoid(l2_ref[...].astype(F32) + gb2_ref[...])
    o_ref[...] = (g1 * y_conv + g2 * y_s5).astype(BF16)


def _merge(vc, ys, proj, cp_w, cp_b, wa, wb, gate_b, tm, col0):
    t, ch = vc.shape
    d = cp_w.shape[1]
    tn = ch
    assert col0 % tn == 0 and d % tn == 0
    cb, nj = col0 // tn, d // tn
    wcol = lambda rows: pl.BlockSpec((rows, tn), lambda j, i: (0, j))
    return pl.pallas_call(
        _merge_kernel,
        grid=(nj, t // tm),
        in_specs=[
            pl.BlockSpec((tm, ch), lambda j, i: (i, 0)),
            pl.BlockSpec((tm, ch), lambda j, i: (i, 0)),
            pl.BlockSpec((tm, tn), lambda j, i: (i, cb + j)),
            pl.BlockSpec((tm, tn), lambda j, i: (i, cb + nj + j)),
            wcol(ch), wcol(1), wcol(ch), wcol(ch), wcol(1), wcol(1),
        ],
        out_specs=pl.BlockSpec((tm, tn), lambda j, i: (i, j)),
        out_shape=jax.ShapeDtypeStruct((t, d), BF16),
        compiler_params=_cparams(("arbitrary", "arbitrary")),
        name="merge",
    )(vc, ys, proj, proj, cp_w.astype(BF16), cp_b.reshape(1, d), wa.astype(BF16), wb.astype(BF16),
      gate_b[:d].reshape(1, d), gate_b[d:].reshape(1, d))


def _split_bf16(v):
    hi = v.astype(BF16)
    lo = (v - hi.astype(F32)).astype(BF16)
    return hi, lo


def _bf16_bits_hi(v):
    b = lax.bitcast_convert_type(v, jnp.uint32)
    return (b + jnp.uint32(0x7FFF) + ((b >> 16) & jnp.uint32(1))) & jnp.uint32(0xFFFF0000)


def _pack_bf16_pairs(v):
    half = v.shape[-1] // 2
    return _bf16_bits_hi(v[:, half:]) | (_bf16_bits_hi(v[:, :half]) >> 16)


def _unpack_bf16_pairs(p):
    lo = lax.bitcast_convert_type(p << 16, F32).astype(BF16)
    hi = lax.bitcast_convert_type(p & jnp.uint32(0xFFFF0000), F32).astype(BF16)
    return lo, hi


def _outproj_kernel(m_ref, x_ref, ada_ref, wo_ref, g1_ref, g2_ref, rwh_ref, rwl_ref, rb_ref,
                    x1_ref, h2p_ref, idx_ref, gate_ref, *, ne):
    m = jnp.dot(m_ref[...], wo_ref[...], preferred_element_type=F32)
    x1 = x_ref[...] + ada_ref[0, 2:3, :] * _rms(m, g1_ref[...])
    x1_ref[...] = x1
    h2 = _rms(x1, g2_ref[...]) * (1.0 + ada_ref[0, 4:5, :]) + ada_ref[0, 3:4, :]
    h2p_ref[...] = _pack_bf16_pairs(h2)
    hh, hl = _split_bf16(h2)
    lg = jnp.dot(hh, rwh_ref[...], preferred_element_type=F32)
    lg = lg + jnp.dot(hl, rwh_ref[...], preferred_element_type=F32)
    lg = lg + jnp.dot(hh, rwl_ref[...], preferred_element_type=F32)
    lg = lg + rb_ref[...]
    lane = lax.broadcasted_iota(jnp.int32, lg.shape, 1)
    work = jnp.where(lane < ne, lg, -jnp.inf)
    vals, idxs = [], []
    for _ in range(TOP_K):
        mx = jnp.max(work, axis=-1, keepdims=True)
        ix = jnp.min(jnp.where(work == mx, lane, lg.shape[1]), axis=-1, keepdims=True)
        vals.append(mx)
        idxs.append(ix)
        work = jnp.where(lane == ix, -jnp.inf, work)
    ex = [jnp.exp(v - vals[0]) for v in vals]
    den = ex[0]
    for e in ex[1:]:
        den = den + e
    idx_out = jnp.zeros(lg.shape, jnp.int32)
    gate_out = jnp.zeros(lg.shape, F32)
    for k in range(TOP_K):
        idx_out = jnp.where(lane == k, idxs[k], idx_out)
        gate_out = jnp.where(lane == k, ex[k] / den, gate_out)
    idx_ref[...] = idx_out
    gate_ref[...] = gate_out


def _outproj(merged, x2, ada3, w_out, post_mix_g, pre_ffn_g, router_w, router_b, seq, tm):
    t, d = x2.shape
    ne = router_w.shape[1]
    nep = max(V7X_LANES, ne)
    rw = jnp.zeros((d, nep), F32).at[:, :ne].set(router_w)
    rw_hi, rw_lo = _split_bf16(rw)
    rb = jnp.zeros((1, nep), F32).at[0, :ne].set(router_b)
    per_b = seq // tm
    full = lambda shape: pl.BlockSpec(shape, lambda i: (0, 0))
    row = lambda w: pl.BlockSpec((tm, w), lambda i: (i, 0))
    x1, h2p, top_idx, gates = pl.pallas_call(
        functools.partial(_outproj_kernel, ne=ne),
        grid=(t // tm,),
        in_specs=[
            row(d), row(d),
            pl.BlockSpec((1, N_ADA, d), lambda i: (i // per_b, 0, 0)),
            full((d, d)), full((1, d)), full((1, d)), full((d, nep)), full((d, nep)), full((1, nep)),
        ],
        out_specs=[row(d), row(d // 2), row(nep), row(nep)],
        out_shape=[jax.ShapeDtypeStruct((t, d), F32), jax.ShapeDtypeStruct((t, d // 2), jnp.uint32),
                   jax.ShapeDtypeStruct((t, nep), jnp.int32), jax.ShapeDtypeStruct((t, nep), F32)],
        compiler_params=_cparams(("arbitrary",)),
        name="outproj",
    )(merged, x2, ada3, w_out.astype(BF16), post_mix_g.reshape(1, d), pre_ffn_g.reshape(1, d),
      rw_hi, rw_lo, rb)
    return x1, h2p, top_idx[:, :TOP_K], gates


GATHER_UNROLL = 8


def _row_copy(idx_ref, src_hbm, buf, sems, slot, r):
    return pltpu.make_async_copy(src_hbm.at[pl.ds(idx_ref[0, 0, r], 1), :],
                                 buf.at[slot, pl.ds(r, 1), :], sems.at[slot])


def _start_rows(idx_ref, src_hbm, buf, sems, slot, n, unrolled=False):
    if unrolled:
        for r in range(n):
            _row_copy(idx_ref, src_hbm, buf, sems, slot, r).start(priority=r % 2)
        return

    def start(r, carry):
        _row_copy(idx_ref, src_hbm, buf, sems, slot, r).start()
        return carry
    lax.fori_loop(0, n, start, 0, unroll=GATHER_UNROLL)


def _wait_rows(src_hbm, buf, sems, slot):
    n = buf.shape[1]
    pltpu.make_async_copy(src_hbm.at[pl.ds(0, n), :], buf.at[slot], sems.at[slot]).wait()


def _gather_kernel(tok_ref, h_hbm, o_ref, buf, sems):
    i = pl.program_id(0)
    n_steps = pl.num_programs(0) - 1

    @pl.when(i < n_steps)
    def _():
        _start_rows(tok_ref, h_hbm, buf, sems, i % 2, buf.shape[1], unrolled=True)

    @pl.when(i > 0)
    def _():
        slot = (i - 1) % 2
        _wait_rows(h_hbm, buf, sems, slot)
        o_ref[...] = buf[slot]


def _moe_gather(tok_buf, h2p, rows):
    n_rows = tok_buf.size
    assert n_rows % rows == 0
    n_steps = n_rows // rows
    t, w = h2p.shape
    return pl.pallas_call(
        _gather_kernel,
        grid=(n_steps + 1,),
        in_specs=[
            pl.BlockSpec((1, 1, rows), lambda i: (jnp.minimum(i, n_steps - 1), 0, 0), memory_space=pltpu.SMEM),
            pl.BlockSpec(memory_space=pl.ANY),
        ],
        out_specs=pl.BlockSpec((rows, w), lambda i: (jnp.maximum(i - 1, 0), 0)),
        out_shape=jax.ShapeDtypeStruct((n_rows, w), h2p.dtype),
        scratch_shapes=[pltpu.VMEM((2, rows, w), h2p.dtype), pltpu.SemaphoreType.DMA((2,))],
        compiler_params=_cparams(("arbitrary",)),
        name="moe_gather",
    )(tok_buf.reshape(n_steps, 1, rows), h2p)


def _group_weights(gf_ref, gn_ref, e_ref, i, first_step, more_passes, copy_of, gcnt):
    @pl.when(first_step)
    def _():
        gcnt[0] = 0
        copy_of(e_ref[0], False, 0).start()

    slot = gcnt[0] % 2

    @pl.when(gf_ref[i] == 1)
    def _():
        copy_of(e_ref[0], False, slot).wait()
        nxt = gn_ref[i]

        @pl.when(nxt >= 0)
        def _():
            copy_of(nxt, False, 1 - slot).start()

        @pl.when(jnp.logical_and(nxt < 0, more_passes))
        def _():
            copy_of(e_ref[0], True, 1 - slot).start()

        gcnt[0] = gcnt[0] + 1

    return slot


def _up_kernel(e_ref, nv_ref, gf_ref, gn_ref, x_ref, w_hbm, bg_ref, bl_ref, p_ref, o_ref,
               wp_ref, wbuf, wsem, gcnt):
    j, i = pl.program_id(0), pl.program_id(1)
    tn = wp_ref.shape[1]
    half = V7X_MXU // 2
    hd = x_ref.shape[1]

    def copy_of(expert, next_pass, slot):
        col = (j + 1) * tn if next_pass else j * tn
        return pltpu.make_async_copy(w_hbm.at[expert, :, pl.ds(col, tn)], wbuf.at[slot], wsem.at[slot])

    slot = _group_weights(gf_ref, gn_ref, e_ref, i, jnp.logical_and(j == 0, i == 0),
                          j < pl.num_programs(0) - 1, copy_of, gcnt)

    @pl.when(gf_ref[i] == 1)
    def _():
        for cb in range(tn // V7X_MXU):
            cols = slice(cb * V7X_MXU, (cb + 1) * V7X_MXU)
            wp_ref[:, cols] = jnp.dot(wbuf[slot, :, cols].astype(BF16), p_ref[...],
                                      preferred_element_type=F32).astype(BF16)

    @pl.when(nv_ref[i] > 0)
    def _():
        x_lo, x_hi = _unpack_bf16_pairs(x_ref[...])
        for cb in range(tn // V7X_MXU):
            cols = slice(cb * V7X_MXU, (cb + 1) * V7X_MXU)
            hb = (jnp.dot(x_lo, wp_ref[0:hd, cols], preferred_element_type=F32)
                  + jnp.dot(x_hi, wp_ref[hd:2 * hd, cols], preferred_element_type=F32))
            out_cols = slice(cb * half, (cb + 1) * half)
            x_glu = hb[:, :half] + bg_ref[0, :, out_cols]
            x_lin = hb[:, half:] + bl_ref[0, :, out_cols]
            x_glu = jnp.minimum(x_glu, SWIGLU_LIMIT)
            x_lin = jnp.clip(x_lin, -SWIGLU_LIMIT, SWIGLU_LIMIT)
            act = x_glu * _sigmoid(SWIGLU_ALPHA * x_glu) * (x_lin + 1.0)
            o_ref[:, out_cols] = act.astype(BF16)

    @pl.when(nv_ref[i] == 0)
    def _():
        o_ref[...] = jnp.zeros(o_ref.shape, o_ref.dtype)


def _moe_up(blk_e, nvalid, gfirst, gnext, xs, w1, b1, tn):
    n_rows, hd = xs.shape
    ne, d, f2 = w1.shape
    assert d == 2 * hd
    f = f2 // 2
    n_blocks = n_rows // MOE_BLOCK
    half = V7X_MXU // 2
    c = jnp.arange(V7X_MXU)
    perm = (((c % 2) * half + c // 2)[:, None] == c[None, :]).astype(BF16)
    b1g = b1[:, 0::2].reshape(ne, 1, f)
    b1l = b1[:, 1::2].reshape(ne, 1, f)
    return pl.pallas_call(
        _up_kernel,
        grid_spec=pltpu.PrefetchScalarGridSpec(
            num_scalar_prefetch=4,
            grid=(f2 // tn, n_blocks),
            in_specs=[
                pl.BlockSpec((MOE_BLOCK, hd), lambda j, i, e, nv, gf, gn: (i, 0)),
                pl.BlockSpec(memory_space=pl.ANY),
                pl.BlockSpec((1, 1, tn // 2), lambda j, i, e, nv, gf, gn: (e[i], 0, j)),
                pl.BlockSpec((1, 1, tn // 2), lambda j, i, e, nv, gf, gn: (e[i], 0, j)),
                pl.BlockSpec((V7X_MXU, V7X_MXU), lambda j, i, e, nv, gf, gn: (0, 0)),
            ],
            out_specs=pl.BlockSpec((MOE_BLOCK, tn // 2), lambda j, i, e, nv, gf, gn: (i, j)),
            scratch_shapes=[pltpu.VMEM((d, tn), BF16), pltpu.VMEM((2, d, tn), F32),
                            pltpu.SemaphoreType.DMA((2,)), pltpu.SMEM((1,), jnp.int32)],
        ),
        out_shape=jax.ShapeDtypeStruct((n_rows, f), BF16),
        compiler_params=_cparams(("arbitrary", "arbitrary")),
        name="moe_up",
    )(blk_e, nvalid, gfirst, gnext, xs, w1, b1g, b1l, perm)


def _down_kernel(e_ref, nv_ref, gf_ref, gn_ref, a_ref, w_hbm, b_ref, o_ref, wb_ref, wbuf, wsem, gcnt):
    i = pl.program_id(0)

    def copy_of(expert, next_pass, slot):
        return pltpu.make_async_copy(w_hbm.at[expert], wbuf.at[slot], wsem.at[slot])

    slot = _group_weights(gf_ref, gn_ref, e_ref, i, i == 0, False, copy_of, gcnt)

    @pl.when(gf_ref[i] == 1)
    def _():
        wb_ref[...] = wbuf[slot].astype(BF16)

    @pl.when(nv_ref[i] > 0)
    def _():
        o_ref[...] = jnp.dot(a_ref[...], wb_ref[...], preferred_element_type=F32) + b_ref[0]

    @pl.when(nv_ref[i] == 0)
    def _():
        o_ref[...] = jnp.zeros(o_ref.shape, o_ref.dtype)


def _moe_down(blk_e, nvalid, gfirst, gnext, act, w2, b2):
    n_rows, f = act.shape
    ne, _, d = w2.shape
    n_blocks = n_rows // MOE_BLOCK
    return pl.pallas_call(
        _down_kernel,
        grid_spec=pltpu.PrefetchScalarGridSpec(
            num_scalar_prefetch=4,
            grid=(n_blocks,),
            in_specs=[
                pl.BlockSpec((MOE_BLOCK, f), lambda i, e, nv, gf, gn: (i, 0)),
                pl.BlockSpec(memory_space=pl.ANY),
                pl.BlockSpec((1, 1, d), lambda i, e, nv, gf, gn: (e[i], 0, 0)),
            ],
            out_specs=pl.BlockSpec((MOE_BLOCK, d), lambda i, e, nv, gf, gn: (i, 0)),
            scratch_shapes=[pltpu.VMEM((f, d), BF16), pltpu.VMEM((2, f, d), F32),
                            pltpu.SemaphoreType.DMA((2,)), pltpu.SMEM((1,), jnp.int32)],
        ),
        out_shape=jax.ShapeDtypeStruct((n_rows, d), F32),
        compiler_params=_cparams(("arbitrary",)),
        name="moe_down",
    )(blk_e, nvalid, gfirst, gnext, act, w2, b2.reshape(ne, 1, d))


def _combine_kernel(pos_ref, y_hbm, gate_ref, x1_ref, ada_ref, g_ref, o_ref, buf, sems, *, tt):
    i = pl.program_id(0)
    n_tiles = pl.num_programs(0) - 1
    n_rows = TOP_K * tt

    def finish(slot):
        gates = gate_ref[...]
        f = buf[slot, 0:tt, :] * gates[:, 0:1]
        for k in range(1, TOP_K):
            f = f + buf[slot, k * tt:(k + 1) * tt, :] * gates[:, k:k + 1]
        o_ref[...] = x1_ref[...] + ada_ref[0, 5:6, :] * _rms(f, g_ref[...])

    @pl.when(i == 0)
    def _():
        _start_rows(pos_ref, y_hbm, buf, sems, 0, n_rows)

    @pl.when(jnp.logical_and(i > 0, i < n_tiles))
    def _():
        _wait_rows(y_hbm, buf, sems, (i - 1) % 2)
        _start_rows(pos_ref, y_hbm, buf, sems, i % 2, n_rows, unrolled=True)
        finish((i - 1) % 2)

    @pl.when(i == n_tiles)
    def _():
        _wait_rows(y_hbm, buf, sems, (i - 1) % 2)
        finish((i - 1) % 2)


def _moe_combine(pos, y_buf, gates, x1, ada3, post_ffn_g, seq, tt):
    t, d = x1.shape
    per_b = seq // tt
    nt = t // tt
    pos_t = pos.reshape(nt, tt, TOP_K).transpose(0, 2, 1).reshape(nt, 1, TOP_K * tt)
    prev = lambda i: jnp.maximum(i - 1, 0)
    return pl.pallas_call(
        functools.partial(_combine_kernel, tt=tt),
        grid=(nt + 1,),
        in_specs=[
            pl.BlockSpec((1, 1, TOP_K * tt), lambda i: (jnp.minimum(i, nt - 1), 0, 0), memory_space=pltpu.SMEM),
            pl.BlockSpec(memory_space=pl.ANY),
            pl.BlockSpec((tt, gates.shape[1]), lambda i: (prev(i), 0)),
            pl.BlockSpec((tt, d), lambda i: (prev(i), 0)),
            pl.BlockSpec((1, N_ADA, d), lambda i: (prev(i) // per_b, 0, 0)),
            pl.BlockSpec((1, d), lambda i: (0, 0)),
        ],
        out_specs=pl.BlockSpec((tt, d), lambda i: (prev(i), 0)),
        out_shape=jax.ShapeDtypeStruct((t, d), F32),
        scratch_shapes=[pltpu.VMEM((2, TOP_K * tt, d), F32), pltpu.SemaphoreType.DMA((2,))],
        compiler_params=_cparams(("arbitrary",)),
        name="moe_combine",
    )(pos_t, y_buf, gates, x1, ada3, post_ffn_g.reshape(1, d))


def _route(top_idx, n_experts):
    n_tok = top_idx.shape[0]
    n_asg = n_tok * TOP_K
    i32 = jnp.int32
    flat_e = top_idx.reshape(n_asg)
    order = jnp.argsort(flat_e).astype(i32)
    counts = jnp.sum(flat_e[:, None] == jnp.arange(n_experts, dtype=i32)[None, :], axis=0, dtype=i32)
    starts = jnp.cumsum(counts) - counts
    padded = (counts + MOE_BLOCK - 1) // MOE_BLOCK * MOE_BLOCK
    pends = jnp.cumsum(padded)
    pstarts = pends - padded
    n_blocks = -(-n_asg // MOE_BLOCK) + n_experts
    blk_start = jnp.arange(n_blocks, dtype=i32) * MOE_BLOCK
    blk_e = jnp.minimum(jnp.sum(pends[None, :] <= blk_start[:, None], axis=1, dtype=i32), n_experts - 1)
    off = blk_start - pstarts[blk_e]
    nvalid = jnp.clip(counts[blk_e] - off, 0, MOE_BLOCK)
    j = off[:, None] + jnp.arange(MOE_BLOCK, dtype=i32)[None, :]
    valid = j < counts[blk_e][:, None]
    src = jnp.clip(starts[blk_e][:, None] + j, 0, n_asg - 1)
    asg = order[src]
    row_id = blk_start[:, None] + jnp.arange(MOE_BLOCK, dtype=i32)[None, :]
    tok_buf = jnp.where(valid, asg // TOP_K, row_id % n_tok)
    rank = jnp.argsort(order).astype(i32)
    pos = (pstarts - starts)[flat_e] + rank
    blk = jnp.arange(n_blocks, dtype=i32)
    prev_e = jnp.concatenate([jnp.full((1,), -1, i32), blk_e[:-1]])
    gfirst = jnp.logical_and(nvalid > 0, blk_e != prev_e).astype(i32)
    cand = jnp.where(gfirst == 1, blk, n_blocks)
    later = jnp.concatenate([lax.cummin(cand[::-1])[::-1][1:], jnp.full((1,), n_blocks, i32)])
    gnext = jnp.where(later < n_blocks, blk_e[jnp.minimum(later, n_blocks - 1)], -1)
    return tok_buf, pos, blk_e, nvalid, gfirst, gnext


def _pick(n, pref):
    return pref if n % pref == 0 else n


def kernel(x, c, ada_w, ada_b, pre_mix_g, post_mix_g, pre_ffn_g, post_ffn_g, w_in, gate_b, dw_w, dw_b, cln_g, cln_b, cp_w, cp_b, s5_a_re, s5_a_im, s5_log_dt, s5_b_re, s5_b_im, s5_c_re, s5_c_im, s5_d, glu_wa, glu_wb, w_out, router_w, router_b, w1, b1, w2, b2):
    bsz, seq, d = x.shape
    t = bsz * seq
    depth = ada_w.shape[0]
    conv_ch = dw_w.shape[-1]
    ng, ns, nh = s5_b_re.shape[1:]
    s5_w = ng * nh
    ne = router_w.shape[-1]
    col_s5 = 2 * conv_ch
    col_gate = col_s5 + s5_w
    assert conv_ch == s5_w and V7X_MXU % nh == 0 and ng % (V7X_MXU // nh) == 0
    tm = _pick(seq, 512)
    tm_out = _pick(seq, 256)
    tn_in = _pick(w_in.shape[-1], conv_ch)

    x2 = x.reshape(t, d)
    for l in range(depth):
        ada3 = _ada(c, ada_w[l], ada_b[l]).reshape(bsz, N_ADA, d)
        proj = _inproj(x2, ada3, pre_mix_g[l], w_in[l], seq, _pick(seq, 1024), tn_in)
        vc = _conv_branch(proj, dw_w[l], dw_b[l], cln_g[l], cln_b[l], bsz, seq, tm)

        s5_params = _s5_params(s5_a_re[l], s5_a_im[l], s5_log_dt[l], s5_b_re[l], s5_b_im[l],
                               s5_c_re[l], s5_c_im[l], s5_d[l])
        ys = _s5_branch(proj, col_s5, s5_params, bsz, seq)

        merged = _merge(vc, ys, proj, cp_w[l], cp_b[l], glu_wa[l], glu_wb[l], gate_b[l], tm, col_gate)
        x1, h2p, top_idx, gates = _outproj(merged, x2, ada3, w_out[l], post_mix_g[l], pre_ffn_g[l],
                                           router_w[l], router_b[l], seq, tm_out)

        tok_buf, pos, blk_e, nvalid, gfirst, gnext = _route(top_idx, ne)
        nb = tok_buf.shape[0]
        xs = _moe_gather(tok_buf, h2p, MOE_BLOCK * next(k for k in (4, 2, 1) if nb % k == 0))
        act = _moe_up(blk_e, nvalid, gfirst, gnext, xs, w1[l], b1[l], _pick(w1.shape[-1], 2048))
        y_buf = _moe_down(blk_e, nvalid, gfirst, gnext, act, w2[l], b2[l])
        x2 = _moe_combine(pos, y_buf, gates, x1, ada3, post_ffn_g[l], seq, _pick(seq, 256))
    return x2.reshape(bsz, seq, d)
```

```python
import functools
import math

import jax
import jax.numpy as jnp
from jax import lax
from jax.experimental import pallas as pl
from jax.experimental.pallas import tpu as pltpu

EPS = 1e-6
N_ADA = 6
TOP_K = 4
MOE_BLOCK = 256
SWIGLU_ALPHA = 1.702
SWIGLU_LIMIT = 7.0
S5_CHUNK = 16
V7X_LANES = 128
V7X_SUBLANES = 8
V7X_MXU = 256
VMEM_LIMIT = 56 * 1024 * 1024

F32 = jnp.float32
BF16 = jnp.bfloat16


def _cparams(sem):
    return pltpu.CompilerParams(dimension_semantics=sem, vmem_limit_bytes=VMEM_LIMIT)


def _sigmoid(v):
    return 1.0 / (1.0 + jnp.exp(-v))


def _rms(v, g):
    return v * lax.rsqrt(jnp.mean(v * v, axis=-1, keepdims=True) + EPS) * g


def _ada_kernel(c_ref, w_ref, b_ref, o_ref):
    c = c_ref[...]
    s = (c * _sigmoid(c)).astype(BF16)
    o_ref[...] = jnp.dot(s, w_ref[...].astype(BF16), preferred_element_type=F32) + b_ref[...]


def _ada(c, ada_w, ada_b):
    bsz, d = c.shape
    n = ada_w.shape[1]
    tn = 1024 if n % 1024 == 0 else n
    return pl.pallas_call(
        _ada_kernel,
        grid=(n // tn,),
        in_specs=[
            pl.BlockSpec((bsz, d), lambda j: (0, 0)),
            pl.BlockSpec((d, tn), lambda j: (0, j)),
            pl.BlockSpec((1, tn), lambda j: (0, j)),
        ],
        out_specs=pl.BlockSpec((bsz, tn), lambda j: (0, j)),
        out_shape=jax.ShapeDtypeStruct((bsz, n), F32),
        compiler_params=_cparams(("arbitrary",)),
        name="ada",
    )(c, ada_w, ada_b.reshape(1, n))


def _inproj_kernel(x_ref, ada_ref, g_ref, w_ref, o_ref, h_ref):
    @pl.when(pl.program_id(1) == 0)
    def _():
        y = _rms(x_ref[...], g_ref[...])
        h = y * (1.0 + ada_ref[0, 1:2, :]) + ada_ref[0, 0:1, :]
        h_ref[...] = h.astype(BF16)

    o_ref[...] = jnp.dot(h_ref[...], w_ref[...].astype(BF16), preferred_element_type=F32).astype(BF16)


def _inproj(x2, ada3, g, w_bf, seq, tm, tn):
    t, d = x2.shape
    n = w_bf.shape[1]
    per_b = seq // tm
    return pl.pallas_call(
        _inproj_kernel,
        grid=(t // tm, n // tn),
        in_specs=[
            pl.BlockSpec((tm, d), lambda i, j: (i, 0)),
            pl.BlockSpec((1, N_ADA, d), lambda i, j: (i // per_b, 0, 0)),
            pl.BlockSpec((1, d), lambda i, j: (0, 0)),
            pl.BlockSpec((d, tn), lambda i, j: (0, j)),
        ],
        out_specs=pl.BlockSpec((tm, tn), lambda i, j: (i, j)),
        out_shape=jax.ShapeDtypeStruct((t, n), BF16),
        scratch_shapes=[pltpu.VMEM((tm, d), BF16)],
        compiler_params=_cparams(("arbitrary", "arbitrary")),
        name="inproj",
    )(x2, ada3, g.reshape(1, d), w_bf)


CONV_HALO = 32
CONV_ROWS = 64


def _conv_kernel(pv_ref, pg_ref, w_ref, b_ref, g_ref, beta_ref, o_ref, vext, vsh, cbuf, *, taps, tl):
    t = pl.program_id(1)
    ch = cbuf.shape[1]

    @pl.when(t == 0)
    def _():
        vext[0:CONV_HALO, :] = jnp.zeros((CONV_HALO, ch), F32)

    @pl.when(t > 0)
    def _():
        vext[0:CONV_HALO, :] = vext[tl:tl + CONV_HALO, :]

    pv = pv_ref[...].astype(F32)
    pg = pg_ref[...].astype(F32)
    vext[CONV_HALO:CONV_HALO + tl, :] = pv * _sigmoid(pg)

    n_sh = vsh.shape[1]
    for p in range(1, V7X_SUBLANES):
        vsh[p - 1] = vext[p:p + n_sh, :]

    off = CONV_HALO - (taps - 1)
    for cb in range(ch // V7X_LANES):
        lanes = slice(cb * V7X_LANES, (cb + 1) * V7X_LANES)

        for rc in range(tl // CONV_ROWS):
            r0 = rc * CONV_ROWS
            acc = jnp.zeros((CONV_ROWS, V7X_LANES), F32)
            for k in range(taps):
                p = (off + k) % V7X_SUBLANES
                base = r0 + off + k - p
                win = vext[base:base + CONV_ROWS, lanes] if p == 0 else vsh[p - 1, base:base + CONV_ROWS, lanes]
                acc = acc + w_ref[k:k + 1, lanes] * win
            cbuf[r0:r0 + CONV_ROWS, lanes] = acc + b_ref[:, lanes]

    v = cbuf[...]
    mu = jnp.mean(v, axis=-1, keepdims=True)
    xc = v - mu
    var = jnp.mean(xc * xc, axis=-1, keepdims=True)
    y = xc * lax.rsqrt(var + EPS) * g_ref[...] + beta_ref[...]
    o_ref[...] = (y * _sigmoid(y)).astype(BF16)


def _conv_branch(proj, dw_w, dw_b, cln_g, cln_b, bsz, seq, tl):
    taps, ch = dw_w.shape
    assert taps - 1 <= CONV_HALO and tl % CONV_ROWS == 0 and ch % V7X_LANES == 0
    per_b = seq // tl
    vec = lambda a: a.reshape(1, ch)
    return pl.pallas_call(
        functools.partial(_conv_kernel, taps=taps, tl=tl),
        grid=(bsz, per_b),
        in_specs=[
            pl.BlockSpec((tl, ch), lambda b, t: (b * per_b + t, 0)),
            pl.BlockSpec((tl, ch), lambda b, t: (b * per_b + t, 1)),
            pl.BlockSpec((taps, ch), lambda b, t: (0, 0)),
            pl.BlockSpec((1, ch), lambda b, t: (0, 0)),
            pl.BlockSpec((1, ch), lambda b, t: (0, 0)),
            pl.BlockSpec((1, ch), lambda b, t: (0, 0)),
        ],
        out_specs=pl.BlockSpec((tl, ch), lambda b, t: (b * per_b + t, 0)),
        out_shape=jax.ShapeDtypeStruct((bsz * seq, ch), BF16),
        scratch_shapes=[pltpu.VMEM((CONV_HALO + tl, ch), F32),
                        pltpu.VMEM((V7X_SUBLANES - 1, CONV_HALO + tl - V7X_SUBLANES, ch), F32),
                        pltpu.VMEM((tl, ch), F32)],
        compiler_params=_cparams(("arbitrary", "arbitrary")),
        name="conv",
    )(proj, proj, dw_w, vec(dw_b), vec(cln_g), vec(cln_b))


S5_SUPER = 16


def _s5_params(a_re, a_im, log_dt, b_re, b_im, c_re, c_im, d_skip):
    ng, ns = a_re.shape
    nh = b_re.shape[-1]
    gt = V7X_MXU // nh
    nj = ng // gt
    dt = jnp.exp(log_dt)[:, None]
    mag = jnp.exp(a_re * dt)
    abar_re = mag * jnp.cos(a_im * dt)
    abar_im = mag * jnp.sin(a_im * dt)
    den = a_re * a_re + a_im * a_im
    num_re = abar_re - 1.0
    coef_re = (num_re * a_re + abar_im * a_im) / den
    coef_im = (abar_im * a_re - num_re * a_im) / den
    bb_re = coef_re[:, :, None] * b_re - coef_im[:, :, None] * b_im
    bb_im = coef_re[:, :, None] * b_im + coef_im[:, :, None] * b_re
    p = jnp.asarray([1.0, S5_CHUNK, S5_CHUNK * S5_SUPER] + [S5_CHUNK * r for r in range(S5_SUPER)], F32)
    pmag = jnp.exp(p[:, None, None] * (a_re * dt)[None])
    pw_re = pmag * jnp.cos(p[:, None, None] * (a_im * dt)[None])
    pw_im = pmag * jnp.sin(p[:, None, None] * (a_im * dt)[None])
    tile = lambda a: a.reshape(-1, nj, gt * ns).transpose(1, 0, 2)
    apow = jnp.concatenate([tile(pw_re), tile(pw_im)], axis=1)
    eye = jnp.eye(gt, dtype=F32)

    def bdiag_in(bb):
        return jnp.einsum('jgnh,gk->jghkn', bb.reshape(nj, gt, ns, nh), eye).reshape(nj, gt * nh, gt * ns)

    def bdiag_out(cc):
        return jnp.einsum('jghn,gk->jgnkh', cc.reshape(nj, gt, nh, ns), eye).reshape(nj, gt * ns, gt * nh)

    bdb = jnp.stack([bdiag_in(bb_re), bdiag_in(bb_im)], axis=1)
    bdc = jnp.stack([bdiag_out(c_re), bdiag_out(-c_im)], axis=1)
    return bdb.astype(BF16), bdc.astype(BF16), apow, d_skip.reshape(nj, 1, gt * nh)


def _cmul_add(ar, ai, xr, xi, vr, vi):
    return ar * xr - ai * xi + vr, ar * xi + ai * xr + vi


def _s5_kernel(u_ref, bdb_ref, bdc_ref, ap_ref, d_ref, p1_ref, p1t_ref, p2_ref, p2t_ref, o_ref,
               u2n_ref, u2_ref, hr_ref, hi_ref, zr_ref, zi_ref, gr_ref, gi_ref, y2_ref, y2n_ref,
               *, n_super):
    tc, nr, nm = S5_CHUNK, S5_SUPER, n_super
    tile = tc * nr
    np_ = 3 + nr
    apow = lambda k: (ap_ref[0, k:k + 1, :], ap_ref[0, np_ + k:np_ + k + 1, :])
    ar, ai = apow(0)

    for m in range(nm):
        pu = jnp.dot(p1_ref[...], u_ref[m * tile:(m + 1) * tile, :], preferred_element_type=F32).astype(BF16)
        for s in range(tc):
            u2n_ref[s, m * nr:(m + 1) * nr, :] = pu[s * nr:(s + 1) * nr, :]
    for s in range(tc):
        u2_ref[s] = jnp.dot(p2_ref[...], u2n_ref[s], preferred_element_type=F32).astype(BF16)

    def drive(s):
        u = u2_ref[s]
        return (jnp.dot(u, bdb_ref[0, 0], preferred_element_type=F32),
                jnp.dot(u, bdb_ref[0, 1], preferred_element_type=F32))

    hr_ref[...] = jnp.zeros(hr_ref.shape, F32)
    hi_ref[...] = jnp.zeros(hi_ref.shape, F32)

    def pass1(s, carry):
        vr, vi = drive(s)
        hr, hi = _cmul_add(ar, ai, hr_ref[...], hi_ref[...], vr, vi)
        hr_ref[...] = hr
        hi_ref[...] = hi
        return carry

    lax.fori_loop(0, tc, pass1, 0, unroll=4)

    a16r, a16i = apow(1)
    qr = jnp.zeros((nm, hr_ref.shape[1]), F32)
    qi = qr
    for r in range(nr):
        rows = slice(r * nm, (r + 1) * nm)
        zr_ref[rows, :] = qr
        zi_ref[rows, :] = qi
        qr, qi = _cmul_add(a16r, a16i, qr, qi, hr_ref[rows, :], hi_ref[rows, :])
    a256r, a256i = apow(2)
    gr = jnp.zeros((1, hr_ref.shape[1]), F32)
    gi = gr
    for m in range(nm):
        gr_ref[m:m + 1, :] = gr
        gi_ref[m:m + 1, :] = gi
        gr, gi = _cmul_add(a256r, a256i, gr, gi, qr[m:m + 1, :], qi[m:m + 1, :])
    gpr = gr_ref[...]
    gpi = gi_ref[...]
    for r in range(nr):
        rows = slice(r * nm, (r + 1) * nm)
        pr, pi = apow(3 + r)
        zr, zi = _cmul_add(pr, pi, gpr, gpi, zr_ref[rows, :], zi_ref[rows, :])
        zr_ref[rows, :] = zr
        zi_ref[rows, :] = zi

    def pass2(t, carry):
        vr, vi = drive(t)
        sr, si = _cmul_add(ar, ai, zr_ref[...], zi_ref[...], vr, vi)
        zr_ref[...] = sr
        zi_ref[...] = si
        y = (jnp.dot(sr.astype(BF16), bdc_ref[0, 0], preferred_element_type=F32)
             + jnp.dot(si.astype(BF16), bdc_ref[0, 1], preferred_element_type=F32))
        y = y + d_ref[0] * u2_ref[t].astype(F32)
        y = 0.5 * y * (1.0 + jnp.tanh(math.sqrt(2.0 / math.pi) * (y + 0.044715 * (y * y * y))))
        y2_ref[t] = y.astype(BF16)
        return carry

    lax.fori_loop(0, tc, pass2, 0, unroll=4)

    for t in range(tc):
        yn = jnp.dot(p2t_ref[...], y2_ref[t], preferred_element_type=F32).astype(BF16)
        for m in range(nm):
            y2n_ref[m * tile + t * nr:m * tile + (t + 1) * nr, :] = yn[m * nr:(m + 1) * nr, :]
    for m in range(nm):
        rows = slice(m * tile, (m + 1) * tile)
        o_ref[rows, :] = jnp.dot(p1t_ref[...], y2n_ref[rows, :], preferred_element_type=F32).astype(BF16)


def _s5_branch(proj, col0, params, bsz, seq):
    bdb, bdc, apow, dskip = params
    nj, _, w, sl = bdb.shape
    tile = S5_CHUNK * S5_SUPER
    assert seq % tile == 0 and col0 % w == 0
    nm = seq // tile
    assert nm % V7X_SUBLANES == 0, "row slabs of the chunk recurrence must be whole sublane tiles"
    rows = S5_SUPER * nm
    i1 = jnp.arange(tile)
    p1 = (i1[:, None] == ((i1 % S5_CHUNK) * S5_SUPER + i1 // S5_CHUNK)[None, :]).astype(BF16)
    i2 = jnp.arange(rows)
    p2 = (i2[:, None] == ((i2 % S5_SUPER) * nm + i2 // S5_SUPER)[None, :]).astype(BF16)
    const = lambda a: pl.BlockSpec(a.shape, lambda b, j: (0,) * a.ndim)
    per_j = lambda a: pl.BlockSpec((1,) + a.shape[1:], lambda b, j: (j,) + (0,) * (a.ndim - 1))
    return pl.pallas_call(
        functools.partial(_s5_kernel, n_super=nm),
        grid=(bsz, nj),
        in_specs=[
            pl.BlockSpec((seq, w), lambda b, j: (b, col0 // w + j)),
            per_j(bdb), per_j(bdc), per_j(apow), per_j(dskip),
            const(p1), const(p1), const(p2), const(p2),
        ],
        out_specs=pl.BlockSpec((seq, w), lambda b, j: (b, j)),
        out_shape=jax.ShapeDtypeStruct((bsz * seq, nj * w), BF16),
        scratch_shapes=[
            pltpu.VMEM((S5_CHUNK, rows, w), BF16), pltpu.VMEM((S5_CHUNK, rows, w), BF16),
            pltpu.VMEM((rows, sl), F32), pltpu.VMEM((rows, sl), F32),
            pltpu.VMEM((rows, sl), F32), pltpu.VMEM((rows, sl), F32),
            pltpu.VMEM((nm, sl), F32), pltpu.VMEM((nm, sl), F32),
            pltpu.VMEM((S5_CHUNK, rows, w), BF16), pltpu.VMEM((seq, w), BF16),
        ],
        compiler_params=_cparams(("arbitrary", "arbitrary")),
        name="s5",
    )(proj, bdb, bdc, apow, dskip, p1, p1.T, p2, p2.T)


def _merge_kernel(vc_ref, ys_ref, l1_ref, l2_ref, cpw_ref, cpb_ref, wa_ref, wb_ref, gb1_ref, gb2_ref, o_ref):
    y_conv = jnp.dot(vc_ref[...], cpw_ref[...], preferred_element_type=F32) + cpb_ref[...]
    ys = ys_ref[...]
    a = jnp.dot(ys, wa_ref[...], preferred_element_type=F32)
    b = jnp.dot(ys, wb_ref[...], preferred_element_type=F32)
    y_s5 = a * _sigmoid(b)
    g1 = _sigmoid(l1_ref[...].astype(F32) + gb1_ref[...])
    g2 = _sigmoid(l2_ref[...].astype(F32) + gb2_ref[...])
    o_ref[...] = (g1 * y_conv + g2 * y_s5).astype(BF16)


def _merge(vc, ys, proj, cp_w, cp_b, wa, wb, gate_b, tm, col0):
    t, ch = vc.shape
    d = cp_w.shape[1]
    tn = ch
    assert col0 % tn == 0 and d % tn == 0
    cb, nj = col0 // tn, d // tn
    wcol = lambda rows: pl.BlockSpec((rows, tn), lambda j, i: (0, j))
    return pl.pallas_call(
        _merge_kernel,
        grid=(nj, t // tm),
        in_specs=[
            pl.BlockSpec((tm, ch), lambda j, i: (i, 0)),
            pl.BlockSpec((tm, ch), lambda j, i: (i, 0)),
            pl.BlockSpec((tm, tn), lambda j, i: (i, cb + j)),
            pl.BlockSpec((tm, tn), lambda j, i: (i, cb + nj + j)),
            wcol(ch), wcol(1), wcol(ch), wcol(ch), wcol(1), wcol(1),
        ],
        out_specs=pl.BlockSpec((tm, tn), lambda j, i: (i, j)),
        out_shape=jax.ShapeDtypeStruct((t, d), BF16),
        compiler_params=_cparams(("arbitrary", "arbitrary")),
        name="merge",
    )(vc, ys, proj, proj, cp_w.astype(BF16), cp_b.reshape(1, d), wa.astype(BF16), wb.astype(BF16),
      gate_b[:d].reshape(1, d), gate_b[d:].reshape(1, d))


def _split_bf16(v):
    hi = v.astype(BF16)
    lo = (v - hi.astype(F32)).astype(BF16)
    return hi, lo


def _bf16_bits_hi(v):
    b = lax.bitcast_convert_type(v, jnp.uint32)
    return (b + jnp.uint32(0x7FFF) + ((b >> 16) & jnp.uint32(1))) & jnp.uint32(0xFFFF0000)


def _pack_bf16_pairs(v):
    half = v.shape[-1] // 2
    return _bf16_bits_hi(v[:, half:]) | (_bf16_bits_hi(v[:, :half]) >> 16)


def _unpack_bf16_pairs(p):
    lo = lax.bitcast_convert_type(p << 16, F32).astype(BF16)
    hi = lax.bitcast_convert_type(p & jnp.uint32(0xFFFF0000), F32).astype(BF16)
    return lo, hi


def _outproj_kernel(m_ref, x_ref, ada_ref, wo_ref, g1_ref, g2_ref, rwh_ref, rwl_ref, rb_ref,
                    x1_ref, h2p_ref, idx_ref, gate_ref, *, ne):
    m = jnp.dot(m_ref[...], wo_ref[...], preferred_element_type=F32)
    x1 = x_ref[...] + ada_ref[0, 2:3, :] * _rms(m, g1_ref[...])
    x1_ref[...] = x1
    h2 = _rms(x1, g2_ref[...]) * (1.0 + ada_ref[0, 4:5, :]) + ada_ref[0, 3:4, :]
    h2p_ref[...] = _pack_bf16_pairs(h2)
    hh, hl = _split_bf16(h2)
    lg = jnp.dot(hh, rwh_ref[...], preferred_element_type=F32)
    lg = lg + jnp.dot(hl, rwh_ref[...], preferred_element_type=F32)
    lg = lg + jnp.dot(hh, rwl_ref[...], preferred_element_type=F32)
    lg = lg + rb_ref[...]
    lane = lax.broadcasted_iota(jnp.int32, lg.shape, 1)
    work = jnp.where(lane < ne, lg, -jnp.inf)
    vals, idxs = [], []
    for _ in range(TOP_K):
        mx = jnp.max(work, axis=-1, keepdims=True)
        ix = jnp.min(jnp.where(work == mx, lane, lg.shape[1]), axis=-1, keepdims=True)
        vals.append(mx)
        idxs.append(ix)
        work = jnp.where(lane == ix, -jnp.inf, work)
    ex = [jnp.exp(v - vals[0]) for v in vals]
    den = ex[0]
    for e in ex[1:]:
        den = den + e
    idx_out = jnp.zeros(lg.shape, jnp.int32)
    gate_out = jnp.zeros(lg.shape, F32)
    for k in range(TOP_K):
        idx_out = jnp.where(lane == k, idxs[k], idx_out)
        gate_out = jnp.where(lane == k, ex[k] / den, gate_out)
    idx_ref[...] = idx_out
    gate_ref[...] = gate_out


def _outproj(merged, x2, ada3, w_out, post_mix_g, pre_ffn_g, router_w, router_b, seq, tm):
    t, d = x2.shape
    ne = router_w.shape[1]
    nep = max(V7X_LANES, ne)
    rw = jnp.zeros((d, nep), F32).at[:, :ne].set(router_w)
    rw_hi, rw_lo = _split_bf16(rw)
    rb = jnp.zeros((1, nep), F32).at[0, :ne].set(router_b)
    per_b = seq // tm
    full = lambda shape: pl.BlockSpec(shape, lambda i: (0, 0))
    row = lambda w: pl.BlockSpec((tm, w), lambda i: (i, 0))
    x1, h2p, top_idx, gates = pl.pallas_call(
        functools.partial(_outproj_kernel, ne=ne),
        grid=(t // tm,),
        in_specs=[
            row(d), row(d),
            pl.BlockSpec((1, N_ADA, d), lambda i: (i // per_b, 0, 0)),
            full((d, d)), full((1, d)), full((1, d)), full((d, nep)), full((d, nep)), full((1, nep)),
        ],
        out_specs=[row(d), row(d // 2), row(nep), row(nep)],
        out_shape=[jax.ShapeDtypeStruct((t, d), F32), jax.ShapeDtypeStruct((t, d // 2), jnp.uint32),
                   jax.ShapeDtypeStruct((t, nep), jnp.int32), jax.ShapeDtypeStruct((t, nep), F32)],
        compiler_params=_cparams(("arbitrary",)),
        name="outproj",
    )(merged, x2, ada3, w_out.astype(BF16), post_mix_g.reshape(1, d), pre_ffn_g.reshape(1, d),
      rw_hi, rw_lo, rb)
    return x1, h2p, top_idx[:, :TOP_K], gates


GATHER_UNROLL = 8


def _row_copy(idx_ref, src_hbm, buf, sems, slot, r):
    return pltpu.make_async_copy(src_hbm.at[pl.ds(idx_ref[0, 0, r], 1), :],
                                 buf.at[slot, pl.ds(r, 1), :], sems.at[slot])


def _start_rows(idx_ref, src_hbm, buf, sems, slot, n, unrolled=False):
    if unrolled:
        for r in range(n):
            _row_copy(idx_ref, src_hbm, buf, sems, slot, r).start(priority=r % 2)
        return

    def start(r, carry):
        _row_copy(idx_ref, src_hbm, buf, sems, slot, r).start()
        return carry
    lax.fori_loop(0, n, start, 0, unroll=GATHER_UNROLL)


def _wait_rows(src_hbm, buf, sems, slot):
    n = buf.shape[1]
    pltpu.make_async_copy(src_hbm.at[pl.ds(0, n), :], buf.at[slot], sems.at[slot]).wait()


def _gather_kernel(tok_ref, h_hbm, o_ref, buf, sems):
    i = pl.program_id(0)
    n_steps = pl.num_programs(0) - 1

    for slot in (0, 1):
        @pl.when(jnp.logical_and(i < n_steps, i % 2 == slot))
        def _():
            _start_rows(tok_ref, h_hbm, buf, sems, slot, buf.shape[1], unrolled=True)

    @pl.when(i > 0)
    def _():
        slot = (i - 1) % 2
        _wait_rows(h_hbm, buf, sems, slot)
        o_ref[...] = buf[slot]


def _moe_gather(tok_buf, h2p, rows):
    n_rows = tok_buf.size
    assert n_rows % rows == 0
    n_steps = n_rows // rows
    t, w = h2p.shape
    return pl.pallas_call(
        _gather_kernel,
        grid=(n_steps + 1,),
        in_specs=[
            pl.BlockSpec((1, 1, rows), lambda i: (jnp.minimum(i, n_steps - 1), 0, 0), memory_space=pltpu.SMEM),
            pl.BlockSpec(memory_space=pl.ANY),
        ],
        out_specs=pl.BlockSpec((rows, w), lambda i: (jnp.maximum(i - 1, 0), 0)),
        out_shape=jax.ShapeDtypeStruct((n_rows, w), h2p.dtype),
        scratch_shapes=[pltpu.VMEM((2, rows, w), h2p.dtype), pltpu.SemaphoreType.DMA((2,))],
        compiler_params=_cparams(("arbitrary",)),
        name="moe_gather",
    )(tok_buf.reshape(n_steps, 1, rows), h2p)


def _group_weights(gf_ref, gn_ref, e_ref, i, first_step, more_passes, copy_of, gcnt):
    @pl.when(first_step)
    def _():
        gcnt[0] = 0
        copy_of(e_ref[0], False, 0).start()

    slot = gcnt[0] % 2

    @pl.when(gf_ref[i] == 1)
    def _():
        copy_of(e_ref[0], False, slot).wait()
        nxt = gn_ref[i]

        @pl.when(nxt >= 0)
        def _():
            copy_of(nxt, False, 1 - slot).start()

        @pl.when(jnp.logical_and(nxt < 0, more_passes))
        def _():
            copy_of(e_ref[0], True, 1 - slot).start()

        gcnt[0] = gcnt[0] + 1

    return slot


def _up_kernel(e_ref, nv_ref, gf_ref, gn_ref, x_ref, w_hbm, bg_ref, bl_ref, p_ref, o_ref,
               wp_ref, wbuf, wsem, gcnt):
    j, i = pl.program_id(0), pl.program_id(1)
    tn = wp_ref.shape[1]
    half = V7X_MXU // 2
    hd = x_ref.shape[1]

    def copy_of(expert, next_pass, slot):
        col = (j + 1) * tn if next_pass else j * tn
        return pltpu.make_async_copy(w_hbm.at[expert, :, pl.ds(col, tn)], wbuf.at[slot], wsem.at[slot])

    slot = _group_weights(gf_ref, gn_ref, e_ref, i, jnp.logical_and(j == 0, i == 0),
                          j < pl.num_programs(0) - 1, copy_of, gcnt)

    @pl.when(gf_ref[i] == 1)
    def _():
        for cb in range(tn // V7X_MXU):
            cols = slice(cb * V7X_MXU, (cb + 1) * V7X_MXU)
            wp_ref[:, cols] = jnp.dot(wbuf[slot, :, cols].astype(BF16), p_ref[...],
                                      preferred_element_type=F32).astype(BF16)

    @pl.when(nv_ref[i] > 0)
    def _():
        x_lo, x_hi = _unpack_bf16_pairs(x_ref[...])
        for cb in range(tn // V7X_MXU):
            cols = slice(cb * V7X_MXU, (cb + 1) * V7X_MXU)
            hb = (jnp.dot(x_lo, wp_ref[0:hd, cols], preferred_element_type=F32)
                  + jnp.dot(x_hi, wp_ref[hd:2 * hd, cols], preferred_element_type=F32))
            out_cols = slice(cb * half, (cb + 1) * half)
            x_glu = hb[:, :half] + bg_ref[0, :, out_cols]
            x_lin = hb[:, half:] + bl_ref[0, :, out_cols]
            x_glu = jnp.minimum(x_glu, SWIGLU_LIMIT)
            x_lin = jnp.clip(x_lin, -SWIGLU_LIMIT, SWIGLU_LIMIT)
            act = x_glu * _sigmoid(SWIGLU_ALPHA * x_glu) * (x_lin + 1.0)
            o_ref[:, out_cols] = act.astype(BF16)

    @pl.when(nv_ref[i] == 0)
    def _():
        o_ref[...] = jnp.zeros(o_ref.shape, o_ref.dtype)


def _moe_up(blk_e, nvalid, gfirst, gnext, xs, w1, b1, tn):
    n_rows, hd = xs.shape
    ne, d, f2 = w1.shape
    assert d == 2 * hd
    f = f2 // 2
    n_blocks = n_rows // MOE_BLOCK
    half = V7X_MXU // 2
    c = jnp.arange(V7X_MXU)
    perm = (((c % 2) * half + c // 2)[:, None] == c[None, :]).astype(BF16)
    b1g = b1[:, 0::2].reshape(ne, 1, f)
    b1l = b1[:, 1::2].reshape(ne, 1, f)
    return pl.pallas_call(
        _up_kernel,
        grid_spec=pltpu.PrefetchScalarGridSpec(
            num_scalar_prefetch=4,
            grid=(f2 // tn, n_blocks),
            in_specs=[
                pl.BlockSpec((MOE_BLOCK, hd), lambda j, i, e, nv, gf, gn: (i, 0)),
                pl.BlockSpec(memory_space=pl.ANY),
                pl.BlockSpec((1, 1, tn // 2), lambda j, i, e, nv, gf, gn: (e[i], 0, j)),
                pl.BlockSpec((1, 1, tn // 2), lambda j, i, e, nv, gf, gn: (e[i], 0, j)),
                pl.BlockSpec((V7X_MXU, V7X_MXU), lambda j, i, e, nv, gf, gn: (0, 0)),
            ],
            out_specs=pl.BlockSpec((MOE_BLOCK, tn // 2), lambda j, i, e, nv, gf, gn: (i, j)),
            scratch_shapes=[pltpu.VMEM((d, tn), BF16), pltpu.VMEM((2, d, tn), F32),
                            pltpu.SemaphoreType.DMA((2,)), pltpu.SMEM((1,), jnp.int32)],
        ),
        out_shape=jax.ShapeDtypeStruct((n_rows, f), BF16),
        compiler_params=_cparams(("arbitrary", "arbitrary")),
        name="moe_up",
    )(blk_e, nvalid, gfirst, gnext, xs, w1, b1g, b1l, perm)


def _down_kernel(e_ref, nv_ref, gf_ref, gn_ref, a_ref, w_hbm, b_ref, o_ref, wb_ref, wbuf, wsem, gcnt):
    i = pl.program_id(0)

    def copy_of(expert, next_pass, slot):
        return pltpu.make_async_copy(w_hbm.at[expert], wbuf.at[slot], wsem.at[slot])

    slot = _group_weights(gf_ref, gn_ref, e_ref, i, i == 0, False, copy_of, gcnt)

    @pl.when(gf_ref[i] == 1)
    def _():
        wb_ref[...] = wbuf[slot].astype(BF16)

    @pl.when(nv_ref[i] > 0)
    def _():
        o_ref[...] = jnp.dot(a_ref[...], wb_ref[...], preferred_element_type=F32) + b_ref[0]

    @pl.when(nv_ref[i] == 0)
    def _():
        o_ref[...] = jnp.zeros(o_ref.shape, o_ref.dtype)


def _moe_down(blk_e, nvalid, gfirst, gnext, act, w2, b2):
    n_rows, f = act.shape
    ne, _, d = w2.shape
    n_blocks = n_rows // MOE_BLOCK
    return pl.pallas_call(
        _down_kernel,
        grid_spec=pltpu.PrefetchScalarGridSpec(
            num_scalar_prefetch=4,
            grid=(n_blocks,),
            in_specs=[
                pl.BlockSpec((MOE_BLOCK, f), lambda i, e, nv, gf, gn: (i, 0)),
                pl.BlockSpec(memory_space=pl.ANY),
                pl.BlockSpec((1, 1, d), lambda i, e, nv, gf, gn: (e[i], 0, 0)),
            ],
            out_specs=pl.BlockSpec((MOE_BLOCK, d), lambda i, e, nv, gf, gn: (i, 0)),
            scratch_shapes=[pltpu.VMEM((f, d), BF16), pltpu.VMEM((2, f, d), F32),
                            pltpu.SemaphoreType.DMA((2,)), pltpu.SMEM((1,), jnp.int32)],
        ),
        out_shape=jax.ShapeDtypeStruct((n_rows, d), F32),
        compiler_params=_cparams(("arbitrary",)),
        name="moe_down",
    )(blk_e, nvalid, gfirst, gnext, act, w2, b2.reshape(ne, 1, d))


def _combine_kernel(pos_ref, y_hbm, gate_ref, x1_ref, ada_ref, g_ref, o_ref, buf, sems, *, tt):
    i = pl.program_id(0)
    n_tiles = pl.num_programs(0) - 1
    n_rows = TOP_K * tt

    def finish(slot):
        gates = gate_ref[...]
        f = buf[slot, 0:tt, :] * gates[:, 0:1]
        for k in range(1, TOP_K):
            f = f + buf[slot, k * tt:(k + 1) * tt, :] * gates[:, k:k + 1]
        o_ref[...] = x1_ref[...] + ada_ref[0, 5:6, :] * _rms(f, g_ref[...])

    @pl.when(i == 0)
    def _():
        _start_rows(pos_ref, y_hbm, buf, sems, 0, n_rows)

    for slot in (0, 1):
        @pl.when(jnp.logical_and(jnp.logical_and(i > 0, i < n_tiles), i % 2 == slot))
        def _():
            _wait_rows(y_hbm, buf, sems, 1 - slot)
            _start_rows(pos_ref, y_hbm, buf, sems, slot, n_rows, unrolled=True)
            finish(1 - slot)

    @pl.when(i == n_tiles)
    def _():
        _wait_rows(y_hbm, buf, sems, (i - 1) % 2)
        finish((i - 1) % 2)


def _moe_combine(pos, y_buf, gates, x1, ada3, post_ffn_g, seq, tt):
    t, d = x1.shape
    per_b = seq // tt
    nt = t // tt
    pos_t = pos.reshape(nt, tt, TOP_K).transpose(0, 2, 1).reshape(nt, 1, TOP_K * tt)
    prev = lambda i: jnp.maximum(i - 1, 0)
    return pl.pallas_call(
        functools.partial(_combine_kernel, tt=tt),
        grid=(nt + 1,),
        in_specs=[
            pl.BlockSpec((1, 1, TOP_K * tt), lambda i: (jnp.minimum(i, nt - 1), 0, 0), memory_space=pltpu.SMEM),
            pl.BlockSpec(memory_space=pl.ANY),
            pl.BlockSpec((tt, gates.shape[1]), lambda i: (prev(i), 0)),
            pl.BlockSpec((tt, d), lambda i: (prev(i), 0)),
            pl.BlockSpec((1, N_ADA, d), lambda i: (prev(i) // per_b, 0, 0)),
            pl.BlockSpec((1, d), lambda i: (0, 0)),
        ],
        out_specs=pl.BlockSpec((tt, d), lambda i: (prev(i), 0)),
        out_shape=jax.ShapeDtypeStruct((t, d), F32),
        scratch_shapes=[pltpu.VMEM((2, TOP_K * tt, d), F32), pltpu.SemaphoreType.DMA((2,))],
        compiler_params=_cparams(("arbitrary",)),
        name="moe_combine",
    )(pos_t, y_buf, gates, x1, ada3, post_ffn_g.reshape(1, d))


def _route(top_idx, n_experts):
    n_tok = top_idx.shape[0]
    n_asg = n_tok * TOP_K
    i32 = jnp.int32
    flat_e = top_idx.reshape(n_asg)
    order = jnp.argsort(flat_e).astype(i32)
    counts = jnp.sum(flat_e[:, None] == jnp.arange(n_experts, dtype=i32)[None, :], axis=0, dtype=i32)
    starts = jnp.cumsum(counts) - counts
    padded = (counts + MOE_BLOCK - 1) // MOE_BLOCK * MOE_BLOCK
    pends = jnp.cumsum(padded)
    pstarts = pends - padded
    n_blocks = -(-n_asg // MOE_BLOCK) + n_experts
    blk_start = jnp.arange(n_blocks, dtype=i32) * MOE_BLOCK
    blk_e = jnp.minimum(jnp.sum(pends[None, :] <= blk_start[:, None], axis=1, dtype=i32), n_experts - 1)
    off = blk_start - pstarts[blk_e]
    nvalid = jnp.clip(counts[blk_e] - off, 0, MOE_BLOCK)
    j = off[:, None] + jnp.arange(MOE_BLOCK, dtype=i32)[None, :]
    valid = j < counts[blk_e][:, None]
    src = jnp.clip(starts[blk_e][:, None] + j, 0, n_asg - 1)
    asg = order[src]
    row_id = blk_start[:, None] + jnp.arange(MOE_BLOCK, dtype=i32)[None, :]
    tok_buf = jnp.where(valid, asg // TOP_K, row_id % n_tok)
    rank = jnp.argsort(order).astype(i32)
    pos = (pstarts - starts)[flat_e] + rank
    blk = jnp.arange(n_blocks, dtype=i32)
    prev_e = jnp.concatenate([jnp.full((1,), -1, i32), blk_e[:-1]])
    gfirst = jnp.logical_and(nvalid > 0, blk_e != prev_e).astype(i32)
    cand = jnp.where(gfirst == 1, blk, n_blocks)
    later = jnp.concatenate([lax.cummin(cand[::-1])[::-1][1:], jnp.full((1,), n_blocks, i32)])
    gnext = jnp.where(later < n_blocks, blk_e[jnp.minimum(later, n_blocks - 1)], -1)
    return tok_buf, pos, blk_e, nvalid, gfirst, gnext


def _pick(n, pref):
    return pref if n % pref == 0 else n


def kernel(x, c, ada_w, ada_b, pre_mix_g, post_mix_g, pre_ffn_g, post_ffn_g, w_in, gate_b, dw_w, dw_b, cln_g, cln_b, cp_w, cp_b, s5_a_re, s5_a_im, s5_log_dt, s5_b_re, s5_b_im, s5_c_re, s5_c_im, s5_d, glu_wa, glu_wb, w_out, router_w, router_b, w1, b1, w2, b2):
    bsz, seq, d = x.shape
    t = bsz * seq
    depth = ada_w.shape[0]
    conv_ch = dw_w.shape[-1]
    ng, ns, nh = s5_b_re.shape[1:]
    s5_w = ng * nh
    ne = router_w.shape[-1]
    col_s5 = 2 * conv_ch
    col_gate = col_s5 + s5_w
    assert conv_ch == s5_w and V7X_MXU % nh == 0 and ng % (V7X_MXU // nh) == 0
    tm = _pick(seq, 512)
    tm_out = _pick(seq, 256)
    tn_in = _pick(w_in.shape[-1], conv_ch)

    x2 = x.reshape(t, d)
    for l in range(depth):
        ada3 = _ada(c, ada_w[l], ada_b[l]).reshape(bsz, N_ADA, d)
        proj = _inproj(x2, ada3, pre_mix_g[l], w_in[l], seq, _pick(seq, 1024), tn_in)
        vc = _conv_branch(proj, dw_w[l], dw_b[l], cln_g[l], cln_b[l], bsz, seq, tm)

        s5_params = _s5_params(s5_a_re[l], s5_a_im[l], s5_log_dt[l], s5_b_re[l], s5_b_im[l],
                               s5_c_re[l], s5_c_im[l], s5_d[l])
        ys = _s5_branch(proj, col_s5, s5_params, bsz, seq)

        merged = _merge(vc, ys, proj, cp_w[l], cp_b[l], glu_wa[l], glu_wb[l], gate_b[l], tm, col_gate)
        x1, h2p, top_idx, gates = _outproj(merged, x2, ada3, w_out[l], post_mix_g[l], pre_ffn_g[l],
                                           router_w[l], router_b[l], seq, tm_out)

        tok_buf, pos, blk_e, nvalid, gfirst, gnext = _route(top_idx, ne)
        nb = tok_buf.shape[0]
        xs = _moe_gather(tok_buf, h2p, MOE_BLOCK * next(k for k in (4, 2, 1) if nb % k == 0))
        act = _moe_up(blk_e, nvalid, gfirst, gnext, xs, w1[l], b1[l], _pick(w1.shape[-1], 2048))
        y_buf = _moe_down(blk_e, nvalid, gfirst, gnext, act, w2[l], b2[l])
        x2 = _moe_combine(pos, y_buf, gates, x1, ada3, post_ffn_g[l], seq, _pick(seq, 256))
    return x2.reshape(bsz, seq, d)
```

```python
import functools
import math

import jax
import jax.numpy as jnp
from jax import lax
from jax.experimental import pallas as pl
from jax.experimental.pallas import tpu as pltpu

EPS = 1e-6
N_ADA = 6
TOP_K = 4
MOE_BLOCK = 256
SWIGLU_ALPHA = 1.702
SWIGLU_LIMIT = 7.0
S5_CHUNK = 16
V7X_LANES = 128
V7X_SUBLANES = 8
V7X_MXU = 256
VMEM_LIMIT = 56 * 1024 * 1024

F32 = jnp.float32
BF16 = jnp.bfloat16


def _cparams(sem):
    return pltpu.CompilerParams(dimension_semantics=sem, vmem_limit_bytes=VMEM_LIMIT)


def _sigmoid(v):
    return 1.0 / (1.0 + jnp.exp(-v))


def _rms(v, g):
    return v * lax.rsqrt(jnp.mean(v * v, axis=-1, keepdims=True) + EPS) * g


def _ada_kernel(c_ref, w_ref, b_ref, o_ref):
    c = c_ref[...]
    s = (c * _sigmoid(c)).astype(BF16)
    o_ref[...] = jnp.dot(s, w_ref[...].astype(BF16), preferred_element_type=F32) + b_ref[...]


def _ada(c, ada_w, ada_b):
    bsz, d = c.shape
    n = ada_w.shape[1]
    tn = 1024 if n % 1024 == 0 else n
    return pl.pallas_call(
        _ada_kernel,
        grid=(n // tn,),
        in_specs=[
            pl.BlockSpec((bsz, d), lambda j: (0, 0)),
            pl.BlockSpec((d, tn), lambda j: (0, j)),
            pl.BlockSpec((1, tn), lambda j: (0, j)),
        ],
        out_specs=pl.BlockSpec((bsz, tn), lambda j: (0, j)),
        out_shape=jax.ShapeDtypeStruct((bsz, n), F32),
        compiler_params=_cparams(("arbitrary",)),
        name="ada",
    )(c, ada_w, ada_b.reshape(1, n))


def _inproj_kernel(x_ref, ada_ref, g_ref, w_ref, o_ref, h_ref):
    @pl.when(pl.program_id(1) == 0)
    def _():
        y = _rms(x_ref[...], g_ref[...])
        h = y * (1.0 + ada_ref[0, 1:2, :]) + ada_ref[0, 0:1, :]
        h_ref[...] = h.astype(BF16)

    o_ref[...] = jnp.dot(h_ref[...], w_ref[...].astype(BF16), preferred_element_type=F32).astype(BF16)


def _inproj(x2, ada3, g, w_bf, seq, tm, tn):
    t, d = x2.shape
    n = w_bf.shape[1]
    per_b = seq // tm
    return pl.pallas_call(
        _inproj_kernel,
        grid=(t // tm, n // tn),
        in_specs=[
            pl.BlockSpec((tm, d), lambda i, j: (i, 0)),
            pl.BlockSpec((1, N_ADA, d), lambda i, j: (i // per_b, 0, 0)),
            pl.BlockSpec((1, d), lambda i, j: (0, 0)),
            pl.BlockSpec((d, tn), lambda i, j: (0, j)),
        ],
        out_specs=pl.BlockSpec((tm, tn), lambda i, j: (i, j)),
        out_shape=jax.ShapeDtypeStruct((t, n), BF16),
        scratch_shapes=[pltpu.VMEM((tm, d), BF16)],
        compiler_params=_cparams(("arbitrary", "arbitrary")),
        name="inproj",
    )(x2, ada3, g.reshape(1, d), w_bf)


CONV_HALO = 32
CONV_ROWS = 64


def _conv_kernel(pv_ref, pg_ref, w_ref, b_ref, g_ref, beta_ref, o_ref, vext, vsh, cbuf, *, taps, tl):
    t = pl.program_id(1)
    ch = cbuf.shape[1]

    @pl.when(t == 0)
    def _():
        vext[0:CONV_HALO, :] = jnp.zeros((CONV_HALO, ch), F32)

    @pl.when(t > 0)
    def _():
        vext[0:CONV_HALO, :] = vext[tl:tl + CONV_HALO, :]

    pv = pv_ref[...].astype(F32)
    pg = pg_ref[...].astype(F32)
    vext[CONV_HALO:CONV_HALO + tl, :] = pv * _sigmoid(pg)

    n_sh = vsh.shape[1]
    for p in range(1, V7X_SUBLANES):
        vsh[p - 1] = vext[p:p + n_sh, :]

    off = CONV_HALO - (taps - 1)
    for cb in range(ch // V7X_LANES):
        lanes = slice(cb * V7X_LANES, (cb + 1) * V7X_LANES)

        for rc in range(tl // CONV_ROWS):
            r0 = rc * CONV_ROWS
            acc = jnp.zeros((CONV_ROWS, V7X_LANES), F32)
            for k in range(taps):
                p = (off + k) % V7X_SUBLANES
                base = r0 + off + k - p
                win = vext[base:base + CONV_ROWS, lanes] if p == 0 else vsh[p - 1, base:base + CONV_ROWS, lanes]
                acc = acc + w_ref[k:k + 1, lanes] * win
            cbuf[r0:r0 + CONV_ROWS, lanes] = acc + b_ref[:, lanes]

    v = cbuf[...]
    mu = jnp.mean(v, axis=-1, keepdims=True)
    xc = v - mu
    var = jnp.mean(xc * xc, axis=-1, keepdims=True)
    y = xc * lax.rsqrt(var + EPS) * g_ref[...] + beta_ref[...]
    o_ref[...] = (y * _sigmoid(y)).astype(BF16)


def _conv_branch(proj, dw_w, dw_b, cln_g, cln_b, bsz, seq, tl):
    taps, ch = dw_w.shape
    assert taps - 1 <= CONV_HALO and tl % CONV_ROWS == 0 and ch % V7X_LANES == 0
    per_b = seq // tl
    vec = lambda a: a.reshape(1, ch)
    return pl.pallas_call(
        functools.partial(_conv_kernel, taps=taps, tl=tl),
        grid=(bsz, per_b),
        in_specs=[
            pl.BlockSpec((tl, ch), lambda b, t: (b * per_b + t, 0)),
            pl.BlockSpec((tl, ch), lambda b, t: (b * per_b + t, 1)),
            pl.BlockSpec((taps, ch), lambda b, t: (0, 0)),
            pl.BlockSpec((1, ch), lambda b, t: (0, 0)),
            pl.BlockSpec((1, ch), lambda b, t: (0, 0)),
            pl.BlockSpec((1, ch), lambda b, t: (0, 0)),
        ],
        out_specs=pl.BlockSpec((tl, ch), lambda b, t: (b * per_b + t, 0)),
        out_shape=jax.ShapeDtypeStruct((bsz * seq, ch), BF16),
        scratch_shapes=[pltpu.VMEM((CONV_HALO + tl, ch), F32),
                        pltpu.VMEM((V7X_SUBLANES - 1, CONV_HALO + tl - V7X_SUBLANES, ch), F32),
                        pltpu.VMEM((tl, ch), F32)],
        compiler_params=_cparams(("arbitrary", "arbitrary")),
        name="conv",
    )(proj, proj, dw_w, vec(dw_b), vec(cln_g), vec(cln_b))


S5_SUPER = 16


def _s5_params(a_re, a_im, log_dt, b_re, b_im, c_re, c_im, d_skip):
    ng, ns = a_re.shape
    nh = b_re.shape[-1]
    gt = V7X_MXU // nh
    nj = ng // gt
    dt = jnp.exp(log_dt)[:, None]
    mag = jnp.exp(a_re * dt)
    abar_re = mag * jnp.cos(a_im * dt)
    abar_im = mag * jnp.sin(a_im * dt)
    den = a_re * a_re + a_im * a_im
    num_re = abar_re - 1.0
    coef_re = (num_re * a_re + abar_im * a_im) / den
    coef_im = (abar_im * a_re - num_re * a_im) / den
    bb_re = coef_re[:, :, None] * b_re - coef_im[:, :, None] * b_im
    bb_im = coef_re[:, :, None] * b_im + coef_im[:, :, None] * b_re
    p = jnp.asarray([1.0, S5_CHUNK, S5_CHUNK * S5_SUPER] + [S5_CHUNK * r for r in range(S5_SUPER)], F32)
    pmag = jnp.exp(p[:, None, None] * (a_re * dt)[None])
    pw_re = pmag * jnp.cos(p[:, None, None] * (a_im * dt)[None])
    pw_im = pmag * jnp.sin(p[:, None, None] * (a_im * dt)[None])
    tile = lambda a: a.reshape(-1, nj, gt * ns).transpose(1, 0, 2)
    apow = jnp.concatenate([tile(pw_re), tile(pw_im)], axis=1)
    eye = jnp.eye(gt, dtype=F32)

    def bdiag_in(bb):
        return jnp.einsum('jgnh,gk->jghkn', bb.reshape(nj, gt, ns, nh), eye).reshape(nj, gt * nh, gt * ns)

    def bdiag_out(cc):
        return jnp.einsum('jghn,gk->jgnkh', cc.reshape(nj, gt, nh, ns), eye).reshape(nj, gt * ns, gt * nh)

    bdb = jnp.stack([bdiag_in(bb_re), bdiag_in(bb_im)], axis=1)
    bdc = jnp.stack([bdiag_out(c_re), bdiag_out(-c_im)], axis=1)
    return bdb.astype(BF16), bdc.astype(BF16), apow, d_skip.reshape(nj, 1, gt * nh)


def _cmul_add(ar, ai, xr, xi, vr, vi):
    return ar * xr - ai * xi + vr, ar * xi + ai * xr + vi


def _s5_kernel(u_ref, bdb_ref, bdc_ref, ap_ref, d_ref, p1_ref, p1t_ref, p2_ref, p2t_ref, o_ref,
               u2n_ref, u2_ref, hr_ref, hi_ref, zr_ref, zi_ref, gr_ref, gi_ref, y2_ref, y2n_ref,
               *, n_super):
    tc, nr, nm = S5_CHUNK, S5_SUPER, n_super
    tile = tc * nr
    np_ = 3 + nr
    apow = lambda k: (ap_ref[0, k:k + 1, :], ap_ref[0, np_ + k:np_ + k + 1, :])
    ar, ai = apow(0)

    for m in range(nm):
        pu = jnp.dot(p1_ref[...], u_ref[m * tile:(m + 1) * tile, :], preferred_element_type=F32).astype(BF16)
        for s in range(tc):
            u2n_ref[s, m * nr:(m + 1) * nr, :] = pu[s * nr:(s + 1) * nr, :]
    for s in range(tc):
        u2_ref[s] = jnp.dot(p2_ref[...], u2n_ref[s], preferred_element_type=F32).astype(BF16)

    def drive(s):
        u = u2_ref[s]
        return (jnp.dot(u, bdb_ref[0, 0], preferred_element_type=F32),
                jnp.dot(u, bdb_ref[0, 1], preferred_element_type=F32))

    hr_ref[...] = jnp.zeros(hr_ref.shape, F32)
    hi_ref[...] = jnp.zeros(hi_ref.shape, F32)

    def pass1(s, carry):
        vr, vi = drive(s)
        hr, hi = _cmul_add(ar, ai, hr_ref[...], hi_ref[...], vr, vi)
        hr_ref[...] = hr
        hi_ref[...] = hi
        return carry

    lax.fori_loop(0, tc, pass1, 0, unroll=4)

    a16r, a16i = apow(1)
    qr = jnp.zeros((nm, hr_ref.shape[1]), F32)
    qi = qr
    for r in range(nr):
        rows = slice(r * nm, (r + 1) * nm)
        zr_ref[rows, :] = qr
        zi_ref[rows, :] = qi
        qr, qi = _cmul_add(a16r, a16i, qr, qi, hr_ref[rows, :], hi_ref[rows, :])
    a256r, a256i = apow(2)
    gr = jnp.zeros((1, hr_ref.shape[1]), F32)
    gi = gr
    for m in range(nm):
        gr_ref[m:m + 1, :] = gr
        gi_ref[m:m + 1, :] = gi
        gr, gi = _cmul_add(a256r, a256i, gr, gi, qr[m:m + 1, :], qi[m:m + 1, :])
    gpr = gr_ref[...]
    gpi = gi_ref[...]
    for r in range(nr):
        rows = slice(r * nm, (r + 1) * nm)
        pr, pi = apow(3 + r)
        zr, zi = _cmul_add(pr, pi, gpr, gpi, zr_ref[rows, :], zi_ref[rows, :])
        zr_ref[rows, :] = zr
        zi_ref[rows, :] = zi

    def pass2(t, carry):
        vr, vi = drive(t)
        sr, si = _cmul_add(ar, ai, zr_ref[...], zi_ref[...], vr, vi)
        zr_ref[...] = sr
        zi_ref[...] = si
        y = (jnp.dot(sr.astype(BF16), bdc_ref[0, 0], preferred_element_type=F32)
             + jnp.dot(si.astype(BF16), bdc_ref[0, 1], preferred_element_type=F32))
        y = y + d_ref[0] * u2_ref[t].astype(F32)
        y = 0.5 * y * (1.0 + jnp.tanh(math.sqrt(2.0 / math.pi) * (y + 0.044715 * (y * y * y))))
        y2_ref[t] = y.astype(BF16)
        return carry

    lax.fori_loop(0, tc, pass2, 0, unroll=4)

    for t in range(tc):
        yn = jnp.dot(p2t_ref[...], y2_ref[t], preferred_element_type=F32).astype(BF16)
        for m in range(nm):
            y2n_ref[m * tile + t * nr:m * tile + (t + 1) * nr, :] = yn[m * nr:(m + 1) * nr, :]
    for m in range(nm):
        rows = slice(m * tile, (m + 1) * tile)
        o_ref[rows, :] = jnp.dot(p1t_ref[...], y2n_ref[rows, :], preferred_element_type=F32).astype(BF16)


def _s5_branch(proj, col0, params, bsz, seq):
    bdb, bdc, apow, dskip = params
    nj, _, w, sl = bdb.shape
    tile = S5_CHUNK * S5_SUPER
    assert seq % tile == 0 and col0 % w == 0
    nm = seq // tile
    assert nm % V7X_SUBLANES == 0, "row slabs of the chunk recurrence must be whole sublane tiles"
    rows = S5_SUPER * nm
    i1 = jnp.arange(tile)
    p1 = (i1[:, None] == ((i1 % S5_CHUNK) * S5_SUPER + i1 // S5_CHUNK)[None, :]).astype(BF16)
    i2 = jnp.arange(rows)
    p2 = (i2[:, None] == ((i2 % S5_SUPER) * nm + i2 // S5_SUPER)[None, :]).astype(BF16)
    const = lambda a: pl.BlockSpec(a.shape, lambda b, j: (0,) * a.ndim)
    per_j = lambda a: pl.BlockSpec((1,) + a.shape[1:], lambda b, j: (j,) + (0,) * (a.ndim - 1))
    return pl.pallas_call(
        functools.partial(_s5_kernel, n_super=nm),
        grid=(bsz, nj),
        in_specs=[
            pl.BlockSpec((seq, w), lambda b, j: (b, col0 // w + j)),
            per_j(bdb), per_j(bdc), per_j(apow), per_j(dskip),
            const(p1), const(p1), const(p2), const(p2),
        ],
        out_specs=pl.BlockSpec((seq, w), lambda b, j: (b, j)),
        out_shape=jax.ShapeDtypeStruct((bsz * seq, nj * w), BF16),
        scratch_shapes=[
            pltpu.VMEM((S5_CHUNK, rows, w), BF16), pltpu.VMEM((S5_CHUNK, rows, w), BF16),
            pltpu.VMEM((rows, sl), F32), pltpu.VMEM((rows, sl), F32),
            pltpu.VMEM((rows, sl), F32), pltpu.VMEM((rows, sl), F32),
            pltpu.VMEM((nm, sl), F32), pltpu.VMEM((nm, sl), F32),
            pltpu.VMEM((S5_CHUNK, rows, w), BF16), pltpu.VMEM((seq, w), BF16),
        ],
        compiler_params=_cparams(("arbitrary", "arbitrary")),
        name="s5",
    )(proj, bdb, bdc, apow, dskip, p1, p1.T, p2, p2.T)


def _merge_kernel(vc_ref, ys_ref, l1_ref, l2_ref, cpw_ref, cpb_ref, wa_ref, wb_ref, gb1_ref, gb2_ref, o_ref):
    y_conv = jnp.dot(vc_ref[...], cpw_ref[...], preferred_element_type=F32) + cpb_ref[...]
    ys = ys_ref[...]
    a = jnp.dot(ys, wa_ref[...], preferred_element_type=F32)
    b = jnp.dot(ys, wb_ref[...], preferred_element_type=F32)
    y_s5 = a * _sigmoid(b)
    g1 = _sigmoid(l1_ref[...].astype(F32) + gb1_ref[...])
    g2 = _sigmoid(l2_ref[...].astype(F32) + gb2_ref[...])
    o_ref[...] = (g1 * y_conv + g2 * y_s5).astype(BF16)


def _merge(vc, ys, proj, cp_w, cp_b, wa, wb, gate_b, tm, col0):
    t, ch = vc.shape
    d = cp_w.shape[1]
    tn = ch
    assert col0 % tn == 0 and d % tn == 0
    cb, nj = col0 // tn, d // tn
    wcol = lambda rows: pl.BlockSpec((rows, tn), lambda j, i: (0, j))
    return pl.pallas_call(
        _merge_kernel,
        grid=(nj, t // tm),
        in_specs=[
            pl.BlockSpec((tm, ch), lambda j, i: (i, 0)),
            pl.BlockSpec((tm, ch), lambda j, i: (i, 0)),
            pl.BlockSpec((tm, tn), lambda j, i: (i, cb + j)),
            pl.BlockSpec((tm, tn), lambda j, i: (i, cb + nj + j)),
            wcol(ch), wcol(1), wcol(ch), wcol(ch), wcol(1), wcol(1),
        ],
        out_specs=pl.BlockSpec((tm, tn), lambda j, i: (i, j)),
        out_shape=jax.ShapeDtypeStruct((t, d), BF16),
        compiler_params=_cparams(("arbitrary", "arbitrary")),
        name="merge",
    )(vc, ys, proj, proj, cp_w.astype(BF16), cp_b.reshape(1, d), wa.astype(BF16), wb.astype(BF16),
      gate_b[:d].reshape(1, d), gate_b[d:].reshape(1, d))


def _split_bf16(v):
    hi = v.astype(BF16)
    lo = (v - hi.astype(F32)).astype(BF16)
    return hi, lo


def _pack_bf16_pairs(v):
    half = v.shape[-1] // 2
    b = lax.bitcast_convert_type(v, jnp.uint32)
    return (b[:, half:] & jnp.uint32(0xFFFF0000)) | (b[:, :half] >> 16)


def _unpack_bf16_pairs(p):
    lo = lax.bitcast_convert_type(p << 16, F32).astype(BF16)
    hi = lax.bitcast_convert_type(p & jnp.uint32(0xFFFF0000), F32).astype(BF16)
    return lo, hi


def _outproj_kernel(m_ref, x_ref, ada_ref, wo_ref, g1_ref, g2_ref, rwh_ref, rwl_ref, rb_ref,
                    x1_ref, h2p_ref, idx_ref, gate_ref, *, ne):
    m = jnp.dot(m_ref[...], wo_ref[...], preferred_element_type=F32)
    x1 = x_ref[...] + ada_ref[0, 2:3, :] * _rms(m, g1_ref[...])
    x1_ref[...] = x1
    h2 = _rms(x1, g2_ref[...]) * (1.0 + ada_ref[0, 4:5, :]) + ada_ref[0, 3:4, :]
    hh = h2.astype(jnp.bfloat16)
    hf = hh.astype(F32)
    h2p_ref[...] = _pack_bf16_pairs(hf)
    hl = (h2 - hf).astype(BF16)
    lg = jnp.dot(hh, rwh_ref[...], preferred_element_type=F32)
    lg = lg + jnp.dot(hl, rwh_ref[...], preferred_element_type=F32)
    lg = lg + jnp.dot(hh, rwl_ref[...], preferred_element_type=F32)
    lg = lg + rb_ref[...]
    lane = lax.broadcasted_iota(jnp.int32, lg.shape, 1)
    work = jnp.where(lane < ne, lg, -jnp.inf)
    vals, idxs = [], []
    for _ in range(TOP_K):
        mx = jnp.max(work, axis=-1, keepdims=True)
        ix = jnp.min(jnp.where(work == mx, lane, lg.shape[1]), axis=-1, keepdims=True)
        vals.append(mx)
        idxs.append(ix)
        work = jnp.where(lane == ix, -jnp.inf, work)
    ex = [jnp.exp(v - vals[0]) for v in vals]
    den = ex[0]
    for e in ex[1:]:
        den = den + e
    idx_out = jnp.zeros(lg.shape, jnp.int32)
    gate_out = jnp.zeros(lg.shape, F32)
    for k in range(TOP_K):
        idx_out = jnp.where(lane == k, idxs[k], idx_out)
        gate_out = jnp.where(lane == k, ex[k] / den, gate_out)
    idx_ref[...] = idx_out
    gate_ref[...] = gate_out


def _outproj(merged, x2, ada3, w_out, post_mix_g, pre_ffn_g, router_w, router_b, seq, tm):
    t, d = x2.shape
    ne = router_w.shape[1]
    nep = max(V7X_LANES, ne)
    rw = jnp.zeros((d, nep), F32).at[:, :ne].set(router_w)
    rw_hi, rw_lo = _split_bf16(rw)
    rb = jnp.zeros((1, nep), F32).at[0, :ne].set(router_b)
    per_b = seq // tm
    full = lambda shape: pl.BlockSpec(shape, lambda i: (0, 0))
    row = lambda w: pl.BlockSpec((tm, w), lambda i: (i, 0))
    x1, h2p, top_idx, gates = pl.pallas_call(
        functools.partial(_outproj_kernel, ne=ne),
        grid=(t // tm,),
        in_specs=[
            row(d), row(d),
            pl.BlockSpec((1, N_ADA, d), lambda i: (i // per_b, 0, 0)),
            full((d, d)), full((1, d)), full((1, d)), full((d, nep)), full((d, nep)), full((1, nep)),
        ],
        out_specs=[row(d), row(d // 2), row(nep), row(nep)],
        out_shape=[jax.ShapeDtypeStruct((t, d), F32), jax.ShapeDtypeStruct((t, d // 2), jnp.uint32),
                   jax.ShapeDtypeStruct((t, nep), jnp.int32), jax.ShapeDtypeStruct((t, nep), F32)],
        compiler_params=_cparams(("arbitrary",)),
        name="outproj",
    )(merged, x2, ada3, w_out.astype(BF16), post_mix_g.reshape(1, d), pre_ffn_g.reshape(1, d),
      rw_hi, rw_lo, rb)
    return x1, h2p, top_idx[:, :TOP_K], gates


GATHER_UNROLL = 8


def _row_copy(idx_ref, src_hbm, buf, sems, slot, r):
    return pltpu.make_async_copy(src_hbm.at[pl.ds(idx_ref[0, 0, r], 1), :],
                                 buf.at[slot, pl.ds(r, 1), :], sems.at[slot])


def _start_rows(idx_ref, src_hbm, buf, sems, slot, n, unrolled=False):
    if unrolled:
        for r in range(n):
            _row_copy(idx_ref, src_hbm, buf, sems, slot, r).start(priority=r % 2)
        return

    def start(r, carry):
        _row_copy(idx_ref, src_hbm, buf, sems, slot, r).start()
        return carry
    lax.fori_loop(0, n, start, 0, unroll=GATHER_UNROLL)


def _wait_rows(src_hbm, buf, sems, slot):
    n = buf.shape[1]
    pltpu.make_async_copy(src_hbm.at[pl.ds(0, n), :], buf.at[slot], sems.at[slot]).wait()


def _gather_kernel(tok_ref, h_hbm, o_ref, buf, sems):
    i = pl.program_id(0)
    n_steps = pl.num_programs(0) - 1

    for slot in (0, 1):
        @pl.when(jnp.logical_and(i < n_steps, i % 2 == slot))
        def _():
            _start_rows(tok_ref, h_hbm, buf, sems, slot, buf.shape[1], unrolled=True)

    @pl.when(i > 0)
    def _():
        slot = (i - 1) % 2
        _wait_rows(h_hbm, buf, sems, slot)
        o_ref[...] = buf[slot]


def _moe_gather(tok_buf, h2p, rows):
    n_rows = tok_buf.size
    assert n_rows % rows == 0
    n_steps = n_rows // rows
    t, w = h2p.shape
    return pl.pallas_call(
        _gather_kernel,
        grid=(n_steps + 1,),
        in_specs=[
            pl.BlockSpec((1, 1, rows), lambda i: (jnp.minimum(i, n_steps - 1), 0, 0), memory_space=pltpu.SMEM),
            pl.BlockSpec(memory_space=pl.ANY),
        ],
        out_specs=pl.BlockSpec((rows, w), lambda i: (jnp.maximum(i - 1, 0), 0)),
        out_shape=jax.ShapeDtypeStruct((n_rows, w), h2p.dtype),
        scratch_shapes=[pltpu.VMEM((2, rows, w), h2p.dtype), pltpu.SemaphoreType.DMA((2,))],
        compiler_params=_cparams(("arbitrary",)),
        name="moe_gather",
    )(tok_buf.reshape(n_steps, 1, rows), h2p)


def _group_weights(gf_ref, gn_ref, e_ref, i, first_step, more_passes, copy_of, gcnt):
    @pl.when(first_step)
    def _():
        gcnt[0] = 0
        copy_of(e_ref[0], False, 0).start()

    slot = gcnt[0] % 2

    @pl.when(gf_ref[i] == 1)
    def _():
        copy_of(e_ref[0], False, slot).wait()
        nxt = gn_ref[i]

        @pl.when(nxt >= 0)
        def _():
            copy_of(nxt, False, 1 - slot).start()

        @pl.when(jnp.logical_and(nxt < 0, more_passes))
        def _():
            copy_of(e_ref[0], True, 1 - slot).start()

        gcnt[0] = gcnt[0] + 1

    return slot


def _up_kernel(e_ref, nv_ref, gf_ref, gn_ref, x_ref, w_hbm, bg_ref, bl_ref, p_ref, o_ref,
               wp_ref, wbuf, wsem, gcnt):
    j, i = pl.program_id(0), pl.program_id(1)
    tn = wp_ref.shape[1]
    half = V7X_MXU // 2
    hd = x_ref.shape[1]

    def copy_of(expert, next_pass, slot):
        col = (j + 1) * tn if next_pass else j * tn
        return pltpu.make_async_copy(w_hbm.at[expert, :, pl.ds(col, tn)], wbuf.at[slot], wsem.at[slot])

    slot = _group_weights(gf_ref, gn_ref, e_ref, i, jnp.logical_and(j == 0, i == 0),
                          j < pl.num_programs(0) - 1, copy_of, gcnt)

    @pl.when(gf_ref[i] == 1)
    def _():
        for cb in range(tn // V7X_MXU):
            cols = slice(cb * V7X_MXU, (cb + 1) * V7X_MXU)
            wp_ref[:, cols] = jnp.dot(wbuf[slot, :, cols].astype(BF16), p_ref[...],
                                      preferred_element_type=F32).astype(BF16)

    @pl.when(nv_ref[i] > 0)
    def _():
        x_lo, x_hi = _unpack_bf16_pairs(x_ref[...])
        for cb in range(tn // V7X_MXU):
            cols = slice(cb * V7X_MXU, (cb + 1) * V7X_MXU)
            hb = (jnp.dot(x_lo, wp_ref[0:hd, cols], preferred_element_type=F32)
                  + jnp.dot(x_hi, wp_ref[hd:2 * hd, cols], preferred_element_type=F32))
            out_cols = slice(cb * half, (cb + 1) * half)
            x_glu = hb[:, :half] + bg_ref[0, :, out_cols]
            x_lin = hb[:, half:] + bl_ref[0, :, out_cols]
            x_glu = jnp.minimum(x_glu, SWIGLU_LIMIT)
            x_lin = jnp.clip(x_lin, -SWIGLU_LIMIT, SWIGLU_LIMIT)
            act = x_glu * _sigmoid(SWIGLU_ALPHA * x_glu) * (x_lin + 1.0)
            o_ref[:, out_cols] = act.astype(BF16)

    @pl.when(nv_ref[i] == 0)
    def _():
        o_ref[...] = jnp.zeros(o_ref.shape, o_ref.dtype)


def _moe_up(blk_e, nvalid, gfirst, gnext, xs, w1, b1, tn):
    n_rows, hd = xs.shape
    ne, d, f2 = w1.shape
    assert d == 2 * hd
    f = f2 // 2
    n_blocks = n_rows // MOE_BLOCK
    half = V7X_MXU // 2
    c = jnp.arange(V7X_MXU)
    perm = (((c % 2) * half + c // 2)[:, None] == c[None, :]).astype(BF16)
    b1g = b1[:, 0::2].reshape(ne, 1, f)
    b1l = b1[:, 1::2].reshape(ne, 1, f)
    return pl.pallas_call(
        _up_kernel,
        grid_spec=pltpu.PrefetchScalarGridSpec(
            num_scalar_prefetch=4,
            grid=(f2 // tn, n_blocks),
            in_specs=[
                pl.BlockSpec((MOE_BLOCK, hd), lambda j, i, e, nv, gf, gn: (i, 0)),
                pl.BlockSpec(memory_space=pl.ANY),
                pl.BlockSpec((1, 1, tn // 2), lambda j, i, e, nv, gf, gn: (e[i], 0, j)),
                pl.BlockSpec((1, 1, tn // 2), lambda j, i, e, nv, gf, gn: (e[i], 0, j)),
                pl.BlockSpec((V7X_MXU, V7X_MXU), lambda j, i, e, nv, gf, gn: (0, 0)),
            ],
            out_specs=pl.BlockSpec((MOE_BLOCK, tn // 2), lambda j, i, e, nv, gf, gn: (i, j)),
            scratch_shapes=[pltpu.VMEM((d, tn), BF16), pltpu.VMEM((2, d, tn), F32),
                            pltpu.SemaphoreType.DMA((2,)), pltpu.SMEM((1,), jnp.int32)],
        ),
        out_shape=jax.ShapeDtypeStruct((n_rows, f), BF16),
        compiler_params=_cparams(("arbitrary", "arbitrary")),
        name="moe_up",
    )(blk_e, nvalid, gfirst, gnext, xs, w1, b1g, b1l, perm)


def _down_kernel(e_ref, nv_ref, gf_ref, gn_ref, a_ref, w_hbm, b_ref, o_ref, wb_ref, wbuf, wsem, gcnt):
    i = pl.program_id(0)

    def copy_of(expert, next_pass, slot):
        return pltpu.make_async_copy(w_hbm.at[expert], wbuf.at[slot], wsem.at[slot])

    slot = _group_weights(gf_ref, gn_ref, e_ref, i, i == 0, False, copy_of, gcnt)

    @pl.when(gf_ref[i] == 1)
    def _():
        wb_ref[...] = wbuf[slot].astype(BF16)

    @pl.when(nv_ref[i] > 0)
    def _():
        o_ref[...] = jnp.dot(a_ref[...], wb_ref[...], preferred_element_type=F32) + b_ref[0]

    @pl.when(nv_ref[i] == 0)
    def _():
        o_ref[...] = jnp.zeros(o_ref.shape, o_ref.dtype)


def _moe_down(blk_e, nvalid, gfirst, gnext, act, w2, b2):
    n_rows, f = act.shape
    ne, _, d = w2.shape
    n_blocks = n_rows // MOE_BLOCK
    return pl.pallas_call(
        _down_kernel,
        grid_spec=pltpu.PrefetchScalarGridSpec(
            num_scalar_prefetch=4,
            grid=(n_blocks,),
            in_specs=[
                pl.BlockSpec((MOE_BLOCK, f), lambda i, e, nv, gf, gn: (i, 0)),
                pl.BlockSpec(memory_space=pl.ANY),
                pl.BlockSpec((1, 1, d), lambda i, e, nv, gf, gn: (e[i], 0, 0)),
            ],
            out_specs=pl.BlockSpec((MOE_BLOCK, d), lambda i, e, nv, gf, gn: (i, 0)),
            scratch_shapes=[pltpu.VMEM((f, d), BF16), pltpu.VMEM((2, f, d), F32),
                            pltpu.SemaphoreType.DMA((2,)), pltpu.SMEM((1,), jnp.int32)],
        ),
        out_shape=jax.ShapeDtypeStruct((n_rows, d), F32),
        compiler_params=_cparams(("arbitrary",)),
        name="moe_down",
    )(blk_e, nvalid, gfirst, gnext, act, w2, b2.reshape(ne, 1, d))


def _combine_kernel(pos_ref, y_hbm, gate_ref, x1_ref, ada_ref, g_ref, o_ref, buf, sems, *, tt):
    i = pl.program_id(0)
    n_tiles = pl.num_programs(0) - 1
    n_rows = TOP_K * tt

    def finish(slot):
        gates = gate_ref[...]
        f = buf[slot, 0:tt, :] * gates[:, 0:1]
        for k in range(1, TOP_K):
            f = f + buf[slot, k * tt:(k + 1) * tt, :] * gates[:, k:k + 1]
        o_ref[...] = x1_ref[...] + ada_ref[0, 5:6, :] * _rms(f, g_ref[...])

    @pl.when(i == 0)
    def _():
        _start_rows(pos_ref, y_hbm, buf, sems, 0, n_rows)

    for slot in (0, 1):
        @pl.when(jnp.logical_and(jnp.logical_and(i > 0, i < n_tiles), i % 2 == slot))
        def _():
            _wait_rows(y_hbm, buf, sems, 1 - slot)
            _start_rows(pos_ref, y_hbm, buf, sems, slot, n_rows, unrolled=True)
            finish(1 - slot)

    @pl.when(i == n_tiles)
    def _():
        _wait_rows(y_hbm, buf, sems, (i - 1) % 2)
        finish((i - 1) % 2)


def _moe_combine(pos, y_buf, gates, x1, ada3, post_ffn_g, seq, tt):
    t, d = x1.shape
    per_b = seq // tt
    nt = t // tt
    pos_t = pos.reshape(nt, tt, TOP_K).transpose(0, 2, 1).reshape(nt, 1, TOP_K * tt)
    prev = lambda i: jnp.maximum(i - 1, 0)
    return pl.pallas_call(
        functools.partial(_combine_kernel, tt=tt),
        grid=(nt + 1,),
        in_specs=[
            pl.BlockSpec((1, 1, TOP_K * tt), lambda i: (jnp.minimum(i, nt - 1), 0, 0), memory_space=pltpu.SMEM),
            pl.BlockSpec(memory_space=pl.ANY),
            pl.BlockSpec((tt, gates.shape[1]), lambda i: (prev(i), 0)),
            pl.BlockSpec((tt, d), lambda i: (prev(i), 0)),
            pl.BlockSpec((1, N_ADA, d), lambda i: (prev(i) // per_b, 0, 0)),
            pl.BlockSpec((1, d), lambda i: (0, 0)),
        ],
        out_specs=pl.BlockSpec((tt, d), lambda i: (prev(i), 0)),
        out_shape=jax.ShapeDtypeStruct((t, d), F32),
        scratch_shapes=[pltpu.VMEM((2, TOP_K * tt, d), F32), pltpu.SemaphoreType.DMA((2,))],
        compiler_params=_cparams(("arbitrary",)),
        name="moe_combine",
    )(pos_t, y_buf, gates, x1, ada3, post_ffn_g.reshape(1, d))


def _route(top_idx, n_experts):
    n_tok = top_idx.shape[0]
    n_asg = n_tok * TOP_K
    i32 = jnp.int32
    flat_e = top_idx.reshape(n_asg)
    order = jnp.argsort(flat_e).astype(i32)
    counts = jnp.sum(flat_e[:, None] == jnp.arange(n_experts, dtype=i32)[None, :], axis=0, dtype=i32)
    starts = jnp.cumsum(counts) - counts
    padded = (counts + MOE_BLOCK - 1) // MOE_BLOCK * MOE_BLOCK
    pends = jnp.cumsum(padded)
    pstarts = pends - padded
    n_blocks = -(-n_asg // MOE_BLOCK) + n_experts
    blk_start = jnp.arange(n_blocks, dtype=i32) * MOE_BLOCK
    blk_e = jnp.minimum(jnp.sum(pends[None, :] <= blk_start[:, None], axis=1, dtype=i32), n_experts - 1)
    off = blk_start - pstarts[blk_e]
    nvalid = jnp.clip(counts[blk_e] - off, 0, MOE_BLOCK)
    j = off[:, None] + jnp.arange(MOE_BLOCK, dtype=i32)[None, :]
    valid = j < counts[blk_e][:, None]
    src = jnp.clip(starts[blk_e][:, None] + j, 0, n_asg - 1)
    asg = order[src]
    row_id = blk_start[:, None] + jnp.arange(MOE_BLOCK, dtype=i32)[None, :]
    tok_buf = jnp.where(valid, asg // TOP_K, row_id % n_tok)
    rank = jnp.argsort(order).astype(i32)
    pos = (pstarts - starts)[flat_e] + rank
    blk = jnp.arange(n_blocks, dtype=i32)
    prev_e = jnp.concatenate([jnp.full((1,), -1, i32), blk_e[:-1]])
    gfirst = jnp.logical_and(nvalid > 0, blk_e != prev_e).astype(i32)
    cand = jnp.where(gfirst == 1, blk, n_blocks)
    later = jnp.concatenate([lax.cummin(cand[::-1])[::-1][1:], jnp.full((1,), n_blocks, i32)])
    gnext = jnp.where(later < n_blocks, blk_e[jnp.minimum(later, n_blocks - 1)], -1)
    return tok_buf, pos, blk_e, nvalid, gfirst, gnext


def _pick(n, pref):
    return pref if n % pref == 0 else n


def kernel(x, c, ada_w, ada_b, pre_mix_g, post_mix_g, pre_ffn_g, post_ffn_g, w_in, gate_b, dw_w, dw_b, cln_g, cln_b, cp_w, cp_b, s5_a_re, s5_a_im, s5_log_dt, s5_b_re, s5_b_im, s5_c_re, s5_c_im, s5_d, glu_wa, glu_wb, w_out, router_w, router_b, w1, b1, w2, b2):
    bsz, seq, d = x.shape
    t = bsz * seq
    depth = ada_w.shape[0]
    conv_ch = dw_w.shape[-1]
    ng, ns, nh = s5_b_re.shape[1:]
    s5_w = ng * nh
    ne = router_w.shape[-1]
    col_s5 = 2 * conv_ch
    col_gate = col_s5 + s5_w
    assert conv_ch == s5_w and V7X_MXU % nh == 0 and ng % (V7X_MXU // nh) == 0
    tm = _pick(seq, 512)
    tm_out = _pick(seq, 512)
    tn_in = _pick(w_in.shape[-1], conv_ch)

    x2 = x.reshape(t, d)
    for l in range(depth):
        ada3 = _ada(c, ada_w[l], ada_b[l]).reshape(bsz, N_ADA, d)
        proj = _inproj(x2, ada3, pre_mix_g[l], w_in[l], seq, _pick(seq, 1024), tn_in)
        vc = _conv_branch(proj, dw_w[l], dw_b[l], cln_g[l], cln_b[l], bsz, seq, tm)

        s5_params = _s5_params(s5_a_re[l], s5_a_im[l], s5_log_dt[l], s5_b_re[l], s5_b_im[l],
                               s5_c_re[l], s5_c_im[l], s5_d[l])
        ys = _s5_branch(proj, col_s5, s5_params, bsz, seq)

        merged = _merge(vc, ys, proj, cp_w[l], cp_b[l], glu_wa[l], glu_wb[l], gate_b[l], tm, col_gate)
        x1, h2p, top_idx, gates = _outproj(merged, x2, ada3, w_out[l], post_mix_g[l], pre_ffn_g[l],
                                           router_w[l], router_b[l], seq, tm_out)

        tok_buf, pos, blk_e, nvalid, gfirst, gnext = _route(top_idx, ne)
        nb = tok_buf.shape[0]
        xs = _moe_gather(tok_buf, h2p, MOE_BLOCK * next(k for k in (4, 2, 1) if nb % k == 0))
        act = _moe_up(blk_e, nvalid, gfirst, gnext, xs, w1[l], b1[l], _pick(w1.shape[-1], 2048))
        y_buf = _moe_down(blk_e, nvalid, gfirst, gnext, act, w2[l], b2[l])
        x2 = _moe_combine(pos, y_buf, gates, x1, ada3, post_ffn_g[l], seq, _pick(seq, 256))
    return x2.reshape(bsz, seq, d)
```

```python
import functools
import math

import jax
import jax.numpy as jnp
from jax import lax
from jax.experimental import pallas as pl
from jax.experimental.pallas import tpu as pltpu

EPS = 1e-6
N_ADA = 6
TOP_K = 4
MOE_BLOCK = 256
SWIGLU_ALPHA = 1.702
SWIGLU_LIMIT = 7.0
S5_CHUNK = 16
V7X_LANES = 128
V7X_SUBLANES = 8
V7X_MXU = 256
VMEM_LIMIT = 56 * 1024 * 1024

F32 = jnp.float32
BF16 = jnp.bfloat16


def _cparams(sem):
    return pltpu.CompilerParams(dimension_semantics=sem, vmem_limit_bytes=VMEM_LIMIT)


def _sigmoid(v):
    return 1.0 / (1.0 + jnp.exp(-v))


def _rms(v, g):
    return v * lax.rsqrt(jnp.mean(v * v, axis=-1, keepdims=True) + EPS) * g


def _ada_kernel(c_ref, w_ref, b_ref, o_ref):
    c = c_ref[...]
    s = (c * _sigmoid(c)).astype(BF16)
    o_ref[...] = jnp.dot(s, w_ref[...].astype(BF16), preferred_element_type=F32) + b_ref[...]


def _ada(c, ada_w, ada_b):
    bsz, d = c.shape
    n = ada_w.shape[1]
    tn = 1024 if n % 1024 == 0 else n
    return pl.pallas_call(
        _ada_kernel,
        grid=(n // tn,),
        in_specs=[
            pl.BlockSpec((bsz, d), lambda j: (0, 0)),
            pl.BlockSpec((d, tn), lambda j: (0, j)),
            pl.BlockSpec((1, tn), lambda j: (0, j)),
        ],
        out_specs=pl.BlockSpec((bsz, tn), lambda j: (0, j)),
        out_shape=jax.ShapeDtypeStruct((bsz, n), F32),
        compiler_params=_cparams(("arbitrary",)),
        name="ada",
    )(c, ada_w, ada_b.reshape(1, n))


def _inproj_kernel(x_ref, ada_ref, g_ref, w_ref, o_ref, h_ref):
    @pl.when(pl.program_id(1) == 0)
    def _():
        y = _rms(x_ref[...], g_ref[...])
        h = y * (1.0 + ada_ref[0, 1:2, :]) + ada_ref[0, 0:1, :]
        h_ref[...] = h.astype(BF16)

    o_ref[...] = jnp.dot(h_ref[...], w_ref[...].astype(BF16), preferred_element_type=F32).astype(BF16)


def _inproj(x2, ada3, g, w_bf, seq, tm, tn):
    t, d = x2.shape
    n = w_bf.shape[1]
    per_b = seq // tm
    return pl.pallas_call(
        _inproj_kernel,
        grid=(t // tm, n // tn),
        in_specs=[
            pl.BlockSpec((tm, d), lambda i, j: (i, 0)),
            pl.BlockSpec((1, N_ADA, d), lambda i, j: (i // per_b, 0, 0)),
            pl.BlockSpec((1, d), lambda i, j: (0, 0)),
            pl.BlockSpec((d, tn), lambda i, j: (0, j)),
        ],
        out_specs=pl.BlockSpec((tm, tn), lambda i, j: (i, j)),
        out_shape=jax.ShapeDtypeStruct((t, n), BF16),
        scratch_shapes=[pltpu.VMEM((tm, d), BF16)],
        compiler_params=_cparams(("arbitrary", "arbitrary")),
        name="inproj",
    )(x2, ada3, g.reshape(1, d), w_bf)


CONV_HALO = 32
CONV_ROWS = 64


def _conv_kernel(pv_ref, pg_ref, w_ref, b_ref, g_ref, beta_ref, o_ref, vext, vsh, cbuf, *, taps, tl):
    t = pl.program_id(1)
    ch = cbuf.shape[1]

    @pl.when(t == 0)
    def _():
        vext[0:CONV_HALO, :] = jnp.zeros((CONV_HALO, ch), F32)

    @pl.when(t > 0)
    def _():
        vext[0:CONV_HALO, :] = vext[tl:tl + CONV_HALO, :]

    pv = pv_ref[...].astype(F32)
    pg = pg_ref[...].astype(F32)
    vext[CONV_HALO:CONV_HALO + tl, :] = pv * _sigmoid(pg)

    n_sh = vsh.shape[1]
    for p in range(1, V7X_SUBLANES):
        vsh[p - 1] = vext[p:p + n_sh, :]

    off = CONV_HALO - (taps - 1)
    for cb in range(ch // V7X_LANES):
        lanes = slice(cb * V7X_LANES, (cb + 1) * V7X_LANES)

        for rc in range(tl // CONV_ROWS):
            r0 = rc * CONV_ROWS
            acc = jnp.zeros((CONV_ROWS, V7X_LANES), F32)
            for k in range(taps):
                p = (off + k) % V7X_SUBLANES
                base = r0 + off + k - p
                win = vext[base:base + CONV_ROWS, lanes] if p == 0 else vsh[p - 1, base:base + CONV_ROWS, lanes]
                acc = acc + w_ref[k:k + 1, lanes] * win
            cbuf[r0:r0 + CONV_ROWS, lanes] = acc + b_ref[:, lanes]

    v = cbuf[...]
    mu = jnp.mean(v, axis=-1, keepdims=True)
    xc = v - mu
    var = jnp.mean(xc * xc, axis=-1, keepdims=True)
    y = xc * lax.rsqrt(var + EPS) * g_ref[...] + beta_ref[...]
    o_ref[...] = (y * _sigmoid(y)).astype(BF16)


def _conv_branch(proj, dw_w, dw_b, cln_g, cln_b, bsz, seq, tl):
    taps, ch = dw_w.shape
    assert taps - 1 <= CONV_HALO and tl % CONV_ROWS == 0 and ch % V7X_LANES == 0
    per_b = seq // tl
    vec = lambda a: a.reshape(1, ch)
    return pl.pallas_call(
        functools.partial(_conv_kernel, taps=taps, tl=tl),
        grid=(bsz, per_b),
        in_specs=[
            pl.BlockSpec((tl, ch), lambda b, t: (b * per_b + t, 0)),
            pl.BlockSpec((tl, ch), lambda b, t: (b * per_b + t, 1)),
            pl.BlockSpec((taps, ch), lambda b, t: (0, 0)),
            pl.BlockSpec((1, ch), lambda b, t: (0, 0)),
            pl.BlockSpec((1, ch), lambda b, t: (0, 0)),
            pl.BlockSpec((1, ch), lambda b, t: (0, 0)),
        ],
        out_specs=pl.BlockSpec((tl, ch), lambda b, t: (b * per_b + t, 0)),
        out_shape=jax.ShapeDtypeStruct((bsz * seq, ch), BF16),
        scratch_shapes=[pltpu.VMEM((CONV_HALO + tl, ch), F32),
                        pltpu.VMEM((V7X_SUBLANES - 1, CONV_HALO + tl - V7X_SUBLANES, ch), F32),
                        pltpu.VMEM((tl, ch), F32)],
        compiler_params=_cparams(("arbitrary", "arbitrary")),
        name="conv",
    )(proj, proj, dw_w, vec(dw_b), vec(cln_g), vec(cln_b))


S5_SUPER = 16


def _s5_params(a_re, a_im, log_dt, b_re, b_im, c_re, c_im, d_skip):
    ng, ns = a_re.shape
    nh = b_re.shape[-1]
    gt = V7X_MXU // nh
    nj = ng // gt
    dt = jnp.exp(log_dt)[:, None]
    mag = jnp.exp(a_re * dt)
    abar_re = mag * jnp.cos(a_im * dt)
    abar_im = mag * jnp.sin(a_im * dt)
    den = a_re * a_re + a_im * a_im
    num_re = abar_re - 1.0
    coef_re = (num_re * a_re + abar_im * a_im) / den
    coef_im = (abar_im * a_re - num_re * a_im) / den
    bb_re = coef_re[:, :, None] * b_re - coef_im[:, :, None] * b_im
    bb_im = coef_re[:, :, None] * b_im + coef_im[:, :, None] * b_re
    p = jnp.asarray([1.0, S5_CHUNK, S5_CHUNK * S5_SUPER] + [S5_CHUNK * r for r in range(S5_SUPER)], F32)
    pmag = jnp.exp(p[:, None, None] * (a_re * dt)[None])
    pw_re = pmag * jnp.cos(p[:, None, None] * (a_im * dt)[None])
    pw_im = pmag * jnp.sin(p[:, None, None] * (a_im * dt)[None])
    tile = lambda a: a.reshape(-1, nj, gt * ns).transpose(1, 0, 2)
    apow = jnp.concatenate([tile(pw_re), tile(pw_im)], axis=1)
    eye = jnp.eye(gt, dtype=F32)

    def bdiag_in(bb):
        return jnp.einsum('jgnh,gk->jghkn', bb.reshape(nj, gt, ns, nh), eye).reshape(nj, gt * nh, gt * ns)

    def bdiag_out(cc):
        return jnp.einsum('jghn,gk->jgnkh', cc.reshape(nj, gt, nh, ns), eye).reshape(nj, gt * ns, gt * nh)

    bdb = jnp.stack([bdiag_in(bb_re), bdiag_in(bb_im)], axis=1)
    bdc = jnp.stack([bdiag_out(c_re), bdiag_out(-c_im)], axis=1)
    return bdb.astype(BF16), bdc.astype(BF16), apow, d_skip.reshape(nj, 1, gt * nh)


def _cmul_add(ar, ai, xr, xi, vr, vi):
    return ar * xr - ai * xi + vr, ar * xi + ai * xr + vi


def _s5_kernel(u_ref, bdb_ref, bdc_ref, ap_ref, d_ref, p1_ref, p1t_ref, p2_ref, p2t_ref, o_ref,
               u2n_ref, u2_ref, hr_ref, hi_ref, zr_ref, zi_ref, gr_ref, gi_ref, y2_ref, y2n_ref,
               *, n_super):
    tc, nr, nm = S5_CHUNK, S5_SUPER, n_super
    tile = tc * nr
    np_ = 3 + nr
    apow = lambda k: (ap_ref[0, k:k + 1, :], ap_ref[0, np_ + k:np_ + k + 1, :])
    ar, ai = apow(0)

    for m in range(nm):
        pu = jnp.dot(p1_ref[...], u_ref[m * tile:(m + 1) * tile, :], preferred_element_type=F32).astype(BF16)
        for s in range(tc):
            u2n_ref[s, m * nr:(m + 1) * nr, :] = pu[s * nr:(s + 1) * nr, :]
    for s in range(tc):
        u2_ref[s] = jnp.dot(p2_ref[...], u2n_ref[s], preferred_element_type=F32).astype(BF16)

    def drive(s):
        u = u2_ref[s]
        return (jnp.dot(u, bdb_ref[0, 0], preferred_element_type=F32),
                jnp.dot(u, bdb_ref[0, 1], preferred_element_type=F32))

    hr_ref[...] = jnp.zeros(hr_ref.shape, F32)
    hi_ref[...] = jnp.zeros(hi_ref.shape, F32)

    def pass1(s, carry):
        vr, vi = drive(s)
        hr, hi = _cmul_add(ar, ai, hr_ref[...], hi_ref[...], vr, vi)
        hr_ref[...] = hr
        hi_ref[...] = hi
        return carry

    lax.fori_loop(0, tc, pass1, 0, unroll=4)

    a16r, a16i = apow(1)
    qr = jnp.zeros((nm, hr_ref.shape[1]), F32)
    qi = qr
    for r in range(nr):
        rows = slice(r * nm, (r + 1) * nm)
        zr_ref[rows, :] = qr
        zi_ref[rows, :] = qi
        qr, qi = _cmul_add(a16r, a16i, qr, qi, hr_ref[rows, :], hi_ref[rows, :])
    a256r, a256i = apow(2)
    gr = jnp.zeros((1, hr_ref.shape[1]), F32)
    gi = gr
    for m in range(nm):
        gr_ref[m:m + 1, :] = gr
        gi_ref[m:m + 1, :] = gi
        gr, gi = _cmul_add(a256r, a256i, gr, gi, qr[m:m + 1, :], qi[m:m + 1, :])
    gpr = gr_ref[...]
    gpi = gi_ref[...]
    for r in range(nr):
        rows = slice(r * nm, (r + 1) * nm)
        pr, pi = apow(3 + r)
        zr, zi = _cmul_add(pr, pi, gpr, gpi, zr_ref[rows, :], zi_ref[rows, :])
        zr_ref[rows, :] = zr
        zi_ref[rows, :] = zi

    def pass2(t, carry):
        vr, vi = drive(t)
        sr, si = _cmul_add(ar, ai, zr_ref[...], zi_ref[...], vr, vi)
        zr_ref[...] = sr
        zi_ref[...] = si
        y = (jnp.dot(sr.astype(BF16), bdc_ref[0, 0], preferred_element_type=F32)
             + jnp.dot(si.astype(BF16), bdc_ref[0, 1], preferred_element_type=F32))
        y = y + d_ref[0] * u2_ref[t].astype(F32)
        y = 0.5 * y * (1.0 + jnp.tanh(math.sqrt(2.0 / math.pi) * (y + 0.044715 * (y * y * y))))
        y2_ref[t] = y.astype(BF16)
        return carry

    lax.fori_loop(0, tc, pass2, 0, unroll=4)

    for t in range(tc):
        yn = jnp.dot(p2t_ref[...], y2_ref[t], preferred_element_type=F32).astype(BF16)
        for m in range(nm):
            y2n_ref[m * tile + t * nr:m * tile + (t + 1) * nr, :] = yn[m * nr:(m + 1) * nr, :]
    for m in range(nm):
        rows = slice(m * tile, (m + 1) * tile)
        o_ref[rows, :] = jnp.dot(p1t_ref[...], y2n_ref[rows, :], preferred_element_type=F32).astype(BF16)


def _s5_branch(proj, col0, params, bsz, seq):
    bdb, bdc, apow, dskip = params
    nj, _, w, sl = bdb.shape
    tile = S5_CHUNK * S5_SUPER
    assert seq % tile == 0 and col0 % w == 0
    nm = seq // tile
    assert nm % V7X_SUBLANES == 0, "row slabs of the chunk recurrence must be whole sublane tiles"
    rows = S5_SUPER * nm
    i1 = jnp.arange(tile)
    p1 = (i1[:, None] == ((i1 % S5_CHUNK) * S5_SUPER + i1 // S5_CHUNK)[None, :]).astype(BF16)
    i2 = jnp.arange(rows)
    p2 = (i2[:, None] == ((i2 % S5_SUPER) * nm + i2 // S5_SUPER)[None, :]).astype(BF16)
    const = lambda a: pl.BlockSpec(a.shape, lambda b, j: (0,) * a.ndim)
    per_j = lambda a: pl.BlockSpec((1,) + a.shape[1:], lambda b, j: (j,) + (0,) * (a.ndim - 1))
    return pl.pallas_call(
        functools.partial(_s5_kernel, n_super=nm),
        grid=(bsz, nj),
        in_specs=[
            pl.BlockSpec((seq, w), lambda b, j: (b, col0 // w + j)),
            per_j(bdb), per_j(bdc), per_j(apow), per_j(dskip),
            const(p1), const(p1), const(p2), const(p2),
        ],
        out_specs=pl.BlockSpec((seq, w), lambda b, j: (b, j)),
        out_shape=jax.ShapeDtypeStruct((bsz * seq, nj * w), BF16),
        scratch_shapes=[
            pltpu.VMEM((S5_CHUNK, rows, w), BF16), pltpu.VMEM((S5_CHUNK, rows, w), BF16),
            pltpu.VMEM((rows, sl), F32), pltpu.VMEM((rows, sl), F32),
            pltpu.VMEM((rows, sl), F32), pltpu.VMEM((rows, sl), F32),
            pltpu.VMEM((nm, sl), F32), pltpu.VMEM((nm, sl), F32),
            pltpu.VMEM((S5_CHUNK, rows, w), BF16), pltpu.VMEM((seq, w), BF16),
        ],
        compiler_params=_cparams(("arbitrary", "arbitrary")),
        name="s5",
    )(proj, bdb, bdc, apow, dskip, p1, p1.T, p2, p2.T)


def _merge_kernel(vc_ref, ys_ref, l1_ref, l2_ref, cpw_ref, cpb_ref, wa_ref, wb_ref, gb1_ref, gb2_ref, o_ref):
    y_conv = jnp.dot(vc_ref[...], cpw_ref[...], preferred_element_type=F32) + cpb_ref[...]
    ys = ys_ref[...]
    a = jnp.dot(ys, wa_ref[...], preferred_element_type=F32)
    b = jnp.dot(ys, wb_ref[...], preferred_element_type=F32)
    y_s5 = a * _sigmoid(b)
    g1 = _sigmoid(l1_ref[...].astype(F32) + gb1_ref[...])
    g2 = _sigmoid(l2_ref[...].astype(F32) + gb2_ref[...])
    o_ref[...] = (g1 * y_conv + g2 * y_s5).astype(BF16)


def _merge(vc, ys, proj, cp_w, cp_b, wa, wb, gate_b, tm, col0):
    t, ch = vc.shape
    d = cp_w.shape[1]
    tn = ch
    assert col0 % tn == 0 and d % tn == 0
    cb, nj = col0 // tn, d // tn
    wcol = lambda rows: pl.BlockSpec((rows, tn), lambda j, i: (0, j))
    return pl.pallas_call(
        _merge_kernel,
        grid=(nj, t // tm),
        in_specs=[
            pl.BlockSpec((tm, ch), lambda j, i: (i, 0)),
            pl.BlockSpec((tm, ch), lambda j, i: (i, 0)),
            pl.BlockSpec((tm, tn), lambda j, i: (i, cb + j)),
            pl.BlockSpec((tm, tn), lambda j, i: (i, cb + nj + j)),
            wcol(ch), wcol(1), wcol(ch), wcol(ch), wcol(1), wcol(1),
        ],
        out_specs=pl.BlockSpec((tm, tn), lambda j, i: (i, j)),
        out_shape=jax.ShapeDtypeStruct((t, d), BF16),
        compiler_params=_cparams(("arbitrary", "arbitrary")),
        name="merge",
    )(vc, ys, proj, proj, cp_w.astype(BF16), cp_b.reshape(1, d), wa.astype(BF16), wb.astype(BF16),
      gate_b[:d].reshape(1, d), gate_b[d:].reshape(1, d))


def _split_bf16(v):
    hi = v.astype(BF16)
    lo = (v - hi.astype(F32)).astype(BF16)
    return hi, lo


def _pack_bf16_pairs(v):
    half = v.shape[-1] // 2
    b = lax.bitcast_convert_type(v, jnp.uint32)
    return (b[:, half:] & jnp.uint32(0xFFFF0000)) | (b[:, :half] >> 16)


def _unpack_bf16_pairs(p):
    lo = lax.bitcast_convert_type(p << 16, F32).astype(BF16)
    hi = lax.bitcast_convert_type(p & jnp.uint32(0xFFFF0000), F32).astype(BF16)
    return lo, hi


def _outproj_kernel(m_ref, x_ref, ada_ref, wo_ref, g1_ref, g2_ref, rwh_ref, rwl_ref, rb_ref,
                    x1_ref, h2p_ref, idx_ref, gate_ref, *, ne):
    m = jnp.dot(m_ref[...], wo_ref[...], preferred_element_type=F32)
    x1 = x_ref[...] + ada_ref[0, 2:3, :] * _rms(m, g1_ref[...])
    x1_ref[...] = x1
    h2 = _rms(x1, g2_ref[...]) * (1.0 + ada_ref[0, 4:5, :]) + ada_ref[0, 3:4, :]
    hh = h2.astype(jnp.bfloat16)
    hf = hh.astype(F32)
    h2p_ref[...] = _pack_bf16_pairs(hf)
    hl = (h2 - hf).astype(BF16)
    lg = jnp.dot(hh, rwh_ref[...], preferred_element_type=F32)
    lg = lg + jnp.dot(hl, rwh_ref[...], preferred_element_type=F32)
    lg = lg + jnp.dot(hh, rwl_ref[...], preferred_element_type=F32)
    lg = lg + rb_ref[...]
    lane = lax.broadcasted_iota(jnp.int32, lg.shape, 1)
    work = jnp.where(lane < ne, lg, -jnp.inf)
    vals, idxs = [], []
    for _ in range(TOP_K):
        mx = jnp.max(work, axis=-1, keepdims=True)
        ix = jnp.min(jnp.where(work == mx, lane, lg.shape[1]), axis=-1, keepdims=True)
        vals.append(mx)
        idxs.append(ix)
        work = jnp.where(lane == ix, -jnp.inf, work)
    ex = [jnp.exp(v - vals[0]) for v in vals]
    den = ex[0]
    for e in ex[1:]:
        den = den + e
    idx_out = jnp.zeros(lg.shape, jnp.int32)
    gate_out = jnp.zeros(lg.shape, F32)
    for k in range(TOP_K):
        idx_out = jnp.where(lane == k, idxs[k], idx_out)
        gate_out = jnp.where(lane == k, ex[k] / den, gate_out)
    idx_ref[...] = idx_out
    gate_ref[...] = gate_out


def _outproj(merged, x2, ada3, w_out, post_mix_g, pre_ffn_g, router_w, router_b, seq, tm):
    t, d = x2.shape
    ne = router_w.shape[1]
    nep = max(V7X_LANES, ne)
    rw = jnp.zeros((d, nep), F32).at[:, :ne].set(router_w)
    rw_hi, rw_lo = _split_bf16(rw)
    rb = jnp.zeros((1, nep), F32).at[0, :ne].set(router_b)
    per_b = seq // tm
    full = lambda shape: pl.BlockSpec(shape, lambda i: (0, 0))
    row = lambda w: pl.BlockSpec((tm, w), lambda i: (i, 0))
    x1, h2p, top_idx, gates = pl.pallas_call(
        functools.partial(_outproj_kernel, ne=ne),
        grid=(t // tm,),
        in_specs=[
            row(d), row(d),
            pl.BlockSpec((1, N_ADA, d), lambda i: (i // per_b, 0, 0)),
            full((d, d)), full((1, d)), full((1, d)), full((d, nep)), full((d, nep)), full((1, nep)),
        ],
        out_specs=[row(d), row(d // 2), row(nep), row(nep)],
        out_shape=[jax.ShapeDtypeStruct((t, d), F32), jax.ShapeDtypeStruct((t, d // 2), jnp.uint32),
                   jax.ShapeDtypeStruct((t, nep), jnp.int32), jax.ShapeDtypeStruct((t, nep), F32)],
        compiler_params=_cparams(("arbitrary",)),
        name="outproj",
    )(merged, x2, ada3, w_out.astype(BF16), post_mix_g.reshape(1, d), pre_ffn_g.reshape(1, d),
      rw_hi, rw_lo, rb)
    return x1, h2p, top_idx[:, :TOP_K], gates


GATHER_UNROLL = 8


def _row_copy(idx_ref, src_hbm, buf, sems, slot, r):
    return pltpu.make_async_copy(src_hbm.at[pl.ds(idx_ref[0, 0, r], 1), :],
                                 buf.at[slot, pl.ds(r, 1), :], sems.at[slot])


def _start_rows(idx_ref, src_hbm, buf, sems, slot, n, unrolled=False):
    if unrolled:
        for r in range(n):
            _row_copy(idx_ref, src_hbm, buf, sems, slot, r).start()
        return

    def start(r, carry):
        _row_copy(idx_ref, src_hbm, buf, sems, slot, r).start()
        return carry
    lax.fori_loop(0, n, start, 0, unroll=GATHER_UNROLL)


def _wait_rows(src_hbm, buf, sems, slot):
    n = buf.shape[1]
    pltpu.make_async_copy(src_hbm.at[pl.ds(0, n), :], buf.at[slot], sems.at[slot]).wait()


def _gather_kernel(tok_ref, h_hbm, o_ref, buf, sems):
    i = pl.program_id(0)
    n_steps = pl.num_programs(0) - 1

    for slot in (0, 1):
        @pl.when(jnp.logical_and(i < n_steps, i % 2 == slot))
        def _():
            _start_rows(tok_ref, h_hbm, buf, sems, slot, buf.shape[1], unrolled=True)

    @pl.when(i > 0)
    def _():
        slot = (i - 1) % 2
        _wait_rows(h_hbm, buf, sems, slot)
        o_ref[...] = buf[slot]


def _moe_gather(tok_buf, h2p, rows):
    n_rows = tok_buf.size
    assert n_rows % rows == 0
    n_steps = n_rows // rows
    t, w = h2p.shape
    return pl.pallas_call(
        _gather_kernel,
        grid=(n_steps + 1,),
        in_specs=[
            pl.BlockSpec((1, 1, rows), lambda i: (jnp.minimum(i, n_steps - 1), 0, 0), memory_space=pltpu.SMEM),
            pl.BlockSpec(memory_space=pl.ANY),
        ],
        out_specs=pl.BlockSpec((rows, w), lambda i: (jnp.maximum(i - 1, 0), 0)),
        out_shape=jax.ShapeDtypeStruct((n_rows, w), h2p.dtype),
        scratch_shapes=[pltpu.VMEM((2, rows, w), h2p.dtype), pltpu.SemaphoreType.DMA((2,))],
        compiler_params=_cparams(("arbitrary",)),
        name="moe_gather",
    )(tok_buf.reshape(n_steps, 1, rows), h2p)


def _group_weights(gf_ref, gn_ref, e_ref, i, first_step, more_passes, copy_of, gcnt):
    @pl.when(first_step)
    def _():
        gcnt[0] = 0
        copy_of(e_ref[0], False, 0).start()

    slot = gcnt[0] % 2

    @pl.when(gf_ref[i] == 1)
    def _():
        copy_of(e_ref[0], False, slot).wait()
        nxt = gn_ref[i]

        @pl.when(nxt >= 0)
        def _():
            copy_of(nxt, False, 1 - slot).start()

        @pl.when(jnp.logical_and(nxt < 0, more_passes))
        def _():
            copy_of(e_ref[0], True, 1 - slot).start()

        gcnt[0] = gcnt[0] + 1

    return slot


def _up_kernel(e_ref, nv_ref, gf_ref, gn_ref, x_ref, w_hbm, bg_ref, bl_ref, p_ref, o_ref,
               wp_ref, wbuf, wsem, gcnt):
    j, i = pl.program_id(0), pl.program_id(1)
    tn = wp_ref.shape[1]
    half = V7X_MXU // 2
    hd = x_ref.shape[1]

    def copy_of(expert, next_pass, slot):
        col = (j + 1) * tn if next_pass else j * tn
        return pltpu.make_async_copy(w_hbm.at[expert, :, pl.ds(col, tn)], wbuf.at[slot], wsem.at[slot])

    slot = _group_weights(gf_ref, gn_ref, e_ref, i, jnp.logical_and(j == 0, i == 0),
                          j < pl.num_programs(0) - 1, copy_of, gcnt)

    @pl.when(gf_ref[i] == 1)
    def _():
        for cb in range(tn // V7X_MXU):
            cols = slice(cb * V7X_MXU, (cb + 1) * V7X_MXU)
            wp_ref[:, cols] = jnp.dot(wbuf[slot, :, cols].astype(BF16), p_ref[...],
                                      preferred_element_type=F32).astype(BF16)

    @pl.when(nv_ref[i] > 0)
    def _():
        x_lo, x_hi = _unpack_bf16_pairs(x_ref[...])
        for cb in range(tn // V7X_MXU):
            cols = slice(cb * V7X_MXU, (cb + 1) * V7X_MXU)
            hb = (jnp.dot(x_lo, wp_ref[0:hd, cols], preferred_element_type=F32)
                  + jnp.dot(x_hi, wp_ref[hd:2 * hd, cols], preferred_element_type=F32))
            out_cols = slice(cb * half, (cb + 1) * half)
            x_glu = hb[:, :half] + bg_ref[0, :, out_cols]
            x_lin = hb[:, half:] + bl_ref[0, :, out_cols]
            x_glu = jnp.minimum(x_glu, SWIGLU_LIMIT)
            x_lin = jnp.clip(x_lin, -SWIGLU_LIMIT, SWIGLU_LIMIT)
            act = x_glu * _sigmoid(SWIGLU_ALPHA * x_glu) * (x_lin + 1.0)
            o_ref[:, out_cols] = act.astype(BF16)

    @pl.when(nv_ref[i] == 0)
    def _():
        o_ref[...] = jnp.zeros(o_ref.shape, o_ref.dtype)


def _moe_up(blk_e, nvalid, gfirst, gnext, xs, w1, b1, tn):
    n_rows, hd = xs.shape
    ne, d, f2 = w1.shape
    assert d == 2 * hd
    f = f2 // 2
    n_blocks = n_rows // MOE_BLOCK
    half = V7X_MXU // 2
    c = jnp.arange(V7X_MXU)
    perm = (((c % 2) * half + c // 2)[:, None] == c[None, :]).astype(BF16)
    b1g = b1[:, 0::2].reshape(ne, 1, f)
    b1l = b1[:, 1::2].reshape(ne, 1, f)
    return pl.pallas_call(
        _up_kernel,
        grid_spec=pltpu.PrefetchScalarGridSpec(
            num_scalar_prefetch=4,
            grid=(f2 // tn, n_blocks),
            in_specs=[
                pl.BlockSpec((MOE_BLOCK, hd), lambda j, i, e, nv, gf, gn: (i, 0)),
                pl.BlockSpec(memory_space=pl.ANY),
                pl.BlockSpec((1, 1, tn // 2), lambda j, i, e, nv, gf, gn: (e[i], 0, j)),
                pl.BlockSpec((1, 1, tn // 2), lambda j, i, e, nv, gf, gn: (e[i], 0, j)),
                pl.BlockSpec((V7X_MXU, V7X_MXU), lambda j, i, e, nv, gf, gn: (0, 0)),
            ],
            out_specs=pl.BlockSpec((MOE_BLOCK, tn // 2), lambda j, i, e, nv, gf, gn: (i, j)),
            scratch_shapes=[pltpu.VMEM((d, tn), BF16), pltpu.VMEM((2, d, tn), F32),
                            pltpu.SemaphoreType.DMA((2,)), pltpu.SMEM((1,), jnp.int32)],
        ),
        out_shape=jax.ShapeDtypeStruct((n_rows, f), BF16),
        compiler_params=_cparams(("arbitrary", "arbitrary")),
        name="moe_up",
    )(blk_e, nvalid, gfirst, gnext, xs, w1, b1g, b1l, perm)


def _down_kernel(e_ref, nv_ref, gf_ref, gn_ref, a_ref, w_hbm, b_ref, o_ref, wb_ref, wbuf, wsem, gcnt):
    i = pl.program_id(0)

    def copy_of(expert, next_pass, slot):
        return pltpu.make_async_copy(w_hbm.at[expert], wbuf.at[slot], wsem.at[slot])

    slot = _group_weights(gf_ref, gn_ref, e_ref, i, i == 0, False, copy_of, gcnt)

    @pl.when(gf_ref[i] == 1)
    def _():
        wb_ref[...] = wbuf[slot].astype(BF16)

    @pl.when(nv_ref[i] > 0)
    def _():
        o_ref[...] = jnp.dot(a_ref[...], wb_ref[...], preferred_element_type=F32) + b_ref[0]

    @pl.when(nv_ref[i] == 0)
    def _():
        o_ref[...] = jnp.zeros(o_ref.shape, o_ref.dtype)


def _moe_down(blk_e, nvalid, gfirst, gnext, act, w2, b2):
    n_rows, f = act.shape
    ne, _, d = w2.shape
    n_blocks = n_rows // MOE_BLOCK
    return pl.pallas_call(
        _down_kernel,
        grid_spec=pltpu.PrefetchScalarGridSpec(
            num_scalar_prefetch=4,
            grid=(n_blocks,),
            in_specs=[
                pl.BlockSpec((MOE_BLOCK, f), lambda i, e, nv, gf, gn: (i, 0)),
                pl.BlockSpec(memory_space=pl.ANY),
                pl.BlockSpec((1, 1, d), lambda i, e, nv, gf, gn: (e[i], 0, 0)),
            ],
            out_specs=pl.BlockSpec((MOE_BLOCK, d), lambda i, e, nv, gf, gn: (i, 0)),
            scratch_shapes=[pltpu.VMEM((f, d), BF16), pltpu.VMEM((2, f, d), F32),
                            pltpu.SemaphoreType.DMA((2,)), pltpu.SMEM((1,), jnp.int32)],
        ),
        out_shape=jax.ShapeDtypeStruct((n_rows, d), F32),
        compiler_params=_cparams(("arbitrary",)),
        name="moe_down",
    )(blk_e, nvalid, gfirst, gnext, act, w2, b2.reshape(ne, 1, d))


def _combine_kernel(pos_ref, y_hbm, gate_ref, x1_ref, ada_ref, g_ref, o_ref, buf, sems, *, tt):
    i = pl.program_id(0)
    n_tiles = pl.num_programs(0) - 1
    n_rows = TOP_K * tt

    def finish(slot):
        gates = gate_ref[...]
        f = buf[slot, 0:tt, :] * gates[:, 0:1]
        for k in range(1, TOP_K):
            f = f + buf[slot, k * tt:(k + 1) * tt, :] * gates[:, k:k + 1]
        o_ref[...] = x1_ref[...] + ada_ref[0, 5:6, :] * _rms(f, g_ref[...])

    @pl.when(i == 0)
    def _():
        _start_rows(pos_ref, y_hbm, buf, sems, 0, n_rows)

    for slot in (0, 1):
        @pl.when(jnp.logical_and(jnp.logical_and(i > 0, i < n_tiles), i % 2 == slot))
        def _():
            _wait_rows(y_hbm, buf, sems, 1 - slot)
            _start_rows(pos_ref, y_hbm, buf, sems, slot, n_rows, unrolled=True)
            finish(1 - slot)

    @pl.when(i == n_tiles)
    def _():
        _wait_rows(y_hbm, buf, sems, (i - 1) % 2)
        finish((i - 1) % 2)


def _moe_combine(pos, y_buf, gates, x1, ada3, post_ffn_g, seq, tt):
    t, d = x1.shape
    per_b = seq // tt
    nt = t // tt
    pos_t = pos.reshape(nt, tt, TOP_K).transpose(0, 2, 1).reshape(nt, 1, TOP_K * tt)
    prev = lambda i: jnp.maximum(i - 1, 0)
    return pl.pallas_call(
        functools.partial(_combine_kernel, tt=tt),
        grid=(nt + 1,),
        in_specs=[
            pl.BlockSpec((1, 1, TOP_K * tt), lambda i: (jnp.minimum(i, nt - 1), 0, 0), memory_space=pltpu.SMEM),
            pl.BlockSpec(memory_space=pl.ANY),
            pl.BlockSpec((tt, gates.shape[1]), lambda i: (prev(i), 0)),
            pl.BlockSpec((tt, d), lambda i: (prev(i), 0)),
            pl.BlockSpec((1, N_ADA, d), lambda i: (prev(i) // per_b, 0, 0)),
            pl.BlockSpec((1, d), lambda i: (0, 0)),
        ],
        out_specs=pl.BlockSpec((tt, d), lambda i: (prev(i), 0)),
        out_shape=jax.ShapeDtypeStruct((t, d), F32),
        scratch_shapes=[pltpu.VMEM((2, TOP_K * tt, d), F32), pltpu.SemaphoreType.DMA((2,))],
        compiler_params=_cparams(("arbitrary",)),
        name="moe_combine",
    )(pos_t, y_buf, gates, x1, ada3, post_ffn_g.reshape(1, d))


def _route(top_idx, n_experts):
    n_tok = top_idx.shape[0]
    n_asg = n_tok * TOP_K
    i32 = jnp.int32
    flat_e = top_idx.reshape(n_asg)
    order = jnp.argsort(flat_e).astype(i32)
    counts = jnp.sum(flat_e[:, None] == jnp.arange(n_experts, dtype=i32)[None, :], axis=0, dtype=i32)
    starts = jnp.cumsum(counts) - counts
    padded = (counts + MOE_BLOCK - 1) // MOE_BLOCK * MOE_BLOCK
    pends = jnp.cumsum(padded)
    pstarts = pends - padded
    n_blocks = -(-n_asg // MOE_BLOCK) + n_experts
    blk_start = jnp.arange(n_blocks, dtype=i32) * MOE_BLOCK
    blk_e = jnp.minimum(jnp.sum(pends[None, :] <= blk_start[:, None], axis=1, dtype=i32), n_experts - 1)
    off = blk_start - pstarts[blk_e]
    nvalid = jnp.clip(counts[blk_e] - off, 0, MOE_BLOCK)
    j = off[:, None] + jnp.arange(MOE_BLOCK, dtype=i32)[None, :]
    valid = j < counts[blk_e][:, None]
    src = jnp.clip(starts[blk_e][:, None] + j, 0, n_asg - 1)
    asg = order[src]
    row_id = blk_start[:, None] + jnp.arange(MOE_BLOCK, dtype=i32)[None, :]
    tok_buf = jnp.where(valid, asg // TOP_K, row_id % n_tok)
    rank = jnp.argsort(order).astype(i32)
    pos = (pstarts - starts)[flat_e] + rank
    blk = jnp.arange(n_blocks, dtype=i32)
    prev_e = jnp.concatenate([jnp.full((1,), -1, i32), blk_e[:-1]])
    gfirst = jnp.logical_and(nvalid > 0, blk_e != prev_e).astype(i32)
    cand = jnp.where(gfirst == 1, blk, n_blocks)
    later = jnp.concatenate([lax.cummin(cand[::-1])[::-1][1:], jnp.full((1,), n_blocks, i32)])
    gnext = jnp.where(later < n_blocks, blk_e[jnp.minimum(later, n_blocks - 1)], -1)
    return tok_buf, pos, blk_e, nvalid, gfirst, gnext


def _pick(n, pref):
    return pref if n % pref == 0 else n


def kernel(x, c, ada_w, ada_b, pre_mix_g, post_mix_g, pre_ffn_g, post_ffn_g, w_in, gate_b, dw_w, dw_b, cln_g, cln_b, cp_w, cp_b, s5_a_re, s5_a_im, s5_log_dt, s5_b_re, s5_b_im, s5_c_re, s5_c_im, s5_d, glu_wa, glu_wb, w_out, router_w, router_b, w1, b1, w2, b2):
    bsz, seq, d = x.shape
    t = bsz * seq
    depth = ada_w.shape[0]
    conv_ch = dw_w.shape[-1]
    ng, ns, nh = s5_b_re.shape[1:]
    s5_w = ng * nh
    ne = router_w.shape[-1]
    col_s5 = 2 * conv_ch
    col_gate = col_s5 + s5_w
    assert conv_ch == s5_w and V7X_MXU % nh == 0 and ng % (V7X_MXU // nh) == 0
    tm = _pick(seq, 512)
    tm_out = _pick(seq, 512)
    tn_in = _pick(w_in.shape[-1], conv_ch)

    x2 = x.reshape(t, d)
    for l in range(depth):
        ada3 = _ada(c, ada_w[l], ada_b[l]).reshape(bsz, N_ADA, d)
        proj = _inproj(x2, ada3, pre_mix_g[l], w_in[l], seq, _pick(seq, 1024), tn_in)
        vc = _conv_branch(proj, dw_w[l], dw_b[l], cln_g[l], cln_b[l], bsz, seq, tm)

        s5_params = _s5_params(s5_a_re[l], s5_a_im[l], s5_log_dt[l], s5_b_re[l], s5_b_im[l],
                               s5_c_re[l], s5_c_im[l], s5_d[l])
        ys = _s5_branch(proj, col_s5, s5_params, bsz, seq)

        merged = _merge(vc, ys, proj, cp_w[l], cp_b[l], glu_wa[l], glu_wb[l], gate_b[l],
                        _pick(seq, 1024), col_gate)
        x1, h2p, top_idx, gates = _outproj(merged, x2, ada3, w_out[l], post_mix_g[l], pre_ffn_g[l],
                                           router_w[l], router_b[l], seq, tm_out)

        tok_buf, pos, blk_e, nvalid, gfirst, gnext = _route(top_idx, ne)
        nb = tok_buf.shape[0]
        xs = _moe_gather(tok_buf, h2p, MOE_BLOCK * next(k for k in (4, 2, 1) if nb % k == 0))
        act = _moe_up(blk_e, nvalid, gfirst, gnext, xs, w1[l], b1[l], _pick(w1.shape[-1], 2048))
        y_buf = _moe_down(blk_e, nvalid, gfirst, gnext, act, w2[l], b2[l])
        x2 = _moe_combine(pos, y_buf, gates, x1, ada3, post_ffn_g[l], seq, _pick(seq, 256))
    return x2.reshape(bsz, seq, d)
```

```python
import functools
import math

import jax
import jax.numpy as jnp
from jax import lax
from jax.experimental import pallas as pl
from jax.experimental.pallas import tpu as pltpu

EPS = 1e-6
N_ADA = 6
TOP_K = 4
MOE_BLOCK = 256
SWIGLU_ALPHA = 1.702
SWIGLU_LIMIT = 7.0
S5_CHUNK = 16
V7X_LANES = 128
V7X_SUBLANES = 8
V7X_MXU = 256
VMEM_LIMIT = 56 * 1024 * 1024

F32 = jnp.float32
BF16 = jnp.bfloat16


def _cparams(sem):
    return pltpu.CompilerParams(dimension_semantics=sem, vmem_limit_bytes=VMEM_LIMIT)


def _sigmoid(v):
    return 1.0 / (1.0 + jnp.exp(-v))


def _rms(v, g):
    return v * lax.rsqrt(jnp.mean(v * v, axis=-1, keepdims=True) + EPS) * g


def _ada_kernel(c_ref, w_ref, b_ref, o_ref):
    c = c_ref[...]
    s = (c * _sigmoid(c)).astype(BF16)
    o_ref[...] = jnp.dot(s, w_ref[...].astype(BF16), preferred_element_type=F32) + b_ref[...]


def _ada(c, ada_w, ada_b):
    bsz, d = c.shape
    n = ada_w.shape[1]
    tn = 1024 if n % 1024 == 0 else n
    return pl.pallas_call(
        _ada_kernel,
        grid=(n // tn,),
        in_specs=[
            pl.BlockSpec((bsz, d), lambda j: (0, 0)),
            pl.BlockSpec((d, tn), lambda j: (0, j)),
            pl.BlockSpec((1, tn), lambda j: (0, j)),
        ],
        out_specs=pl.BlockSpec((bsz, tn), lambda j: (0, j)),
        out_shape=jax.ShapeDtypeStruct((bsz, n), F32),
        compiler_params=_cparams(("arbitrary",)),
        name="ada",
    )(c, ada_w, ada_b.reshape(1, n))


def _inproj_kernel(x_ref, ada_ref, g_ref, w_ref, o_ref, h_ref):
    @pl.when(pl.program_id(1) == 0)
    def _():
        y = _rms(x_ref[...], g_ref[...])
        h = y * (1.0 + ada_ref[0, 1:2, :]) + ada_ref[0, 0:1, :]
        h_ref[...] = h.astype(BF16)

    o_ref[...] = jnp.dot(h_ref[...], w_ref[...].astype(BF16), preferred_element_type=F32).astype(BF16)


def _inproj(x2, ada3, g, w_bf, seq, tm, tn):
    t, d = x2.shape
    n = w_bf.shape[1]
    per_b = seq // tm
    return pl.pallas_call(
        _inproj_kernel,
        grid=(t // tm, n // tn),
        in_specs=[
            pl.BlockSpec((tm, d), lambda i, j: (i, 0)),
            pl.BlockSpec((1, N_ADA, d), lambda i, j: (i // per_b, 0, 0)),
            pl.BlockSpec((1, d), lambda i, j: (0, 0)),
            pl.BlockSpec((d, tn), lambda i, j: (0, j)),
        ],
        out_specs=pl.BlockSpec((tm, tn), lambda i, j: (i, j)),
        out_shape=jax.ShapeDtypeStruct((t, n), BF16),
        scratch_shapes=[pltpu.VMEM((tm, d), BF16)],
        compiler_params=_cparams(("arbitrary", "arbitrary")),
        name="inproj",
    )(x2, ada3, g.reshape(1, d), w_bf)


CONV_HALO = 32
CONV_ROWS = 64


def _conv_kernel(pv_ref, pg_ref, w_ref, b_ref, g_ref, beta_ref, o_ref, vext, vsh, cbuf, *, taps, tl):
    t = pl.program_id(1)
    ch = cbuf.shape[1]

    @pl.when(t == 0)
    def _():
        vext[0:CONV_HALO, :] = jnp.zeros((CONV_HALO, ch), F32)

    @pl.when(t > 0)
    def _():
        vext[0:CONV_HALO, :] = vext[tl:tl + CONV_HALO, :]

    pv = pv_ref[...].astype(F32)
    pg = pg_ref[...].astype(F32)
    vext[CONV_HALO:CONV_HALO + tl, :] = pv * _sigmoid(pg)

    n_sh = vsh.shape[1]
    for p in range(1, V7X_SUBLANES):
        vsh[p - 1] = vext[p:p + n_sh, :]

    off = CONV_HALO - (taps - 1)
    for cb in range(ch // V7X_LANES):
        lanes = slice(cb * V7X_LANES, (cb + 1) * V7X_LANES)

        for rc in range(tl // CONV_ROWS):
            r0 = rc * CONV_ROWS
            acc = jnp.zeros((CONV_ROWS, V7X_LANES), F32)
            for k in range(taps):
                p = (off + k) % V7X_SUBLANES
                base = r0 + off + k - p
                win = vext[base:base + CONV_ROWS, lanes] if p == 0 else vsh[p - 1, base:base + CONV_ROWS, lanes]
                acc = acc + w_ref[k:k + 1, lanes] * win
            cbuf[r0:r0 + CONV_ROWS, lanes] = acc + b_ref[:, lanes]

    v = cbuf[...]
    mu = jnp.mean(v, axis=-1, keepdims=True)
    xc = v - mu
    var = jnp.mean(xc * xc, axis=-1, keepdims=True)
    y = xc * lax.rsqrt(var + EPS) * g_ref[...] + beta_ref[...]
    o_ref[...] = (y * _sigmoid(y)).astype(BF16)


def _conv_branch(proj, dw_w, dw_b, cln_g, cln_b, bsz, seq, tl):
    taps, ch = dw_w.shape
    assert taps - 1 <= CONV_HALO and tl % CONV_ROWS == 0 and ch % V7X_LANES == 0
    per_b = seq // tl
    vec = lambda a: a.reshape(1, ch)
    return pl.pallas_call(
        functools.partial(_conv_kernel, taps=taps, tl=tl),
        grid=(bsz, per_b),
        in_specs=[
            pl.BlockSpec((tl, ch), lambda b, t: (b * per_b + t, 0)),
            pl.BlockSpec((tl, ch), lambda b, t: (b * per_b + t, 1)),
            pl.BlockSpec((taps, ch), lambda b, t: (0, 0)),
            pl.BlockSpec((1, ch), lambda b, t: (0, 0)),
            pl.BlockSpec((1, ch), lambda b, t: (0, 0)),
            pl.BlockSpec((1, ch), lambda b, t: (0, 0)),
        ],
        out_specs=pl.BlockSpec((tl, ch), lambda b, t: (b * per_b + t, 0)),
        out_shape=jax.ShapeDtypeStruct((bsz * seq, ch), BF16),
        scratch_shapes=[pltpu.VMEM((CONV_HALO + tl, ch), F32),
                        pltpu.VMEM((V7X_SUBLANES - 1, CONV_HALO + tl - V7X_SUBLANES, ch), F32),
                        pltpu.VMEM((tl, ch), F32)],
        compiler_params=_cparams(("arbitrary", "arbitrary")),
        name="conv",
    )(proj, proj, dw_w, vec(dw_b), vec(cln_g), vec(cln_b))


S5_SUPER = 16


def _s5_params(a_re, a_im, log_dt, b_re, b_im, c_re, c_im, d_skip):
    ng, ns = a_re.shape
    nh = b_re.shape[-1]
    gt = V7X_MXU // nh
    nj = ng // gt
    dt = jnp.exp(log_dt)[:, None]
    mag = jnp.exp(a_re * dt)
    abar_re = mag * jnp.cos(a_im * dt)
    abar_im = mag * jnp.sin(a_im * dt)
    den = a_re * a_re + a_im * a_im
    num_re = abar_re - 1.0
    coef_re = (num_re * a_re + abar_im * a_im) / den
    coef_im = (abar_im * a_re - num_re * a_im) / den
    bb_re = coef_re[:, :, None] * b_re - coef_im[:, :, None] * b_im
    bb_im = coef_re[:, :, None] * b_im + coef_im[:, :, None] * b_re
    p = jnp.asarray([1.0, S5_CHUNK, S5_CHUNK * S5_SUPER] + [S5_CHUNK * r for r in range(S5_SUPER)], F32)
    pmag = jnp.exp(p[:, None, None] * (a_re * dt)[None])
    pw_re = pmag * jnp.cos(p[:, None, None] * (a_im * dt)[None])
    pw_im = pmag * jnp.sin(p[:, None, None] * (a_im * dt)[None])
    tile = lambda a: a.reshape(-1, nj, gt * ns).transpose(1, 0, 2)
    apow = jnp.concatenate([tile(pw_re), tile(pw_im)], axis=1)
    eye = jnp.eye(gt, dtype=F32)

    def bdiag_in(bb):
        return jnp.einsum('jgnh,gk->jghkn', bb.reshape(nj, gt, ns, nh), eye).reshape(nj, gt * nh, gt * ns)

    def bdiag_out(cc):
        return jnp.einsum('jghn,gk->jgnkh', cc.reshape(nj, gt, nh, ns), eye).reshape(nj, gt * ns, gt * nh)

    bdb = jnp.stack([bdiag_in(bb_re), bdiag_in(bb_im)], axis=1)
    bdc = jnp.stack([bdiag_out(c_re), bdiag_out(-c_im)], axis=1)
    return bdb.astype(BF16), bdc.astype(BF16), apow, d_skip.reshape(nj, 1, gt * nh)


def _cmul_add(ar, ai, xr, xi, vr, vi):
    return ar * xr - ai * xi + vr, ar * xi + ai * xr + vi


def _s5_kernel(u_ref, bdb_ref, bdc_ref, ap_ref, d_ref, p1_ref, p1t_ref, p2_ref, p2t_ref, o_ref,
               u2n_ref, u2_ref, hr_ref, hi_ref, zr_ref, zi_ref, gr_ref, gi_ref, y2_ref, y2n_ref,
               *, n_super):
    tc, nr, nm = S5_CHUNK, S5_SUPER, n_super
    tile = tc * nr
    np_ = 3 + nr
    apow = lambda k: (ap_ref[0, k:k + 1, :], ap_ref[0, np_ + k:np_ + k + 1, :])
    ar, ai = apow(0)

    for m in range(nm):
        pu = jnp.dot(p1_ref[...], u_ref[m * tile:(m + 1) * tile, :], preferred_element_type=F32).astype(BF16)
        for s in range(tc):
            u2n_ref[s, m * nr:(m + 1) * nr, :] = pu[s * nr:(s + 1) * nr, :]
    for s in range(tc):
        u2_ref[s] = jnp.dot(p2_ref[...], u2n_ref[s], preferred_element_type=F32).astype(BF16)

    def drive(s):
        u = u2_ref[s]
        return (jnp.dot(u, bdb_ref[0, 0], preferred_element_type=F32),
                jnp.dot(u, bdb_ref[0, 1], preferred_element_type=F32))

    hr_ref[...] = jnp.zeros(hr_ref.shape, F32)
    hi_ref[...] = jnp.zeros(hi_ref.shape, F32)

    def pass1(s, carry):
        vr, vi = drive(s)
        hr, hi = _cmul_add(ar, ai, hr_ref[...], hi_ref[...], vr, vi)
        hr_ref[...] = hr
        hi_ref[...] = hi
        return carry

    lax.fori_loop(0, tc, pass1, 0, unroll=4)

    a16r, a16i = apow(1)
    qr = jnp.zeros((nm, hr_ref.shape[1]), F32)
    qi = qr
    for r in range(nr):
        rows = slice(r * nm, (r + 1) * nm)
        zr_ref[rows, :] = qr
        zi_ref[rows, :] = qi
        qr, qi = _cmul_add(a16r, a16i, qr, qi, hr_ref[rows, :], hi_ref[rows, :])
    a256r, a256i = apow(2)
    gr = jnp.zeros((1, hr_ref.shape[1]), F32)
    gi = gr
    for m in range(nm):
        gr_ref[m:m + 1, :] = gr
        gi_ref[m:m + 1, :] = gi
        gr, gi = _cmul_add(a256r, a256i, gr, gi, qr[m:m + 1, :], qi[m:m + 1, :])
    gpr = gr_ref[...]
    gpi = gi_ref[...]
    for r in range(nr):
        rows = slice(r * nm, (r + 1) * nm)
        pr, pi = apow(3 + r)
        zr, zi = _cmul_add(pr, pi, gpr, gpi, zr_ref[rows, :], zi_ref[rows, :])
        zr_ref[rows, :] = zr
        zi_ref[rows, :] = zi

    def pass2(t, carry):
        vr, vi = drive(t)
        sr, si = _cmul_add(ar, ai, zr_ref[...], zi_ref[...], vr, vi)
        zr_ref[...] = sr
        zi_ref[...] = si
        y = (jnp.dot(sr.astype(BF16), bdc_ref[0, 0], preferred_element_type=F32)
             + jnp.dot(si.astype(BF16), bdc_ref[0, 1], preferred_element_type=F32))
        y = y + d_ref[0] * u2_ref[t].astype(F32)
        y = 0.5 * y * (1.0 + jnp.tanh(math.sqrt(2.0 / math.pi) * (y + 0.044715 * (y * y * y))))
        y2_ref[t] = y.astype(BF16)
        return carry

    lax.fori_loop(0, tc, pass2, 0, unroll=4)

    for t in range(tc):
        yn = jnp.dot(p2t_ref[...], y2_ref[t], preferred_element_type=F32).astype(BF16)
        for m in range(nm):
            y2n_ref[m * tile + t * nr:m * tile + (t + 1) * nr, :] = yn[m * nr:(m + 1) * nr, :]
    for m in range(nm):
        rows = slice(m * tile, (m + 1) * tile)
        o_ref[rows, :] = jnp.dot(p1t_ref[...], y2n_ref[rows, :], preferred_element_type=F32).astype(BF16)


def _s5_branch(proj, col0, params, bsz, seq):
    bdb, bdc, apow, dskip = params
    nj, _, w, sl = bdb.shape
    tile = S5_CHUNK * S5_SUPER
    assert seq % tile == 0 and col0 % w == 0
    nm = seq // tile
    assert nm % V7X_SUBLANES == 0, "row slabs of the chunk recurrence must be whole sublane tiles"
    rows = S5_SUPER * nm
    i1 = jnp.arange(tile)
    p1 = (i1[:, None] == ((i1 % S5_CHUNK) * S5_SUPER + i1 // S5_CHUNK)[None, :]).astype(BF16)
    i2 = jnp.arange(rows)
    p2 = (i2[:, None] == ((i2 % S5_SUPER) * nm + i2 // S5_SUPER)[None, :]).astype(BF16)
    const = lambda a: pl.BlockSpec(a.shape, lambda b, j: (0,) * a.ndim)
    per_j = lambda a: pl.BlockSpec((1,) + a.shape[1:], lambda b, j: (j,) + (0,) * (a.ndim - 1))
    return pl.pallas_call(
        functools.partial(_s5_kernel, n_super=nm),
        grid=(bsz, nj),
        in_specs=[
            pl.BlockSpec((seq, w), lambda b, j: (b, col0 // w + j)),
            per_j(bdb), per_j(bdc), per_j(apow), per_j(dskip),
            const(p1), const(p1), const(p2), const(p2),
        ],
        out_specs=pl.BlockSpec((seq, w), lambda b, j: (b, j)),
        out_shape=jax.ShapeDtypeStruct((bsz * seq, nj * w), BF16),
        scratch_shapes=[
            pltpu.VMEM((S5_CHUNK, rows, w), BF16), pltpu.VMEM((S5_CHUNK, rows, w), BF16),
            pltpu.VMEM((rows, sl), F32), pltpu.VMEM((rows, sl), F32),
            pltpu.VMEM((rows, sl), F32), pltpu.VMEM((rows, sl), F32),
            pltpu.VMEM((nm, sl), F32), pltpu.VMEM((nm, sl), F32),
            pltpu.VMEM((S5_CHUNK, rows, w), BF16), pltpu.VMEM((seq, w), BF16),
        ],
        compiler_params=_cparams(("arbitrary", "arbitrary")),
        name="s5",
    )(proj, bdb, bdc, apow, dskip, p1, p1.T, p2, p2.T)


def _merge_kernel(vc_ref, ys_ref, l1_ref, l2_ref, cpw_ref, cpb_ref, wa_ref, wb_ref, gb1_ref, gb2_ref, o_ref):
    y_conv = jnp.dot(vc_ref[...], cpw_ref[...], preferred_element_type=F32) + cpb_ref[...]
    ys = ys_ref[...]
    a = jnp.dot(ys, wa_ref[...], preferred_element_type=F32)
    b = jnp.dot(ys, wb_ref[...], preferred_element_type=F32)
    y_s5 = a * _sigmoid(b)
    g1 = _sigmoid(l1_ref[...].astype(F32) + gb1_ref[...])
    g2 = _sigmoid(l2_ref[...].astype(F32) + gb2_ref[...])
    o_ref[...] = (g1 * y_conv + g2 * y_s5).astype(BF16)


def _merge(vc, ys, proj, cp_w, cp_b, wa, wb, gate_b, tm, col0):
    t, ch = vc.shape
    d = cp_w.shape[1]
    tn = ch
    assert col0 % tn == 0 and d % tn == 0
    cb, nj = col0 // tn, d // tn
    wcol = lambda rows: pl.BlockSpec((rows, tn), lambda j, i: (0, j))
    return pl.pallas_call(
        _merge_kernel,
        grid=(nj, t // tm),
        in_specs=[
            pl.BlockSpec((tm, ch), lambda j, i: (i, 0)),
            pl.BlockSpec((tm, ch), lambda j, i: (i, 0)),
            pl.BlockSpec((tm, tn), lambda j, i: (i, cb + j)),
            pl.BlockSpec((tm, tn), lambda j, i: (i, cb + nj + j)),
            wcol(ch), wcol(1), wcol(ch), wcol(ch), wcol(1), wcol(1),
        ],
        out_specs=pl.BlockSpec((tm, tn), lambda j, i: (i, j)),
        out_shape=jax.ShapeDtypeStruct((t, d), BF16),
        compiler_params=_cparams(("arbitrary", "arbitrary")),
        name="merge",
    )(vc, ys, proj, proj, cp_w.astype(BF16), cp_b.reshape(1, d), wa.astype(BF16), wb.astype(BF16),
      gate_b[:d].reshape(1, d), gate_b[d:].reshape(1, d))


def _split_bf16(v):
    hi = v.astype(BF16)
    lo = (v - hi.astype(F32)).astype(BF16)
    return hi, lo


def _pack_bf16_pairs(v):
    half = v.shape[-1] // 2
    b = lax.bitcast_convert_type(v, jnp.uint32)
    return (b[:, half:] & jnp.uint32(0xFFFF0000)) | (b[:, :half] >> 16)


def _unpack_bf16_pairs(p):
    lo = lax.bitcast_convert_type(p << 16, F32).astype(BF16)
    hi = lax.bitcast_convert_type(p & jnp.uint32(0xFFFF0000), F32).astype(BF16)
    return lo, hi


def _outproj_kernel(m_ref, x_ref, ada_ref, wo_ref, g1_ref, g2_ref, rwh_ref, rwl_ref, rb_ref,
                    x1_ref, h2p_ref, idx_ref, gate_ref, *, ne):
    m = jnp.dot(m_ref[...], wo_ref[...], preferred_element_type=F32)
    x1 = x_ref[...] + ada_ref[0, 2:3, :] * _rms(m, g1_ref[...])
    x1_ref[...] = x1
    h2 = _rms(x1, g2_ref[...]) * (1.0 + ada_ref[0, 4:5, :]) + ada_ref[0, 3:4, :]
    hh = h2.astype(jnp.bfloat16)
    hf = hh.astype(F32)
    h2p_ref[...] = _pack_bf16_pairs(hf)
    hl = (h2 - hf).astype(BF16)
    lg = jnp.dot(hh, rwh_ref[...], preferred_element_type=F32)
    lg = lg + jnp.dot(hl, rwh_ref[...], preferred_element_type=F32)
    lg = lg + jnp.dot(hh, rwl_ref[...], preferred_element_type=F32)
    lg = lg + rb_ref[...]
    lane = lax.broadcasted_iota(jnp.int32, lg.shape, 1)
    work = jnp.where(lane < ne, lg, -jnp.inf)
    vals, idxs = [], []
    for _ in range(TOP_K):
        mx = jnp.max(work, axis=-1, keepdims=True)
        ix = jnp.min(jnp.where(work == mx, lane, lg.shape[1]), axis=-1, keepdims=True)
        vals.append(mx)
        idxs.append(ix)
        work = jnp.where(lane == ix, -jnp.inf, work)
    ex = [jnp.exp(v - vals[0]) for v in vals]
    den = ex[0]
    for e in ex[1:]:
        den = den + e
    idx_out = jnp.zeros(lg.shape, jnp.int32)
    gate_out = jnp.zeros(lg.shape, F32)
    for k in range(TOP_K):
        idx_out = jnp.where(lane == k, idxs[k], idx_out)
        gate_out = jnp.where(lane == k, ex[k] / den, gate_out)
    idx_ref[...] = idx_out
    gate_ref[...] = gate_out


def _outproj(merged, x2, ada3, w_out, post_mix_g, pre_ffn_g, router_w, router_b, seq, tm):
    t, d = x2.shape
    ne = router_w.shape[1]
    nep = max(V7X_LANES, ne)
    rw = jnp.zeros((d, nep), F32).at[:, :ne].set(router_w)
    rw_hi, rw_lo = _split_bf16(rw)
    rb = jnp.zeros((1, nep), F32).at[0, :ne].set(router_b)
    per_b = seq // tm
    full = lambda shape: pl.BlockSpec(shape, lambda i: (0, 0))
    row = lambda w: pl.BlockSpec((tm, w), lambda i: (i, 0))
    x1, h2p, top_idx, gates = pl.pallas_call(
        functools.partial(_outproj_kernel, ne=ne),
        grid=(t // tm,),
        in_specs=[
            row(d), row(d),
            pl.BlockSpec((1, N_ADA, d), lambda i: (i // per_b, 0, 0)),
            full((d, d)), full((1, d)), full((1, d)), full((d, nep)), full((d, nep)), full((1, nep)),
        ],
        out_specs=[row(d), row(d // 2), row(nep), row(nep)],
        out_shape=[jax.ShapeDtypeStruct((t, d), F32), jax.ShapeDtypeStruct((t, d // 2), jnp.uint32),
                   jax.ShapeDtypeStruct((t, nep), jnp.int32), jax.ShapeDtypeStruct((t, nep), F32)],
        compiler_params=_cparams(("arbitrary",)),
        name="outproj",
    )(merged, x2, ada3, w_out.astype(BF16), post_mix_g.reshape(1, d), pre_ffn_g.reshape(1, d),
      rw_hi, rw_lo, rb)
    return x1, h2p, top_idx[:, :TOP_K], gates


GATHER_UNROLL = 8


def _row_copy(idx_ref, src_hbm, buf, sems, slot, r):
    return pltpu.make_async_copy(src_hbm.at[pl.ds(idx_ref[0, 0, r], 1), :],
                                 buf.at[slot, pl.ds(r, 1), :], sems.at[slot])


def _start_rows(idx_ref, src_hbm, buf, sems, slot, n, unrolled=False):
    if unrolled:
        for r in range(n):
            _row_copy(idx_ref, src_hbm, buf, sems, slot, r).start(priority=r % 2)
        return

    def start(r, carry):
        _row_copy(idx_ref, src_hbm, buf, sems, slot, r).start()
        return carry
    lax.fori_loop(0, n, start, 0, unroll=GATHER_UNROLL)


def _wait_rows(src_hbm, buf, sems, slot):
    n = buf.shape[1]
    pltpu.make_async_copy(src_hbm.at[pl.ds(0, n), :], buf.at[slot], sems.at[slot]).wait()


def _gather_kernel(tok_ref, h_hbm, o_ref, buf, sems):
    i = pl.program_id(0)
    n_steps = pl.num_programs(0) - 1

    for slot in (0, 1):
        @pl.when(jnp.logical_and(i < n_steps, i % 2 == slot))
        def _():
            _start_rows(tok_ref, h_hbm, buf, sems, slot, buf.shape[1], unrolled=True)

    @pl.when(i > 0)
    def _():
        slot = (i - 1) % 2
        _wait_rows(h_hbm, buf, sems, slot)
        o_ref[...] = buf[slot]


def _moe_gather(tok_buf, h2p, rows):
    n_rows = tok_buf.size
    assert n_rows % rows == 0
    n_steps = n_rows // rows
    t, w = h2p.shape
    return pl.pallas_call(
        _gather_kernel,
        grid=(n_steps + 1,),
        in_specs=[
            pl.BlockSpec((1, 1, rows), lambda i: (jnp.minimum(i, n_steps - 1), 0, 0), memory_space=pltpu.SMEM),
            pl.BlockSpec(memory_space=pl.ANY),
        ],
        out_specs=pl.BlockSpec((rows, w), lambda i: (jnp.maximum(i - 1, 0), 0)),
        out_shape=jax.ShapeDtypeStruct((n_rows, w), h2p.dtype),
        scratch_shapes=[pltpu.VMEM((2, rows, w), h2p.dtype), pltpu.SemaphoreType.DMA((2,))],
        compiler_params=_cparams(("arbitrary",)),
        name="moe_gather",
    )(tok_buf.reshape(n_steps, 1, rows), h2p)


def _group_weights(gf_ref, gn_ref, e_ref, i, first_step, more_passes, copy_of, gcnt):
    @pl.when(first_step)
    def _():
        gcnt[0] = 0
        copy_of(e_ref[0], False, 0).start()

    slot = gcnt[0] % 2

    @pl.when(gf_ref[i] == 1)
    def _():
        copy_of(e_ref[0], False, slot).wait()
        nxt = gn_ref[i]

        @pl.when(nxt >= 0)
        def _():
            copy_of(nxt, False, 1 - slot).start()

        @pl.when(jnp.logical_and(nxt < 0, more_passes))
        def _():
            copy_of(e_ref[0], True, 1 - slot).start()

        gcnt[0] = gcnt[0] + 1

    return slot


def _up_kernel(e_ref, nv_ref, gf_ref, gn_ref, x_ref, w_hbm, bg_ref, bl_ref, p_ref, o_ref,
               wp_ref, wbuf, wsem, gcnt):
    j, i = pl.program_id(0), pl.program_id(1)
    tn = wp_ref.shape[1]
    half = V7X_MXU // 2
    hd = x_ref.shape[1]

    def copy_of(expert, next_pass, slot):
        col = (j + 1) * tn if next_pass else j * tn
        return pltpu.make_async_copy(w_hbm.at[expert, :, pl.ds(col, tn)], wbuf.at[slot], wsem.at[slot])

    slot = _group_weights(gf_ref, gn_ref, e_ref, i, jnp.logical_and(j == 0, i == 0),
                          j < pl.num_programs(0) - 1, copy_of, gcnt)

    @pl.when(gf_ref[i] == 1)
    def _():
        for cb in range(tn // V7X_MXU):
            cols = slice(cb * V7X_MXU, (cb + 1) * V7X_MXU)
            wp_ref[:, cols] = jnp.dot(wbuf[slot, :, cols].astype(BF16), p_ref[...],
                                      preferred_element_type=F32).astype(BF16)

    @pl.when(nv_ref[i] > 0)
    def _():
        x_lo, x_hi = _unpack_bf16_pairs(x_ref[...])
        for cb in range(tn // V7X_MXU):
            cols = slice(cb * V7X_MXU, (cb + 1) * V7X_MXU)
            hb = (jnp.dot(x_lo, wp_ref[0:hd, cols], preferred_element_type=F32)
                  + jnp.dot(x_hi, wp_ref[hd:2 * hd, cols], preferred_element_type=F32))
            out_cols = slice(cb * half, (cb + 1) * half)
            x_glu = hb[:, :half] + bg_ref[0, :, out_cols]
            x_lin = hb[:, half:] + bl_ref[0, :, out_cols]
            x_glu = jnp.minimum(x_glu, SWIGLU_LIMIT)
            x_lin = jnp.clip(x_lin, -SWIGLU_LIMIT, SWIGLU_LIMIT)
            act = x_glu * _sigmoid(SWIGLU_ALPHA * x_glu) * (x_lin + 1.0)
            o_ref[:, out_cols] = act.astype(BF16)

    @pl.when(nv_ref[i] == 0)
    def _():
        o_ref[...] = jnp.zeros(o_ref.shape, o_ref.dtype)


def _moe_up(blk_e, nvalid, gfirst, gnext, xs, w1, b1, tn):
    n_rows, hd = xs.shape
    ne, d, f2 = w1.shape
    assert d == 2 * hd
    f = f2 // 2
    n_blocks = n_rows // MOE_BLOCK
    half = V7X_MXU // 2
    c = jnp.arange(V7X_MXU)
    perm = (((c % 2) * half + c // 2)[:, None] == c[None, :]).astype(BF16)
    b1g = b1[:, 0::2].reshape(ne, 1, f)
    b1l = b1[:, 1::2].reshape(ne, 1, f)
    return pl.pallas_call(
        _up_kernel,
        grid_spec=pltpu.PrefetchScalarGridSpec(
            num_scalar_prefetch=4,
            grid=(f2 // tn, n_blocks),
            in_specs=[
                pl.BlockSpec((MOE_BLOCK, hd), lambda j, i, e, nv, gf, gn: (i, 0)),
                pl.BlockSpec(memory_space=pl.ANY),
                pl.BlockSpec((1, 1, tn // 2), lambda j, i, e, nv, gf, gn: (e[i], 0, j)),
                pl.BlockSpec((1, 1, tn // 2), lambda j, i, e, nv, gf, gn: (e[i], 0, j)),
                pl.BlockSpec((V7X_MXU, V7X_MXU), lambda j, i, e, nv, gf, gn: (0, 0)),
            ],
            out_specs=pl.BlockSpec((MOE_BLOCK, tn // 2), lambda j, i, e, nv, gf, gn: (i, j)),
            scratch_shapes=[pltpu.VMEM((d, tn), BF16), pltpu.VMEM((2, d, tn), F32),
                            pltpu.SemaphoreType.DMA((2,)), pltpu.SMEM((1,), jnp.int32)],
        ),
        out_shape=jax.ShapeDtypeStruct((n_rows, f), BF16),
        compiler_params=_cparams(("arbitrary", "arbitrary")),
        name="moe_up",
    )(blk_e, nvalid, gfirst, gnext, xs, w1, b1g, b1l, perm)


def _down_kernel(e_ref, nv_ref, gf_ref, gn_ref, a_ref, w_hbm, b_ref, o_ref, wb_ref, wbuf, wsem, gcnt):
    i = pl.program_id(0)

    def copy_of(expert, next_pass, slot):
        return pltpu.make_async_copy(w_hbm.at[expert], wbuf.at[slot], wsem.at[slot])

    slot = _group_weights(gf_ref, gn_ref, e_ref, i, i == 0, False, copy_of, gcnt)

    @pl.when(gf_ref[i] == 1)
    def _():
        wb_ref[...] = wbuf[slot].astype(BF16)

    @pl.when(nv_ref[i] > 0)
    def _():
        o_ref[...] = jnp.dot(a_ref[...], wb_ref[...], preferred_element_type=F32) + b_ref[0]

    @pl.when(nv_ref[i] == 0)
    def _():
        o_ref[...] = jnp.zeros(o_ref.shape, o_ref.dtype)


def _moe_down(blk_e, nvalid, gfirst, gnext, act, w2, b2):
    n_rows, f = act.shape
    ne, _, d = w2.shape
    n_blocks = n_rows // MOE_BLOCK
    return pl.pallas_call(
        _down_kernel,
        grid_spec=pltpu.PrefetchScalarGridSpec(
            num_scalar_prefetch=4,
            grid=(n_blocks,),
            in_specs=[
                pl.BlockSpec((MOE_BLOCK, f), lambda i, e, nv, gf, gn: (i, 0)),
                pl.BlockSpec(memory_space=pl.ANY),
                pl.BlockSpec((1, 1, d), lambda i, e, nv, gf, gn: (e[i], 0, 0)),
            ],
            out_specs=pl.BlockSpec((MOE_BLOCK, d), lambda i, e, nv, gf, gn: (i, 0)),
            scratch_shapes=[pltpu.VMEM((f, d), BF16), pltpu.VMEM((2, f, d), F32),
                            pltpu.SemaphoreType.DMA((2,)), pltpu.SMEM((1,), jnp.int32)],
        ),
        out_shape=jax.ShapeDtypeStruct((n_rows, d), F32),
        compiler_params=_cparams(("arbitrary",)),
        name="moe_down",
    )(blk_e, nvalid, gfirst, gnext, act, w2, b2.reshape(ne, 1, d))


def _combine_kernel(pos_ref, y_hbm, gate_ref, x1_ref, ada_ref, g_ref, o_ref, buf, sems, *, tt):
    i = pl.program_id(0)
    n_tiles = pl.num_programs(0) - 1
    n_rows = TOP_K * tt

    def finish(slot):
        gates = gate_ref[...]
        f = buf[slot, 0:tt, :] * gates[:, 0:1]
        for k in range(1, TOP_K):
            f = f + buf[slot, k * tt:(k + 1) * tt, :] * gates[:, k:k + 1]
        o_ref[...] = x1_ref[...] + ada_ref[0, 5:6, :] * _rms(f, g_ref[...])

    @pl.when(i == 0)
    def _():
        _start_rows(pos_ref, y_hbm, buf, sems, 0, n_rows)

    for slot in (0, 1):
        @pl.when(jnp.logical_and(jnp.logical_and(i > 0, i < n_tiles), i % 2 == slot))
        def _():
            _wait_rows(y_hbm, buf, sems, 1 - slot)
            _start_rows(pos_ref, y_hbm, buf, sems, slot, n_rows, unrolled=True)
            finish(1 - slot)

    @pl.when(i == n_tiles)
    def _():
        _wait_rows(y_hbm, buf, sems, (i - 1) % 2)
        finish((i - 1) % 2)


def _moe_combine(pos, y_buf, gates, x1, ada3, post_ffn_g, seq, tt):
    t, d = x1.shape
    per_b = seq // tt
    nt = t // tt
    pos_t = pos.reshape(nt, tt, TOP_K).transpose(0, 2, 1).reshape(nt, 1, TOP_K * tt)
    prev = lambda i: jnp.maximum(i - 1, 0)
    return pl.pallas_call(
        functools.partial(_combine_kernel, tt=tt),
        grid=(nt + 1,),
        in_specs=[
            pl.BlockSpec((1, 1, TOP_K * tt), lambda i: (jnp.minimum(i, nt - 1), 0, 0), memory_space=pltpu.SMEM),
            pl.BlockSpec(memory_space=pl.ANY),
            pl.BlockSpec((tt, gates.shape[1]), lambda i: (prev(i), 0)),
            pl.BlockSpec((tt, d), lambda i: (prev(i), 0)),
            pl.BlockSpec((1, N_ADA, d), lambda i: (prev(i) // per_b, 0, 0)),
            pl.BlockSpec((1, d), lambda i: (0, 0)),
        ],
        out_specs=pl.BlockSpec((tt, d), lambda i: (prev(i), 0)),
        out_shape=jax.ShapeDtypeStruct((t, d), F32),
        scratch_shapes=[pltpu.VMEM((2, TOP_K * tt, d), F32), pltpu.SemaphoreType.DMA((2,))],
        compiler_params=_cparams(("arbitrary",)),
        name="moe_combine",
    )(pos_t, y_buf, gates, x1, ada3, post_ffn_g.reshape(1, d))


def _route(top_idx, n_experts):
    n_tok = top_idx.shape[0]
    n_asg = n_tok * TOP_K
    i32 = jnp.int32
    flat_e = top_idx.reshape(n_asg)
    order = jnp.argsort(flat_e).astype(i32)
    counts = jnp.sum(flat_e[:, None] == jnp.arange(n_experts, dtype=i32)[None, :], axis=0, dtype=i32)
    starts = jnp.cumsum(counts) - counts
    padded = (counts + MOE_BLOCK - 1) // MOE_BLOCK * MOE_BLOCK
    pends = jnp.cumsum(padded)
    pstarts = pends - padded
    n_blocks = -(-n_asg // MOE_BLOCK) + n_experts
    blk_start = jnp.arange(n_blocks, dtype=i32) * MOE_BLOCK
    blk_e = jnp.minimum(jnp.sum(pends[None, :] <= blk_start[:, None], axis=1, dtype=i32), n_experts - 1)
    off = blk_start - pstarts[blk_e]
    nvalid = jnp.clip(counts[blk_e] - off, 0, MOE_BLOCK)
    j = off[:, None] + jnp.arange(MOE_BLOCK, dtype=i32)[None, :]
    valid = j < counts[blk_e][:, None]
    src = jnp.clip(starts[blk_e][:, None] + j, 0, n_asg - 1)
    asg = order[src]
    row_id = blk_start[:, None] + jnp.arange(MOE_BLOCK, dtype=i32)[None, :]
    tok_buf = jnp.where(valid, asg // TOP_K, row_id % n_tok)
    rank = jnp.argsort(order).astype(i32)
    pos = (pstarts - starts)[flat_e] + rank
    blk = jnp.arange(n_blocks, dtype=i32)
    prev_e = jnp.concatenate([jnp.full((1,), -1, i32), blk_e[:-1]])
    gfirst = jnp.logical_and(nvalid > 0, blk_e != prev_e).astype(i32)
    cand = jnp.where(gfirst == 1, blk, n_blocks)
    later = jnp.concatenate([lax.cummin(cand[::-1])[::-1][1:], jnp.full((1,), n_blocks, i32)])
    gnext = jnp.where(later < n_blocks, blk_e[jnp.minimum(later, n_blocks - 1)], -1)
    return tok_buf, pos, blk_e, nvalid, gfirst, gnext


def _pick(n, pref):
    return pref if n % pref == 0 else n


def kernel(x, c, ada_w, ada_b, pre_mix_g, post_mix_g, pre_ffn_g, post_ffn_g, w_in, gate_b, dw_w, dw_b, cln_g, cln_b, cp_w, cp_b, s5_a_re, s5_a_im, s5_log_dt, s5_b_re, s5_b_im, s5_c_re, s5_c_im, s5_d, glu_wa, glu_wb, w_out, router_w, router_b, w1, b1, w2, b2):
    bsz, seq, d = x.shape
    t = bsz * seq
    depth = ada_w.shape[0]
    conv_ch = dw_w.shape[-1]
    ng, ns, nh = s5_b_re.shape[1:]
    s5_w = ng * nh
    ne = router_w.shape[-1]
    col_s5 = 2 * conv_ch
    col_gate = col_s5 + s5_w
    assert conv_ch == s5_w and V7X_MXU % nh == 0 and ng % (V7X_MXU // nh) == 0
    tm = _pick(seq, 512)
    tm_out = _pick(seq, 512)
    tn_in = _pick(w_in.shape[-1], conv_ch)

    x2 = x.reshape(t, d)
    for l in range(depth):
        ada3 = _ada(c, ada_w[l], ada_b[l]).reshape(bsz, N_ADA, d)
        proj = _inproj(x2, ada3, pre_mix_g[l], w_in[l], seq, _pick(seq, 1024), tn_in)
        vc = _conv_branch(proj, dw_w[l], dw_b[l], cln_g[l], cln_b[l], bsz, seq, tm)

        s5_params = _s5_params(s5_a_re[l], s5_a_im[l], s5_log_dt[l], s5_b_re[l], s5_b_im[l],
                               s5_c_re[l], s5_c_im[l], s5_d[l])
        ys = _s5_branch(proj, col_s5, s5_params, bsz, seq)

        merged = _merge(vc, ys, proj, cp_w[l], cp_b[l], glu_wa[l], glu_wb[l], gate_b[l],
                        _pick(seq, 1024), col_gate)
        x1, h2p, top_idx, gates = _outproj(merged, x2, ada3, w_out[l], post_mix_g[l], pre_ffn_g[l],
                                           router_w[l], router_b[l], seq, tm_out)

        tok_buf, pos, blk_e, nvalid, gfirst, gnext = _route(top_idx, ne)
        nb = tok_buf.shape[0]
        xs = _moe_gather(tok_buf, h2p, MOE_BLOCK * next(k for k in (4, 2, 1) if nb % k == 0))
        act = _moe_up(blk_e, nvalid, gfirst, gnext, xs, w1[l], b1[l], _pick(w1.shape[-1], 2048))
        y_buf = _moe_down(blk_e, nvalid, gfirst, gnext, act, w2[l], b2[l])
        x2 = _moe_combine(pos, y_buf, gates, x1, ada3, post_ffn_g[l], seq, _pick(seq, 256))
    return x2.reshape(bsz, seq, d)
```

```python
import functools
import math

import jax
import jax.numpy as jnp
from jax import lax
from jax.experimental import pallas as pl
from jax.experimental.pallas import tpu as pltpu

EPS = 1e-6
N_ADA = 6
TOP_K = 4
MOE_BLOCK = 256
SWIGLU_ALPHA = 1.702
SWIGLU_LIMIT = 7.0
S5_CHUNK = 16
V7X_LANES = 128
V7X_SUBLANES = 8
V7X_MXU = 256
VMEM_LIMIT = 56 * 1024 * 1024

F32 = jnp.float32
BF16 = jnp.bfloat16


def _cparams(sem):
    return pltpu.CompilerParams(dimension_semantics=sem, vmem_limit_bytes=VMEM_LIMIT)


def _sigmoid(v):
    return 1.0 / (1.0 + jnp.exp(-v))


def _rms(v, g):
    return v * lax.rsqrt(jnp.mean(v * v, axis=-1, keepdims=True) + EPS) * g


def _ada_kernel(c_ref, w_ref, b_ref, o_ref):
    c = c_ref[...]
    s = (c * _sigmoid(c)).astype(BF16)
    o_ref[...] = jnp.dot(s, w_ref[...].astype(BF16), preferred_element_type=F32) + b_ref[...]


def _ada(c, ada_w, ada_b):
    bsz, d = c.shape
    n = ada_w.shape[1]
    tn = 1024 if n % 1024 == 0 else n
    return pl.pallas_call(
        _ada_kernel,
        grid=(n // tn,),
        in_specs=[
            pl.BlockSpec((bsz, d), lambda j: (0, 0)),
            pl.BlockSpec((d, tn), lambda j: (0, j)),
            pl.BlockSpec((1, tn), lambda j: (0, j)),
        ],
        out_specs=pl.BlockSpec((bsz, tn), lambda j: (0, j)),
        out_shape=jax.ShapeDtypeStruct((bsz, n), F32),
        compiler_params=_cparams(("arbitrary",)),
        name="ada",
    )(c, ada_w, ada_b.reshape(1, n))


def _inproj_kernel(x_ref, ada_ref, g_ref, w_ref, o_ref, h_ref):
    @pl.when(pl.program_id(1) == 0)
    def _():
        y = _rms(x_ref[...], g_ref[...])
        h = y * (1.0 + ada_ref[0, 1:2, :]) + ada_ref[0, 0:1, :]
        h_ref[...] = h.astype(BF16)

    o_ref[...] = jnp.dot(h_ref[...], w_ref[...].astype(BF16), preferred_element_type=F32).astype(BF16)


def _inproj(x2, ada3, g, w_in, seq, tm, tn):
    t, d = x2.shape
    n = w_in.shape[1]
    per_b = seq // tm
    return pl.pallas_call(
        _inproj_kernel,
        grid=(t // tm, n // tn),
        in_specs=[
            pl.BlockSpec((tm, d), lambda i, j: (i, 0)),
            pl.BlockSpec((1, N_ADA, d), lambda i, j: (i // per_b, 0, 0)),
            pl.BlockSpec((1, d), lambda i, j: (0, 0)),
            pl.BlockSpec((d, tn), lambda i, j: (0, j)),
        ],
        out_specs=pl.BlockSpec((tm, tn), lambda i, j: (i, j)),
        out_shape=jax.ShapeDtypeStruct((t, n), BF16),
        scratch_shapes=[pltpu.VMEM((tm, d), BF16)],
        compiler_params=_cparams(("arbitrary", "arbitrary")),
        name="inproj",
    )(x2, ada3, g.reshape(1, d), w_in)


CONV_HALO = 32
CONV_ROWS = 64


def _conv_kernel(pv_ref, pg_ref, w_ref, b_ref, g_ref, beta_ref, o_ref, vext, vsh, cbuf, *, taps, tl):
    t = pl.program_id(1)
    ch = cbuf.shape[1]

    @pl.when(t == 0)
    def _():
        vext[0:CONV_HALO, :] = jnp.zeros((CONV_HALO, ch), F32)

    @pl.when(t > 0)
    def _():
        vext[0:CONV_HALO, :] = vext[tl:tl + CONV_HALO, :]

    pv = pv_ref[...].astype(F32)
    pg = pg_ref[...].astype(F32)
    vext[CONV_HALO:CONV_HALO + tl, :] = pv * _sigmoid(pg)

    n_sh = vsh.shape[1]
    for p in range(1, V7X_SUBLANES):
        vsh[p - 1] = vext[p:p + n_sh, :]

    off = CONV_HALO - (taps - 1)
    for cb in range(ch // V7X_LANES):
        lanes = slice(cb * V7X_LANES, (cb + 1) * V7X_LANES)

        for rc in range(tl // CONV_ROWS):
            r0 = rc * CONV_ROWS
            acc = jnp.zeros((CONV_ROWS, V7X_LANES), F32)
            for k in range(taps):
                p = (off + k) % V7X_SUBLANES
                base = r0 + off + k - p
                win = vext[base:base + CONV_ROWS, lanes] if p == 0 else vsh[p - 1, base:base + CONV_ROWS, lanes]
                acc = acc + w_ref[k:k + 1, lanes] * win
            cbuf[r0:r0 + CONV_ROWS, lanes] = acc + b_ref[:, lanes]

    v = cbuf[...]
    mu = jnp.mean(v, axis=-1, keepdims=True)
    xc = v - mu
    var = jnp.mean(xc * xc, axis=-1, keepdims=True)
    y = xc * lax.rsqrt(var + EPS) * g_ref[...] + beta_ref[...]
    o_ref[...] = (y * _sigmoid(y)).astype(BF16)


def _conv_branch(proj, dw_w, dw_b, cln_g, cln_b, bsz, seq, tl):
    taps, ch = dw_w.shape
    assert taps - 1 <= CONV_HALO and tl % CONV_ROWS == 0 and ch % V7X_LANES == 0
    per_b = seq // tl
    vec = lambda a: a.reshape(1, ch)
    return pl.pallas_call(
        functools.partial(_conv_kernel, taps=taps, tl=tl),
        grid=(bsz, per_b),
        in_specs=[
            pl.BlockSpec((tl, ch), lambda b, t: (b * per_b + t, 0)),
            pl.BlockSpec((tl, ch), lambda b, t: (b * per_b + t, 1)),
            pl.BlockSpec((taps, ch), lambda b, t: (0, 0)),
            pl.BlockSpec((1, ch), lambda b, t: (0, 0)),
            pl.BlockSpec((1, ch), lambda b, t: (0, 0)),
            pl.BlockSpec((1, ch), lambda b, t: (0, 0)),
        ],
        out_specs=pl.BlockSpec((tl, ch), lambda b, t: (b * per_b + t, 0)),
        out_shape=jax.ShapeDtypeStruct((bsz * seq, ch), BF16),
        scratch_shapes=[pltpu.VMEM((CONV_HALO + tl, ch), F32),
                        pltpu.VMEM((V7X_SUBLANES - 1, CONV_HALO + tl - V7X_SUBLANES, ch), F32),
                        pltpu.VMEM((tl, ch), F32)],
        compiler_params=_cparams(("arbitrary", "arbitrary")),
        name="conv",
    )(proj, proj, dw_w, vec(dw_b), vec(cln_g), vec(cln_b))


S5_SUPER = 16


def _s5_params(a_re, a_im, log_dt, b_re, b_im, c_re, c_im, d_skip):
    ng, ns = a_re.shape
    nh = b_re.shape[-1]
    gt = V7X_MXU // nh
    nj = ng // gt
    dt = jnp.exp(log_dt)[:, None]
    mag = jnp.exp(a_re * dt)
    abar_re = mag * jnp.cos(a_im * dt)
    abar_im = mag * jnp.sin(a_im * dt)
    den = a_re * a_re + a_im * a_im
    num_re = abar_re - 1.0
    coef_re = (num_re * a_re + abar_im * a_im) / den
    coef_im = (abar_im * a_re - num_re * a_im) / den
    bb_re = coef_re[:, :, None] * b_re - coef_im[:, :, None] * b_im
    bb_im = coef_re[:, :, None] * b_im + coef_im[:, :, None] * b_re
    p = jnp.asarray([1.0, S5_CHUNK, S5_CHUNK * S5_SUPER] + [S5_CHUNK * r for r in range(S5_SUPER)], F32)
    pmag = jnp.exp(p[:, None, None] * (a_re * dt)[None])
    pw_re = pmag * jnp.cos(p[:, None, None] * (a_im * dt)[None])
    pw_im = pmag * jnp.sin(p[:, None, None] * (a_im * dt)[None])
    tile = lambda a: a.reshape(-1, nj, gt * ns).transpose(1, 0, 2)
    apow = jnp.concatenate([tile(pw_re), tile(pw_im)], axis=1)
    eye = jnp.eye(gt, dtype=F32)

    def bdiag_in(bb):
        return jnp.einsum('jgnh,gk->jghkn', bb.reshape(nj, gt, ns, nh), eye).reshape(nj, gt * nh, gt * ns)

    def bdiag_out(cc):
        return jnp.einsum('jghn,gk->jgnkh', cc.reshape(nj, gt, nh, ns), eye).reshape(nj, gt * ns, gt * nh)

    bdb = jnp.stack([bdiag_in(bb_re), bdiag_in(bb_im)], axis=1)
    bdc = jnp.stack([bdiag_out(c_re), bdiag_out(-c_im)], axis=1)
    return bdb.astype(BF16), bdc.astype(BF16), apow, d_skip.reshape(nj, 1, gt * nh)


def _cmul_add(ar, ai, xr, xi, vr, vi):
    return ar * xr - ai * xi + vr, ar * xi + ai * xr + vi


def _s5_kernel(u_ref, bdb_ref, bdc_ref, ap_ref, d_ref, p1_ref, p1t_ref, p2_ref, p2t_ref, o_ref,
               u2n_ref, u2_ref, hr_ref, hi_ref, zr_ref, zi_ref, gr_ref, gi_ref, y2_ref, y2n_ref,
               *, n_super):
    tc, nr, nm = S5_CHUNK, S5_SUPER, n_super
    tile = tc * nr
    np_ = 3 + nr
    apow = lambda k: (ap_ref[0, k:k + 1, :], ap_ref[0, np_ + k:np_ + k + 1, :])
    ar, ai = apow(0)

    for m in range(nm):
        pu = jnp.dot(p1_ref[...], u_ref[m * tile:(m + 1) * tile, :], preferred_element_type=F32).astype(BF16)
        for s in range(tc):
            u2n_ref[s, m * nr:(m + 1) * nr, :] = pu[s * nr:(s + 1) * nr, :]
    for s in range(tc):
        u2_ref[s] = jnp.dot(p2_ref[...], u2n_ref[s], preferred_element_type=F32).astype(BF16)

    def drive(s):
        u = u2_ref[s]
        return (jnp.dot(u, bdb_ref[0, 0], preferred_element_type=F32),
                jnp.dot(u, bdb_ref[0, 1], preferred_element_type=F32))

    hr_ref[...] = jnp.zeros(hr_ref.shape, F32)
    hi_ref[...] = jnp.zeros(hi_ref.shape, F32)

    def pass1(s, carry):
        vr, vi = drive(s)
        hr, hi = _cmul_add(ar, ai, hr_ref[...], hi_ref[...], vr, vi)
        hr_ref[...] = hr
        hi_ref[...] = hi
        return carry

    lax.fori_loop(0, tc, pass1, 0, unroll=8)

    a16r, a16i = apow(1)
    qr = jnp.zeros((nm, hr_ref.shape[1]), F32)
    qi = qr
    for r in range(nr):
        rows = slice(r * nm, (r + 1) * nm)
        zr_ref[rows, :] = qr
        zi_ref[rows, :] = qi
        qr, qi = _cmul_add(a16r, a16i, qr, qi, hr_ref[rows, :], hi_ref[rows, :])
    a256r, a256i = apow(2)
    gr = jnp.zeros((1, hr_ref.shape[1]), F32)
    gi = gr
    for m in range(nm):
        gr_ref[m:m + 1, :] = gr
        gi_ref[m:m + 1, :] = gi
        gr, gi = _cmul_add(a256r, a256i, gr, gi, qr[m:m + 1, :], qi[m:m + 1, :])
    gpr = gr_ref[...]
    gpi = gi_ref[...]
    for r in range(nr):
        rows = slice(r * nm, (r + 1) * nm)
        pr, pi = apow(3 + r)
        zr, zi = _cmul_add(pr, pi, gpr, gpi, zr_ref[rows, :], zi_ref[rows, :])
        zr_ref[rows, :] = zr
        zi_ref[rows, :] = zi

    def pass2(t, carry):
        vr, vi = drive(t)
        sr, si = _cmul_add(ar, ai, zr_ref[...], zi_ref[...], vr, vi)
        zr_ref[...] = sr
        zi_ref[...] = si
        y = (jnp.dot(sr.astype(BF16), bdc_ref[0, 0], preferred_element_type=F32)
             + jnp.dot(si.astype(BF16), bdc_ref[0, 1], preferred_element_type=F32))
        y = y + d_ref[0] * u2_ref[t].astype(F32)
        y = 0.5 * y * (1.0 + jnp.tanh(math.sqrt(2.0 / math.pi) * (y + 0.044715 * (y * y * y))))
        y2_ref[t] = y.astype(BF16)
        return carry

    lax.fori_loop(0, tc, pass2, 0, unroll=8)

    for t in range(tc):
        yn = jnp.dot(p2t_ref[...], y2_ref[t], preferred_element_type=F32).astype(BF16)
        for m in range(nm):
            y2n_ref[m * tile + t * nr:m * tile + (t + 1) * nr, :] = yn[m * nr:(m + 1) * nr, :]
    for m in range(nm):
        rows = slice(m * tile, (m + 1) * tile)
        o_ref[rows, :] = jnp.dot(p1t_ref[...], y2n_ref[rows, :], preferred_element_type=F32).astype(BF16)


def _s5_branch(proj, col0, params, bsz, seq):
    bdb, bdc, apow, dskip = params
    nj, _, w, sl = bdb.shape
    tile = S5_CHUNK * S5_SUPER
    assert seq % tile == 0 and col0 % w == 0
    nm = seq // tile
    assert nm % V7X_SUBLANES == 0, "row slabs of the chunk recurrence must be whole sublane tiles"
    rows = S5_SUPER * nm
    i1 = jnp.arange(tile)
    p1 = (i1[:, None] == ((i1 % S5_CHUNK) * S5_SUPER + i1 // S5_CHUNK)[None, :]).astype(BF16)
    i2 = jnp.arange(rows)
    p2 = (i2[:, None] == ((i2 % S5_SUPER) * nm + i2 // S5_SUPER)[None, :]).astype(BF16)
    const = lambda a: pl.BlockSpec(a.shape, lambda b, j: (0,) * a.ndim)
    per_j = lambda a: pl.BlockSpec((1,) + a.shape[1:], lambda b, j: (j,) + (0,) * (a.ndim - 1))
    return pl.pallas_call(
        functools.partial(_s5_kernel, n_super=nm),
        grid=(bsz, nj),
        in_specs=[
            pl.BlockSpec((seq, w), lambda b, j: (b, col0 // w + j)),
            per_j(bdb), per_j(bdc), per_j(apow), per_j(dskip),
            const(p1), const(p1), const(p2), const(p2),
        ],
        out_specs=pl.BlockSpec((seq, w), lambda b, j: (b, j)),
        out_shape=jax.ShapeDtypeStruct((bsz * seq, nj * w), BF16),
        scratch_shapes=[
            pltpu.VMEM((S5_CHUNK, rows, w), BF16), pltpu.VMEM((S5_CHUNK, rows, w), BF16),
            pltpu.VMEM((rows, sl), F32), pltpu.VMEM((rows, sl), F32),
            pltpu.VMEM((rows, sl), F32), pltpu.VMEM((rows, sl), F32),
            pltpu.VMEM((nm, sl), F32), pltpu.VMEM((nm, sl), F32),
            pltpu.VMEM((S5_CHUNK, rows, w), BF16), pltpu.VMEM((seq, w), BF16),
        ],
        compiler_params=_cparams(("arbitrary", "arbitrary")),
        name="s5",
    )(proj, bdb, bdc, apow, dskip, p1, p1.T, p2, p2.T)


def _merge_kernel(vc_ref, ys_ref, l1_ref, l2_ref, cpw_ref, cpb_ref, wa_ref, wb_ref, gb1_ref, gb2_ref, o_ref):
    y_conv = jnp.dot(vc_ref[...], cpw_ref[...], preferred_element_type=F32) + cpb_ref[...]
    ys = ys_ref[...]
    a = jnp.dot(ys, wa_ref[...], preferred_element_type=F32)
    b = jnp.dot(ys, wb_ref[...], preferred_element_type=F32)
    y_s5 = a * _sigmoid(b)
    g1 = _sigmoid(l1_ref[...].astype(F32) + gb1_ref[...])
    g2 = _sigmoid(l2_ref[...].astype(F32) + gb2_ref[...])
    o_ref[...] = (g1 * y_conv + g2 * y_s5).astype(BF16)


def _merge(vc, ys, proj, cp_w, cp_b, wa, wb, gate_b, tm, col0):
    t, ch = vc.shape
    d = cp_w.shape[1]
    tn = ch
    assert col0 % tn == 0 and d % tn == 0
    cb, nj = col0 // tn, d // tn
    wcol = lambda rows: pl.BlockSpec((rows, tn), lambda j, i: (0, j))
    return pl.pallas_call(
        _merge_kernel,
        grid=(nj, t // tm),
        in_specs=[
            pl.BlockSpec((tm, ch), lambda j, i: (i, 0)),
            pl.BlockSpec((tm, ch), lambda j, i: (i, 0)),
            pl.BlockSpec((tm, tn), lambda j, i: (i, cb + j)),
            pl.BlockSpec((tm, tn), lambda j, i: (i, cb + nj + j)),
            wcol(ch), wcol(1), wcol(ch), wcol(ch), wcol(1), wcol(1),
        ],
        out_specs=pl.BlockSpec((tm, tn), lambda j, i: (i, j)),
        out_shape=jax.ShapeDtypeStruct((t, d), BF16),
        compiler_params=_cparams(("arbitrary", "arbitrary")),
        name="merge",
    )(vc, ys, proj, proj, cp_w.astype(BF16), cp_b.reshape(1, d), wa.astype(BF16), wb.astype(BF16),
      gate_b[:d].reshape(1, d), gate_b[d:].reshape(1, d))


def _split_bf16(v):
    hi = v.astype(BF16)
    lo = (v - hi.astype(F32)).astype(BF16)
    return hi, lo


def _pack_bf16_pairs(v):
    half = v.shape[-1] // 2
    b = lax.bitcast_convert_type(v, jnp.uint32)
    return (b[:, half:] & jnp.uint32(0xFFFF0000)) | (b[:, :half] >> 16)


def _unpack_bf16_pairs(p):
    lo = lax.bitcast_convert_type(p << 16, F32).astype(BF16)
    hi = lax.bitcast_convert_type(p & jnp.uint32(0xFFFF0000), F32).astype(BF16)
    return lo, hi


def _outproj_kernel(m_ref, x_ref, ada_ref, wo_ref, g1_ref, g2_ref, rwh_ref, rwl_ref, rb_ref,
                    x1_ref, h2p_ref, idx_ref, gate_ref, *, ne):
    m = jnp.dot(m_ref[...], wo_ref[...], preferred_element_type=F32)
    x1 = x_ref[...] + ada_ref[0, 2:3, :] * _rms(m, g1_ref[...])
    x1_ref[...] = x1
    h2 = _rms(x1, g2_ref[...]) * (1.0 + ada_ref[0, 4:5, :]) + ada_ref[0, 3:4, :]
    hh = h2.astype(jnp.bfloat16)
    hf = hh.astype(F32)
    h2p_ref[...] = _pack_bf16_pairs(hf)
    hl = (h2 - hf).astype(BF16)
    lg = jnp.dot(hh, rwh_ref[...], preferred_element_type=F32)
    lg = lg + jnp.dot(hl, rwh_ref[...], preferred_element_type=F32)
    lg = lg + jnp.dot(hh, rwl_ref[...], preferred_element_type=F32)
    lg = lg + rb_ref[...]
    lane = lax.broadcasted_iota(jnp.int32, lg.shape, 1)
    work = jnp.where(lane < ne, lg, -jnp.inf)
    vals, idxs = [], []
    for _ in range(TOP_K):
        mx = jnp.max(work, axis=-1, keepdims=True)
        ix = jnp.min(jnp.where(work == mx, lane, lg.shape[1]), axis=-1, keepdims=True)
        vals.append(mx)
        idxs.append(ix)
        work = jnp.where(lane == ix, -jnp.inf, work)
    ex = [jnp.exp(v - vals[0]) for v in vals]
    den = ex[0]
    for e in ex[1:]:
        den = den + e
    idx_out = jnp.zeros(lg.shape, jnp.int32)
    gate_out = jnp.zeros(lg.shape, F32)
    for k in range(TOP_K):
        idx_out = jnp.where(lane == k, idxs[k], idx_out)
        gate_out = jnp.where(lane == k, ex[k] / den, gate_out)
    idx_ref[...] = idx_out
    gate_ref[...] = gate_out


def _outproj(merged, x2, ada3, w_out, post_mix_g, pre_ffn_g, router_w, router_b, seq, tm):
    t, d = x2.shape
    ne = router_w.shape[1]
    nep = max(V7X_LANES, ne)
    rw = jnp.zeros((d, nep), F32).at[:, :ne].set(router_w)
    rw_hi, rw_lo = _split_bf16(rw)
    rb = jnp.zeros((1, nep), F32).at[0, :ne].set(router_b)
    per_b = seq // tm
    full = lambda shape: pl.BlockSpec(shape, lambda i: (0, 0))
    row = lambda w: pl.BlockSpec((tm, w), lambda i: (i, 0))
    x1, h2p, top_idx, gates = pl.pallas_call(
        functools.partial(_outproj_kernel, ne=ne),
        grid=(t // tm,),
        in_specs=[
            row(d), row(d),
            pl.BlockSpec((1, N_ADA, d), lambda i: (i // per_b, 0, 0)),
            full((d, d)), full((1, d)), full((1, d)), full((d, nep)), full((d, nep)), full((1, nep)),
        ],
        out_specs=[row(d), row(d // 2), row(nep), row(nep)],
        out_shape=[jax.ShapeDtypeStruct((t, d), F32), jax.ShapeDtypeStruct((t, d // 2), jnp.uint32),
                   jax.ShapeDtypeStruct((t, nep), jnp.int32), jax.ShapeDtypeStruct((t, nep), F32)],
        compiler_params=_cparams(("arbitrary",)),
        name="outproj",
    )(merged, x2, ada3, w_out.astype(BF16), post_mix_g.reshape(1, d), pre_ffn_g.reshape(1, d),
      rw_hi, rw_lo, rb)
    return x1, h2p, top_idx[:, :TOP_K], gates


GATHER_UNROLL = 8


def _row_copy(idx_ref, src_hbm, buf, sems, slot, r):
    return pltpu.make_async_copy(src_hbm.at[pl.ds(idx_ref[0, 0, r], 1), :],
                                 buf.at[slot, pl.ds(r, 1), :], sems.at[slot])


def _start_rows(idx_ref, src_hbm, buf, sems, slot, n, unrolled=False):
    if unrolled:
        for r in range(n):
            _row_copy(idx_ref, src_hbm, buf, sems, slot, r).start(priority=r % 2)
        return

    def start(r, carry):
        _row_copy(idx_ref, src_hbm, buf, sems, slot, r).start()
        return carry
    lax.fori_loop(0, n, start, 0, unroll=GATHER_UNROLL)


def _wait_rows(src_hbm, buf, sems, slot):
    n = buf.shape[1]
    pltpu.make_async_copy(src_hbm.at[pl.ds(0, n), :], buf.at[slot], sems.at[slot]).wait()


def _gather_kernel(tok_ref, h_hbm, o_ref, buf, sems):
    i = pl.program_id(0)
    n_steps = pl.num_programs(0) - 1

    for slot in (0, 1):
        @pl.when(jnp.logical_and(i < n_steps, i % 2 == slot))
        def _():
            _start_rows(tok_ref, h_hbm, buf, sems, slot, buf.shape[1], unrolled=True)

    @pl.when(i > 0)
    def _():
        slot = (i - 1) % 2
        _wait_rows(h_hbm, buf, sems, slot)
        o_ref[...] = buf[slot]


def _moe_gather(tok_buf, h2p, rows):
    n_rows = tok_buf.size
    assert n_rows % rows == 0
    n_steps = n_rows // rows
    t, w = h2p.shape
    return pl.pallas_call(
        _gather_kernel,
        grid=(n_steps + 1,),
        in_specs=[
            pl.BlockSpec((1, 1, rows), lambda i: (jnp.minimum(i, n_steps - 1), 0, 0), memory_space=pltpu.SMEM),
            pl.BlockSpec(memory_space=pl.ANY),
        ],
        out_specs=pl.BlockSpec((rows, w), lambda i: (jnp.maximum(i - 1, 0), 0)),
        out_shape=jax.ShapeDtypeStruct((n_rows, w), h2p.dtype),
        scratch_shapes=[pltpu.VMEM((2, rows, w), h2p.dtype), pltpu.SemaphoreType.DMA((2,))],
        compiler_params=_cparams(("arbitrary",)),
        name="moe_gather",
    )(tok_buf.reshape(n_steps, 1, rows), h2p)


def _group_weights(gf_ref, gn_ref, e_ref, i, first_step, more_passes, copy_of, gcnt):
    @pl.when(first_step)
    def _():
        gcnt[0] = 0
        copy_of(e_ref[0], False, 0).start()

    slot = gcnt[0] % 2

    @pl.when(gf_ref[i] == 1)
    def _():
        copy_of(e_ref[0], False, slot).wait()
        nxt = gn_ref[i]

        @pl.when(nxt >= 0)
        def _():
            copy_of(nxt, False, 1 - slot).start()

        @pl.when(jnp.logical_and(nxt < 0, more_passes))
        def _():
            copy_of(e_ref[0], True, 1 - slot).start()

        gcnt[0] = gcnt[0] + 1

    return slot


def _up_kernel(e_ref, nv_ref, gf_ref, gn_ref, x_ref, w_hbm, bg_ref, bl_ref, p_ref, o_ref,
               wp_ref, wbuf, wsem, gcnt):
    j, i = pl.program_id(0), pl.program_id(1)
    tn = wp_ref.shape[1]
    half = V7X_MXU // 2
    hd = x_ref.shape[1]

    def copy_of(expert, next_pass, slot):
        col = (j + 1) * tn if next_pass else j * tn
        return pltpu.make_async_copy(w_hbm.at[expert, :, pl.ds(col, tn)], wbuf.at[slot], wsem.at[slot])

    slot = _group_weights(gf_ref, gn_ref, e_ref, i, jnp.logical_and(j == 0, i == 0),
                          j < pl.num_programs(0) - 1, copy_of, gcnt)

    @pl.when(gf_ref[i] == 1)
    def _():
        for cb in range(tn // V7X_MXU):
            cols = slice(cb * V7X_MXU, (cb + 1) * V7X_MXU)
            wp_ref[:, cols] = jnp.dot(wbuf[slot, :, cols].astype(BF16), p_ref[...],
                                      preferred_element_type=F32).astype(BF16)

    @pl.when(nv_ref[i] > 0)
    def _():
        x_lo, x_hi = _unpack_bf16_pairs(x_ref[...])
        for cb in range(tn // V7X_MXU):
            cols = slice(cb * V7X_MXU, (cb + 1) * V7X_MXU)
            hb = (jnp.dot(x_lo, wp_ref[0:hd, cols], preferred_element_type=F32)
                  + jnp.dot(x_hi, wp_ref[hd:2 * hd, cols], preferred_element_type=F32))
            out_cols = slice(cb * half, (cb + 1) * half)
            x_glu = hb[:, :half] + bg_ref[0, :, out_cols]
            x_lin = hb[:, half:] + bl_ref[0, :, out_cols]
            x_glu = jnp.minimum(x_glu, SWIGLU_LIMIT)
            x_lin = jnp.clip(x_lin, -SWIGLU_LIMIT, SWIGLU_LIMIT)
            act = x_glu * _sigmoid(SWIGLU_ALPHA * x_glu) * (x_lin + 1.0)
            o_ref[:, out_cols] = act.astype(BF16)

    @pl.when(nv_ref[i] == 0)
    def _():
        o_ref[...] = jnp.zeros(o_ref.shape, o_ref.dtype)


def _moe_up(blk_e, nvalid, gfirst, gnext, xs, w1, b1, tn):
    n_rows, hd = xs.shape
    ne, d, f2 = w1.shape
    assert d == 2 * hd
    f = f2 // 2
    n_blocks = n_rows // MOE_BLOCK
    half = V7X_MXU // 2
    c = jnp.arange(V7X_MXU)
    perm = (((c % 2) * half + c // 2)[:, None] == c[None, :]).astype(BF16)
    b1g = b1[:, 0::2].reshape(ne, 1, f)
    b1l = b1[:, 1::2].reshape(ne, 1, f)
    return pl.pallas_call(
        _up_kernel,
        grid_spec=pltpu.PrefetchScalarGridSpec(
            num_scalar_prefetch=4,
            grid=(f2 // tn, n_blocks),
            in_specs=[
                pl.BlockSpec((MOE_BLOCK, hd), lambda j, i, e, nv, gf, gn: (i, 0)),
                pl.BlockSpec(memory_space=pl.ANY),
                pl.BlockSpec((1, 1, tn // 2), lambda j, i, e, nv, gf, gn: (e[i], 0, j)),
                pl.BlockSpec((1, 1, tn // 2), lambda j, i, e, nv, gf, gn: (e[i], 0, j)),
                pl.BlockSpec((V7X_MXU, V7X_MXU), lambda j, i, e, nv, gf, gn: (0, 0)),
            ],
            out_specs=pl.BlockSpec((MOE_BLOCK, tn // 2), lambda j, i, e, nv, gf, gn: (i, j)),
            scratch_shapes=[pltpu.VMEM((d, tn), BF16), pltpu.VMEM((2, d, tn), F32),
                            pltpu.SemaphoreType.DMA((2,)), pltpu.SMEM((1,), jnp.int32)],
        ),
        out_shape=jax.ShapeDtypeStruct((n_rows, f), BF16),
        compiler_params=_cparams(("arbitrary", "arbitrary")),
        name="moe_up",
    )(blk_e, nvalid, gfirst, gnext, xs, w1, b1g, b1l, perm)


def _down_kernel(e_ref, nv_ref, gf_ref, gn_ref, a_ref, w_hbm, b_ref, o_ref, wb_ref, wbuf, wsem, gcnt):
    i = pl.program_id(0)

    def copy_of(expert, next_pass, slot):
        return pltpu.make_async_copy(w_hbm.at[expert], wbuf.at[slot], wsem.at[slot])

    slot = _group_weights(gf_ref, gn_ref, e_ref, i, i == 0, False, copy_of, gcnt)

    @pl.when(gf_ref[i] == 1)
    def _():
        wb_ref[...] = wbuf[slot].astype(BF16)

    @pl.when(nv_ref[i] > 0)
    def _():
        o_ref[...] = jnp.dot(a_ref[...], wb_ref[...], preferred_element_type=F32) + b_ref[0]

    @pl.when(nv_ref[i] == 0)
    def _():
        o_ref[...] = jnp.zeros(o_ref.shape, o_ref.dtype)


def _moe_down(blk_e, nvalid, gfirst, gnext, act, w2, b2):
    n_rows, f = act.shape
    ne, _, d = w2.shape
    n_blocks = n_rows // MOE_BLOCK
    return pl.pallas_call(
        _down_kernel,
        grid_spec=pltpu.PrefetchScalarGridSpec(
            num_scalar_prefetch=4,
            grid=(n_blocks,),
            in_specs=[
                pl.BlockSpec((MOE_BLOCK, f), lambda i, e, nv, gf, gn: (i, 0)),
                pl.BlockSpec(memory_space=pl.ANY),
                pl.BlockSpec((1, 1, d), lambda i, e, nv, gf, gn: (e[i], 0, 0)),
            ],
            out_specs=pl.BlockSpec((MOE_BLOCK, d), lambda i, e, nv, gf, gn: (i, 0)),
            scratch_shapes=[pltpu.VMEM((f, d), BF16), pltpu.VMEM((2, f, d), F32),
                            pltpu.SemaphoreType.DMA((2,)), pltpu.SMEM((1,), jnp.int32)],
        ),
        out_shape=jax.ShapeDtypeStruct((n_rows, d), F32),
        compiler_params=_cparams(("arbitrary",)),
        name="moe_down",
    )(blk_e, nvalid, gfirst, gnext, act, w2, b2.reshape(ne, 1, d))


def _combine_kernel(pos_ref, y_hbm, gate_ref, x1_ref, ada_ref, g_ref, o_ref, buf, sems, *, tt):
    i = pl.program_id(0)
    n_tiles = pl.num_programs(0) - 1
    n_rows = TOP_K * tt

    def finish(slot):
        gates = gate_ref[...]
        f = buf[slot, 0:tt, :] * gates[:, 0:1]
        for k in range(1, TOP_K):
            f = f + buf[slot, k * tt:(k + 1) * tt, :] * gates[:, k:k + 1]
        o_ref[...] = x1_ref[...] + ada_ref[0, 5:6, :] * _rms(f, g_ref[...])

    @pl.when(i == 0)
    def _():
        _start_rows(pos_ref, y_hbm, buf, sems, 0, n_rows)

    for slot in (0, 1):
        @pl.when(jnp.logical_and(jnp.logical_and(i > 0, i < n_tiles), i % 2 == slot))
        def _():
            _wait_rows(y_hbm, buf, sems, 1 - slot)
            _start_rows(pos_ref, y_hbm, buf, sems, slot, n_rows, unrolled=True)
            finish(1 - slot)

    @pl.when(i == n_tiles)
    def _():
        _wait_rows(y_hbm, buf, sems, (i - 1) % 2)
        finish((i - 1) % 2)


def _moe_combine(pos, y_buf, gates, x1, ada3, post_ffn_g, seq, tt):
    t, d = x1.shape
    per_b = seq // tt
    nt = t // tt
    pos_t = pos.reshape(nt, tt, TOP_K).transpose(0, 2, 1).reshape(nt, 1, TOP_K * tt)
    prev = lambda i: jnp.maximum(i - 1, 0)
    return pl.pallas_call(
        functools.partial(_combine_kernel, tt=tt),
        grid=(nt + 1,),
        in_specs=[
            pl.BlockSpec((1, 1, TOP_K * tt), lambda i: (jnp.minimum(i, nt - 1), 0, 0), memory_space=pltpu.SMEM),
            pl.BlockSpec(memory_space=pl.ANY),
            pl.BlockSpec((tt, gates.shape[1]), lambda i: (prev(i), 0)),
            pl.BlockSpec((tt, d), lambda i: (prev(i), 0)),
            pl.BlockSpec((1, N_ADA, d), lambda i: (prev(i) // per_b, 0, 0)),
            pl.BlockSpec((1, d), lambda i: (0, 0)),
        ],
        out_specs=pl.BlockSpec((tt, d), lambda i: (prev(i), 0)),
        out_shape=jax.ShapeDtypeStruct((t, d), F32),
        scratch_shapes=[pltpu.VMEM((2, TOP_K * tt, d), F32), pltpu.SemaphoreType.DMA((2,))],
        compiler_params=_cparams(("arbitrary",)),
        name="moe_combine",
    )(pos_t, y_buf, gates, x1, ada3, post_ffn_g.reshape(1, d))


def _route(top_idx, n_experts):
    n_tok = top_idx.shape[0]
    n_asg = n_tok * TOP_K
    i32 = jnp.int32
    flat_e = top_idx.reshape(n_asg)
    order = jnp.argsort(flat_e).astype(i32)
    counts = jnp.sum(flat_e[:, None] == jnp.arange(n_experts, dtype=i32)[None, :], axis=0, dtype=i32)
    starts = jnp.cumsum(counts) - counts
    padded = (counts + MOE_BLOCK - 1) // MOE_BLOCK * MOE_BLOCK
    pends = jnp.cumsum(padded)
    pstarts = pends - padded
    n_blocks = -(-n_asg // MOE_BLOCK) + n_experts
    blk_start = jnp.arange(n_blocks, dtype=i32) * MOE_BLOCK
    blk_e = jnp.minimum(jnp.sum(pends[None, :] <= blk_start[:, None], axis=1, dtype=i32), n_experts - 1)
    off = blk_start - pstarts[blk_e]
    nvalid = jnp.clip(counts[blk_e] - off, 0, MOE_BLOCK)
    j = off[:, None] + jnp.arange(MOE_BLOCK, dtype=i32)[None, :]
    valid = j < counts[blk_e][:, None]
    row_id = blk_start[:, None] + jnp.arange(MOE_BLOCK, dtype=i32)[None, :]
    src = jnp.where(valid, starts[blk_e][:, None] + j, row_id % n_asg)
    asg = order[src]
    tok_buf = asg // TOP_K
    rank = jnp.argsort(order).astype(i32)
    pos = (pstarts - starts)[flat_e] + rank
    blk = jnp.arange(n_blocks, dtype=i32)
    prev_e = jnp.concatenate([jnp.full((1,), -1, i32), blk_e[:-1]])
    gfirst = jnp.logical_and(nvalid > 0, blk_e != prev_e).astype(i32)
    cand = jnp.where(gfirst == 1, blk, n_blocks)
    later = jnp.concatenate([lax.cummin(cand[::-1])[::-1][1:], jnp.full((1,), n_blocks, i32)])
    gnext = jnp.where(later < n_blocks, blk_e[jnp.minimum(later, n_blocks - 1)], -1)
    return tok_buf, pos, blk_e, nvalid, gfirst, gnext


def _pick(n, pref):
    return pref if n % pref == 0 else n


def kernel(x, c, ada_w, ada_b, pre_mix_g, post_mix_g, pre_ffn_g, post_ffn_g, w_in, gate_b, dw_w, dw_b, cln_g, cln_b, cp_w, cp_b, s5_a_re, s5_a_im, s5_log_dt, s5_b_re, s5_b_im, s5_c_re, s5_c_im, s5_d, glu_wa, glu_wb, w_out, router_w, router_b, w1, b1, w2, b2):
    bsz, seq, d = x.shape
    t = bsz * seq
    depth = ada_w.shape[0]
    conv_ch = dw_w.shape[-1]
    ng, ns, nh = s5_b_re.shape[1:]
    s5_w = ng * nh
    ne = router_w.shape[-1]
    col_s5 = 2 * conv_ch
    col_gate = col_s5 + s5_w
    assert conv_ch == s5_w and V7X_MXU % nh == 0 and ng % (V7X_MXU // nh) == 0
    tm = _pick(seq, 512)
    tm_out = _pick(seq, 512)
    tn_in = _pick(w_in.shape[-1], conv_ch)

    x2 = x.reshape(t, d)
    for l in range(depth):
        ada3 = _ada(c, ada_w[l], ada_b[l]).reshape(bsz, N_ADA, d)
        proj = _inproj(x2, ada3, pre_mix_g[l], w_in[l], seq, _pick(seq, 1024), tn_in)
        vc = _conv_branch(proj, dw_w[l], dw_b[l], cln_g[l], cln_b[l], bsz, seq, tm)

        s5_params = _s5_params(s5_a_re[l], s5_a_im[l], s5_log_dt[l], s5_b_re[l], s5_b_im[l],
                               s5_c_re[l], s5_c_im[l], s5_d[l])
        ys = _s5_branch(proj, col_s5, s5_params, bsz, seq)

        merged = _merge(vc, ys, proj, cp_w[l], cp_b[l], glu_wa[l], glu_wb[l], gate_b[l],
                        _pick(seq, 1024), col_gate)
        x1, h2p, top_idx, gates = _outproj(merged, x2, ada3, w_out[l], post_mix_g[l], pre_ffn_g[l],
                                           router_w[l], router_b[l], seq, tm_out)

        tok_buf, pos, blk_e, nvalid, gfirst, gnext = _route(top_idx, ne)
        nb = tok_buf.shape[0]
        xs = _moe_gather(tok_buf, h2p, MOE_BLOCK * next(k for k in (4, 2, 1) if nb % k == 0))
        act = _moe_up(blk_e, nvalid, gfirst, gnext, xs, w1[l], b1[l], _pick(w1.shape[-1], 2048))
        y_buf = _moe_down(blk_e, nvalid, gfirst, gnext, act, w2[l], b2[l])
        x2 = _moe_combine(pos, y_buf, gates, x1, ada3, post_ffn_g[l], seq, _pick(seq, 256))
    return x2.reshape(bsz, seq, d)
```

```python
import functools
import math

import jax
import jax.numpy as jnp
from jax import lax
from jax.experimental import pallas as pl
from jax.experimental.pallas import tpu as pltpu

EPS = 1e-6
N_ADA = 6
TOP_K = 4
MOE_BLOCK = 256
SWIGLU_ALPHA = 1.702
SWIGLU_LIMIT = 7.0
S5_CHUNK = 16
V7X_LANES = 128
V7X_SUBLANES = 8
V7X_MXU = 256
VMEM_LIMIT = 56 * 1024 * 1024

F32 = jnp.float32
BF16 = jnp.bfloat16


def _cparams(sem):
    return pltpu.CompilerParams(dimension_semantics=sem, vmem_limit_bytes=VMEM_LIMIT)


def _sigmoid(v):
    return 1.0 / (1.0 + jnp.exp(-v))


def _rms(v, g):
    return v * lax.rsqrt(jnp.mean(v * v, axis=-1, keepdims=True) + EPS) * g


def _ada_kernel(c_ref, w_ref, b_ref, o_ref):
    c = c_ref[...]
    s = (c * _sigmoid(c)).astype(BF16)
    o_ref[...] = jnp.dot(s, w_ref[...].astype(BF16), preferred_element_type=F32) + b_ref[...]


def _ada(c, ada_w, ada_b):
    bsz, d = c.shape
    n = ada_w.shape[1]
    tn = 1024 if n % 1024 == 0 else n
    return pl.pallas_call(
        _ada_kernel,
        grid=(n // tn,),
        in_specs=[
            pl.BlockSpec((bsz, d), lambda j: (0, 0)),
            pl.BlockSpec((d, tn), lambda j: (0, j)),
            pl.BlockSpec((1, tn), lambda j: (0, j)),
        ],
        out_specs=pl.BlockSpec((bsz, tn), lambda j: (0, j)),
        out_shape=jax.ShapeDtypeStruct((bsz, n), F32),
        compiler_params=_cparams(("arbitrary",)),
        name="ada",
    )(c, ada_w, ada_b.reshape(1, n))


def _inproj_kernel(x_ref, ada_ref, g_ref, w_ref, o_ref, h_ref):
    @pl.when(pl.program_id(1) == 0)
    def _():
        y = _rms(x_ref[...], g_ref[...])
        h = y * (1.0 + ada_ref[0, 1:2, :]) + ada_ref[0, 0:1, :]
        h_ref[...] = h.astype(BF16)

    o_ref[...] = jnp.dot(h_ref[...], w_ref[...].astype(BF16), preferred_element_type=F32).astype(BF16)


def _inproj(x2, ada3, g, w_in, seq, tm, tn):
    t, d = x2.shape
    n = w_in.shape[1]
    per_b = seq // tm
    return pl.pallas_call(
        _inproj_kernel,
        grid=(t // tm, n // tn),
        in_specs=[
            pl.BlockSpec((tm, d), lambda i, j: (i, 0)),
            pl.BlockSpec((1, N_ADA, d), lambda i, j: (i // per_b, 0, 0)),
            pl.BlockSpec((1, d), lambda i, j: (0, 0)),
            pl.BlockSpec((d, tn), lambda i, j: (0, j)),
        ],
        out_specs=pl.BlockSpec((tm, tn), lambda i, j: (i, j)),
        out_shape=jax.ShapeDtypeStruct((t, n), BF16),
        scratch_shapes=[pltpu.VMEM((tm, d), BF16)],
        compiler_params=_cparams(("arbitrary", "arbitrary")),
        name="inproj",
    )(x2, ada3, g.reshape(1, d), w_in)


CONV_HALO = 32
CONV_ROWS = 64


def _conv_kernel(pv_ref, pg_ref, w_ref, b_ref, g_ref, beta_ref, cpw_ref, cpb_ref, o_ref, vext, vsh, cbuf, *, taps, tl):
    t = pl.program_id(1)
    ch = cbuf.shape[1]

    @pl.when(t == 0)
    def _():
        vext[0:CONV_HALO, :] = jnp.zeros((CONV_HALO, ch), F32)

    @pl.when(t > 0)
    def _():
        vext[0:CONV_HALO, :] = vext[tl:tl + CONV_HALO, :]

    pv = pv_ref[...].astype(F32)
    pg = pg_ref[...].astype(F32)
    vext[CONV_HALO:CONV_HALO + tl, :] = pv * _sigmoid(pg)

    n_sh = vsh.shape[1]
    for p in range(1, V7X_SUBLANES):
        vsh[p - 1] = vext[p:p + n_sh, :]

    off = CONV_HALO - (taps - 1)
    for cb in range(ch // V7X_LANES):
        lanes = slice(cb * V7X_LANES, (cb + 1) * V7X_LANES)

        for rc in range(tl // CONV_ROWS):
            r0 = rc * CONV_ROWS
            acc = jnp.zeros((CONV_ROWS, V7X_LANES), F32)
            for k in range(taps):
                p = (off + k) % V7X_SUBLANES
                base = r0 + off + k - p
                win = vext[base:base + CONV_ROWS, lanes] if p == 0 else vsh[p - 1, base:base + CONV_ROWS, lanes]
                acc = acc + w_ref[k:k + 1, lanes] * win
            cbuf[r0:r0 + CONV_ROWS, lanes] = acc + b_ref[:, lanes]

    v = cbuf[...]
    mu = jnp.mean(v, axis=-1, keepdims=True)
    xc = v - mu
    var = jnp.mean(xc * xc, axis=-1, keepdims=True)
    y = xc * lax.rsqrt(var + EPS) * g_ref[...] + beta_ref[...]
    vc = (y * _sigmoid(y)).astype(BF16)
    o_ref[...] = (jnp.dot(vc, cpw_ref[...], preferred_element_type=F32) + cpb_ref[...]).astype(BF16)


def _conv_branch(proj, dw_w, dw_b, cln_g, cln_b, cp_w, cp_b, bsz, seq, tl):
    taps, ch = dw_w.shape
    d = cp_w.shape[1]
    assert taps - 1 <= CONV_HALO and tl % CONV_ROWS == 0 and ch % V7X_LANES == 0
    per_b = seq // tl
    vec = lambda a: a.reshape(1, ch)
    return pl.pallas_call(
        functools.partial(_conv_kernel, taps=taps, tl=tl),
        grid=(bsz, per_b),
        in_specs=[
            pl.BlockSpec((tl, ch), lambda b, t: (b * per_b + t, 0)),
            pl.BlockSpec((tl, ch), lambda b, t: (b * per_b + t, 1)),
            pl.BlockSpec((taps, ch), lambda b, t: (0, 0)),
            pl.BlockSpec((1, ch), lambda b, t: (0, 0)),
            pl.BlockSpec((1, ch), lambda b, t: (0, 0)),
            pl.BlockSpec((1, ch), lambda b, t: (0, 0)),
            pl.BlockSpec((ch, d), lambda b, t: (0, 0)),
            pl.BlockSpec((1, d), lambda b, t: (0, 0)),
        ],
        out_specs=pl.BlockSpec((tl, d), lambda b, t: (b * per_b + t, 0)),
        out_shape=jax.ShapeDtypeStruct((bsz * seq, d), BF16),
        scratch_shapes=[pltpu.VMEM((CONV_HALO + tl, ch), F32),
                        pltpu.VMEM((V7X_SUBLANES - 1, CONV_HALO + tl - V7X_SUBLANES, ch), F32),
                        pltpu.VMEM((tl, ch), F32)],
        compiler_params=_cparams(("arbitrary", "arbitrary")),
        name="conv",
    )(proj, proj, dw_w, vec(dw_b), vec(cln_g), vec(cln_b), cp_w.astype(BF16), cp_b.reshape(1, d))


S5_SUPER = 16


def _s5_params(a_re, a_im, log_dt, b_re, b_im, c_re, c_im, d_skip):
    ng, ns = a_re.shape
    nh = b_re.shape[-1]
    gt = V7X_MXU // nh
    nj = ng // gt
    dt = jnp.exp(log_dt)[:, None]
    mag = jnp.exp(a_re * dt)
    abar_re = mag * jnp.cos(a_im * dt)
    abar_im = mag * jnp.sin(a_im * dt)
    den = a_re * a_re + a_im * a_im
    num_re = abar_re - 1.0
    coef_re = (num_re * a_re + abar_im * a_im) / den
    coef_im = (abar_im * a_re - num_re * a_im) / den
    bb_re = coef_re[:, :, None] * b_re - coef_im[:, :, None] * b_im
    bb_im = coef_re[:, :, None] * b_im + coef_im[:, :, None] * b_re
    p = jnp.asarray([1.0, S5_CHUNK, S5_CHUNK * S5_SUPER] + [S5_CHUNK * r for r in range(S5_SUPER)], F32)
    pmag = jnp.exp(p[:, None, None] * (a_re * dt)[None])
    pw_re = pmag * jnp.cos(p[:, None, None] * (a_im * dt)[None])
    pw_im = pmag * jnp.sin(p[:, None, None] * (a_im * dt)[None])
    tile = lambda a: a.reshape(-1, nj, gt * ns).transpose(1, 0, 2)
    apow = jnp.concatenate([tile(pw_re), tile(pw_im)], axis=1)
    eye = jnp.eye(gt, dtype=F32)

    def bdiag_in(bb):
        return jnp.einsum('jgnh,gk->jghkn', bb.reshape(nj, gt, ns, nh), eye).reshape(nj, gt * nh, gt * ns)

    def bdiag_out(cc):
        return jnp.einsum('jghn,gk->jgnkh', cc.reshape(nj, gt, nh, ns), eye).reshape(nj, gt * ns, gt * nh)

    bdb = jnp.stack([bdiag_in(bb_re), bdiag_in(bb_im)], axis=1)
    bdc = jnp.stack([bdiag_out(c_re), bdiag_out(-c_im)], axis=1)
    return bdb.astype(BF16), bdc.astype(BF16), apow, d_skip.reshape(nj, 1, gt * nh)


def _cmul_add(ar, ai, xr, xi, vr, vi):
    return ar * xr - ai * xi + vr, ar * xi + ai * xr + vi


def _s5_kernel(u_ref, bdb_ref, bdc_ref, ap_ref, d_ref, p1_ref, p1t_ref, p2_ref, p2t_ref, o_ref,
               u2n_ref, u2_ref, hr_ref, hi_ref, zr_ref, zi_ref, gr_ref, gi_ref, y2_ref, y2n_ref,
               *, n_super):
    tc, nr, nm = S5_CHUNK, S5_SUPER, n_super
    tile = tc * nr
    np_ = 3 + nr
    apow = lambda k: (ap_ref[0, k:k + 1, :], ap_ref[0, np_ + k:np_ + k + 1, :])
    ar, ai = apow(0)

    for m in range(nm):
        pu = jnp.dot(p1_ref[...], u_ref[m * tile:(m + 1) * tile, :], preferred_element_type=F32).astype(BF16)
        for s in range(tc):
            u2n_ref[s, m * nr:(m + 1) * nr, :] = pu[s * nr:(s + 1) * nr, :]
    for s in range(tc):
        u2_ref[s] = jnp.dot(p2_ref[...], u2n_ref[s], preferred_element_type=F32).astype(BF16)

    def drive(s):
        u = u2_ref[s]
        return (jnp.dot(u, bdb_ref[0, 0], preferred_element_type=F32),
                jnp.dot(u, bdb_ref[0, 1], preferred_element_type=F32))

    hr_ref[...] = jnp.zeros(hr_ref.shape, F32)
    hi_ref[...] = jnp.zeros(hi_ref.shape, F32)

    def pass1(s, carry):
        vr, vi = drive(s)
        hr, hi = _cmul_add(ar, ai, hr_ref[...], hi_ref[...], vr, vi)
        hr_ref[...] = hr
        hi_ref[...] = hi
        return carry

    lax.fori_loop(0, tc, pass1, 0, unroll=8)

    a16r, a16i = apow(1)
    qr = jnp.zeros((nm, hr_ref.shape[1]), F32)
    qi = qr
    for r in range(nr):
        rows = slice(r * nm, (r + 1) * nm)
        zr_ref[rows, :] = qr
        zi_ref[rows, :] = qi
        qr, qi = _cmul_add(a16r, a16i, qr, qi, hr_ref[rows, :], hi_ref[rows, :])
    a256r, a256i = apow(2)
    gr = jnp.zeros((1, hr_ref.shape[1]), F32)
    gi = gr
    for m in range(nm):
        gr_ref[m:m + 1, :] = gr
        gi_ref[m:m + 1, :] = gi
        gr, gi = _cmul_add(a256r, a256i, gr, gi, qr[m:m + 1, :], qi[m:m + 1, :])
    gpr = gr_ref[...]
    gpi = gi_ref[...]
    for r in range(nr):
        rows = slice(r * nm, (r + 1) * nm)
        pr, pi = apow(3 + r)
        zr, zi = _cmul_add(pr, pi, gpr, gpi, zr_ref[rows, :], zi_ref[rows, :])
        zr_ref[rows, :] = zr
        zi_ref[rows, :] = zi

    def pass2(t, carry):
        vr, vi = drive(t)
        sr, si = _cmul_add(ar, ai, zr_ref[...], zi_ref[...], vr, vi)
        zr_ref[...] = sr
        zi_ref[...] = si
        y = (jnp.dot(sr.astype(BF16), bdc_ref[0, 0], preferred_element_type=F32)
             + jnp.dot(si.astype(BF16), bdc_ref[0, 1], preferred_element_type=F32))
        y = y + d_ref[0] * u2_ref[t].astype(F32)
        y = 0.5 * y * (1.0 + jnp.tanh(math.sqrt(2.0 / math.pi) * (y + 0.044715 * (y * y * y))))
        y2_ref[t] = y.astype(BF16)
        return carry

    lax.fori_loop(0, tc, pass2, 0, unroll=8)

    for t in range(tc):
        yn = jnp.dot(p2t_ref[...], y2_ref[t], preferred_element_type=F32).astype(BF16)
        for m in range(nm):
            y2n_ref[m * tile + t * nr:m * tile + (t + 1) * nr, :] = yn[m * nr:(m + 1) * nr, :]
    for m in range(nm):
        rows = slice(m * tile, (m + 1) * tile)
        o_ref[rows, :] = jnp.dot(p1t_ref[...], y2n_ref[rows, :], preferred_element_type=F32).astype(BF16)


def _s5_branch(proj, col0, params, bsz, seq):
    bdb, bdc, apow, dskip = params
    nj, _, w, sl = bdb.shape
    tile = S5_CHUNK * S5_SUPER
    assert seq % tile == 0 and col0 % w == 0
    nm = seq // tile
    assert nm % V7X_SUBLANES == 0, "row slabs of the chunk recurrence must be whole sublane tiles"
    rows = S5_SUPER * nm
    i1 = jnp.arange(tile)
    p1 = (i1[:, None] == ((i1 % S5_CHUNK) * S5_SUPER + i1 // S5_CHUNK)[None, :]).astype(BF16)
    i2 = jnp.arange(rows)
    p2 = (i2[:, None] == ((i2 % S5_SUPER) * nm + i2 // S5_SUPER)[None, :]).astype(BF16)
    const = lambda a: pl.BlockSpec(a.shape, lambda b, j: (0,) * a.ndim)
    per_j = lambda a: pl.BlockSpec((1,) + a.shape[1:], lambda b, j: (j,) + (0,) * (a.ndim - 1))
    return pl.pallas_call(
        functools.partial(_s5_kernel, n_super=nm),
        grid=(bsz, nj),
        in_specs=[
            pl.BlockSpec((seq, w), lambda b, j: (b, col0 // w + j)),
            per_j(bdb), per_j(bdc), per_j(apow), per_j(dskip),
            const(p1), const(p1), const(p2), const(p2),
        ],
        out_specs=pl.BlockSpec((seq, w), lambda b, j: (b, j)),
        out_shape=jax.ShapeDtypeStruct((bsz * seq, nj * w), BF16),
        scratch_shapes=[
            pltpu.VMEM((S5_CHUNK, rows, w), BF16), pltpu.VMEM((S5_CHUNK, rows, w), BF16),
            pltpu.VMEM((rows, sl), F32), pltpu.VMEM((rows, sl), F32),
            pltpu.VMEM((rows, sl), F32), pltpu.VMEM((rows, sl), F32),
            pltpu.VMEM((nm, sl), F32), pltpu.VMEM((nm, sl), F32),
            pltpu.VMEM((S5_CHUNK, rows, w), BF16), pltpu.VMEM((seq, w), BF16),
        ],
        compiler_params=_cparams(("arbitrary", "arbitrary")),
        name="s5",
    )(proj, bdb, bdc, apow, dskip, p1, p1.T, p2, p2.T)


def _merge_kernel(yc_ref, ys_ref, l1_ref, l2_ref, wa_ref, wb_ref, gb1_ref, gb2_ref, o_ref):
    y_conv = yc_ref[...].astype(F32)
    ys = ys_ref[...]
    a = jnp.dot(ys, wa_ref[...], preferred_element_type=F32)
    b = jnp.dot(ys, wb_ref[...], preferred_element_type=F32)
    y_s5 = a * _sigmoid(b)
    g1 = _sigmoid(l1_ref[...].astype(F32) + gb1_ref[...])
    g2 = _sigmoid(l2_ref[...].astype(F32) + gb2_ref[...])
    o_ref[...] = (g1 * y_conv + g2 * y_s5).astype(BF16)


def _merge(yc, ys, proj, wa, wb, gate_b, tm, col0):
    t, ch = ys.shape
    d = wa.shape[1]
    tn = ch
    assert col0 % tn == 0 and d % tn == 0
    cb, nj = col0 // tn, d // tn
    wcol = lambda rows: pl.BlockSpec((rows, tn), lambda j, i: (0, j))
    return pl.pallas_call(
        _merge_kernel,
        grid=(nj, t // tm),
        in_specs=[
            pl.BlockSpec((tm, tn), lambda j, i: (i, j)),
            pl.BlockSpec((tm, ch), lambda j, i: (i, 0)),
            pl.BlockSpec((tm, tn), lambda j, i: (i, cb + j)),
            pl.BlockSpec((tm, tn), lambda j, i: (i, cb + nj + j)),
            wcol(ch), wcol(ch), wcol(1), wcol(1),
        ],
        out_specs=pl.BlockSpec((tm, tn), lambda j, i: (i, j)),
        out_shape=jax.ShapeDtypeStruct((t, d), BF16),
        compiler_params=_cparams(("arbitrary", "arbitrary")),
        name="merge",
    )(yc, ys, proj, proj, wa.astype(BF16), wb.astype(BF16),
      gate_b[:d].reshape(1, d), gate_b[d:].reshape(1, d))


def _split_bf16(v):
    hi = v.astype(BF16)
    lo = (v - hi.astype(F32)).astype(BF16)
    return hi, lo


def _pack_bf16_pairs(v):
    half = v.shape[-1] // 2
    b = lax.bitcast_convert_type(v, jnp.uint32)
    return (b[:, half:] & jnp.uint32(0xFFFF0000)) | (b[:, :half] >> 16)


def _unpack_bf16_pairs(p):
    lo = lax.bitcast_convert_type(p << 16, F32).astype(BF16)
    hi = lax.bitcast_convert_type(p & jnp.uint32(0xFFFF0000), F32).astype(BF16)
    return lo, hi


def _outproj_kernel(m_ref, x_ref, ada_ref, wo_ref, g1_ref, g2_ref, rwh_ref, rwl_ref, rb_ref,
                    x1_ref, h2p_ref, idx_ref, gate_ref, *, ne):
    m = jnp.dot(m_ref[...], wo_ref[...], preferred_element_type=F32)
    x1 = x_ref[...] + ada_ref[0, 2:3, :] * _rms(m, g1_ref[...])
    x1_ref[...] = x1
    h2 = _rms(x1, g2_ref[...]) * (1.0 + ada_ref[0, 4:5, :]) + ada_ref[0, 3:4, :]
    hh = h2.astype(jnp.bfloat16)
    hf = hh.astype(F32)
    h2p_ref[...] = _pack_bf16_pairs(hf)
    hl = (h2 - hf).astype(BF16)
    lg = jnp.dot(hh, rwh_ref[...], preferred_element_type=F32)
    lg = lg + jnp.dot(hl, rwh_ref[...], preferred_element_type=F32)
    lg = lg + jnp.dot(hh, rwl_ref[...], preferred_element_type=F32)
    lg = lg + rb_ref[...]
    lane = lax.broadcasted_iota(jnp.int32, lg.shape, 1)
    work = jnp.where(lane < ne, lg, -jnp.inf)
    vals, idxs = [], []
    for _ in range(TOP_K):
        mx = jnp.max(work, axis=-1, keepdims=True)
        ix = jnp.min(jnp.where(work == mx, lane, lg.shape[1]), axis=-1, keepdims=True)
        vals.append(mx)
        idxs.append(ix)
        work = jnp.where(lane == ix, -jnp.inf, work)
    ex = [jnp.exp(v - vals[0]) for v in vals]
    den = ex[0]
    for e in ex[1:]:
        den = den + e
    idx_out = jnp.zeros(lg.shape, jnp.int32)
    gate_out = jnp.zeros(lg.shape, F32)
    for k in range(TOP_K):
        idx_out = jnp.where(lane == k, idxs[k], idx_out)
        gate_out = jnp.where(lane == k, ex[k] / den, gate_out)
    idx_ref[...] = idx_out
    gate_ref[...] = gate_out


def _outproj(merged, x2, ada3, w_out, post_mix_g, pre_ffn_g, router_w, router_b, seq, tm):
    t, d = x2.shape
    ne = router_w.shape[1]
    nep = max(V7X_LANES, ne)
    rw = jnp.zeros((d, nep), F32).at[:, :ne].set(router_w)
    rw_hi, rw_lo = _split_bf16(rw)
    rb = jnp.zeros((1, nep), F32).at[0, :ne].set(router_b)
    per_b = seq // tm
    full = lambda shape: pl.BlockSpec(shape, lambda i: (0, 0))
    row = lambda w: pl.BlockSpec((tm, w), lambda i: (i, 0))
    x1, h2p, top_idx, gates = pl.pallas_call(
        functools.partial(_outproj_kernel, ne=ne),
        grid=(t // tm,),
        in_specs=[
            row(d), row(d),
            pl.BlockSpec((1, N_ADA, d), lambda i: (i // per_b, 0, 0)),
            full((d, d)), full((1, d)), full((1, d)), full((d, nep)), full((d, nep)), full((1, nep)),
        ],
        out_specs=[row(d), row(d // 2), row(nep), row(nep)],
        out_shape=[jax.ShapeDtypeStruct((t, d), F32), jax.ShapeDtypeStruct((t, d // 2), jnp.uint32),
                   jax.ShapeDtypeStruct((t, nep), jnp.int32), jax.ShapeDtypeStruct((t, nep), F32)],
        compiler_params=_cparams(("arbitrary",)),
        name="outproj",
    )(merged, x2, ada3, w_out.astype(BF16), post_mix_g.reshape(1, d), pre_ffn_g.reshape(1, d),
      rw_hi, rw_lo, rb)
    return x1, h2p, top_idx[:, :TOP_K], gates


GATHER_UNROLL = 8


def _row_copy(idx_ref, src_hbm, buf, sems, slot, r):
    return pltpu.make_async_copy(src_hbm.at[pl.ds(idx_ref[0, 0, r], 1), :],
                                 buf.at[slot, pl.ds(r, 1), :], sems.at[slot])


def _start_rows(idx_ref, src_hbm, buf, sems, slot, n, unrolled=False):
    if unrolled:
        for r in range(n):
            _row_copy(idx_ref, src_hbm, buf, sems, slot, r).start(priority=r % 2)
        return

    def start(r, carry):
        _row_copy(idx_ref, src_hbm, buf, sems, slot, r).start()
        return carry
    lax.fori_loop(0, n, start, 0, unroll=GATHER_UNROLL)


def _wait_rows(src_hbm, buf, sems, slot):
    n = buf.shape[1]
    pltpu.make_async_copy(src_hbm.at[pl.ds(0, n), :], buf.at[slot], sems.at[slot]).wait()


def _gather_kernel(tok_ref, h_hbm, o_ref, buf, sems):
    i = pl.program_id(0)
    n_steps = pl.num_programs(0) - 1

    for slot in (0, 1):
        @pl.when(jnp.logical_and(i < n_steps, i % 2 == slot))
        def _():
            _start_rows(tok_ref, h_hbm, buf, sems, slot, buf.shape[1], unrolled=True)

    @pl.when(i > 0)
    def _():
        slot = (i - 1) % 2
        _wait_rows(h_hbm, buf, sems, slot)
        o_ref[...] = buf[slot]


def _moe_gather(tok_buf, h2p, rows):
    n_rows = tok_buf.size
    assert n_rows % rows == 0
    n_steps = n_rows // rows
    t, w = h2p.shape
    return pl.pallas_call(
        _gather_kernel,
        grid=(n_steps + 1,),
        in_specs=[
            pl.BlockSpec((1, 1, rows), lambda i: (jnp.minimum(i, n_steps - 1), 0, 0), memory_space=pltpu.SMEM),
            pl.BlockSpec(memory_space=pl.ANY),
        ],
        out_specs=pl.BlockSpec((rows, w), lambda i: (jnp.maximum(i - 1, 0), 0)),
        out_shape=jax.ShapeDtypeStruct((n_rows, w), h2p.dtype),
        scratch_shapes=[pltpu.VMEM((2, rows, w), h2p.dtype), pltpu.SemaphoreType.DMA((2,))],
        compiler_params=_cparams(("arbitrary",)),
        name="moe_gather",
    )(tok_buf.reshape(n_steps, 1, rows), h2p)


def _group_weights(gf_ref, gn_ref, e_ref, i, first_step, more_passes, copy_of, gcnt):
    @pl.when(first_step)
    def _():
        gcnt[0] = 0
        copy_of(e_ref[0], False, 0).start()

    slot = gcnt[0] % 2

    @pl.when(gf_ref[i] == 1)
    def _():
        copy_of(e_ref[0], False, slot).wait()
        nxt = gn_ref[i]

        @pl.when(nxt >= 0)
        def _():
            copy_of(nxt, False, 1 - slot).start()

        @pl.when(jnp.logical_and(nxt < 0, more_passes))
        def _():
            copy_of(e_ref[0], True, 1 - slot).start()

        gcnt[0] = gcnt[0] + 1

    return slot


def _up_kernel(e_ref, nv_ref, gf_ref, gn_ref, x_ref, w_hbm, bg_ref, bl_ref, p_ref, o_ref,
               wp_ref, wbuf, wsem, gcnt):
    j, i = pl.program_id(0), pl.program_id(1)
    tn = wp_ref.shape[1]
    half = V7X_MXU // 2
    hd = x_ref.shape[1]

    def copy_of(expert, next_pass, slot):
        col = (j + 1) * tn if next_pass else j * tn
        return pltpu.make_async_copy(w_hbm.at[expert, :, pl.ds(col, tn)], wbuf.at[slot], wsem.at[slot])

    slot = _group_weights(gf_ref, gn_ref, e_ref, i, jnp.logical_and(j == 0, i == 0),
                          j < pl.num_programs(0) - 1, copy_of, gcnt)

    @pl.when(gf_ref[i] == 1)
    def _():
        for cb in range(tn // V7X_MXU):
            cols = slice(cb * V7X_MXU, (cb + 1) * V7X_MXU)
            wp_ref[:, cols] = jnp.dot(wbuf[slot, :, cols].astype(BF16), p_ref[...],
                                      preferred_element_type=F32).astype(BF16)

    @pl.when(nv_ref[i] > 0)
    def _():
        x_lo, x_hi = _unpack_bf16_pairs(x_ref[...])
        for cb in range(tn // V7X_MXU):
            cols = slice(cb * V7X_MXU, (cb + 1) * V7X_MXU)
            hb = (jnp.dot(x_lo, wp_ref[0:hd, cols], preferred_element_type=F32)
                  + jnp.dot(x_hi, wp_ref[hd:2 * hd, cols], preferred_element_type=F32))
            out_cols = slice(cb * half, (cb + 1) * half)
            x_glu = hb[:, :half] + bg_ref[0, :, out_cols]
            x_lin = hb[:, half:] + bl_ref[0, :, out_cols]
            x_glu = jnp.minimum(x_glu, SWIGLU_LIMIT)
            x_lin = jnp.clip(x_lin, -SWIGLU_LIMIT, SWIGLU_LIMIT)
            act = x_glu * _sigmoid(SWIGLU_ALPHA * x_glu) * (x_lin + 1.0)
            o_ref[:, out_cols] = act.astype(BF16)

    @pl.when(nv_ref[i] == 0)
    def _():
        o_ref[...] = jnp.zeros(o_ref.shape, o_ref.dtype)


def _moe_up(blk_e, nvalid, gfirst, gnext, xs, w1, b1, tn):
    n_rows, hd = xs.shape
    ne, d, f2 = w1.shape
    assert d == 2 * hd
    f = f2 // 2
    n_blocks = n_rows // MOE_BLOCK
    half = V7X_MXU // 2
    c = jnp.arange(V7X_MXU)
    perm = (((c % 2) * half + c // 2)[:, None] == c[None, :]).astype(BF16)
    b1g = b1[:, 0::2].reshape(ne, 1, f)
    b1l = b1[:, 1::2].reshape(ne, 1, f)
    return pl.pallas_call(
        _up_kernel,
        grid_spec=pltpu.PrefetchScalarGridSpec(
            num_scalar_prefetch=4,
            grid=(f2 // tn, n_blocks),
            in_specs=[
                pl.BlockSpec((MOE_BLOCK, hd), lambda j, i, e, nv, gf, gn: (i, 0)),
                pl.BlockSpec(memory_space=pl.ANY),
                pl.BlockSpec((1, 1, tn // 2), lambda j, i, e, nv, gf, gn: (e[i], 0, j)),
                pl.BlockSpec((1, 1, tn // 2), lambda j, i, e, nv, gf, gn: (e[i], 0, j)),
                pl.BlockSpec((V7X_MXU, V7X_MXU), lambda j, i, e, nv, gf, gn: (0, 0)),
            ],
            out_specs=pl.BlockSpec((MOE_BLOCK, tn // 2), lambda j, i, e, nv, gf, gn: (i, j)),
            scratch_shapes=[pltpu.VMEM((d, tn), BF16), pltpu.VMEM((2, d, tn), F32),
                            pltpu.SemaphoreType.DMA((2,)), pltpu.SMEM((1,), jnp.int32)],
        ),
        out_shape=jax.ShapeDtypeStruct((n_rows, f), BF16),
        compiler_params=_cparams(("arbitrary", "arbitrary")),
        name="moe_up",
    )(blk_e, nvalid, gfirst, gnext, xs, w1, b1g, b1l, perm)


def _down_kernel(e_ref, nv_ref, gf_ref, gn_ref, a_ref, w_hbm, b_ref, o_ref, wb_ref, wbuf, wsem, gcnt):
    i = pl.program_id(0)

    def copy_of(expert, next_pass, slot):
        return pltpu.make_async_copy(w_hbm.at[expert], wbuf.at[slot], wsem.at[slot])

    slot = _group_weights(gf_ref, gn_ref, e_ref, i, i == 0, False, copy_of, gcnt)

    @pl.when(gf_ref[i] == 1)
    def _():
        wb_ref[...] = wbuf[slot].astype(BF16)

    @pl.when(nv_ref[i] > 0)
    def _():
        o_ref[...] = jnp.dot(a_ref[...], wb_ref[...], preferred_element_type=F32) + b_ref[0]

    @pl.when(nv_ref[i] == 0)
    def _():
        o_ref[...] = jnp.zeros(o_ref.shape, o_ref.dtype)


def _moe_down(blk_e, nvalid, gfirst, gnext, act, w2, b2):
    n_rows, f = act.shape
    ne, _, d = w2.shape
    n_blocks = n_rows // MOE_BLOCK
    return pl.pallas_call(
        _down_kernel,
        grid_spec=pltpu.PrefetchScalarGridSpec(
            num_scalar_prefetch=4,
            grid=(n_blocks,),
            in_specs=[
                pl.BlockSpec((MOE_BLOCK, f), lambda i, e, nv, gf, gn: (i, 0)),
                pl.BlockSpec(memory_space=pl.ANY),
                pl.BlockSpec((1, 1, d), lambda i, e, nv, gf, gn: (e[i], 0, 0)),
            ],
            out_specs=pl.BlockSpec((MOE_BLOCK, d), lambda i, e, nv, gf, gn: (i, 0)),
            scratch_shapes=[pltpu.VMEM((f, d), BF16), pltpu.VMEM((2, f, d), F32),
                            pltpu.SemaphoreType.DMA((2,)), pltpu.SMEM((1,), jnp.int32)],
        ),
        out_shape=jax.ShapeDtypeStruct((n_rows, d), F32),
        compiler_params=_cparams(("arbitrary",)),
        name="moe_down",
    )(blk_e, nvalid, gfirst, gnext, act, w2, b2.reshape(ne, 1, d))


def _combine_kernel(pos_ref, y_hbm, gate_ref, x1_ref, ada_ref, g_ref, o_ref, buf, sems, *, tt):
    i = pl.program_id(0)
    n_tiles = pl.num_programs(0) - 1
    n_rows = TOP_K * tt

    def finish(slot):
        gates = gate_ref[...]
        f = buf[slot, 0:tt, :] * gates[:, 0:1]
        for k in range(1, TOP_K):
            f = f + buf[slot, k * tt:(k + 1) * tt, :] * gates[:, k:k + 1]
        o_ref[...] = x1_ref[...] + ada_ref[0, 5:6, :] * _rms(f, g_ref[...])

    @pl.when(i == 0)
    def _():
        _start_rows(pos_ref, y_hbm, buf, sems, 0, n_rows)

    for slot in (0, 1):
        @pl.when(jnp.logical_and(jnp.logical_and(i > 0, i < n_tiles), i % 2 == slot))
        def _():
            _wait_rows(y_hbm, buf, sems, 1 - slot)
            _start_rows(pos_ref, y_hbm, buf, sems, slot, n_rows, unrolled=True)
            finish(1 - slot)

    @pl.when(i == n_tiles)
    def _():
        _wait_rows(y_hbm, buf, sems, (i - 1) % 2)
        finish((i - 1) % 2)


def _moe_combine(pos, y_buf, gates, x1, ada3, post_ffn_g, seq, tt):
    t, d = x1.shape
    per_b = seq // tt
    nt = t // tt
    pos_t = pos.reshape(nt, tt, TOP_K).transpose(0, 2, 1).reshape(nt, 1, TOP_K * tt)
    prev = lambda i: jnp.maximum(i - 1, 0)
    return pl.pallas_call(
        functools.partial(_combine_kernel, tt=tt),
        grid=(nt + 1,),
        in_specs=[
            pl.BlockSpec((1, 1, TOP_K * tt), lambda i: (jnp.minimum(i, nt - 1), 0, 0), memory_space=pltpu.SMEM),
            pl.BlockSpec(memory_space=pl.ANY),
            pl.BlockSpec((tt, gates.shape[1]), lambda i: (prev(i), 0)),
            pl.BlockSpec((tt, d), lambda i: (prev(i), 0)),
            pl.BlockSpec((1, N_ADA, d), lambda i: (prev(i) // per_b, 0, 0)),
            pl.BlockSpec((1, d), lambda i: (0, 0)),
        ],
        out_specs=pl.BlockSpec((tt, d), lambda i: (prev(i), 0)),
        out_shape=jax.ShapeDtypeStruct((t, d), F32),
        scratch_shapes=[pltpu.VMEM((2, TOP_K * tt, d), F32), pltpu.SemaphoreType.DMA((2,))],
        compiler_params=_cparams(("arbitrary",)),
        name="moe_combine",
    )(pos_t, y_buf, gates, x1, ada3, post_ffn_g.reshape(1, d))


def _route(top_idx, n_experts):
    n_tok = top_idx.shape[0]
    n_asg = n_tok * TOP_K
    i32 = jnp.int32
    flat_e = top_idx.reshape(n_asg)
    order = jnp.argsort(flat_e).astype(i32)
    counts = jnp.sum(flat_e[:, None] == jnp.arange(n_experts, dtype=i32)[None, :], axis=0, dtype=i32)
    starts = jnp.cumsum(counts) - counts
    padded = (counts + MOE_BLOCK - 1) // MOE_BLOCK * MOE_BLOCK
    pends = jnp.cumsum(padded)
    pstarts = pends - padded
    n_blocks = -(-n_asg // MOE_BLOCK) + n_experts
    blk_start = jnp.arange(n_blocks, dtype=i32) * MOE_BLOCK
    blk_e = jnp.minimum(jnp.sum(pends[None, :] <= blk_start[:, None], axis=1, dtype=i32), n_experts - 1)
    off = blk_start - pstarts[blk_e]
    nvalid = jnp.clip(counts[blk_e] - off, 0, MOE_BLOCK)
    j = off[:, None] + jnp.arange(MOE_BLOCK, dtype=i32)[None, :]
    valid = j < counts[blk_e][:, None]
    row_id = blk_start[:, None] + jnp.arange(MOE_BLOCK, dtype=i32)[None, :]
    src = jnp.where(valid, starts[blk_e][:, None] + j, row_id % n_asg)
    asg = order[src]
    tok_buf = asg // TOP_K
    rank = jnp.argsort(order).astype(i32)
    pos = (pstarts - starts)[flat_e] + rank
    blk = jnp.arange(n_blocks, dtype=i32)
    prev_e = jnp.concatenate([jnp.full((1,), -1, i32), blk_e[:-1]])
    gfirst = jnp.logical_and(nvalid > 0, blk_e != prev_e).astype(i32)
    cand = jnp.where(gfirst == 1, blk, n_blocks)
    later = jnp.concatenate([lax.cummin(cand[::-1])[::-1][1:], jnp.full((1,), n_blocks, i32)])
    gnext = jnp.where(later < n_blocks, blk_e[jnp.minimum(later, n_blocks - 1)], -1)
    return tok_buf, pos, blk_e, nvalid, gfirst, gnext


def _pick(n, pref):
    return pref if n % pref == 0 else n


def kernel(x, c, ada_w, ada_b, pre_mix_g, post_mix_g, pre_ffn_g, post_ffn_g, w_in, gate_b, dw_w, dw_b, cln_g, cln_b, cp_w, cp_b, s5_a_re, s5_a_im, s5_log_dt, s5_b_re, s5_b_im, s5_c_re, s5_c_im, s5_d, glu_wa, glu_wb, w_out, router_w, router_b, w1, b1, w2, b2):
    bsz, seq, d = x.shape
    t = bsz * seq
    depth = ada_w.shape[0]
    conv_ch = dw_w.shape[-1]
    ng, ns, nh = s5_b_re.shape[1:]
    s5_w = ng * nh
    ne = router_w.shape[-1]
    col_s5 = 2 * conv_ch
    col_gate = col_s5 + s5_w
    assert conv_ch == s5_w and V7X_MXU % nh == 0 and ng % (V7X_MXU // nh) == 0
    tm = _pick(seq, 512)
    tm_out = _pick(seq, 512)
    tn_in = _pick(w_in.shape[-1], conv_ch)

    x2 = x.reshape(t, d)
    for l in range(depth):
        ada3 = _ada(c, ada_w[l], ada_b[l]).reshape(bsz, N_ADA, d)
        proj = _inproj(x2, ada3, pre_mix_g[l], w_in[l], seq, _pick(seq, 1024), tn_in)
        yc = _conv_branch(proj, dw_w[l], dw_b[l], cln_g[l], cln_b[l], cp_w[l], cp_b[l], bsz, seq, tm)

        s5_params = _s5_params(s5_a_re[l], s5_a_im[l], s5_log_dt[l], s5_b_re[l], s5_b_im[l],
                               s5_c_re[l], s5_c_im[l], s5_d[l])
        ys = _s5_branch(proj, col_s5, s5_params, bsz, seq)

        merged = _merge(yc, ys, proj, glu_wa[l], glu_wb[l], gate_b[l], _pick(seq, 1024), col_gate)
        x1, h2p, top_idx, gates = _outproj(merged, x2, ada3, w_out[l], post_mix_g[l], pre_ffn_g[l],
                                           router_w[l], router_b[l], seq, tm_out)

        tok_buf, pos, blk_e, nvalid, gfirst, gnext = _route(top_idx, ne)
        nb = tok_buf.shape[0]
        xs = _moe_gather(tok_buf, h2p, MOE_BLOCK * next(k for k in (4, 2, 1) if nb % k == 0))
        act = _moe_up(blk_e, nvalid, gfirst, gnext, xs, w1[l], b1[l], _pick(w1.shape[-1], 2048))
        y_buf = _moe_down(blk_e, nvalid, gfirst, gnext, act, w2[l], b2[l])
        x2 = _moe_combine(pos, y_buf, gates, x1, ada3, post_ffn_g[l], seq, _pick(seq, 256))
    return x2.reshape(bsz, seq, d)
```
